```python
import math
import jax, jax.numpy as jnp
from jax import lax
import numpy as np

D_MODEL = 1024
BATCH = 16
SEQ = 2048
DEPTH = 4

N_A_LAYERS = DEPTH // 2
N_B_LAYERS = DEPTH - N_A_LAYERS
SSM_GROUP = 16
SSM_GROUPS = D_MODEL // SSM_GROUP
SSM_STATE = 64
DT_MIN = 1e-3
DT_MAX = 1e-1
HEAD_DIM = 64
N_HEADS = D_MODEL // HEAD_DIM
DILATED_BRANCHES = ((128, 1), (512, 4), (2048, 16))
N_BRANCHES = len(DILATED_BRANCHES)
BRANCH_WIDTH = N_HEADS * HEAD_DIM
Q_WIDTH = N_BRANCHES * BRANCH_WIDTH
D_FF = 4 * D_MODEL
BLOCK = 128
EPS = 1e-6
NEG = -1e30

kernel_name = "yoco_s5_dilated_attn_hybrid"


def rms_norm(x, g):
    xf = x.astype(jnp.float32)
    y = xf * lax.rsqrt(jnp.mean(xf * xf, axis=-1, keepdims=True) + EPS)
    return (y * g.astype(jnp.float32)).astype(x.dtype)


def ada_chunks(c, w, b, n):
    m = jax.nn.silu(c) @ w + b
    return jnp.split(m[:, None, :], n, axis=-1)


def s5_mixer(u, lam_re, lam_im, log_dt, b_re, b_im, c_re, c_im, d_skip, w_glu):
    bsz, seq, dm = u.shape
    f32 = jnp.float32
    lam = lax.complex(lam_re.astype(f32), lam_im.astype(f32))
    dt = jnp.exp(log_dt.astype(f32))[:, None]
    lam_bar = jnp.exp(lam * dt)
    b_mat = lax.complex(b_re.astype(f32), b_im.astype(f32))
    b_bar = ((lam_bar - 1.0) / lam)[..., None] * b_mat
    c_mat = lax.complex(c_re.astype(f32), c_im.astype(f32))
    uf = u.astype(f32)
    ug = uf.reshape(bsz, seq, SSM_GROUPS, SSM_GROUP).astype(jnp.complex64)
    bu = jnp.einsum('bsgc,gpc->bsgp', ug, b_bar)
    a = jnp.broadcast_to(lam_bar, (1, seq) + lam_bar.shape)

    def combine(left, right):
        a_l, b_l = left
        a_r, b_r = right
        return a_r * a_l, a_r * b_l + b_r

    _, state = lax.associative_scan(combine, (a, bu), axis=1)
    y = jnp.einsum('bsgp,gcp->bsgc', state, c_mat).real.reshape(bsz, seq, dm)
    y = y + d_skip.astype(f32) * uf
    z = jax.nn.gelu(y).astype(u.dtype)
    val, gate = jnp.split(z @ w_glu, 2, axis=-1)
    return val * jax.nn.sigmoid(gate)


def to_dilated_blocks(t, dil):
    bsz, seq = t.shape[:2]
    rest = t.shape[2:]
    sub = seq // dil
    nb = -(-sub // BLOCK)
    t = jnp.swapaxes(t.reshape((bsz, sub, dil) + rest), 1, 2)
    t = jnp.pad(t, [(0, 0), (0, 0), (0, nb * BLOCK - sub)] + [(0, 0)] * len(rest))
    return t.reshape((bsz, dil, nb, BLOCK) + rest)


def from_dilated_blocks(t, seq):
    bsz, dil, nb, blk = t.shape[:4]
    rest = t.shape[4:]
    sub = seq // dil
    t = t.reshape((bsz, dil, nb * blk) + rest)[:, :, :sub]
    return jnp.swapaxes(t, 1, 2).reshape((bsz, seq) + rest)


def band_keys(t):
    prev = jnp.concatenate([jnp.zeros_like(t[:, :, :1]), t[:, :, :-1]], axis=2)
    return jnp.concatenate([prev, t], axis=3)


def band_mask(nb, span):
    qi = jnp.arange(BLOCK)[:, None]
    kj = jnp.arange(2 * BLOCK)[None, :] - BLOCK
    dist = qi - kj
    rel = (dist >= 0) & (dist <= span)
    abs_k = jnp.arange(nb)[:, None, None] * BLOCK + kj[None]
    return rel[None] & (abs_k >= 0)


def dilated_branch(q, k_band, v_band, span, dil):
    f32 = jnp.float32
    seq = q.shape[1]
    qb = to_dilated_blocks(q, dil).astype(f32)
    nb = qb.shape[2]
    s = jnp.einsum('brnqhe,brnkhe->brnhqk', qb, k_band.astype(f32)) * (HEAD_DIM ** -0.5)
    s = jnp.where(band_mask(nb, span)[None, None, :, None], s, NEG)
    m = jnp.max(s, axis=-1, keepdims=True)
    p = jnp.exp(s - m)
    den = jnp.sum(p, axis=-1)
    o = jnp.einsum('brnhqk,brnkhe->brnqhe', p, v_band.astype(f32))
    o = o / jnp.swapaxes(den, 3, 4)[..., None]
    lse = jnp.swapaxes(m[..., 0] + jnp.log(den), 3, 4)
    return from_dilated_blocks(o, seq), from_dilated_blocks(lse, seq)


def shared_kv(h, c, kv_g, kv_ada_w, kv_ada_b, w_kv):
    bsz, seq, _ = h.shape
    shift, scale = ada_chunks(c, kv_ada_w, kv_ada_b, 2)
    u = rms_norm(h, kv_g) * (1.0 + scale) + shift
    kv = (u @ w_kv).reshape(bsz, seq, 2, N_BRANCHES, N_HEADS, HEAD_DIM)
    k_bands, v_bands = [], []
    for i, (win, dil) in enumerate(DILATED_BRANCHES):
        k_bands.append(band_keys(to_dilated_blocks(kv[:, :, 0, i], dil)))
        v_bands.append(band_keys(to_dilated_blocks(kv[:, :, 1, i], dil)))
    return k_bands, v_bands


def dilated_mixer(u, w_q, k_bands, v_bands, w_o):
    bsz, seq, _ = u.shape
    q = (u @ w_q).reshape(bsz, seq, N_BRANCHES, N_HEADS, HEAD_DIM)
    outs, lses = [], []
    for i, (win, dil) in enumerate(DILATED_BRANCHES):
        o, l = dilated_branch(q[:, :, i], k_bands[i], v_bands[i], win // dil, dil)
        outs.append(o)
        lses.append(l)
    weights = jax.nn.softmax(jnp.stack(lses, axis=-1), axis=-1)
    o = jnp.einsum('gbshe,bshg->bshe', jnp.stack(outs), weights)
    return o.reshape(bsz, seq, BRANCH_WIDTH).astype(u.dtype) @ w_o


def _fwd_setup_inputs(seed: int = 0) -> dict:
    key = jax.random.key(seed)
    ks = jax.random.split(key, 24)
    f32 = jnp.float32

    def nrm(k, shape, std):
        return jax.random.normal(k, shape, f32) * std

    n_idx = jnp.arange(SSM_STATE, dtype=f32)
    gp = (N_A_LAYERS, SSM_GROUPS, SSM_STATE)
    return {
        "x": nrm(ks[0], (BATCH, SEQ, D_MODEL), 1.0),
        "c": nrm(ks[1], (BATCH, D_MODEL), 1.0),
        "ln_g": 1.0 + nrm(ks[2], (DEPTH, 2, D_MODEL), 0.02),
        "ada_w": nrm(ks[3], (DEPTH, 2, D_MODEL, 3 * D_MODEL), 0.5 * D_MODEL ** -0.5),
        "ada_b": nrm(ks[4], (DEPTH, 2, 3 * D_MODEL), 0.02),
        "ssm_lam_re": -0.5 + nrm(ks[5], gp, 0.01),
        "ssm_lam_im": math.pi * n_idx + nrm(ks[6], gp, 0.01),
        "ssm_log_dt": jax.random.uniform(ks[7], (N_A_LAYERS, SSM_GROUPS), f32, math.log(DT_MIN), math.log(DT_MAX)),
        "ssm_b_re": nrm(ks[8], gp + (SSM_GROUP,), (2 * SSM_GROUP) ** -0.5),
        "ssm_b_im": nrm(ks[9], gp + (SSM_GROUP,), (2 * SSM_GROUP) ** -0.5),
        "ssm_c_re": nrm(ks[10], (N_A_LAYERS, SSM_GROUPS, SSM_GROUP, SSM_STATE), 0.5),
        "ssm_c_im": nrm(ks[11], (N_A_LAYERS, SSM_GROUPS, SSM_GROUP, SSM_STATE), 0.5),
        "ssm_d": nrm(ks[12], (N_A_LAYERS, D_MODEL), 1.0),
        "ssm_w_glu": nrm(ks[13], (N_A_LAYERS, D_MODEL, 2 * D_MODEL), D_MODEL ** -0.5),
        "kv_g": 1.0 + nrm(ks[14], (D_MODEL,), 0.02),
        "kv_ada_w": nrm(ks[15], (D_MODEL, 2 * D_MODEL), 0.5 * D_MODEL ** -0.5),
        "kv_ada_b": nrm(ks[16], (2 * D_MODEL,), 0.02),
        "w_kv": nrm(ks[17], (D_MODEL, 2 * Q_WIDTH), D_MODEL ** -0.5),
        "attn_w_q": nrm(ks[18], (N_B_LAYERS, D_MODEL, Q_WIDTH), D_MODEL ** -0.5),
        "attn_w_o": nrm(ks[19], (N_B_LAYERS, BRANCH_WIDTH, D_MODEL), BRANCH_WIDTH ** -0.5),
        "mlp_w1": nrm(ks[20], (DEPTH, D_MODEL, D_FF), D_MODEL ** -0.5),
        "mlp_w2": nrm(ks[21], (DEPTH, D_FF, D_MODEL), D_FF ** -0.5),
        "final_g": 1.0 + nrm(ks[22], (D_MODEL,), 0.02),
    }


def _fwd_reference(x, c, ln_g, ada_w, ada_b, ssm_lam_re, ssm_lam_im, ssm_log_dt, ssm_b_re, ssm_b_im,
              ssm_c_re, ssm_c_im, ssm_d, ssm_w_glu, kv_g, kv_ada_w, kv_ada_b, w_kv,
              attn_w_q, attn_w_o, mlp_w1, mlp_w2, final_g):
    h = x
    k_bands, v_bands = None, None
    for layer in range(DEPTH):
        if layer == N_A_LAYERS:
            k_bands, v_bands = shared_kv(h, c, kv_g, kv_ada_w, kv_ada_b, w_kv)
        shift, scale, gate = ada_chunks(c, ada_w[layer, 0], ada_b[layer, 0], 3)
        u = rms_norm(h, ln_g[layer, 0]) * (1.0 + scale) + shift
        if layer < N_A_LAYERS:
            y = s5_mixer(u, ssm_lam_re[layer], ssm_lam_im[layer], ssm_log_dt[layer], ssm_b_re[layer],
                         ssm_b_im[layer], ssm_c_re[layer], ssm_c_im[layer], ssm_d[layer], ssm_w_glu[layer])
        else:
            j = layer - N_A_LAYERS
            y = dilated_mixer(u, attn_w_q[j], k_bands, v_bands, attn_w_o[j])
        h = h + gate * y
        shift, scale, gate = ada_chunks(c, ada_w[layer, 1], ada_b[layer, 1], 3)
        u = rms_norm(h, ln_g[layer, 1]) * (1.0 + scale) + shift
        h = h + gate * (jnp.square(jax.nn.relu(u @ mlp_w1[layer])) @ mlp_w2[layer])
    return rms_norm(h, final_g)


import jax as _jax
import jax.numpy as _jnp

TWIN_FORMAT = 'train_step'
FWD_PARAMS = ['x', 'c', 'ln_g', 'ada_w', 'ada_b', 'ssm_lam_re', 'ssm_lam_im', 'ssm_log_dt', 'ssm_b_re', 'ssm_b_im', 'ssm_c_re', 'ssm_c_im', 'ssm_d', 'ssm_w_glu', 'kv_g', 'kv_ada_w', 'kv_ada_b', 'w_kv', 'attn_w_q', 'attn_w_o', 'mlp_w1', 'mlp_w2', 'final_g']
TWIN_WEIGHTS = ['ln_g', 'ada_w', 'ada_b', 'ssm_lam_re', 'ssm_lam_im', 'ssm_log_dt', 'ssm_b_re', 'ssm_b_im', 'ssm_c_re', 'ssm_c_im', 'ssm_d', 'ssm_w_glu', 'kv_g', 'kv_ada_w', 'kv_ada_b', 'w_kv', 'attn_w_q', 'attn_w_o', 'mlp_w1', 'mlp_w2', 'final_g']
TWIN_DIFF_INPUT = 'x'
TWIN_INPUTS = ['x', 'c', 'ln_g', 'ada_w', 'ada_b', 'ssm_lam_re', 'ssm_lam_im', 'ssm_log_dt', 'ssm_b_re', 'ssm_b_im', 'ssm_c_re', 'ssm_c_im', 'ssm_d', 'ssm_w_glu', 'kv_g', 'kv_ada_w', 'kv_ada_b', 'w_kv', 'attn_w_q', 'attn_w_o', 'mlp_w1', 'mlp_w2', 'final_g', 'loss_target', 'm_ln_g', 'm_ada_w', 'm_ada_b', 'm_ssm_lam_re', 'm_ssm_lam_im', 'm_ssm_log_dt', 'm_ssm_b_re', 'm_ssm_b_im', 'm_ssm_c_re', 'm_ssm_c_im', 'm_ssm_d', 'm_ssm_w_glu', 'm_kv_g', 'm_kv_ada_w', 'm_kv_ada_b', 'm_w_kv', 'm_attn_w_q', 'm_attn_w_o', 'm_mlp_w1', 'm_mlp_w2', 'm_final_g', 'v_ln_g', 'v_ada_w', 'v_ada_b', 'v_ssm_lam_re', 'v_ssm_lam_im', 'v_ssm_log_dt', 'v_ssm_b_re', 'v_ssm_b_im', 'v_ssm_c_re', 'v_ssm_c_im', 'v_ssm_d', 'v_ssm_w_glu', 'v_kv_g', 'v_kv_ada_w', 'v_kv_ada_b', 'v_w_kv', 'v_attn_w_q', 'v_attn_w_o', 'v_mlp_w1', 'v_mlp_w2', 'v_final_g']
TWIN_OUTPUTS = ['loss', 'grad_x', 'grad_ln_g', 'grad_ada_w', 'grad_ada_b', 'grad_ssm_lam_re', 'grad_ssm_lam_im', 'grad_ssm_log_dt', 'grad_ssm_b_re', 'grad_ssm_b_im', 'grad_ssm_c_re', 'grad_ssm_c_im', 'grad_ssm_d', 'grad_ssm_w_glu', 'grad_kv_g', 'grad_kv_ada_w', 'grad_kv_ada_b', 'grad_w_kv', 'grad_attn_w_q', 'grad_attn_w_o', 'grad_mlp_w1', 'grad_mlp_w2', 'grad_final_g', 'delta_ln_g', 'delta_ada_w', 'delta_ada_b', 'delta_ssm_lam_re', 'delta_ssm_lam_im', 'delta_ssm_log_dt', 'delta_ssm_b_re', 'delta_ssm_b_im', 'delta_ssm_c_re', 'delta_ssm_c_im', 'delta_ssm_d', 'delta_ssm_w_glu', 'delta_kv_g', 'delta_kv_ada_w', 'delta_kv_ada_b', 'delta_w_kv', 'delta_attn_w_q', 'delta_attn_w_o', 'delta_mlp_w1', 'delta_mlp_w2', 'delta_final_g', 'new_m_ln_g', 'new_m_ada_w', 'new_m_ada_b', 'new_m_ssm_lam_re', 'new_m_ssm_lam_im', 'new_m_ssm_log_dt', 'new_m_ssm_b_re', 'new_m_ssm_b_im', 'new_m_ssm_c_re', 'new_m_ssm_c_im', 'new_m_ssm_d', 'new_m_ssm_w_glu', 'new_m_kv_g', 'new_m_kv_ada_w', 'new_m_kv_ada_b', 'new_m_w_kv', 'new_m_attn_w_q', 'new_m_attn_w_o', 'new_m_mlp_w1', 'new_m_mlp_w2', 'new_m_final_g', 'new_v_ln_g', 'new_v_ada_w', 'new_v_ada_b', 'new_v_ssm_lam_re', 'new_v_ssm_lam_im', 'new_v_ssm_log_dt', 'new_v_ssm_b_re', 'new_v_ssm_b_im', 'new_v_ssm_c_re', 'new_v_ssm_c_im', 'new_v_ssm_d', 'new_v_ssm_w_glu', 'new_v_kv_g', 'new_v_kv_ada_w', 'new_v_kv_ada_b', 'new_v_w_kv', 'new_v_attn_w_q', 'new_v_attn_w_o', 'new_v_mlp_w1', 'new_v_mlp_w2', 'new_v_final_g']
TWIN_LEAF_KINDS = {'loss': 'loss', 'grad_x': 'grad_x', 'grad_ln_g': 'grad_w', 'grad_ada_w': 'grad_w', 'grad_ada_b': 'grad_w', 'grad_ssm_lam_re': 'grad_w', 'grad_ssm_lam_im': 'grad_w', 'grad_ssm_log_dt': 'grad_w', 'grad_ssm_b_re': 'grad_w', 'grad_ssm_b_im': 'grad_w', 'grad_ssm_c_re': 'grad_w', 'grad_ssm_c_im': 'grad_w', 'grad_ssm_d': 'grad_w', 'grad_ssm_w_glu': 'grad_w', 'grad_kv_g': 'grad_w', 'grad_kv_ada_w': 'grad_w', 'grad_kv_ada_b': 'grad_w', 'grad_w_kv': 'grad_w', 'grad_attn_w_q': 'grad_w', 'grad_attn_w_o': 'grad_w', 'grad_mlp_w1': 'grad_w', 'grad_mlp_w2': 'grad_w', 'grad_final_g': 'grad_w', 'delta_ln_g': 'delta_w', 'delta_ada_w': 'delta_w', 'delta_ada_b': 'delta_w', 'delta_ssm_lam_re': 'delta_w', 'delta_ssm_lam_im': 'delta_w', 'delta_ssm_log_dt': 'delta_w', 'delta_ssm_b_re': 'delta_w', 'delta_ssm_b_im': 'delta_w', 'delta_ssm_c_re': 'delta_w', 'delta_ssm_c_im': 'delta_w', 'delta_ssm_d': 'delta_w', 'delta_ssm_w_glu': 'delta_w', 'delta_kv_g': 'delta_w', 'delta_kv_ada_w': 'delta_w', 'delta_kv_ada_b': 'delta_w', 'delta_w_kv': 'delta_w', 'delta_attn_w_q': 'delta_w', 'delta_attn_w_o': 'delta_w', 'delta_mlp_w1': 'delta_w', 'delta_mlp_w2': 'delta_w', 'delta_final_g': 'delta_w', 'new_m_ln_g': 'new_m', 'new_m_ada_w': 'new_m', 'new_m_ada_b': 'new_m', 'new_m_ssm_lam_re': 'new_m', 'new_m_ssm_lam_im': 'new_m', 'new_m_ssm_log_dt': 'new_m', 'new_m_ssm_b_re': 'new_m', 'new_m_ssm_b_im': 'new_m', 'new_m_ssm_c_re': 'new_m', 'new_m_ssm_c_im': 'new_m', 'new_m_ssm_d': 'new_m', 'new_m_ssm_w_glu': 'new_m', 'new_m_kv_g': 'new_m', 'new_m_kv_ada_w': 'new_m', 'new_m_kv_ada_b': 'new_m', 'new_m_w_kv': 'new_m', 'new_m_attn_w_q': 'new_m', 'new_m_attn_w_o': 'new_m', 'new_m_mlp_w1': 'new_m', 'new_m_mlp_w2': 'new_m', 'new_m_final_g': 'new_m', 'new_v_ln_g': 'new_v', 'new_v_ada_w': 'new_v', 'new_v_ada_b': 'new_v', 'new_v_ssm_lam_re': 'new_v', 'new_v_ssm_lam_im': 'new_v', 'new_v_ssm_log_dt': 'new_v', 'new_v_ssm_b_re': 'new_v', 'new_v_ssm_b_im': 'new_v', 'new_v_ssm_c_re': 'new_v', 'new_v_ssm_c_im': 'new_v', 'new_v_ssm_d': 'new_v', 'new_v_ssm_w_glu': 'new_v', 'new_v_kv_g': 'new_v', 'new_v_kv_ada_w': 'new_v', 'new_v_kv_ada_b': 'new_v', 'new_v_w_kv': 'new_v', 'new_v_attn_w_q': 'new_v', 'new_v_attn_w_o': 'new_v', 'new_v_mlp_w1': 'new_v', 'new_v_mlp_w2': 'new_v', 'new_v_final_g': 'new_v'}


def _forward(args):
    return _fwd_reference(*[args[k] for k in FWD_PARAMS])


def _output_shape():
    out = _jax.eval_shape(lambda: _forward(_fwd_setup_inputs(0)))
    return out.shape, out.dtype

N_MICROBATCH = 1
ADAM_LR = 0.001
ADAM_B1 = 0.9
ADAM_B2 = 0.999
ADAM_EPS = 1e-08
ADAM_WD = 0.01
ADAM_STEP = 10
PER_EXAMPLE_BATCH_AXIS = {'x': 0, 'c': 0, 'loss_target': 0}
SHARED_INPUTS = []
_WEIGHT_DTYPES = {'ln_g': _jnp.float32, 'ada_w': _jnp.float32, 'ada_b': _jnp.float32, 'ssm_lam_re': _jnp.float32, 'ssm_lam_im': _jnp.float32, 'ssm_log_dt': _jnp.float32, 'ssm_b_re': _jnp.float32, 'ssm_b_im': _jnp.float32, 'ssm_c_re': _jnp.float32, 'ssm_c_im': _jnp.float32, 'ssm_d': _jnp.float32, 'ssm_w_glu': _jnp.float32, 'kv_g': _jnp.float32, 'kv_ada_w': _jnp.float32, 'kv_ada_b': _jnp.float32, 'w_kv': _jnp.float32, 'attn_w_q': _jnp.float32, 'attn_w_o': _jnp.float32, 'mlp_w1': _jnp.float32, 'mlp_w2': _jnp.float32, 'final_g': _jnp.float32}
MOMENT_SCALE = {'ln_g': 5.522536e-02, 'ada_w': 7.407667e-02, 'ada_b': 1.244574e-01, 'ssm_lam_re': 1.159820e-02, 'ssm_lam_im': 1.213110e-02, 'ssm_log_dt': 4.206723e+00, 'ssm_b_re': 6.429175e-03, 'ssm_b_im': 5.571230e-03, 'ssm_c_re': 2.267778e-03, 'ssm_c_im': 2.095246e-03, 'ssm_d': 2.463809e-02, 'ssm_w_glu': 1.848995e-02, 'kv_g': 2.370537e-02, 'kv_ada_w': 2.315849e-02, 'kv_ada_b': 3.619338e-02, 'w_kv': 1.147246e-02, 'attn_w_q': 6.020461e-03, 'attn_w_o': 1.702588e-02, 'mlp_w1': 3.959286e-02, 'mlp_w2': 7.435551e-02, 'final_g': 3.258302e+01}


def _to_microbatches(a, axis):
    t = _jnp.moveaxis(a, axis, 0)
    t = t.reshape((N_MICROBATCH, t.shape[0] // N_MICROBATCH) + t.shape[1:])
    return _jnp.moveaxis(t, 1, axis + 1)


def setup_inputs(seed: int = 0) -> dict:
    inp = _fwd_setup_inputs(seed)
    key = _jax.random.fold_in(_jax.random.key(seed), 7919)
    shape, _ = _output_shape()
    out = dict(inp)
    out["loss_target"] = _jax.random.normal(_jax.random.fold_in(key, 0), shape, _jnp.float32)
    for i, name in enumerate(TWIN_WEIGHTS):
        w = inp[name].astype(_jnp.float32)
        if MOMENT_SCALE is None:
            s = _jnp.sqrt(_jnp.mean(_jnp.square(w)) + 1e-30)
        else:
            s = MOMENT_SCALE[name]
        km, kv = _jax.random.split(_jax.random.fold_in(key, i + 1))
        out[name] = w
        out["m_" + name] = s * _jax.random.normal(km, w.shape, _jnp.float32)
        out["v_" + name] = (s * s) * _jax.random.uniform(kv, w.shape, _jnp.float32, 0.5, 1.5)
    if N_MICROBATCH > 1:
        for name, axis in PER_EXAMPLE_BATCH_AXIS.items():
            out[name] = _to_microbatches(out[name], axis)
    return {'x': out['x'], 'c': out['c'], 'ln_g': out['ln_g'], 'ada_w': out['ada_w'], 'ada_b': out['ada_b'], 'ssm_lam_re': out['ssm_lam_re'], 'ssm_lam_im': out['ssm_lam_im'], 'ssm_log_dt': out['ssm_log_dt'], 'ssm_b_re': out['ssm_b_re'], 'ssm_b_im': out['ssm_b_im'], 'ssm_c_re': out['ssm_c_re'], 'ssm_c_im': out['ssm_c_im'], 'ssm_d': out['ssm_d'], 'ssm_w_glu': out['ssm_w_glu'], 'kv_g': out['kv_g'], 'kv_ada_w': out['kv_ada_w'], 'kv_ada_b': out['kv_ada_b'], 'w_kv': out['w_kv'], 'attn_w_q': out['attn_w_q'], 'attn_w_o': out['attn_w_o'], 'mlp_w1': out['mlp_w1'], 'mlp_w2': out['mlp_w2'], 'final_g': out['final_g'], 'loss_target': out['loss_target'], 'm_ln_g': out['m_ln_g'], 'm_ada_w': out['m_ada_w'], 'm_ada_b': out['m_ada_b'], 'm_ssm_lam_re': out['m_ssm_lam_re'], 'm_ssm_lam_im': out['m_ssm_lam_im'], 'm_ssm_log_dt': out['m_ssm_log_dt'], 'm_ssm_b_re': out['m_ssm_b_re'], 'm_ssm_b_im': out['m_ssm_b_im'], 'm_ssm_c_re': out['m_ssm_c_re'], 'm_ssm_c_im': out['m_ssm_c_im'], 'm_ssm_d': out['m_ssm_d'], 'm_ssm_w_glu': out['m_ssm_w_glu'], 'm_kv_g': out['m_kv_g'], 'm_kv_ada_w': out['m_kv_ada_w'], 'm_kv_ada_b': out['m_kv_ada_b'], 'm_w_kv': out['m_w_kv'], 'm_attn_w_q': out['m_attn_w_q'], 'm_attn_w_o': out['m_attn_w_o'], 'm_mlp_w1': out['m_mlp_w1'], 'm_mlp_w2': out['m_mlp_w2'], 'm_final_g': out['m_final_g'], 'v_ln_g': out['v_ln_g'], 'v_ada_w': out['v_ada_w'], 'v_ada_b': out['v_ada_b'], 'v_ssm_lam_re': out['v_ssm_lam_re'], 'v_ssm_lam_im': out['v_ssm_lam_im'], 'v_ssm_log_dt': out['v_ssm_log_dt'], 'v_ssm_b_re': out['v_ssm_b_re'], 'v_ssm_b_im': out['v_ssm_b_im'], 'v_ssm_c_re': out['v_ssm_c_re'], 'v_ssm_c_im': out['v_ssm_c_im'], 'v_ssm_d': out['v_ssm_d'], 'v_ssm_w_glu': out['v_ssm_w_glu'], 'v_kv_g': out['v_kv_g'], 'v_kv_ada_w': out['v_kv_ada_w'], 'v_kv_ada_b': out['v_kv_ada_b'], 'v_w_kv': out['v_w_kv'], 'v_attn_w_q': out['v_attn_w_q'], 'v_attn_w_o': out['v_attn_w_o'], 'v_mlp_w1': out['v_mlp_w1'], 'v_mlp_w2': out['v_mlp_w2'], 'v_final_g': out['v_final_g']}


def _loss(weights, diff, rest, loss_target):
    with _jax.named_scope("forward"):
        args = {**rest, TWIN_DIFF_INPUT: diff, **{k: w.astype(_WEIGHT_DTYPES[k]) for k, w in weights.items()}}
        y = _forward(args)
    with _jax.named_scope("loss_head"):
        err = _jnp.square(y.astype(_jnp.float32) - loss_target)
        return 0.5 * _jnp.sum(_jnp.mean(err, axis=-1)) if err.ndim else 0.5 * err


def _adamw(w, g, m, v):
    m = ADAM_B1 * m + (1.0 - ADAM_B1) * g
    v = ADAM_B2 * v + (1.0 - ADAM_B2) * _jnp.square(g)
    m_hat = m / (1.0 - ADAM_B1 ** ADAM_STEP)
    v_hat = v / (1.0 - ADAM_B2 ** ADAM_STEP)
    delta = -ADAM_LR * (m_hat / (_jnp.sqrt(v_hat) + ADAM_EPS) + ADAM_WD * w)
    return delta, m, v


def reference(x, c, ln_g, ada_w, ada_b, ssm_lam_re, ssm_lam_im, ssm_log_dt, ssm_b_re, ssm_b_im, ssm_c_re, ssm_c_im, ssm_d, ssm_w_glu, kv_g, kv_ada_w, kv_ada_b, w_kv, attn_w_q, attn_w_o, mlp_w1, mlp_w2, final_g, loss_target, m_ln_g, m_ada_w, m_ada_b, m_ssm_lam_re, m_ssm_lam_im, m_ssm_log_dt, m_ssm_b_re, m_ssm_b_im, m_ssm_c_re, m_ssm_c_im, m_ssm_d, m_ssm_w_glu, m_kv_g, m_kv_ada_w, m_kv_ada_b, m_w_kv, m_attn_w_q, m_attn_w_o, m_mlp_w1, m_mlp_w2, m_final_g, v_ln_g, v_ada_w, v_ada_b, v_ssm_lam_re, v_ssm_lam_im, v_ssm_log_dt, v_ssm_b_re, v_ssm_b_im, v_ssm_c_re, v_ssm_c_im, v_ssm_d, v_ssm_w_glu, v_kv_g, v_kv_ada_w, v_kv_ada_b, v_w_kv, v_attn_w_q, v_attn_w_o, v_mlp_w1, v_mlp_w2, v_final_g):
    given = dict(x=x, c=c, ln_g=ln_g, ada_w=ada_w, ada_b=ada_b, ssm_lam_re=ssm_lam_re, ssm_lam_im=ssm_lam_im, ssm_log_dt=ssm_log_dt, ssm_b_re=ssm_b_re, ssm_b_im=ssm_b_im, ssm_c_re=ssm_c_re, ssm_c_im=ssm_c_im, ssm_d=ssm_d, ssm_w_glu=ssm_w_glu, kv_g=kv_g, kv_ada_w=kv_ada_w, kv_ada_b=kv_ada_b, w_kv=w_kv, attn_w_q=attn_w_q, attn_w_o=attn_w_o, mlp_w1=mlp_w1, mlp_w2=mlp_w2, final_g=final_g, loss_target=loss_target, m_ln_g=m_ln_g, m_ada_w=m_ada_w, m_ada_b=m_ada_b, m_ssm_lam_re=m_ssm_lam_re, m_ssm_lam_im=m_ssm_lam_im, m_ssm_log_dt=m_ssm_log_dt, m_ssm_b_re=m_ssm_b_re, m_ssm_b_im=m_ssm_b_im, m_ssm_c_re=m_ssm_c_re, m_ssm_c_im=m_ssm_c_im, m_ssm_d=m_ssm_d, m_ssm_w_glu=m_ssm_w_glu, m_kv_g=m_kv_g, m_kv_ada_w=m_kv_ada_w, m_kv_ada_b=m_kv_ada_b, m_w_kv=m_w_kv, m_attn_w_q=m_attn_w_q, m_attn_w_o=m_attn_w_o, m_mlp_w1=m_mlp_w1, m_mlp_w2=m_mlp_w2, m_final_g=m_final_g, v_ln_g=v_ln_g, v_ada_w=v_ada_w, v_ada_b=v_ada_b, v_ssm_lam_re=v_ssm_lam_re, v_ssm_lam_im=v_ssm_lam_im, v_ssm_log_dt=v_ssm_log_dt, v_ssm_b_re=v_ssm_b_re, v_ssm_b_im=v_ssm_b_im, v_ssm_c_re=v_ssm_c_re, v_ssm_c_im=v_ssm_c_im, v_ssm_d=v_ssm_d, v_ssm_w_glu=v_ssm_w_glu, v_kv_g=v_kv_g, v_kv_ada_w=v_kv_ada_w, v_kv_ada_b=v_kv_ada_b, v_w_kv=v_w_kv, v_attn_w_q=v_attn_w_q, v_attn_w_o=v_attn_w_o, v_mlp_w1=v_mlp_w1, v_mlp_w2=v_mlp_w2, v_final_g=v_final_g)
    weights = {n: given[n] for n in TWIN_WEIGHTS}
    shared = {n: given[n] for n in SHARED_INPUTS}
    per_example = {n: given[n] for n in ['x', 'c']}
    grad_fn = _jax.value_and_grad(_loss, argnums=(0, 1))

    def one_microbatch(ex, loss_target):
        ex = dict(ex)
        diff = ex.pop(TWIN_DIFF_INPUT)
        return grad_fn(weights, diff, {**shared, **ex}, loss_target)

    if N_MICROBATCH == 1:
        loss, (grad_w, grad_x) = one_microbatch(per_example, given["loss_target"])
    else:
        def body(carry, xs):
            loss_sum, grad_sum = carry
            l_k, (gw_k, gx_k) = one_microbatch(xs[0], xs[1])
            with _jax.named_scope("update"):
                return (loss_sum + l_k, _jax.tree.map(_jnp.add, grad_sum, gw_k)), gx_k

        init = (_jnp.zeros((), _jnp.float32), _jax.tree.map(_jnp.zeros_like, weights))
        (loss, grad_w), grad_x = _jax.lax.scan(body, init, (per_example, given["loss_target"]))
    with _jax.named_scope("update"):
        delta_w, new_m, new_v = {}, {}, {}
        for n in TWIN_WEIGHTS:
            delta_w[n], new_m[n], new_v[n] = _adamw(weights[n], grad_w[n], given["m_" + n], given["v_" + n])
    return (loss, grad_x, *[grad_w[n] for n in TWIN_WEIGHTS], *[delta_w[n] for n in TWIN_WEIGHTS],
            *[new_m[n] for n in TWIN_WEIGHTS], *[new_v[n] for n in TWIN_WEIGHTS])
```

```python
import functools
import math

import jax
import jax.numpy as jnp
from jax import lax
from jax.experimental import pallas as pl
from jax.experimental.pallas import tpu as pltpu

F32 = jnp.float32
BF16 = jnp.bfloat16
N_DEV = 8
AXES = ("x", "y", "c")
V7X_VMEM_LIMIT = 56 * 1024 * 1024
LANES = 128
EPS = 1e-6
NEG = -1e30
HEAD = 64
QBLK = 128
BRANCH_DIL = (1, 4, 16)
ADAM_LR, ADAM_B1, ADAM_B2, ADAM_EPS, ADAM_WD, ADAM_STEP = 0.001, 0.9, 0.999, 1e-08, 0.01, 10

NN = (((1,), (0,)), ((), ()))
NT = (((1,), (1,)), ((), ()))
TN = (((0,), (0,)), ((), ()))


def _params(sem):
    return pltpu.CompilerParams(dimension_semantics=sem, vmem_limit_bytes=V7X_VMEM_LIMIT)


def _mm(name, a, b, *, grid, a_spec, b_spec, out_spec, out_shape, dims, acc_axis=None, acc_shape=None):
    n_acc = grid[acc_axis] if acc_axis is not None else 1

    def body(a_ref, b_ref, o_ref, *scratch):
        r = lax.dot_general(a_ref[...].astype(BF16), b_ref[...].astype(BF16), dims,
                            preferred_element_type=F32)
        if acc_axis is None:
            o_ref[...] = r.astype(o_ref.dtype)
        else:
            acc = scratch[0]
            k = pl.program_id(acc_axis)

            @pl.when(k == 0)
            def _():
                acc[...] = r

            @pl.when(k > 0)
            def _():
                acc[...] += r

            @pl.when(k == n_acc - 1)
            def _():
                o_ref[...] = acc[...].astype(o_ref.dtype)

    sem = tuple("arbitrary" if i == acc_axis else "parallel" for i in range(len(grid)))
    scratch = [] if acc_axis is None else [pltpu.VMEM(acc_shape, F32)]
    return pl.pallas_call(body, name=name, grid=grid, in_specs=[a_spec, b_spec], out_specs=out_spec,
                          out_shape=out_shape, scratch_shapes=scratch, compiler_params=_params(sem))(a, b)


def _tile(n, t):
    if n <= t:
        return n
    for d in range(t - t % 8, 7, -8):
        if n % d == 0:
            return d
    raise ValueError((n, t))


@jax.custom_vjp
def linear_col(x, g):
    return _linear_col_fwd(x, g)[0]


def _linear_col_fwd(x, g):
    t, k = x.shape
    _, _, ns = g.shape
    tm = _tile(t, 1024)
    y = _mm("col_fwd", x, g, grid=(t // tm, N_DEV),
            a_spec=pl.BlockSpec((tm, k), lambda i, j: (i, 0)),
            b_spec=pl.BlockSpec((None, k, ns), lambda i, j: (j, 0, 0)),
            out_spec=pl.BlockSpec((tm, ns), lambda i, j: (i, j)),
            out_shape=jax.ShapeDtypeStruct((t, N_DEV * ns), F32), dims=NN)
    return y, (x, g)


def _linear_col_bwd(res, dy):
    x, g = res
    t, k = x.shape
    _, _, ns = g.shape
    tm = _tile(t, 1024)
    dx = _mm("col_dx", dy, g, grid=(t // tm, N_DEV),
             a_spec=pl.BlockSpec((tm, ns), lambda i, j: (i, j)),
             b_spec=pl.BlockSpec((None, k, ns), lambda i, j: (j, 0, 0)),
             out_spec=pl.BlockSpec((tm, k), lambda i, j: (i, 0)),
             out_shape=jax.ShapeDtypeStruct((t, k), F32), dims=NT, acc_axis=1, acc_shape=(tm, k))
    tt = _tile(t, 1024)
    dg = _mm("col_dw", x, dy, grid=(N_DEV, t // tt),
             a_spec=pl.BlockSpec((tt, k), lambda j, s: (s, 0)),
             b_spec=pl.BlockSpec((tt, ns), lambda j, s: (s, j)),
             out_spec=pl.BlockSpec((None, k, ns), lambda j, s: (j, 0, 0)),
             out_shape=jax.ShapeDtypeStruct((N_DEV, k, ns), BF16), dims=TN, acc_axis=1, acc_shape=(k, ns))
    return dx, dg


linear_col.defvjp(_linear_col_fwd, _linear_col_bwd)


@jax.custom_vjp
def linear(x, w):
    return _linear_fwd(x, w)[0]


def _linear_fwd(x, w):
    t, k = x.shape
    _, n = w.shape
    tm, tk = _tile(t, 1024), _tile(k, 1024)
    y = _mm("lin_fwd", x, w, grid=(t // tm, k // tk),
            a_spec=pl.BlockSpec((tm, tk), lambda i, s: (i, s)),
            b_spec=pl.BlockSpec((tk, n), lambda i, s: (s, 0)),
            out_spec=pl.BlockSpec((tm, n), lambda i, s: (i, 0)),
            out_shape=jax.ShapeDtypeStruct((t, n), F32), dims=NN, acc_axis=1, acc_shape=(tm, n))
    return y, (x, w)


def _linear_bwd(res, dy):
    x, w = res
    t, k = x.shape
    _, n = w.shape
    tm, tk = _tile(t, 1024), _tile(k, 1024)
    dx = _mm("lin_dx", dy, w, grid=(t // tm, k // tk),
             a_spec=pl.BlockSpec((tm, n), lambda i, s: (i, 0)),
             b_spec=pl.BlockSpec((tk, n), lambda i, s: (s, 0)),
             out_spec=pl.BlockSpec((tm, tk), lambda i, s: (i, s)),
             out_shape=jax.ShapeDtypeStruct((t, k), F32), dims=NT)
    tt = _tile(t, 1024)
    dw = _mm("lin_dw", x, dy, grid=(k // tk, t // tt),
             a_spec=pl.BlockSpec((tt, tk), lambda s, r: (r, s)),
             b_spec=pl.BlockSpec((tt, n), lambda s, r: (r, 0)),
             out_spec=pl.BlockSpec((tk, n), lambda s, r: (s, 0)),
             out_shape=jax.ShapeDtypeStruct((k, n), BF16), dims=TN, acc_axis=1, acc_shape=(tk, n))
    return dx, dw


linear.defvjp(_linear_fwd, _linear_bwd)


@jax.custom_vjp
def grouped_mm(x, w):
    return _grouped_fwd(x, w)[0]


def _grouped_fwd(x, w):
    t = x.shape[0]
    p, kin, kout = w.shape
    tm = _tile(t, 1024)
    y = _mm("grp_fwd", x, w, grid=(t // tm, p),
            a_spec=pl.BlockSpec((tm, kin), lambda i, q: (i, q)),
            b_spec=pl.BlockSpec((None, kin, kout), lambda i, q: (q, 0, 0)),
            out_spec=pl.BlockSpec((tm, kout), lambda i, q: (i, q)),
            out_shape=jax.ShapeDtypeStruct((t, p * kout), F32), dims=NN)
    return y, (x, w)


def _grouped_bwd(res, dy):
    x, w = res
    t = x.shape[0]
    p, kin, kout = w.shape
    tm = _tile(t, 1024)
    dx = _mm("grp_dx", dy, w, grid=(t // tm, p),
             a_spec=pl.BlockSpec((tm, kout), lambda i, q: (i, q)),
             b_spec=pl.BlockSpec((None, kin, kout), lambda i, q: (q, 0, 0)),
             out_spec=pl.BlockSpec((tm, kin), lambda i, q: (i, q)),
             out_shape=jax.ShapeDtypeStruct((t, p * kin), F32), dims=NT)
    dw = _mm("grp_dw", x, dy, grid=(p, t // tm),
             a_spec=pl.BlockSpec((tm, kin), lambda q, s: (s, q)),
             b_spec=pl.BlockSpec((tm, kout), lambda q, s: (s, q)),
             out_spec=pl.BlockSpec((None, kin, kout), lambda q, s: (q, 0, 0)),
             out_shape=jax.ShapeDtypeStruct((p, kin, kout), F32), dims=TN, acc_axis=1, acc_shape=(kin, kout))
    return dx, dw


grouped_mm.defvjp(_grouped_fwd, _grouped_bwd)


def _row_ts(widths, seq):
    per_row = 4 * sum(widths)
    ts = 512
    while ts > 8 and ts * per_row * 2 > 24 * 1024 * 1024:
        ts //= 2
    return min(ts, seq)


def make_rowop(name, f, n_row, n_batch, n_vec, f_bwd=None):
    n_in = n_row + n_batch + n_vec

    def in_specs(args, ts):
        specs = []
        for a in args[:n_row]:
            specs.append(pl.BlockSpec((None, ts, a.shape[2]), lambda b, s: (b, s, 0)))
        for a in args[n_row:n_row + n_batch]:
            specs.append(pl.BlockSpec((None, 1, a.shape[2]), lambda b, s: (b, 0, 0)))
        for a in args[n_row + n_batch:]:
            specs.append(pl.BlockSpec((1, a.shape[1]), lambda b, s: (0, 0)))
        return specs

    def out_struct(args, ts):
        blocks = [jax.ShapeDtypeStruct((ts, a.shape[2]), F32) for a in args[:n_row]]
        blocks += [jax.ShapeDtypeStruct((1, a.shape[2]), F32) for a in args[n_row:n_row + n_batch]]
        blocks += [jax.ShapeDtypeStruct((1, a.shape[1]), F32) for a in args[n_row + n_batch:]]
        return jax.eval_shape(f, *blocks)

    def run_fwd(args):
        bsz, seq = args[0].shape[:2]
        outs0 = out_struct(args, 8)
        widths = [a.shape[2] for a in args[:n_row]] + [o.shape[1] for o in outs0]
        ts = _row_ts(widths, seq)

        def body(*refs):
            outs = f(*[r[...] for r in refs[:n_in]])
            for o_ref, o in zip(refs[n_in:], outs):
                o_ref[...] = o

        return pl.pallas_call(
            body, name=name + "_fwd", grid=(bsz, seq // ts), in_specs=in_specs(args, ts),
            out_specs=[pl.BlockSpec((None, ts, o.shape[1]), lambda b, s: (b, s, 0)) for o in outs0],
            out_shape=[jax.ShapeDtypeStruct((bsz, seq, o.shape[1]), F32) for o in outs0],
            compiler_params=_params(("parallel", "parallel")))(*args)

    def run_bwd(args, cts):
        bsz, seq = args[0].shape[:2]
        widths = [a.shape[2] for a in args[:n_row]] * 2 + [c.shape[2] for c in cts]
        ts = _row_ts(widths, seq)
        n_ct = len(cts)

        def body(*refs):
            ins = [r[...] for r in refs[:n_in]]
            ct = [r[...] for r in refs[n_in:n_in + n_ct]]
            if f_bwd is None:
                _, vjp = jax.vjp(f, *ins)
                grads = vjp(tuple(ct))
            else:
                grads = f_bwd(ins, ct)
            g_refs = refs[n_in + n_ct:]
            b, s = pl.program_id(0), pl.program_id(1)
            for i in range(n_row):
                g_refs[i][...] = grads[i]
            for i in range(n_row, n_row + n_batch):
                @pl.when(s == 0)
                def _(i=i):
                    g_refs[i][...] = grads[i]

                @pl.when(s > 0)
                def _(i=i):
                    g_refs[i][...] += grads[i]
            for i in range(n_row + n_batch, n_in):
                first = jnp.logical_and(b == 0, s == 0)

                @pl.when(first)
                def _(i=i):
                    g_refs[i][...] = grads[i]

                @pl.when(jnp.logical_not(first))
                def _(i=i):
                    g_refs[i][...] += grads[i]

        ct_specs = [pl.BlockSpec((None, ts, c.shape[2]), lambda b, s: (b, s, 0)) for c in cts]
        return pl.pallas_call(
            body, name=name + "_bwd", grid=(bsz, seq // ts), in_specs=in_specs(args, ts) + ct_specs,
            out_specs=in_specs(args, ts), out_shape=[jax.ShapeDtypeStruct(a.shape, F32) for a in args],
            compiler_params=_params(("arbitrary", "arbitrary")))(*args, *cts)

    @jax.custom_vjp
    def op(*args):
        return tuple(run_fwd(args))

    def fwd(*args):
        return tuple(run_fwd(args)), args

    def bwd(args, cts):
        return tuple(run_bwd(args, list(cts)))

    op.defvjp(fwd, bwd)
    return op


def _modnorm_f(h, scale, shift, g):
    y = h * lax.rsqrt(jnp.mean(h * h, axis=-1, keepdims=True) + EPS) * g
    return (y * (1.0 + scale) + shift,)


def _gate_res_f(h, y, gate):
    return (h + gate * y,)


def _relu2_f(a):
    return (jnp.square(jnp.maximum(a, 0.0)),)


def _gelu_skip_f(y, u, d):
    return (jax.nn.gelu(y + d * u),)


def _glu_f(vg):
    n = vg.shape[1] // 2
    return (vg[:, :n] * jax.nn.sigmoid(vg[:, n:]),)


def _glu_b(ins, cts):
    (vg,), (ct,) = ins, cts
    n = vg.shape[1] // 2
    sg = jax.nn.sigmoid(vg[:, n:])
    return (jnp.concatenate([ct * sg, ct * vg[:, :n] * sg * (1.0 - sg)], axis=1),)


def _combine_f(o1, o2, o3, l1, l2, l3):
    m = jnp.maximum(jnp.maximum(l1, l2), l3)
    e1, e2, e3 = jnp.exp(l1 - m), jnp.exp(l2 - m), jnp.exp(l3 - m)
    return ((e1 * o1 + e2 * o2 + e3 * o3) / (e1 + e2 + e3),)


def _final_loss_f(h, target, g):
    y = h * lax.rsqrt(jnp.mean(h * h, axis=-1, keepdims=True) + EPS) * g
    return (0.5 * jnp.mean(jnp.square(y - target), axis=-1, keepdims=True),)


modnorm = make_rowop("modnorm", _modnorm_f, 1, 2, 1)
gate_res = make_rowop("gate_res", _gate_res_f, 2, 1, 0)
relu2 = make_rowop("relu2", _relu2_f, 1, 0, 0)
gelu_skip = make_rowop("gelu_skip", _gelu_skip_f, 2, 0, 1)
glu = make_rowop("glu", _glu_f, 1, 0, 0, _glu_b)
combine = make_rowop("combine", _combine_f, 6, 0, 0)
final_loss = make_rowop("final_loss", _final_loss_f, 2, 0, 1)


SCAN_TB = 64
HALF = 512


def _scan_fwd_call(bu, a_re, a_im):
    bsz, rows, width = bu.shape
    seq = rows // 8
    tb = _tile(seq, SCAN_TB)

    def body(bu_ref, ar_ref, ai_ref, o_ref, carry):
        @pl.when(pl.program_id(0) == 0)
        def _():
            carry[...] = jnp.zeros_like(carry)

        ar, ai = ar_ref[...], ai_ref[...]

        def step(t, st):
            r0 = pl.multiple_of(t * 8, 8)
            new = []
            for b in range(bsz):
                sre, sim = st[2 * b], st[2 * b + 1]
                nre = ar * sre - ai * sim + bu_ref[b, pl.ds(r0, 8), pl.ds(0, HALF)]
                nim = ar * sim + ai * sre + bu_ref[b, pl.ds(r0, 8), pl.ds(HALF, HALF)]
                o_ref[b, pl.ds(r0, 8), pl.ds(0, HALF)] = nre
                o_ref[b, pl.ds(r0, 8), pl.ds(HALF, HALF)] = nim
                new += [nre, nim]
            return tuple(new)

        fin = lax.fori_loop(0, tb, step, tuple(carry[k] for k in range(2 * bsz)), unroll=4)
        for k in range(2 * bsz):
            carry[k] = fin[k]

    blk = pl.BlockSpec((bsz, tb * 8, width), lambda i: (0, i, 0))
    vec = pl.BlockSpec((8, HALF), lambda i: (0, 0))
    return pl.pallas_call(body, name="s5_scan_fwd", grid=(seq // tb,), in_specs=[blk, vec, vec], out_specs=blk,
                          out_shape=jax.ShapeDtypeStruct(bu.shape, F32),
                          scratch_shapes=[pltpu.VMEM((2 * bsz, 8, HALF), F32)],
                          compiler_params=_params(("arbitrary",)))(bu, a_re, a_im)


def _scan_bwd_call(g, st, a_re, a_im):
    bsz, rows, width = g.shape
    seq = rows // 8
    tb = _tile(seq, SCAN_TB)
    nt = seq // tb

    def body(g_ref, st_ref, prev_ref, ar_ref, ai_ref, db_ref, dar_ref, dai_ref, carry):
        i = pl.program_id(0)

        @pl.when(i == 0)
        def _():
            carry[...] = jnp.zeros_like(carry)
            dar_ref[...] = jnp.zeros_like(dar_ref)
            dai_ref[...] = jnp.zeros_like(dai_ref)

        ar, ai = ar_ref[...], ai_ref[...]

        def lam_step(b, r0, lre, lim):
            nre = g_ref[b, pl.ds(r0, 8), pl.ds(0, HALF)] + ar * lre + ai * lim
            nim = g_ref[b, pl.ds(r0, 8), pl.ds(HALF, HALF)] - ai * lre + ar * lim
            db_ref[b, pl.ds(r0, 8), pl.ds(0, HALF)] = nre
            db_ref[b, pl.ds(r0, 8), pl.ds(HALF, HALF)] = nim
            return nre, nim

        def step(k, c):
            t = tb - 1 - k
            r0 = pl.multiple_of(t * 8, 8)
            p0 = pl.multiple_of(t * 8 - 8, 8)
            lam, dar, dai = list(c[:2 * bsz]), c[2 * bsz], c[2 * bsz + 1]
            for b in range(bsz):
                nre, nim = lam_step(b, r0, lam[2 * b], lam[2 * b + 1])
                pre = st_ref[b, pl.ds(p0, 8), pl.ds(0, HALF)]
                pim = st_ref[b, pl.ds(p0, 8), pl.ds(HALF, HALF)]
                dar = dar + nre * pre + nim * pim
                dai = dai + nim * pre - nre * pim
                lam[2 * b], lam[2 * b + 1] = nre, nim
            return tuple(lam) + (dar, dai)

        zero = jnp.zeros((8, HALF), F32)
        init = tuple(carry[k] for k in range(2 * bsz)) + (zero, zero)
        c = lax.fori_loop(0, tb - 1, step, init, unroll=4)
        lam, dar, dai = list(c[:2 * bsz]), c[2 * bsz], c[2 * bsz + 1]
        keep = jnp.where(i == nt - 1, 0.0, 1.0)
        for b in range(bsz):
            nre, nim = lam_step(b, 0, lam[2 * b], lam[2 * b + 1])
            pre = prev_ref[b, :, pl.ds(0, HALF)] * keep
            pim = prev_ref[b, :, pl.ds(HALF, HALF)] * keep
            dar = dar + nre * pre + nim * pim
            dai = dai + nim * pre - nre * pim
            carry[2 * b], carry[2 * b + 1] = nre, nim
        dar_ref[...] += dar
        dai_ref[...] += dai

    blk = pl.BlockSpec((bsz, tb * 8, width), lambda i: (0, nt - 1 - i, 0))
    prev = pl.BlockSpec((bsz, 8, width), lambda i: (0, jnp.maximum((nt - 1 - i) * tb - 1, 0), 0))
    vec = pl.BlockSpec((8, HALF), lambda i: (0, 0))
    return pl.pallas_call(
        body, name="s5_scan_bwd", grid=(nt,), in_specs=[blk, blk, prev, vec, vec], out_specs=[blk, vec, vec],
        out_shape=[jax.ShapeDtypeStruct(g.shape, F32), jax.ShapeDtypeStruct((8, HALF), F32),
                   jax.ShapeDtypeStruct((8, HALF), F32)],
        scratch_shapes=[pltpu.VMEM((2 * bsz, 8, HALF), F32)],
        compiler_params=_params(("arbitrary",)))(g, st, st, a_re, a_im)


@jax.custom_vjp
def s5_scan(bu, a_re, a_im):
    return _scan_fwd_call(bu, a_re, a_im)


def _s5_scan_fwd(bu, a_re, a_im):
    st = _scan_fwd_call(bu, a_re, a_im)
    return st, (st, a_re, a_im)


def _s5_scan_bwd(res, g):
    st, a_re, a_im = res
    return tuple(_scan_bwd_call(g, st, a_re, a_im))


s5_scan.defvjp(_s5_scan_fwd, _s5_scan_bwd)


def _branch_geometry(dil, seq):
    sub = seq // dil
    hw = {1: 128, 4: 512, 16: 1024}[dil]
    assert sub % QBLK == 0
    return sub, sub // QBLK, hw


def _masks():
    qi = lax.broadcasted_iota(jnp.int32, (QBLK, 2 * QBLK), 0)
    kj = lax.broadcasted_iota(jnp.int32, (QBLK, 2 * QBLK), 1) - QBLK
    dist = qi - kj
    band = jnp.logical_and(dist >= 0, dist <= QBLK)
    ci = lax.broadcasted_iota(jnp.int32, (QBLK, QBLK), 0)
    cj = lax.broadcasted_iota(jnp.int32, (QBLK, QBLK), 1)
    return band, ci >= cj


def _attn_specs(i, dil, sub, hw):
    nq, nkv, no = 3072 // hw, 6144 // hw, 1024 // hw
    q_spec = pl.BlockSpec((None, sub, hw), lambda b, r, h: (b, 0, r * nq + i * no + h))
    k_spec = pl.BlockSpec((None, sub, hw), lambda b, r, h: (b, 0, r * nkv + i * no + h))
    v_spec = pl.BlockSpec((None, sub, hw), lambda b, r, h: (b, 0, r * nkv + (3 + i) * no + h))
    o_spec = pl.BlockSpec((None, sub, hw), lambda b, r, h: (b, 0, r * no + h))
    return q_spec, k_spec, v_spec, o_spec


def _attn_fwd_call(q_all, kv_all, i, dil):
    bsz, seq, _ = q_all.shape
    sub, nb, hw = _branch_geometry(dil, seq)
    scale = HEAD ** -0.5

    def body(q_ref, k_ref, v_ref, o_ref, l_ref):
        band, causal = _masks()

        def one(h, q0, k0, kn, mask):
            ls = pl.ds(h * HEAD, HEAD)
            q = q_ref[pl.ds(q0, QBLK), ls].astype(BF16)
            k = k_ref[pl.ds(k0, kn), ls].astype(BF16)
            v = v_ref[pl.ds(k0, kn), ls].astype(BF16)
            s = lax.dot_general(q, k, NT, preferred_element_type=F32) * scale
            s = jnp.where(mask, s, NEG)
            m = jnp.max(s, axis=-1, keepdims=True)
            p = jnp.exp(s - m)
            den = jnp.sum(p, axis=-1, keepdims=True)
            o = lax.dot_general(p.astype(BF16), v, NN, preferred_element_type=F32) / den
            o_ref[pl.ds(q0, QBLK), ls] = o
            l_ref[pl.ds(q0, QBLK), ls] = jnp.broadcast_to(m + jnp.log(den), (QBLK, HEAD))

        for h in range(hw // HEAD):
            one(h, 0, 0, QBLK, causal)

            def blk(n, carry, h=h):
                q0 = pl.multiple_of(n * QBLK, QBLK)
                k0 = pl.multiple_of(n * QBLK - QBLK, QBLK)
                one(h, q0, k0, 2 * QBLK, band)
                return carry

            lax.fori_loop(1, nb, blk, 0)

    q_spec, k_spec, v_spec, o_spec = _attn_specs(i, dil, sub, hw)
    shape = jax.ShapeDtypeStruct((bsz, sub, dil * 1024), F32)
    o, l = pl.pallas_call(
        body, name="attn_fwd_d%d" % dil, grid=(bsz, dil, 1024 // hw), in_specs=[q_spec, k_spec, v_spec],
        out_specs=[o_spec, o_spec], out_shape=[shape, shape],
        compiler_params=_params(("parallel", "parallel", "parallel")))(
            q_all.reshape(bsz, sub, dil * 3072), kv_all.reshape(bsz, sub, dil * 6144), kv_all.reshape(bsz, sub, dil * 6144))
    return o.reshape(bsz, seq, 1024), l.reshape(bsz, seq, 1024)


def _attn_bwd_call(q_all, kv_all, o, l, do, dl, i, dil):
    bsz, seq, _ = q_all.shape
    sub, nb, hw = _branch_geometry(dil, seq)
    scale = HEAD ** -0.5

    def body(q_ref, k_ref, v_ref, o_ref, l_ref, do_ref, dl_ref, dq_ref, dk_ref, dv_ref):
        band, causal = _masks()
        dk_ref[...] = jnp.zeros_like(dk_ref)
        dv_ref[...] = jnp.zeros_like(dv_ref)

        def one(h, q0, k0, kn, mask):
            ls = pl.ds(h * HEAD, HEAD)
            qs, ks = pl.ds(q0, QBLK), pl.ds(k0, kn)
            q = q_ref[qs, ls].astype(BF16)
            k = k_ref[ks, ls].astype(BF16)
            v = v_ref[ks, ls].astype(BF16)
            s = lax.dot_general(q, k, NT, preferred_element_type=F32) * scale
            s = jnp.where(mask, s, NEG)
            p = jnp.exp(s - l_ref[qs, ls][:, :1])
            d_o = do_ref[qs, ls]
            row = jnp.sum(dl_ref[qs, ls] - d_o * o_ref[qs, ls], axis=-1, keepdims=True)
            d_ob = d_o.astype(BF16)
            dp = lax.dot_general(d_ob, v, NT, preferred_element_type=F32)
            ds = (p * (dp + row)).astype(BF16)
            dq_ref[qs, ls] = lax.dot_general(ds, k, NN, preferred_element_type=F32) * scale
            dk_ref[ks, ls] += lax.dot_general(ds, q, TN, preferred_element_type=F32) * scale
            dv_ref[ks, ls] += lax.dot_general(p.astype(BF16), d_ob, TN, preferred_element_type=F32)

        for h in range(hw // HEAD):
            one(h, 0, 0, QBLK, causal)

            def blk(n, carry, h=h):
                q0 = pl.multiple_of(n * QBLK, QBLK)
                k0 = pl.multiple_of(n * QBLK - QBLK, QBLK)
                one(h, q0, k0, 2 * QBLK, band)
                return carry

            lax.fori_loop(1, nb, blk, 0)

    q_spec, k_spec, v_spec, o_spec = _attn_specs(i, dil, sub, hw)
    shape = jax.ShapeDtypeStruct((bsz, sub, dil * 1024), F32)
    view = lambda a: a.reshape(bsz, sub, dil * a.shape[2])
    dq, dk, dv = pl.pallas_call(
        body, name="attn_bwd_d%d" % dil, grid=(bsz, dil, 1024 // hw),
        in_specs=[q_spec, k_spec, v_spec, o_spec, o_spec, o_spec, o_spec],
        out_specs=[o_spec, o_spec, o_spec], out_shape=[shape, shape, shape],
        compiler_params=_params(("parallel", "parallel", "parallel")))(
            view(q_all), view(kv_all), view(kv_all), view(o), view(l), view(do), view(dl))
    return dq.reshape(bsz, seq, 1024), dk.reshape(bsz, seq, 1024), dv.reshape(bsz, seq, 1024)


@jax.custom_vjp
def attn_branches(q_all, kv_all):
    return _attn_branches_fwd(q_all, kv_all)[0]


def _attn_branches_fwd(q_all, kv_all):
    outs = []
    for i, dil in enumerate(BRANCH_DIL):
        outs += list(_attn_fwd_call(q_all, kv_all, i, dil))
    return tuple(outs), (q_all, kv_all, tuple(outs))


def _attn_branches_bwd(res, cts):
    q_all, kv_all, outs = res
    dq, dk, dv = [], [], []
    for i, dil in enumerate(BRANCH_DIL):
        a, b, c = _attn_bwd_call(q_all, kv_all, outs[2 * i], outs[2 * i + 1], cts[2 * i], cts[2 * i + 1], i, dil)
        dq.append(a)
        dk.append(b)
        dv.append(c)
    return jnp.concatenate(dq, axis=-1), jnp.concatenate(dk + dv, axis=-1)


attn_branches.defvjp(_attn_branches_fwd, _attn_branches_bwd)


def ada_fwd(c_all, w, b):
    n, d, cs = w.shape
    nb = c_all.shape[0]

    def body(c_ref, w_ref, b_ref, o_ref):
        a = jax.nn.silu(c_ref[...]).astype(BF16)
        o_ref[...] = lax.dot_general(a, w_ref[...].astype(BF16), NN, preferred_element_type=F32) + b_ref[...]

    return pl.pallas_call(
        body, name="ada_fwd", grid=(n,),
        in_specs=[pl.BlockSpec((nb, d), lambda i: (0, 0)), pl.BlockSpec((None, d, cs), lambda i: (i, 0, 0)),
                  pl.BlockSpec((None, 1, cs), lambda i: (i, 0, 0))],
        out_specs=pl.BlockSpec((None, nb, cs), lambda i: (i, 0, 0)),
        out_shape=jax.ShapeDtypeStruct((n, nb, cs), F32), compiler_params=_params(("parallel",)))(c_all, w, b)


def ada_bwd(c_all, dm):
    n, nb, cs = dm.shape
    d = c_all.shape[1]

    def body(c_ref, dm_ref, dw_ref, db_ref):
        a = jax.nn.silu(c_ref[...]).astype(BF16)
        g = dm_ref[...]
        dw_ref[...] = lax.dot_general(a, g.astype(BF16), TN, preferred_element_type=F32)
        db_ref[...] = jnp.sum(g, axis=0, keepdims=True)

    return pl.pallas_call(
        body, name="ada_bwd", grid=(n,),
        in_specs=[pl.BlockSpec((nb, d), lambda i: (0, 0)), pl.BlockSpec((None, nb, cs), lambda i: (i, 0, 0))],
        out_specs=[pl.BlockSpec((None, d, cs), lambda i: (i, 0, 0)), pl.BlockSpec((None, 1, cs), lambda i: (i, 0, 0))],
        out_shape=[jax.ShapeDtypeStruct((n, d, cs), F32), jax.ShapeDtypeStruct((n, 1, cs), F32)],
        compiler_params=_params(("parallel",)))(c_all, dm)


def all_gather(x, name):
    m, n = x.shape

    def body(x_ref, out_ref, send_sems, recv_sems, local_sem):
        px, py, pc = lax.axis_index("x"), lax.axis_index("y"), lax.axis_index("c")
        me, sibling = (px, py, pc), (px, py, 1 - pc)
        chips = [(1 - px, py), (px, 1 - py), (1 - px, 1 - py)]

        def rows(qx, qy, qc):
            return out_ref.at[pl.ds((4 * qx + 2 * qy + qc) * m, m), :]

        def copy(k, block, to, src=None):
            return pltpu.make_async_remote_copy(
                src_ref=rows(*block) if src is None else src, dst_ref=rows(*block),
                send_sem=send_sems.at[k], recv_sem=recv_sems.at[k], device_id=to,
                device_id_type=pl.DeviceIdType.MESH)

        mine = pltpu.make_async_copy(x_ref, rows(*me), local_sem)
        mine.start()
        first = [copy(0, me, sibling, src=x_ref)]
        first += [copy(1 + j, me, (*chip, pc), src=x_ref) for j, chip in enumerate(chips)]
        for cp in first:
            cp.start()
        passed = [copy(4 + j, (*chip, pc), sibling) for j, chip in enumerate(chips)]
        for j, chip in enumerate(chips):
            copy(1 + j, (*chip, pc), me).wait_recv()
            passed[j].start()
        copy(0, sibling, me).wait_recv()
        for j, chip in enumerate(chips):
            copy(4 + j, (*chip, 1 - pc), me).wait_recv()
        for cp in first + passed:
            cp.wait_send()
        mine.wait()

    out = pl.pallas_call(
        body, name=name, out_shape=jax.ShapeDtypeStruct((N_DEV * m, n), x.dtype),
        in_specs=[pl.BlockSpec(memory_space=pl.ANY)], out_specs=pl.BlockSpec(memory_space=pl.ANY),
        scratch_shapes=[pltpu.SemaphoreType.DMA((7,)), pltpu.SemaphoreType.DMA((7,)), pltpu.SemaphoreType.DMA(())],
    )(x)
    return out.reshape(N_DEV, m, n)


def exchange_partials(p, name):
    _, m, n = p.shape

    def body(p_ref, out_ref, send_sems, recv_sems, local_sem):
        px, py, pc = lax.axis_index("x"), lax.axis_index("y"), lax.axis_index("c")
        me = 4 * px + 2 * py + pc
        mine = pltpu.make_async_copy(p_ref.at[me], out_ref.at[me], local_sem)
        mine.start()
        copies = []
        for k in range(1, N_DEV):
            qx = 1 - px if k & 4 else px
            qy = 1 - py if k & 2 else py
            qc = 1 - pc if k & 1 else pc
            cp = pltpu.make_async_remote_copy(
                src_ref=p_ref.at[4 * qx + 2 * qy + qc], dst_ref=out_ref.at[me],
                send_sem=send_sems.at[k - 1], recv_sem=recv_sems.at[k - 1], device_id=(qx, qy, qc),
                device_id_type=pl.DeviceIdType.MESH)
            cp.start()
            copies.append(cp)
        for cp in copies:
            cp.wait()
        mine.wait()

    return pl.pallas_call(
        body, name=name, out_shape=jax.ShapeDtypeStruct(p.shape, p.dtype),
        in_specs=[pl.BlockSpec(memory_space=pl.ANY)], out_specs=pl.BlockSpec(memory_space=pl.ANY),
        scratch_shapes=[pltpu.SemaphoreType.DMA((7,)), pltpu.SemaphoreType.DMA((7,)), pltpu.SemaphoreType.DMA(())],
    )(p)


def sum_slots(g):
    _, r, n = g.shape
    tr = _tile(r, 512)

    def body(g_ref, o_ref):
        acc = g_ref[0]
        for s in range(1, N_DEV):
            acc = acc + g_ref[s]
        o_ref[...] = acc

    return pl.pallas_call(
        body, name="sum_slots", grid=(r // tr,), in_specs=[pl.BlockSpec((N_DEV, tr, n), lambda i: (0, i, 0))],
        out_specs=pl.BlockSpec((tr, n), lambda i: (i, 0)), out_shape=jax.ShapeDtypeStruct((r, n), F32),
        compiler_params=_params(("parallel",)))(g)


def adamw(w, m, v, g, name):
    nl, r, n = w.shape
    partials = g.ndim == 4
    tr = _tile(r, max(8, (256 * 1024) // n))
    c1 = 1.0 / (1.0 - ADAM_B1 ** ADAM_STEP)
    c2 = 1.0 / (1.0 - ADAM_B2 ** ADAM_STEP)

    def body(w_ref, m_ref, v_ref, g_ref, go_ref, d_ref, mo_ref, vo_ref):
        if partials:
            grad = g_ref[0].astype(F32)
            for s in range(1, N_DEV):
                grad = grad + g_ref[s].astype(F32)
        else:
            grad = g_ref[...]
        m_new = ADAM_B1 * m_ref[...] + (1.0 - ADAM_B1) * grad
        v_new = ADAM_B2 * v_ref[...] + (1.0 - ADAM_B2) * grad * grad
        go_ref[...] = grad
        mo_ref[...] = m_new
        vo_ref[...] = v_new
        d_ref[...] = -ADAM_LR * ((m_new * c1) / (jnp.sqrt(v_new * c2) + ADAM_EPS) + ADAM_WD * w_ref[...])

    spec = pl.BlockSpec((None, tr, n), lambda l, i: (l, i, 0))
    g_spec = pl.BlockSpec((None, N_DEV, tr, n), lambda l, i: (l, 0, i, 0)) if partials else spec
    shape = jax.ShapeDtypeStruct(w.shape, F32)
    return pl.pallas_call(
        body, name=name, grid=(nl, r // tr), in_specs=[spec, spec, spec, g_spec], out_specs=[spec] * 4,
        out_shape=[shape] * 4, compiler_params=_params(("parallel", "parallel")))(w, m, v, g)


def pack(arrays):
    flat = jnp.concatenate([a.reshape(-1).astype(F32) for a in arrays])
    rows = -(-flat.shape[0] // LANES)
    mult = 8 if rows <= 512 else 512
    rows = -(-rows // mult) * mult
    flat = jnp.pad(flat, (0, rows * LANES - flat.shape[0]))
    return flat.reshape(rows, LANES)


def unpack(slab, shapes):
    flat = slab.reshape(slab.shape[:-2] + (-1,))
    out, off = [], 0
    for s in shapes:
        size = math.prod(s)
        out.append(flat[..., off:off + size].reshape(flat.shape[:-1] + tuple(s)))
        off += size
    return out


def _s5_discretize(lam_re, lam_im, log_dt, b_re, b_im, c_re, c_im):
    dt = jnp.exp(log_dt)[:, None]
    xr, th = lam_re * dt, lam_im * dt
    er = jnp.exp(xr)
    a_re, a_im = er * jnp.cos(th), er * jnp.sin(th)
    am1 = jnp.expm1(xr) * jnp.cos(th) - 2.0 * jnp.square(jnp.sin(0.5 * th))
    den = lam_re * lam_re + lam_im * lam_im
    fr = (am1 * lam_re + a_im * lam_im) / den
    fi = (a_im * lam_re - am1 * lam_im) / den
    bb_re = fr[..., None] * b_re - fi[..., None] * b_im
    bb_im = fr[..., None] * b_im + fi[..., None] * b_re
    eye = jnp.eye(8, dtype=F32)

    def pack_b(bb):
        return jnp.einsum('qgpc,gh->qgchp', bb.reshape(8, 8, 64, 16), eye).reshape(8, 128, 512)

    def pack_c(cc):
        return jnp.einsum('qgcp,gh->qgphc', cc.reshape(8, 8, 16, 64), eye).reshape(8, 512, 128)

    b_mat = jnp.concatenate([pack_b(bb_re), pack_b(bb_im)], axis=-1)
    c_mat = jnp.concatenate([pack_c(c_re), pack_c(-c_im)], axis=1)
    return a_re.reshape(8, HALF), a_im.reshape(8, HALF), b_mat, c_mat


def _local_loss(diff, x, target, weights):
    bsz, seq, d = x.shape
    t = bsz * seq
    rows3 = lambda a: a.reshape(bsz, seq, a.shape[-1])
    rows2 = lambda a: a.reshape(t, a.shape[-1])
    mods, kvmod, ln_g, ssm_d, kv_g, final_g = (diff[k] for k in ("mods", "kvmod", "ln_g", "ssm_d", "kv_g", "final_g"))
    h = x
    kv_all = None
    for layer in range(4):
        if layer == 2:
            shift, scale = kvmod[:, None, :d], kvmod[:, None, d:]
            (ukv,) = modnorm(h, scale, shift, kv_g[None])
            kv_all = rows3(linear_col(rows2(ukv), weights["w_kv"]))
        mod = mods[2 * layer]
        shift, scale, gate = mod[:, None, :d], mod[:, None, d:2 * d], mod[:, None, 2 * d:]
        (u,) = modnorm(h, scale, shift, ln_g[layer, 0][None])
        if layer < 2:
            a_re, a_im, b_mat, c_mat = _s5_discretize(*(diff[k][layer] for k in (
                "ssm_lam_re", "ssm_lam_im", "ssm_log_dt", "ssm_b_re", "ssm_b_im", "ssm_c_re", "ssm_c_im")))
            bu = grouped_mm(rows2(u), b_mat)
            st = s5_scan(bu.reshape(bsz, seq * 8, 2 * HALF), a_re, a_im)
            y0 = grouped_mm(st.reshape(t, 16 * HALF), c_mat)
            (z,) = gelu_skip(rows3(y0), u, ssm_d[layer][None])
            (y,) = glu(rows3(linear_col(rows2(z), weights["w_glu"][layer])))
        else:
            j = layer - 2
            q_all = rows3(linear_col(rows2(u), weights["w_q"][j]))
            o1, l1, o2, l2, o3, l3 = attn_branches(q_all, kv_all)
            (o,) = combine(o1, o2, o3, l1, l2, l3)
            y = rows3(linear(rows2(o), weights["w_o"][j]))
        (h,) = gate_res(h, y, gate)
        mod = mods[2 * layer + 1]
        shift, scale, gate = mod[:, None, :d], mod[:, None, d:2 * d], mod[:, None, 2 * d:]
        (u,) = modnorm(h, scale, shift, ln_g[layer, 1][None])
        (act,) = relu2(rows3(linear_col(rows2(u), weights["w1"][layer])))
        y = rows3(linear(rows2(act), weights["w2"][layer]))
        (h,) = gate_res(h, y, gate)
    (row_loss,) = final_loss(h, target, final_g[None])
    return jnp.sum(row_loss)


SSM_NAMES = ("ssm_lam_re", "ssm_lam_im", "ssm_log_dt", "ssm_b_re", "ssm_b_im", "ssm_c_re", "ssm_c_im")
ARG_NAMES = ("x", "c", "ln_g", "ada_w", "ada_b") + SSM_NAMES + (
    "ssm_d", "ssm_w_glu", "kv_g", "kv_ada_w", "kv_ada_b", "w_kv", "attn_w_q", "attn_w_o", "mlp_w1", "mlp_w2", "final_g")
WEIGHT_NAMES = ARG_NAMES[2:]


def kernel(x, c, ln_g, ada_w, ada_b, ssm_lam_re, ssm_lam_im, ssm_log_dt, ssm_b_re, ssm_b_im, ssm_c_re, ssm_c_im, ssm_d, ssm_w_glu, kv_g, kv_ada_w, kv_ada_b, w_kv, attn_w_q, attn_w_o, mlp_w1, mlp_w2, final_g, loss_target, m_ln_g, m_ada_w, m_ada_b, m_ssm_lam_re, m_ssm_lam_im, m_ssm_log_dt, m_ssm_b_re, m_ssm_b_im, m_ssm_c_re, m_ssm_c_im, m_ssm_d, m_ssm_w_glu, m_kv_g, m_kv_ada_w, m_kv_ada_b, m_w_kv, m_attn_w_q, m_attn_w_o, m_mlp_w1, m_mlp_w2, m_final_g, v_ln_g, v_ada_w, v_ada_b, v_ssm_lam_re, v_ssm_lam_im, v_ssm_log_dt, v_ssm_b_re, v_ssm_b_im, v_ssm_c_re, v_ssm_c_im, v_ssm_d, v_ssm_w_glu, v_kv_g, v_kv_ada_w, v_kv_ada_b, v_w_kv, v_attn_w_q, v_attn_w_o, v_mlp_w1, v_mlp_w2, v_final_g):
    args = locals()
    w = {n: args[n] for n in WEIGHT_NAMES}
    mom = {n: args["m_" + n] for n in WEIGHT_NAMES}
    var = {n: args["v_" + n] for n in WEIGHT_NAMES}
    bsz, seq, d = x.shape
    me = 4 * lax.axis_index("x") + 2 * lax.axis_index("y") + lax.axis_index("c")

    small = all_gather(pack([c, ln_g, ssm_d]), "gather_small")
    c_parts, ln_parts, d_parts = unpack(small, [c.shape, ln_g.shape, ssm_d.shape])
    c_all = c_parts.reshape(N_DEV * bsz, d)
    ln_full = jnp.moveaxis(ln_parts, 0, 2).reshape(4, 2, d)
    d_full = jnp.moveaxis(d_parts, 0, 1).reshape(2, d)

    cs = ada_w.shape[-1]
    mod_cols = ada_fwd(c_all, ada_w.reshape(8, d, cs), ada_b.reshape(8, 1, cs))
    kcs = kv_ada_w.shape[-1]
    kv_cols = ada_fwd(c_all, kv_ada_w[None], jnp.zeros((1, 1, kcs), F32))
    mod_g = all_gather(mod_cols.reshape(8 * N_DEV * bsz, cs), "gather_mod")
    kv_g_all = all_gather(kv_cols.reshape(N_DEV * bsz, kcs), "gather_kvmod")
    mods_all = jnp.moveaxis(mod_g.reshape(N_DEV, 8, N_DEV * bsz, cs), 0, 2).reshape(8, N_DEV * bsz, N_DEV * cs)
    kvmod_all = jnp.moveaxis(kv_g_all, 0, 1).reshape(N_DEV * bsz, N_DEV * kcs) + kv_ada_b[None]
    mods = lax.dynamic_slice_in_dim(mods_all, me * bsz, bsz, axis=1)
    kvmod = lax.dynamic_slice_in_dim(kvmod_all, me * bsz, bsz, axis=0)

    def gather_w(a, name):
        return all_gather(a.astype(BF16).reshape(-1, a.shape[-1]), name).reshape((N_DEV,) + a.shape)

    weights = {
        "w_glu": [gather_w(ssm_w_glu[l], "gather_w_glu") for l in range(2)],
        "w_kv": gather_w(w_kv, "gather_w_kv"),
        "w_q": [gather_w(attn_w_q[l], "gather_w_q") for l in range(2)],
        "w_o": [gather_w(attn_w_o[l], "gather_w_o").reshape(-1, d) for l in range(2)],
        "w1": [gather_w(mlp_w1[l], "gather_w1") for l in range(4)],
        "w2": [gather_w(mlp_w2[l], "gather_w2").reshape(-1, d) for l in range(4)],
    }

    diff = {"mods": mods, "kvmod": kvmod, "ln_g": ln_full, "ssm_d": d_full, "kv_g": kv_g, "final_g": final_g}
    diff.update({n: w[n] for n in SSM_NAMES})
    loss_local, (g_diff, grad_x, g_w) = jax.value_and_grad(_local_loss, argnums=(0, 1, 3))(diff, x, loss_target, weights)
    loss = lax.psum(loss_local, AXES)

    dmod = all_gather(pack([g_diff["mods"], g_diff["kvmod"]]), "gather_dmod")
    dmods_p, dkv_p = unpack(dmod, [g_diff["mods"].shape, g_diff["kvmod"].shape])
    dmods_all = jnp.moveaxis(dmods_p, 0, 1).reshape(8, N_DEV * bsz, 3 * d)
    dkv_all = dkv_p.reshape(N_DEV * bsz, 2 * d)
    g_ada_w, g_ada_b = ada_bwd(c_all, lax.dynamic_slice_in_dim(dmods_all, me * cs, cs, axis=2))
    g_kv_ada_w, _ = ada_bwd(c_all, lax.dynamic_slice_in_dim(dkv_all, me * kcs, kcs, axis=1)[None])
    _, g_kv_ada_b = ada_bwd(c_all, dkv_all[None])

    small_names = ("ln_g", "ssm_d", "kv_g", "final_g") + SSM_NAMES
    partial = all_gather(pack([g_diff[n] for n in small_names]), "gather_small_grads")
    totals = unpack(sum_slots(partial), [g_diff[n].shape for n in small_names])
    g_small = dict(zip(small_names, totals))
    g_small["ln_g"] = lax.dynamic_slice_in_dim(g_small["ln_g"], me * ln_g.shape[-1], ln_g.shape[-1], axis=2)
    g_small["ssm_d"] = lax.dynamic_slice_in_dim(g_small["ssm_d"], me * ssm_d.shape[-1], ssm_d.shape[-1], axis=1)
    g_small["ada_b"] = g_ada_b.reshape(ada_b.shape)
    g_small["kv_ada_b"] = g_kv_ada_b.reshape(kv_ada_b.shape)

    out = {}

    def put(name, res, shape):
        for kind, a in zip(("grad_", "delta_", "new_m_", "new_v_"), res):
            out[kind + name] = a.reshape(shape)

    def big(name, partials, label):
        recv = jnp.stack([exchange_partials(p, "exchange_" + label) for p in partials])
        rows, cols = recv.shape[2:]
        v3 = lambda a: a.reshape(len(partials), rows, cols)
        put(name, adamw(v3(w[name]), v3(mom[name]), v3(var[name]), recv, "adamw_" + label), w[name].shape)

    big("ssm_w_glu", g_w["w_glu"], "w_glu")
    big("w_kv", [g_w["w_kv"]], "w_kv")
    big("attn_w_q", g_w["w_q"], "w_q")
    big("attn_w_o", [g.reshape(N_DEV, -1, d) for g in g_w["w_o"]], "w_o")
    big("mlp_w1", g_w["w1"], "w1")
    big("mlp_w2", [g.reshape(N_DEV, -1, d) for g in g_w["w2"]], "w2")
    v3 = lambda a: a.reshape(8, d, cs)
    put("ada_w", adamw(v3(ada_w), v3(m_ada_w), v3(v_ada_w), g_ada_w, "adamw_ada_w"), ada_w.shape)
    put("kv_ada_w", adamw(kv_ada_w[None], m_kv_ada_w[None], v_kv_ada_w[None], g_kv_ada_w, "adamw_kv_ada_w"), kv_ada_w.shape)
    names = small_names + ("ada_b", "kv_ada_b")
    res = adamw(pack([w[n] for n in names])[None], pack([mom[n] for n in names])[None],
                pack([var[n] for n in names])[None], pack([g_small[n] for n in names])[None], "adamw_small")
    for kind, slab in zip(("grad_", "delta_", "new_m_", "new_v_"), res):
        for n, a in zip(names, unpack(slab[0], [w[n].shape for n in names])):
            out[kind + n] = a

    result = [loss, grad_x]
    for kind in ("grad_", "delta_", "new_m_", "new_v_"):
        result += [out[kind + n] for n in WEIGHT_NAMES]
    return tuple(result)
```

```python
import functools
import math

import jax
import jax.numpy as jnp
from jax import lax
from jax.experimental import pallas as pl
from jax.experimental.pallas import tpu as pltpu

F32 = jnp.float32
BF16 = jnp.bfloat16
N_DEV = 8
AXES = ("x", "y", "c")
V7X_VMEM_LIMIT = 56 * 1024 * 1024
LANES = 128
EPS = 1e-6
NEG = -1e30
HEAD = 64
QBLK = 128
BRANCH_DIL = (1, 4, 16)
ADAM_LR, ADAM_B1, ADAM_B2, ADAM_EPS, ADAM_WD, ADAM_STEP = 0.001, 0.9, 0.999, 1e-08, 0.01, 10

NN = (((1,), (0,)), ((), ()))
NT = (((1,), (1,)), ((), ()))
TN = (((0,), (0,)), ((), ()))


def _params(sem):
    return pltpu.CompilerParams(dimension_semantics=sem, vmem_limit_bytes=V7X_VMEM_LIMIT)


def _mm(name, a, b, *, grid, a_spec, b_spec, out_spec, out_shape, dims, acc_axis=None, acc_shape=None):
    n_acc = grid[acc_axis] if acc_axis is not None else 1

    def body(a_ref, b_ref, o_ref, *scratch):
        r = lax.dot_general(a_ref[...].astype(BF16), b_ref[...].astype(BF16), dims,
                            preferred_element_type=F32)
        if acc_axis is None:
            o_ref[...] = r.astype(o_ref.dtype)
        else:
            acc = scratch[0]
            k = pl.program_id(acc_axis)

            @pl.when(k == 0)
            def _():
                acc[...] = r

            @pl.when(k > 0)
            def _():
                acc[...] += r

            @pl.when(k == n_acc - 1)
            def _():
                o_ref[...] = acc[...].astype(o_ref.dtype)

    sem = tuple("arbitrary" if i == acc_axis else "parallel" for i in range(len(grid)))
    scratch = [] if acc_axis is None else [pltpu.VMEM(acc_shape, F32)]
    return pl.pallas_call(body, name=name, grid=grid, in_specs=[a_spec, b_spec], out_specs=out_spec,
                          out_shape=out_shape, scratch_shapes=scratch, compiler_params=_params(sem))(a, b)


def _tile(n, t):
    if n <= t:
        return n
    for d in range(t - t % 8, 7, -8):
        if n % d == 0:
            return d
    raise ValueError((n, t))


@jax.custom_vjp
def linear_col(x, g):
    return _linear_col_fwd(x, g)[0]


def _linear_col_fwd(x, g):
    t, k = x.shape
    _, _, ns = g.shape
    tm = _tile(t, 1024)
    y = _mm("col_fwd", x, g, grid=(t // tm, N_DEV),
            a_spec=pl.BlockSpec((tm, k), lambda i, j: (i, 0)),
            b_spec=pl.BlockSpec((None, k, ns), lambda i, j: (j, 0, 0)),
            out_spec=pl.BlockSpec((tm, ns), lambda i, j: (i, j)),
            out_shape=jax.ShapeDtypeStruct((t, N_DEV * ns), F32), dims=NN)
    return y, (x, g)


def _linear_col_bwd(res, dy):
    x, g = res
    t, k = x.shape
    _, _, ns = g.shape
    tm = _tile(t, 1024)
    dx = _mm("col_dx", dy, g, grid=(t // tm, N_DEV),
             a_spec=pl.BlockSpec((tm, ns), lambda i, j: (i, j)),
             b_spec=pl.BlockSpec((None, k, ns), lambda i, j: (j, 0, 0)),
             out_spec=pl.BlockSpec((tm, k), lambda i, j: (i, 0)),
             out_shape=jax.ShapeDtypeStruct((t, k), F32), dims=NT, acc_axis=1, acc_shape=(tm, k))
    tt = _tile(t, 1024)
    dg = _mm("col_dw", x, dy, grid=(N_DEV, t // tt),
             a_spec=pl.BlockSpec((tt, k), lambda j, s: (s, 0)),
             b_spec=pl.BlockSpec((tt, ns), lambda j, s: (s, j)),
             out_spec=pl.BlockSpec((None, k, ns), lambda j, s: (j, 0, 0)),
             out_shape=jax.ShapeDtypeStruct((N_DEV, k, ns), BF16), dims=TN, acc_axis=1, acc_shape=(k, ns))
    return dx, dg


linear_col.defvjp(_linear_col_fwd, _linear_col_bwd)


@jax.custom_vjp
def linear(x, w):
    return _linear_fwd(x, w)[0]


def _linear_fwd(x, w):
    t, k = x.shape
    _, n = w.shape
    tm, tk = _tile(t, 1024), _tile(k, 1024)
    y = _mm("lin_fwd", x, w, grid=(t // tm, k // tk),
            a_spec=pl.BlockSpec((tm, tk), lambda i, s: (i, s)),
            b_spec=pl.BlockSpec((tk, n), lambda i, s: (s, 0)),
            out_spec=pl.BlockSpec((tm, n), lambda i, s: (i, 0)),
            out_shape=jax.ShapeDtypeStruct((t, n), F32), dims=NN, acc_axis=1, acc_shape=(tm, n))
    return y, (x, w)


def _linear_bwd(res, dy):
    x, w = res
    t, k = x.shape
    _, n = w.shape
    tm, tk = _tile(t, 1024), _tile(k, 1024)
    dx = _mm("lin_dx", dy, w, grid=(t // tm, k // tk),
             a_spec=pl.BlockSpec((tm, n), lambda i, s: (i, 0)),
             b_spec=pl.BlockSpec((tk, n), lambda i, s: (s, 0)),
             out_spec=pl.BlockSpec((tm, tk), lambda i, s: (i, s)),
             out_shape=jax.ShapeDtypeStruct((t, k), F32), dims=NT)
    tt = _tile(t, 1024)
    dw = _mm("lin_dw", x, dy, grid=(k // tk, t // tt),
             a_spec=pl.BlockSpec((tt, tk), lambda s, r: (r, s)),
             b_spec=pl.BlockSpec((tt, n), lambda s, r: (r, 0)),
             out_spec=pl.BlockSpec((tk, n), lambda s, r: (s, 0)),
             out_shape=jax.ShapeDtypeStruct((k, n), BF16), dims=TN, acc_axis=1, acc_shape=(tk, n))
    return dx, dw


linear.defvjp(_linear_fwd, _linear_bwd)


@jax.custom_vjp
def grouped_mm(x, w):
    return _grouped_fwd(x, w)[0]


def _grouped_fwd(x, w):
    t = x.shape[0]
    p, kin, kout = w.shape
    tm = _tile(t, 1024)
    y = _mm("grp_fwd", x, w, grid=(t // tm, p),
            a_spec=pl.BlockSpec((tm, kin), lambda i, q: (i, q)),
            b_spec=pl.BlockSpec((None, kin, kout), lambda i, q: (q, 0, 0)),
            out_spec=pl.BlockSpec((tm, kout), lambda i, q: (i, q)),
            out_shape=jax.ShapeDtypeStruct((t, p * kout), F32), dims=NN)
    return y, (x, w)


def _grouped_bwd(res, dy):
    x, w = res
    t = x.shape[0]
    p, kin, kout = w.shape
    tm = _tile(t, 1024)
    dx = _mm("grp_dx", dy, w, grid=(t // tm, p),
             a_spec=pl.BlockSpec((tm, kout), lambda i, q: (i, q)),
             b_spec=pl.BlockSpec((None, kin, kout), lambda i, q: (q, 0, 0)),
             out_spec=pl.BlockSpec((tm, kin), lambda i, q: (i, q)),
             out_shape=jax.ShapeDtypeStruct((t, p * kin), F32), dims=NT)
    dw = _mm("grp_dw", x, dy, grid=(p, t // tm),
             a_spec=pl.BlockSpec((tm, kin), lambda q, s: (s, q)),
             b_spec=pl.BlockSpec((tm, kout), lambda q, s: (s, q)),
             out_spec=pl.BlockSpec((None, kin, kout), lambda q, s: (q, 0, 0)),
             out_shape=jax.ShapeDtypeStruct((p, kin, kout), F32), dims=TN, acc_axis=1, acc_shape=(kin, kout))
    return dx, dw


grouped_mm.defvjp(_grouped_fwd, _grouped_bwd)


def _row_ts(widths, seq):
    per_row = 4 * sum(widths)
    ts = 512
    while ts > 8 and ts * per_row * 2 > 24 * 1024 * 1024:
        ts //= 2
    return min(ts, seq)


def make_rowop(name, f, n_row, n_batch, n_vec, f_bwd=None):
    n_in = n_row + n_batch + n_vec

    def in_specs(args, ts):
        specs = []
        for a in args[:n_row]:
            specs.append(pl.BlockSpec((None, ts, a.shape[2]), lambda b, s: (b, s, 0)))
        for a in args[n_row:n_row + n_batch]:
            specs.append(pl.BlockSpec((None, 1, a.shape[2]), lambda b, s: (b, 0, 0)))
        for a in args[n_row + n_batch:]:
            specs.append(pl.BlockSpec((1, a.shape[1]), lambda b, s: (0, 0)))
        return specs

    def out_struct(args, ts):
        blocks = [jax.ShapeDtypeStruct((ts, a.shape[2]), F32) for a in args[:n_row]]
        blocks += [jax.ShapeDtypeStruct((1, a.shape[2]), F32) for a in args[n_row:n_row + n_batch]]
        blocks += [jax.ShapeDtypeStruct((1, a.shape[1]), F32) for a in args[n_row + n_batch:]]
        return jax.eval_shape(f, *blocks)

    def run_fwd(args):
        bsz, seq = args[0].shape[:2]
        outs0 = out_struct(args, 8)
        widths = [a.shape[2] for a in args[:n_row]] + [o.shape[1] for o in outs0]
        ts = _row_ts(widths, seq)

        def body(*refs):
            outs = f(*[r[...] for r in refs[:n_in]])
            for o_ref, o in zip(refs[n_in:], outs):
                o_ref[...] = o

        return pl.pallas_call(
            body, name=name + "_fwd", grid=(bsz, seq // ts), in_specs=in_specs(args, ts),
            out_specs=[pl.BlockSpec((None, ts, o.shape[1]), lambda b, s: (b, s, 0)) for o in outs0],
            out_shape=[jax.ShapeDtypeStruct((bsz, seq, o.shape[1]), F32) for o in outs0],
            compiler_params=_params(("parallel", "parallel")))(*args)

    def run_bwd(args, cts):
        bsz, seq = args[0].shape[:2]
        widths = [a.shape[2] for a in args[:n_row]] * 2 + [c.shape[2] for c in cts]
        ts = _row_ts(widths, seq)
        n_ct = len(cts)

        def body(*refs):
            ins = [r[...] for r in refs[:n_in]]
            ct = [r[...] for r in refs[n_in:n_in + n_ct]]
            if f_bwd is None:
                _, vjp = jax.vjp(f, *ins)
                grads = vjp(tuple(ct))
            else:
                grads = f_bwd(ins, ct)
            g_refs = refs[n_in + n_ct:]
            b, s = pl.program_id(0), pl.program_id(1)
            for i in range(n_row):
                g_refs[i][...] = grads[i]
            for i in range(n_row, n_row + n_batch):
                @pl.when(s == 0)
                def _(i=i):
                    g_refs[i][...] = grads[i]

                @pl.when(s > 0)
                def _(i=i):
                    g_refs[i][...] += grads[i]
            for i in range(n_row + n_batch, n_in):
                first = jnp.logical_and(b == 0, s == 0)

                @pl.when(first)
                def _(i=i):
                    g_refs[i][...] = grads[i]

                @pl.when(jnp.logical_not(first))
                def _(i=i):
                    g_refs[i][...] += grads[i]

        ct_specs = [pl.BlockSpec((None, ts, c.shape[2]), lambda b, s: (b, s, 0)) for c in cts]
        return pl.pallas_call(
            body, name=name + "_bwd", grid=(bsz, seq // ts), in_specs=in_specs(args, ts) + ct_specs,
            out_specs=in_specs(args, ts), out_shape=[jax.ShapeDtypeStruct(a.shape, F32) for a in args],
            compiler_params=_params(("arbitrary", "arbitrary")))(*args, *cts)

    @jax.custom_vjp
    def op(*args):
        return tuple(run_fwd(args))

    def fwd(*args):
        return tuple(run_fwd(args)), args

    def bwd(args, cts):
        return tuple(run_bwd(args, list(cts)))

    op.defvjp(fwd, bwd)
    return op


def _modnorm_f(h, scale, shift, g):
    y = h * lax.rsqrt(jnp.mean(h * h, axis=-1, keepdims=True) + EPS) * g
    return (y * (1.0 + scale) + shift,)


def _gate_res_f(h, y, gate):
    return (h + gate * y,)


def _relu2_f(a):
    return (jnp.square(jnp.maximum(a, 0.0)),)


def _gelu_skip_f(y, u, d):
    return (jax.nn.gelu(y + d * u),)


def _glu_f(vg):
    n = vg.shape[1] // 2
    return (vg[:, :n] * jax.nn.sigmoid(vg[:, n:]),)


def _glu_b(ins, cts):
    (vg,), (ct,) = ins, cts
    n = vg.shape[1] // 2
    sg = jax.nn.sigmoid(vg[:, n:])
    return (jnp.concatenate([ct * sg, ct * vg[:, :n] * sg * (1.0 - sg)], axis=1),)


def _combine_f(o1, o2, o3, l1, l2, l3):
    m = jnp.maximum(jnp.maximum(l1, l2), l3)
    e1, e2, e3 = jnp.exp(l1 - m), jnp.exp(l2 - m), jnp.exp(l3 - m)
    return ((e1 * o1 + e2 * o2 + e3 * o3) / (e1 + e2 + e3),)


def _final_loss_f(h, target, g):
    y = h * lax.rsqrt(jnp.mean(h * h, axis=-1, keepdims=True) + EPS) * g
    return (0.5 * jnp.mean(jnp.square(y - target), axis=-1, keepdims=True),)


modnorm = make_rowop("modnorm", _modnorm_f, 1, 2, 1)
gate_res = make_rowop("gate_res", _gate_res_f, 2, 1, 0)
relu2 = make_rowop("relu2", _relu2_f, 1, 0, 0)
gelu_skip = make_rowop("gelu_skip", _gelu_skip_f, 2, 0, 1)
glu = make_rowop("glu", _glu_f, 1, 0, 0, _glu_b)
combine = make_rowop("combine", _combine_f, 6, 0, 0)
final_loss = make_rowop("final_loss", _final_loss_f, 2, 0, 1)


SCAN_TB = 64
HALF = 512


def _scan_fwd_call(bu, a_re, a_im):
    bsz, rows, width = bu.shape
    seq = rows // 8
    tb = _tile(seq, SCAN_TB)

    def body(bu_ref, ar_ref, ai_ref, o_ref, carry):
        @pl.when(pl.program_id(0) == 0)
        def _():
            carry[...] = jnp.zeros_like(carry)

        ar, ai = ar_ref[...], ai_ref[...]

        def step(t, st):
            r0 = pl.multiple_of(t * 8, 8)
            new = []
            for b in range(bsz):
                sre, sim = st[2 * b], st[2 * b + 1]
                nre = ar * sre - ai * sim + bu_ref[b, pl.ds(r0, 8), pl.ds(0, HALF)]
                nim = ar * sim + ai * sre + bu_ref[b, pl.ds(r0, 8), pl.ds(HALF, HALF)]
                o_ref[b, pl.ds(r0, 8), pl.ds(0, HALF)] = nre
                o_ref[b, pl.ds(r0, 8), pl.ds(HALF, HALF)] = nim
                new += [nre, nim]
            return tuple(new)

        fin = lax.fori_loop(0, tb, step, tuple(carry[k] for k in range(2 * bsz)), unroll=4)
        for k in range(2 * bsz):
            carry[k] = fin[k]

    blk = pl.BlockSpec((bsz, tb * 8, width), lambda i: (0, i, 0))
    vec = pl.BlockSpec((8, HALF), lambda i: (0, 0))
    return pl.pallas_call(body, name="s5_scan_fwd", grid=(seq // tb,), in_specs=[blk, vec, vec], out_specs=blk,
                          out_shape=jax.ShapeDtypeStruct(bu.shape, F32),
                          scratch_shapes=[pltpu.VMEM((2 * bsz, 8, HALF), F32)],
                          compiler_params=_params(("arbitrary",)))(bu, a_re, a_im)


def _scan_bwd_call(g, st, a_re, a_im):
    bsz, rows, width = g.shape
    seq = rows // 8
    tb = _tile(seq, SCAN_TB)
    nt = seq // tb

    def body(g_ref, st_ref, prev_ref, ar_ref, ai_ref, db_ref, dar_ref, dai_ref, carry):
        i = pl.program_id(0)

        @pl.when(i == 0)
        def _():
            carry[...] = jnp.zeros_like(carry)
            dar_ref[...] = jnp.zeros_like(dar_ref)
            dai_ref[...] = jnp.zeros_like(dai_ref)

        ar, ai = ar_ref[...], ai_ref[...]

        def lam_step(b, r0, lre, lim):
            nre = g_ref[b, pl.ds(r0, 8), pl.ds(0, HALF)] + ar * lre + ai * lim
            nim = g_ref[b, pl.ds(r0, 8), pl.ds(HALF, HALF)] - ai * lre + ar * lim
            db_ref[b, pl.ds(r0, 8), pl.ds(0, HALF)] = nre
            db_ref[b, pl.ds(r0, 8), pl.ds(HALF, HALF)] = nim
            return nre, nim

        def step(k, c):
            t = tb - 1 - k
            r0 = pl.multiple_of(t * 8, 8)
            p0 = pl.multiple_of(t * 8 - 8, 8)
            lam, dar, dai = list(c[:2 * bsz]), c[2 * bsz], c[2 * bsz + 1]
            for b in range(bsz):
                nre, nim = lam_step(b, r0, lam[2 * b], lam[2 * b + 1])
                pre = st_ref[b, pl.ds(p0, 8), pl.ds(0, HALF)]
                pim = st_ref[b, pl.ds(p0, 8), pl.ds(HALF, HALF)]
                dar = dar + nre * pre + nim * pim
                dai = dai + nim * pre - nre * pim
                lam[2 * b], lam[2 * b + 1] = nre, nim
            return tuple(lam) + (dar, dai)

        zero = jnp.zeros((8, HALF), F32)
        init = tuple(carry[k] for k in range(2 * bsz)) + (zero, zero)
        c = lax.fori_loop(0, tb - 1, step, init, unroll=4)
        lam, dar, dai = list(c[:2 * bsz]), c[2 * bsz], c[2 * bsz + 1]
        keep = jnp.where(i == nt - 1, 0.0, 1.0)
        for b in range(bsz):
            nre, nim = lam_step(b, 0, lam[2 * b], lam[2 * b + 1])
            pre = prev_ref[b, :, pl.ds(0, HALF)] * keep
            pim = prev_ref[b, :, pl.ds(HALF, HALF)] * keep
            dar = dar + nre * pre + nim * pim
            dai = dai + nim * pre - nre * pim
            carry[2 * b], carry[2 * b + 1] = nre, nim
        dar_ref[...] += dar
        dai_ref[...] += dai

    blk = pl.BlockSpec((bsz, tb * 8, width), lambda i: (0, nt - 1 - i, 0))
    prev = pl.BlockSpec((bsz, 8, width), lambda i: (0, jnp.maximum((nt - 1 - i) * tb - 1, 0), 0))
    vec = pl.BlockSpec((8, HALF), lambda i: (0, 0))
    return pl.pallas_call(
        body, name="s5_scan_bwd", grid=(nt,), in_specs=[blk, blk, prev, vec, vec], out_specs=[blk, vec, vec],
        out_shape=[jax.ShapeDtypeStruct(g.shape, F32), jax.ShapeDtypeStruct((8, HALF), F32),
                   jax.ShapeDtypeStruct((8, HALF), F32)],
        scratch_shapes=[pltpu.VMEM((2 * bsz, 8, HALF), F32)],
        compiler_params=_params(("arbitrary",)))(g, st, st, a_re, a_im)


@jax.custom_vjp
def s5_scan(bu, a_re, a_im):
    return _scan_fwd_call(bu, a_re, a_im)


def _s5_scan_fwd(bu, a_re, a_im):
    st = _scan_fwd_call(bu, a_re, a_im)
    return st, (st, a_re, a_im)


def _s5_scan_bwd(res, g):
    st, a_re, a_im = res
    return tuple(_scan_bwd_call(g, st, a_re, a_im))


s5_scan.defvjp(_s5_scan_fwd, _s5_scan_bwd)


SSM_TB = 64
PITCH = SSM_TB + 8
PIECES = 8
CH = 128
NCH = 2 * HALF // LANES
NRE = NCH // 2


def _slab(t):
    return pl.ds(t, PIECES, stride=PITCH)


def _put_rows(ref, b, q, val, tb):
    for j in range(NCH):
        ref[b, j, pl.ds(q * PITCH, tb), :] = val[:, j * LANES:(j + 1) * LANES]


def _get_rows(ref, q, bsz, tb):
    return jnp.concatenate(
        [jnp.concatenate([ref[b, j, pl.ds(q * PITCH, tb), :] for j in range(NCH)], axis=1) for b in range(bsz)], axis=0)


def _chunks(a):
    return [a[:, j * LANES:(j + 1) * LANES] for j in range(a.shape[1] // LANES)]


def _ssm_fwd_call(u, b_mat, c_mat, a_re, a_im):
    bsz, seq, d = u.shape
    tb = SSM_TB
    nt = seq // tb
    assert seq % tb == 0 and d == PIECES * CH

    def body(u_ref, b_ref, c_ref, ar_ref, ai_ref, y_ref, st_ref, end_ref, bu, carry):
        @pl.when(pl.program_id(0) == 0)
        def _():
            carry[...] = jnp.zeros_like(carry)

        u2 = u_ref[...].reshape(bsz * tb, d).astype(BF16)
        for q in range(PIECES):
            r = lax.dot_general(u2[:, q * CH:(q + 1) * CH], b_ref[q], NN, preferred_element_type=F32)
            for b in range(bsz):
                _put_rows(bu, b, q, r[b * tb:(b + 1) * tb], tb)
        ar, ai = _chunks(ar_ref[...]), _chunks(ai_ref[...])
        st = [[carry[b, j] for j in range(NCH)] for b in range(bsz)]
        for t in range(tb):
            for b in range(bsz):
                for j in range(NRE):
                    sre, sim = st[b][j], st[b][NRE + j]
                    nre = ar[j] * sre - ai[j] * sim + bu[b, j, _slab(t), :]
                    nim = ar[j] * sim + ai[j] * sre + bu[b, NRE + j, _slab(t), :]
                    st_ref[b, j, _slab(t), :] = nre
                    st_ref[b, NRE + j, _slab(t), :] = nim
                    st[b][j], st[b][NRE + j] = nre, nim
        for b in range(bsz):
            for j in range(NCH):
                carry[b, j] = st[b][j]
                end_ref[b, j] = st[b][j]
        for q in range(PIECES):
            s2 = _get_rows(st_ref, q, bsz, tb).astype(BF16)
            r = lax.dot_general(s2, c_ref[q], NN, preferred_element_type=F32)
            for b in range(bsz):
                y_ref[b, :, q * CH:(q + 1) * CH] = r[b * tb:(b + 1) * tb]

    rows = pl.BlockSpec((bsz, tb, d), lambda i: (0, i, 0))
    vec = pl.BlockSpec((PIECES, HALF), lambda i: (0, 0))
    return pl.pallas_call(
        body, name="s5_ssm_fwd", grid=(nt,),
        in_specs=[rows, pl.BlockSpec((PIECES, CH, 2 * HALF), lambda i: (0, 0, 0)),
                  pl.BlockSpec((PIECES, 2 * HALF, CH), lambda i: (0, 0, 0)), vec, vec],
        out_specs=[rows, pl.BlockSpec((bsz, None, NCH, PIECES * PITCH, LANES), lambda i: (0, i, 0, 0, 0)),
                   pl.BlockSpec((bsz, None, NCH, PIECES, LANES), lambda i: (0, i, 0, 0, 0))],
        out_shape=[jax.ShapeDtypeStruct((bsz, seq, d), F32),
                   jax.ShapeDtypeStruct((bsz, nt, NCH, PIECES * PITCH, LANES), F32),
                   jax.ShapeDtypeStruct((bsz, nt, NCH, PIECES, LANES), F32)],
        scratch_shapes=[pltpu.VMEM((bsz, NCH, PIECES * PITCH, LANES), F32), pltpu.VMEM((bsz, NCH, PIECES, LANES), F32)],
        compiler_params=_params(("arbitrary",)))(u, b_mat, c_mat, a_re, a_im)


def _ssm_bwd_call(dy, u, st, ends, b_mat, c_mat, a_re, a_im):
    bsz, seq, d = u.shape
    tb = SSM_TB
    nt = seq // tb

    def body(dy_ref, u_ref, st_ref, prev_ref, b_ref, c_ref, ar_ref, ai_ref, du_ref, db_hbm, dc_hbm, dar_ref, dai_ref,
             lam, carry, db_acc, dc_acc):
        i = pl.program_id(0)

        @pl.when(i == 0)
        def _():
            carry[...] = jnp.zeros_like(carry)
            db_acc[...] = jnp.zeros_like(db_acc)
            dc_acc[...] = jnp.zeros_like(dc_acc)
            dar_ref[...] = jnp.zeros_like(dar_ref)
            dai_ref[...] = jnp.zeros_like(dai_ref)

        dy2 = dy_ref[...].reshape(bsz * tb, d).astype(BF16)
        u2 = u_ref[...].reshape(bsz * tb, d).astype(BF16)
        for q in range(PIECES):
            dyq = dy2[:, q * CH:(q + 1) * CH]
            g = lax.dot_general(dyq, c_ref[q], NT, preferred_element_type=F32)
            for b in range(bsz):
                _put_rows(lam, b, q, g[b * tb:(b + 1) * tb], tb)
            s2 = _get_rows(st_ref, q, bsz, tb).astype(BF16)
            dc_acc[q] += lax.dot_general(s2, dyq, TN, preferred_element_type=F32)

        ar, ai = _chunks(ar_ref[...]), _chunks(ai_ref[...])
        keep = jnp.where(i == nt - 1, 0.0, 1.0)
        lm = [[carry[b, j] for j in range(NCH)] for b in range(bsz)]
        dar = [jnp.zeros((PIECES, LANES), F32) for _ in range(NRE)]
        dai = [jnp.zeros((PIECES, LANES), F32) for _ in range(NRE)]
        for t in range(tb - 1, -1, -1):
            for b in range(bsz):
                for j in range(NRE):
                    lre, lim = lm[b][j], lm[b][NRE + j]
                    nre = lam[b, j, _slab(t), :] + ar[j] * lre + ai[j] * lim
                    nim = lam[b, NRE + j, _slab(t), :] - ai[j] * lre + ar[j] * lim
                    lam[b, j, _slab(t), :] = nre
                    lam[b, NRE + j, _slab(t), :] = nim
                    if t > 0:
                        pre, pim = st_ref[b, j, _slab(t - 1), :], st_ref[b, NRE + j, _slab(t - 1), :]
                    else:
                        pre, pim = prev_ref[b, j] * keep, prev_ref[b, NRE + j] * keep
                    dar[j] = dar[j] + nre * pre + nim * pim
                    dai[j] = dai[j] + nim * pre - nre * pim
                    lm[b][j], lm[b][NRE + j] = nre, nim
        for b in range(bsz):
            for j in range(NCH):
                carry[b, j] = lm[b][j]
        dar_ref[...] += jnp.concatenate(dar, axis=1)
        dai_ref[...] += jnp.concatenate(dai, axis=1)

        for q in range(PIECES):
            l2 = _get_rows(lam, q, bsz, tb).astype(BF16)
            r = lax.dot_general(l2, b_ref[q], NT, preferred_element_type=F32)
            for b in range(bsz):
                du_ref[b, :, q * CH:(q + 1) * CH] = r[b * tb:(b + 1) * tb]
            db_acc[q] += lax.dot_general(u2[:, q * CH:(q + 1) * CH], l2, TN, preferred_element_type=F32)

        @pl.when(i == nt - 1)
        def _():
            pltpu.sync_copy(db_acc, db_hbm)
            pltpu.sync_copy(dc_acc, dc_hbm)

    rows = pl.BlockSpec((bsz, tb, d), lambda i: (0, nt - 1 - i, 0))
    vec = pl.BlockSpec((PIECES, HALF), lambda i: (0, 0))
    hbm = pl.BlockSpec(memory_space=pl.ANY)
    return pl.pallas_call(
        body, name="s5_ssm_bwd", grid=(nt,),
        in_specs=[rows, rows,
                  pl.BlockSpec((bsz, None, NCH, PIECES * PITCH, LANES), lambda i: (0, nt - 1 - i, 0, 0, 0)),
                  pl.BlockSpec((bsz, None, NCH, PIECES, LANES), lambda i: (0, jnp.maximum(nt - 2 - i, 0), 0, 0, 0)),
                  pl.BlockSpec((PIECES, CH, 2 * HALF), lambda i: (0, 0, 0)),
                  pl.BlockSpec((PIECES, 2 * HALF, CH), lambda i: (0, 0, 0)), vec, vec],
        out_specs=[rows, hbm, hbm, vec, vec],
        out_shape=[jax.ShapeDtypeStruct((bsz, seq, d), F32), jax.ShapeDtypeStruct((PIECES, CH, 2 * HALF), F32),
                   jax.ShapeDtypeStruct((PIECES, 2 * HALF, CH), F32), jax.ShapeDtypeStruct((PIECES, HALF), F32),
                   jax.ShapeDtypeStruct((PIECES, HALF), F32)],
        scratch_shapes=[pltpu.VMEM((bsz, NCH, PIECES * PITCH, LANES), F32), pltpu.VMEM((bsz, NCH, PIECES, LANES), F32),
                        pltpu.VMEM((PIECES, CH, 2 * HALF), F32), pltpu.VMEM((PIECES, 2 * HALF, CH), F32)],
        compiler_params=_params(("arbitrary",)))(dy, u, st, ends, b_mat, c_mat, a_re, a_im)


@jax.custom_vjp
def s5_ssm(u, b_mat, c_mat, a_re, a_im):
    return _s5_ssm_fwd(u, b_mat, c_mat, a_re, a_im)[0]


def _s5_ssm_fwd(u, b_mat, c_mat, a_re, a_im):
    b16, c16 = b_mat.astype(BF16), c_mat.astype(BF16)
    y, st, ends = _ssm_fwd_call(u, b16, c16, a_re, a_im)
    return y, (u, st, ends, b16, c16, a_re, a_im)


def _s5_ssm_bwd(res, dy):
    return tuple(_ssm_bwd_call(dy, *res))


s5_ssm.defvjp(_s5_ssm_fwd, _s5_ssm_bwd)


ATT_HW = 2 * HEAD
ATT_HB = 1024 // ATT_HW


def _branch_geometry(dil, seq):
    sub = seq // dil
    assert sub % QBLK == 0
    return sub // QBLK


def _drows(dil, r, start, size):
    if dil == 1:
        return pl.ds(start, size)
    return pl.ds(r + start * dil, size, stride=dil)


def _masks():
    qi = lax.broadcasted_iota(jnp.int32, (QBLK, 2 * QBLK), 0)
    kj = lax.broadcasted_iota(jnp.int32, (QBLK, 2 * QBLK), 1) - QBLK
    dist = qi - kj
    band = jnp.logical_and(dist >= 0, dist <= QBLK)
    ci = lax.broadcasted_iota(jnp.int32, (QBLK, QBLK), 0)
    cj = lax.broadcasted_iota(jnp.int32, (QBLK, QBLK), 1)
    return band, ci >= cj


def _attn_specs(i, seq):
    q_spec = pl.BlockSpec((None, seq, ATT_HW), lambda b, h: (b, 0, i * ATT_HB + h))
    k_spec = pl.BlockSpec((None, seq, ATT_HW), lambda b, h: (b, 0, i * ATT_HB + h))
    v_spec = pl.BlockSpec((None, seq, ATT_HW), lambda b, h: (b, 0, (3 + i) * ATT_HB + h))
    o_spec = pl.BlockSpec((None, seq, ATT_HW), lambda b, h: (b, 0, h))
    return q_spec, k_spec, v_spec, o_spec


def _attn_fwd_call(q_all, kv_all, i, dil):
    bsz, seq, _ = q_all.shape
    nb = _branch_geometry(dil, seq)
    scale = HEAD ** -0.5

    def body(q_ref, k_ref, v_ref, o_ref, l_ref):
        band, causal = _masks()

        def one(r, n, mask):
            qs = _drows(dil, r, n * QBLK, QBLK)
            ks = qs if n == 0 else _drows(dil, r, (n - 1) * QBLK, 2 * QBLK)
            q2 = q_ref[qs, :].astype(BF16)
            k2 = k_ref[ks, :].astype(BF16)
            v2 = v_ref[ks, :].astype(BF16)
            outs, lses = [], []
            for h in range(ATT_HW // HEAD):
                ls = slice(h * HEAD, (h + 1) * HEAD)
                s = lax.dot_general(q2[:, ls], k2[:, ls], NT, preferred_element_type=F32) * scale
                s = jnp.where(mask, s, NEG)
                m = jnp.max(s, axis=-1, keepdims=True)
                p = jnp.exp(s - m)
                den = jnp.sum(p, axis=-1, keepdims=True)
                outs.append(lax.dot_general(p.astype(BF16), v2[:, ls], NN, preferred_element_type=F32) / den)
                lses.append(jnp.broadcast_to(m + jnp.log(den), (QBLK, HEAD)))
            o_ref[qs, :] = jnp.concatenate(outs, axis=1)
            l_ref[qs, :] = jnp.concatenate(lses, axis=1)

        for r in range(dil):
            for n in range(nb):
                one(r, n, causal if n == 0 else band)

    q_spec, k_spec, v_spec, o_spec = _attn_specs(i, seq)
    shape = jax.ShapeDtypeStruct((bsz, seq, 1024), F32)
    return pl.pallas_call(
        body, name="attn_fwd_d%d" % dil, grid=(bsz, ATT_HB), in_specs=[q_spec, k_spec, v_spec],
        out_specs=[o_spec, o_spec], out_shape=[shape, shape],
        compiler_params=_params(("parallel", "parallel")))(q_all, kv_all, kv_all)


def _attn_bwd_call(q_all, kv_all, o, l, do, dl, i, dil, grads):
    bsz, seq, _ = q_all.shape
    nb = _branch_geometry(dil, seq)
    scale = HEAD ** -0.5
    first = grads is None

    def body(q_ref, k_ref, v_ref, o_ref, l_ref, do_ref, dl_ref, *rest):
        dq_ref, dk_ref, dv_ref = rest[-3:]
        band, causal = _masks()

        def one(r, n, mask):
            qs = _drows(dil, r, n * QBLK, QBLK)
            ks = qs if n == 0 else _drows(dil, r, (n - 1) * QBLK, 2 * QBLK)
            q2 = q_ref[qs, :].astype(BF16)
            k2 = k_ref[ks, :].astype(BF16)
            v2 = v_ref[ks, :].astype(BF16)
            l2, do2 = l_ref[qs, :], do_ref[qs, :]
            t2 = dl_ref[qs, :] - do2 * o_ref[qs, :]
            dqs, dks, dvs = [], [], []
            for h in range(ATT_HW // HEAD):
                ls = slice(h * HEAD, (h + 1) * HEAD)
                q, k, v = q2[:, ls], k2[:, ls], v2[:, ls]
                s = lax.dot_general(q, k, NT, preferred_element_type=F32) * scale
                s = jnp.where(mask, s, NEG)
                p = jnp.exp(s - l2[:, h * HEAD:h * HEAD + 1])
                row = jnp.sum(t2[:, ls], axis=-1, keepdims=True)
                d_ob = do2[:, ls].astype(BF16)
                dp = lax.dot_general(d_ob, v, NT, preferred_element_type=F32)
                ds = (p * (dp + row)).astype(BF16)
                dqs.append(lax.dot_general(ds, k, NN, preferred_element_type=F32) * scale)
                dks.append(lax.dot_general(ds, q, TN, preferred_element_type=F32) * scale)
                dvs.append(lax.dot_general(p.astype(BF16), d_ob, TN, preferred_element_type=F32))
            dq_ref[qs, :] = jnp.concatenate(dqs, axis=1)
            return jnp.concatenate(dks, axis=1), jnp.concatenate(dvs, axis=1)

        for r in range(dil):
            dk_cur, dv_cur = one(r, 0, causal)
            for n in range(1, nb):
                dk, dv = one(r, n, band)
                ks = _drows(dil, r, (n - 1) * QBLK, QBLK)
                dk_ref[ks, :] = dk_cur + dk[:QBLK]
                dv_ref[ks, :] = dv_cur + dv[:QBLK]
                dk_cur, dv_cur = dk[QBLK:], dv[QBLK:]
            ks = _drows(dil, r, (nb - 1) * QBLK, QBLK)
            dk_ref[ks, :] = dk_cur
            dv_ref[ks, :] = dv_cur

    q_spec, k_spec, v_spec, o_spec = _attn_specs(i, seq)
    in_specs = [q_spec, k_spec, v_spec, o_spec, o_spec, o_spec, o_spec]
    args = [q_all, kv_all, kv_all, o, l, do, dl]
    aliases = {}
    if not first:
        in_specs += [pl.BlockSpec(memory_space=pl.ANY)] * 3
        args += list(grads)
        aliases = {7: 0, 8: 1, 9: 2}
    shape = jax.ShapeDtypeStruct((bsz, seq, 3 * 1024), F32)
    return pl.pallas_call(
        body, name="attn_bwd_d%d" % dil, grid=(bsz, ATT_HB), in_specs=in_specs,
        out_specs=[q_spec, q_spec, q_spec], out_shape=[shape, shape, shape], input_output_aliases=aliases,
        compiler_params=_params(("parallel", "parallel")))(*args)


@jax.custom_vjp
def attn_branches(q_all, kv_all):
    return _attn_branches_fwd(q_all, kv_all)[0]


def _attn_branches_fwd(q_all, kv_all):
    outs = []
    for i, dil in enumerate(BRANCH_DIL):
        outs += list(_attn_fwd_call(q_all, kv_all, i, dil))
    return tuple(outs), (q_all, kv_all, tuple(outs))


def _attn_branches_bwd(res, cts):
    q_all, kv_all, outs = res
    grads = None
    for i, dil in enumerate(BRANCH_DIL):
        grads = _attn_bwd_call(q_all, kv_all, outs[2 * i], outs[2 * i + 1], cts[2 * i], cts[2 * i + 1], i, dil, grads)
    dq, dk, dv = grads
    return dq, jnp.concatenate([dk, dv], axis=-1)


attn_branches.defvjp(_attn_branches_fwd, _attn_branches_bwd)


def ada_fwd(c_all, w, b):
    n, d, cs = w.shape
    nb = c_all.shape[0]

    def body(c_ref, w_ref, b_ref, o_ref):
        a = jax.nn.silu(c_ref[...]).astype(BF16)
        o_ref[...] = lax.dot_general(a, w_ref[...].astype(BF16), NN, preferred_element_type=F32) + b_ref[...]

    return pl.pallas_call(
        body, name="ada_fwd", grid=(n,),
        in_specs=[pl.BlockSpec((nb, d), lambda i: (0, 0)), pl.BlockSpec((None, d, cs), lambda i: (i, 0, 0)),
                  pl.BlockSpec((None, 1, cs), lambda i: (i, 0, 0))],
        out_specs=pl.BlockSpec((None, nb, cs), lambda i: (i, 0, 0)),
        out_shape=jax.ShapeDtypeStruct((n, nb, cs), F32), compiler_params=_params(("parallel",)))(c_all, w, b)


def ada_bwd(c_all, dm):
    n, nb, cs = dm.shape
    d = c_all.shape[1]

    def body(c_ref, dm_ref, dw_ref, db_ref):
        a = jax.nn.silu(c_ref[...]).astype(BF16)
        g = dm_ref[...]
        dw_ref[...] = lax.dot_general(a, g.astype(BF16), TN, preferred_element_type=F32)
        db_ref[...] = jnp.sum(g, axis=0, keepdims=True)

    return pl.pallas_call(
        body, name="ada_bwd", grid=(n,),
        in_specs=[pl.BlockSpec((nb, d), lambda i: (0, 0)), pl.BlockSpec((None, nb, cs), lambda i: (i, 0, 0))],
        out_specs=[pl.BlockSpec((None, d, cs), lambda i: (i, 0, 0)), pl.BlockSpec((None, 1, cs), lambda i: (i, 0, 0))],
        out_shape=[jax.ShapeDtypeStruct((n, d, cs), F32), jax.ShapeDtypeStruct((n, 1, cs), F32)],
        compiler_params=_params(("parallel",)))(c_all, dm)


def all_gather(x, name):
    m, n = x.shape

    def body(x_ref, out_ref, send_sems, recv_sems, local_sem):
        px, py, pc = lax.axis_index("x"), lax.axis_index("y"), lax.axis_index("c")
        me, sibling = (px, py, pc), (px, py, 1 - pc)
        chips = [(1 - px, py), (px, 1 - py), (1 - px, 1 - py)]

        def rows(qx, qy, qc):
            return out_ref.at[pl.ds((4 * qx + 2 * qy + qc) * m, m), :]

        def copy(k, block, to, src=None):
            return pltpu.make_async_remote_copy(
                src_ref=rows(*block) if src is None else src, dst_ref=rows(*block),
                send_sem=send_sems.at[k], recv_sem=recv_sems.at[k], device_id=to,
                device_id_type=pl.DeviceIdType.MESH)

        mine = pltpu.make_async_copy(x_ref, rows(*me), local_sem)
        mine.start()
        first = [copy(0, me, sibling, src=x_ref)]
        first += [copy(1 + j, me, (*chip, pc), src=x_ref) for j, chip in enumerate(chips)]
        for cp in first:
            cp.start()
        passed = [copy(4 + j, (*chip, pc), sibling) for j, chip in enumerate(chips)]
        for j, chip in enumerate(chips):
            copy(1 + j, (*chip, pc), me).wait_recv()
            passed[j].start()
        copy(0, sibling, me).wait_recv()
        for j, chip in enumerate(chips):
            copy(4 + j, (*chip, 1 - pc), me).wait_recv()
        for cp in first + passed:
            cp.wait_send()
        mine.wait()

    out = pl.pallas_call(
        body, name=name, out_shape=jax.ShapeDtypeStruct((N_DEV * m, n), x.dtype),
        in_specs=[pl.BlockSpec(memory_space=pl.ANY)], out_specs=pl.BlockSpec(memory_space=pl.ANY),
        scratch_shapes=[pltpu.SemaphoreType.DMA((7,)), pltpu.SemaphoreType.DMA((7,)), pltpu.SemaphoreType.DMA(())],
    )(x)
    return out.reshape(N_DEV, m, n)


def exchange_partials(p, name):
    _, m, n = p.shape

    def body(p_ref, out_ref, send_sems, recv_sems, local_sem):
        px, py, pc = lax.axis_index("x"), lax.axis_index("y"), lax.axis_index("c")
        me = 4 * px + 2 * py + pc
        mine = pltpu.make_async_copy(p_ref.at[me], out_ref.at[me], local_sem)
        mine.start()
        copies = []
        for k in range(1, N_DEV):
            qx = 1 - px if k & 4 else px
            qy = 1 - py if k & 2 else py
            qc = 1 - pc if k & 1 else pc
            cp = pltpu.make_async_remote_copy(
                src_ref=p_ref.at[4 * qx + 2 * qy + qc], dst_ref=out_ref.at[me],
                send_sem=send_sems.at[k - 1], recv_sem=recv_sems.at[k - 1], device_id=(qx, qy, qc),
                device_id_type=pl.DeviceIdType.MESH)
            cp.start()
            copies.append(cp)
        for cp in copies:
            cp.wait()
        mine.wait()

    return pl.pallas_call(
        body, name=name, out_shape=jax.ShapeDtypeStruct(p.shape, p.dtype),
        in_specs=[pl.BlockSpec(memory_space=pl.ANY)], out_specs=pl.BlockSpec(memory_space=pl.ANY),
        scratch_shapes=[pltpu.SemaphoreType.DMA((7,)), pltpu.SemaphoreType.DMA((7,)), pltpu.SemaphoreType.DMA(())],
    )(p)


def sum_slots(g):
    _, r, n = g.shape
    tr = _tile(r, 512)

    def body(g_ref, o_ref):
        acc = g_ref[0]
        for s in range(1, N_DEV):
            acc = acc + g_ref[s]
        o_ref[...] = acc

    return pl.pallas_call(
        body, name="sum_slots", grid=(r // tr,), in_specs=[pl.BlockSpec((N_DEV, tr, n), lambda i: (0, i, 0))],
        out_specs=pl.BlockSpec((tr, n), lambda i: (i, 0)), out_shape=jax.ShapeDtypeStruct((r, n), F32),
        compiler_params=_params(("parallel",)))(g)


def adamw(w, m, v, g, name):
    nl, r, n = w.shape
    partials = g.ndim == 4
    tr = _tile(r, max(8, (256 * 1024) // n))
    c1 = 1.0 / (1.0 - ADAM_B1 ** ADAM_STEP)
    c2 = 1.0 / (1.0 - ADAM_B2 ** ADAM_STEP)

    def body(w_ref, m_ref, v_ref, g_ref, go_ref, d_ref, mo_ref, vo_ref):
        if partials:
            grad = g_ref[0].astype(F32)
            for s in range(1, N_DEV):
                grad = grad + g_ref[s].astype(F32)
        else:
            grad = g_ref[...]
        m_new = ADAM_B1 * m_ref[...] + (1.0 - ADAM_B1) * grad
        v_new = ADAM_B2 * v_ref[...] + (1.0 - ADAM_B2) * grad * grad
        go_ref[...] = grad
        mo_ref[...] = m_new
        vo_ref[...] = v_new
        d_ref[...] = -ADAM_LR * ((m_new * c1) / (jnp.sqrt(v_new * c2) + ADAM_EPS) + ADAM_WD * w_ref[...])

    spec = pl.BlockSpec((None, tr, n), lambda l, i: (l, i, 0))
    g_spec = pl.BlockSpec((None, N_DEV, tr, n), lambda l, i: (l, 0, i, 0)) if partials else spec
    shape = jax.ShapeDtypeStruct(w.shape, F32)
    return pl.pallas_call(
        body, name=name, grid=(nl, r // tr), in_specs=[spec, spec, spec, g_spec], out_specs=[spec] * 4,
        out_shape=[shape] * 4, compiler_params=_params(("parallel", "parallel")))(w, m, v, g)


def pack(arrays):
    flat = jnp.concatenate([a.reshape(-1).astype(F32) for a in arrays])
    rows = -(-flat.shape[0] // LANES)
    mult = 8 if rows <= 512 else 512
    rows = -(-rows // mult) * mult
    flat = jnp.pad(flat, (0, rows * LANES - flat.shape[0]))
    return flat.reshape(rows, LANES)


def unpack(slab, shapes):
    flat = slab.reshape(slab.shape[:-2] + (-1,))
    out, off = [], 0
    for s in shapes:
        size = math.prod(s)
        out.append(flat[..., off:off + size].reshape(flat.shape[:-1] + tuple(s)))
        off += size
    return out


def _s5_discretize(lam_re, lam_im, log_dt, b_re, b_im, c_re, c_im):
    dt = jnp.exp(log_dt)[:, None]
    xr, th = lam_re * dt, lam_im * dt
    er = jnp.exp(xr)
    a_re, a_im = er * jnp.cos(th), er * jnp.sin(th)
    am1 = jnp.expm1(xr) * jnp.cos(th) - 2.0 * jnp.square(jnp.sin(0.5 * th))
    den = lam_re * lam_re + lam_im * lam_im
    fr = (am1 * lam_re + a_im * lam_im) / den
    fi = (a_im * lam_re - am1 * lam_im) / den
    bb_re = fr[..., None] * b_re - fi[..., None] * b_im
    bb_im = fr[..., None] * b_im + fi[..., None] * b_re
    eye = jnp.eye(8, dtype=F32)

    def pack_b(bb):
        return jnp.einsum('qgpc,gh->qgchp', bb.reshape(8, 8, 64, 16), eye).reshape(8, 128, 512)

    def pack_c(cc):
        return jnp.einsum('qgcp,gh->qgphc', cc.reshape(8, 8, 16, 64), eye).reshape(8, 512, 128)

    b_mat = jnp.concatenate([pack_b(bb_re), pack_b(bb_im)], axis=-1)
    c_mat = jnp.concatenate([pack_c(c_re), pack_c(-c_im)], axis=1)
    return a_re.reshape(8, HALF), a_im.reshape(8, HALF), b_mat, c_mat


def _local_loss(diff, x, target, weights):
    bsz, seq, d = x.shape
    t = bsz * seq
    rows3 = lambda a: a.reshape(bsz, seq, a.shape[-1])
    rows2 = lambda a: a.reshape(t, a.shape[-1])
    mods, kvmod, ln_g, ssm_d, kv_g, final_g = (diff[k] for k in ("mods", "kvmod", "ln_g", "ssm_d", "kv_g", "final_g"))
    h = x
    kv_all = None
    for layer in range(4):
        if layer == 2:
            shift, scale = kvmod[:, None, :d], kvmod[:, None, d:]
            (ukv,) = modnorm(h, scale, shift, kv_g[None])
            kv_all = rows3(linear_col(rows2(ukv), weights["w_kv"]))
        mod = mods[2 * layer]
        shift, scale, gate = mod[:, None, :d], mod[:, None, d:2 * d], mod[:, None, 2 * d:]
        (u,) = modnorm(h, scale, shift, ln_g[layer, 0][None])
        if layer < 2:
            a_re, a_im, b_mat, c_mat = _s5_discretize(*(diff[k][layer] for k in (
                "ssm_lam_re", "ssm_lam_im", "ssm_log_dt", "ssm_b_re", "ssm_b_im", "ssm_c_re", "ssm_c_im")))
            y0 = s5_ssm(u, b_mat, c_mat, a_re, a_im)
            (z,) = gelu_skip(y0, u, ssm_d[layer][None])
            (y,) = glu(rows3(linear_col(rows2(z), weights["w_glu"][layer])))
        else:
            j = layer - 2
            q_all = rows3(linear_col(rows2(u), weights["w_q"][j]))
            o1, l1, o2, l2, o3, l3 = attn_branches(q_all, kv_all)
            (o,) = combine(o1, o2, o3, l1, l2, l3)
            y = rows3(linear(rows2(o), weights["w_o"][j]))
        (h,) = gate_res(h, y, gate)
        mod = mods[2 * layer + 1]
        shift, scale, gate = mod[:, None, :d], mod[:, None, d:2 * d], mod[:, None, 2 * d:]
        (u,) = modnorm(h, scale, shift, ln_g[layer, 1][None])
        (act,) = relu2(rows3(linear_col(rows2(u), weights["w1"][layer])))
        y = rows3(linear(rows2(act), weights["w2"][layer]))
        (h,) = gate_res(h, y, gate)
    (row_loss,) = final_loss(h, target, final_g[None])
    return jnp.sum(row_loss)


SSM_NAMES = ("ssm_lam_re", "ssm_lam_im", "ssm_log_dt", "ssm_b_re", "ssm_b_im", "ssm_c_re", "ssm_c_im")
ARG_NAMES = ("x", "c", "ln_g", "ada_w", "ada_b") + SSM_NAMES + (
    "ssm_d", "ssm_w_glu", "kv_g", "kv_ada_w", "kv_ada_b", "w_kv", "attn_w_q", "attn_w_o", "mlp_w1", "mlp_w2", "final_g")
WEIGHT_NAMES = ARG_NAMES[2:]


def kernel(x, c, ln_g, ada_w, ada_b, ssm_lam_re, ssm_lam_im, ssm_log_dt, ssm_b_re, ssm_b_im, ssm_c_re, ssm_c_im, ssm_d, ssm_w_glu, kv_g, kv_ada_w, kv_ada_b, w_kv, attn_w_q, attn_w_o, mlp_w1, mlp_w2, final_g, loss_target, m_ln_g, m_ada_w, m_ada_b, m_ssm_lam_re, m_ssm_lam_im, m_ssm_log_dt, m_ssm_b_re, m_ssm_b_im, m_ssm_c_re, m_ssm_c_im, m_ssm_d, m_ssm_w_glu, m_kv_g, m_kv_ada_w, m_kv_ada_b, m_w_kv, m_attn_w_q, m_attn_w_o, m_mlp_w1, m_mlp_w2, m_final_g, v_ln_g, v_ada_w, v_ada_b, v_ssm_lam_re, v_ssm_lam_im, v_ssm_log_dt, v_ssm_b_re, v_ssm_b_im, v_ssm_c_re, v_ssm_c_im, v_ssm_d, v_ssm_w_glu, v_kv_g, v_kv_ada_w, v_kv_ada_b, v_w_kv, v_attn_w_q, v_attn_w_o, v_mlp_w1, v_mlp_w2, v_final_g):
    args = locals()
    w = {n: args[n] for n in WEIGHT_NAMES}
    mom = {n: args["m_" + n] for n in WEIGHT_NAMES}
    var = {n: args["v_" + n] for n in WEIGHT_NAMES}
    bsz, seq, d = x.shape
    me = 4 * lax.axis_index("x") + 2 * lax.axis_index("y") + lax.axis_index("c")

    small = all_gather(pack([c, ln_g, ssm_d]), "gather_small")
    c_parts, ln_parts, d_parts = unpack(small, [c.shape, ln_g.shape, ssm_d.shape])
    c_all = c_parts.reshape(N_DEV * bsz, d)
    ln_full = jnp.moveaxis(ln_parts, 0, 2).reshape(4, 2, d)
    d_full = jnp.moveaxis(d_parts, 0, 1).reshape(2, d)

    cs = ada_w.shape[-1]
    mod_cols = ada_fwd(c_all, ada_w.reshape(8, d, cs), ada_b.reshape(8, 1, cs))
    kcs = kv_ada_w.shape[-1]
    kv_cols = ada_fwd(c_all, kv_ada_w[None], jnp.zeros((1, 1, kcs), F32))
    mod_g = all_gather(mod_cols.reshape(8 * N_DEV * bsz, cs), "gather_mod")
    kv_g_all = all_gather(kv_cols.reshape(N_DEV * bsz, kcs), "gather_kvmod")
    mods_all = jnp.moveaxis(mod_g.reshape(N_DEV, 8, N_DEV * bsz, cs), 0, 2).reshape(8, N_DEV * bsz, N_DEV * cs)
    kvmod_all = jnp.moveaxis(kv_g_all, 0, 1).reshape(N_DEV * bsz, N_DEV * kcs) + kv_ada_b[None]
    mods = lax.dynamic_slice_in_dim(mods_all, me * bsz, bsz, axis=1)
    kvmod = lax.dynamic_slice_in_dim(kvmod_all, me * bsz, bsz, axis=0)

    def gather_w(a, name):
        return all_gather(a.astype(BF16).reshape(-1, a.shape[-1]), name).reshape((N_DEV,) + a.shape)

    weights = {
        "w_glu": [gather_w(ssm_w_glu[l], "gather_w_glu") for l in range(2)],
        "w_kv": gather_w(w_kv, "gather_w_kv"),
        "w_q": [gather_w(attn_w_q[l], "gather_w_q") for l in range(2)],
        "w_o": [gather_w(attn_w_o[l], "gather_w_o").reshape(-1, d) for l in range(2)],
        "w1": [gather_w(mlp_w1[l], "gather_w1") for l in range(4)],
        "w2": [gather_w(mlp_w2[l], "gather_w2").reshape(-1, d) for l in range(4)],
    }

    diff = {"mods": mods, "kvmod": kvmod, "ln_g": ln_full, "ssm_d": d_full, "kv_g": kv_g, "final_g": final_g}
    diff.update({n: w[n] for n in SSM_NAMES})
    loss_local, (g_diff, grad_x, g_w) = jax.value_and_grad(_local_loss, argnums=(0, 1, 3))(diff, x, loss_target, weights)
    loss = lax.psum(loss_local, AXES)

    dmod = all_gather(pack([g_diff["mods"], g_diff["kvmod"]]), "gather_dmod")
    dmods_p, dkv_p = unpack(dmod, [g_diff["mods"].shape, g_diff["kvmod"].shape])
    dmods_all = jnp.moveaxis(dmods_p, 0, 1).reshape(8, N_DEV * bsz, 3 * d)
    dkv_all = dkv_p.reshape(N_DEV * bsz, 2 * d)
    g_ada_w, g_ada_b = ada_bwd(c_all, lax.dynamic_slice_in_dim(dmods_all, me * cs, cs, axis=2))
    g_kv_ada_w, _ = ada_bwd(c_all, lax.dynamic_slice_in_dim(dkv_all, me * kcs, kcs, axis=1)[None])
    _, g_kv_ada_b = ada_bwd(c_all, dkv_all[None])

    small_names = ("ln_g", "ssm_d", "kv_g", "final_g") + SSM_NAMES
    partial = all_gather(pack([g_diff[n] for n in small_names]), "gather_small_grads")
    totals = unpack(sum_slots(partial), [g_diff[n].shape for n in small_names])
    g_small = dict(zip(small_names, totals))
    g_small["ln_g"] = lax.dynamic_slice_in_dim(g_small["ln_g"], me * ln_g.shape[-1], ln_g.shape[-1], axis=2)
    g_small["ssm_d"] = lax.dynamic_slice_in_dim(g_small["ssm_d"], me * ssm_d.shape[-1], ssm_d.shape[-1], axis=1)
    g_small["ada_b"] = g_ada_b.reshape(ada_b.shape)
    g_small["kv_ada_b"] = g_kv_ada_b.reshape(kv_ada_b.shape)

    out = {}

    def put(name, res, shape):
        for kind, a in zip(("grad_", "delta_", "new_m_", "new_v_"), res):
            out[kind + name] = a.reshape(shape)

    def big(name, partials, label):
        recv = jnp.stack([exchange_partials(p, "exchange_" + label) for p in partials])
        rows, cols = recv.shape[2:]
        v3 = lambda a: a.reshape(len(partials), rows, cols)
        put(name, adamw(v3(w[name]), v3(mom[name]), v3(var[name]), recv, "adamw_" + label), w[name].shape)

    big("ssm_w_glu", g_w["w_glu"], "w_glu")
    big("w_kv", [g_w["w_kv"]], "w_kv")
    big("attn_w_q", g_w["w_q"], "w_q")
    big("attn_w_o", [g.reshape(N_DEV, -1, d) for g in g_w["w_o"]], "w_o")
    big("mlp_w1", g_w["w1"], "w1")
    big("mlp_w2", [g.reshape(N_DEV, -1, d) for g in g_w["w2"]], "w2")
    v3 = lambda a: a.reshape(8, d, cs)
    put("ada_w", adamw(v3(ada_w), v3(m_ada_w), v3(v_ada_w), g_ada_w, "adamw_ada_w"), ada_w.shape)
    put("kv_ada_w", adamw(kv_ada_w[None], m_kv_ada_w[None], v_kv_ada_w[None], g_kv_ada_w, "adamw_kv_ada_w"), kv_ada_w.shape)
    names = small_names + ("ada_b", "kv_ada_b")
    res = adamw(pack([w[n] for n in names])[None], pack([mom[n] for n in names])[None],
                pack([var[n] for n in names])[None], pack([g_small[n] for n in names])[None], "adamw_small")
    for kind, slab in zip(("grad_", "delta_", "new_m_", "new_v_"), res):
        for n, a in zip(names, unpack(slab[0], [w[n].shape for n in names])):
            out[kind + n] = a

    result = [loss, grad_x]
    for kind in ("grad_", "delta_", "new_m_", "new_v_"):
        result += [out[kind + n] for n in WEIGHT_NAMES]
    return tuple(result)
```

```python
import functools
import math

import jax
import jax.numpy as jnp
from jax import lax
from jax.experimental import pallas as pl
from jax.experimental.pallas import tpu as pltpu

F32 = jnp.float32
BF16 = jnp.bfloat16
N_DEV = 8
AXES = ("x", "y", "c")
V7X_VMEM_LIMIT = 56 * 1024 * 1024
LANES = 128
EPS = 1e-6
NEG = -1e30
HEAD = 64
QBLK = 128
BRANCH_DIL = (1, 4, 16)
ADAM_LR, ADAM_B1, ADAM_B2, ADAM_EPS, ADAM_WD, ADAM_STEP = 0.001, 0.9, 0.999, 1e-08, 0.01, 10

NN = (((1,), (0,)), ((), ()))
NT = (((1,), (1,)), ((), ()))
TN = (((0,), (0,)), ((), ()))


def _params(sem):
    return pltpu.CompilerParams(dimension_semantics=sem, vmem_limit_bytes=V7X_VMEM_LIMIT)


def _peer(k):
    px, py, pc = lax.axis_index("x"), lax.axis_index("y"), lax.axis_index("c")
    qx = 1 - px if k & 4 else px
    qy = 1 - py if k & 2 else py
    qc = 1 - pc if k & 1 else pc
    return (qx, qy, qc), 4 * qx + 2 * qy + qc


def _carried_copies(kind, src_ref, out_ref, send_sems, recv_sems, local_sem):
    _, me = _peer(0)
    mine = src_ref if kind == "gather" else src_ref.at[me]
    copies = [pltpu.make_async_copy(mine, out_ref.at[me], local_sem)]
    for k in range(1, N_DEV):
        peer, slot = _peer(k)
        copies.append(pltpu.make_async_remote_copy(
            src_ref=src_ref if kind == "gather" else src_ref.at[slot], dst_ref=out_ref.at[me],
            send_sem=send_sems.at[k - 1], recv_sem=recv_sems.at[k - 1], device_id=peer,
            device_id_type=pl.DeviceIdType.MESH))
    return copies


def _mm(name, a, b, *, grid, a_spec, b_spec, out_spec, out_shape, dims, acc_axis=None, acc_shape=None, carry=None):
    n_acc = grid[acc_axis] if acc_axis is not None else 1

    def body(*refs):
        if carry is None:
            a_ref, b_ref, o_ref = refs[:3]
            scratch = refs[3:]
        else:
            a_ref, b_ref, src_ref, o_ref, land_ref = refs[:5]
            scratch, sems = refs[5:-3], refs[-3:]
            ids = [pl.program_id(i) for i in range(len(grid))]
            first = functools.reduce(jnp.logical_and, [i == 0 for i in ids])
            last = functools.reduce(jnp.logical_and, [i == n - 1 for i, n in zip(ids, grid)])

            @pl.when(first)
            def _():
                for cp in _carried_copies(carry[0], src_ref, land_ref, *sems):
                    cp.start()

        r = lax.dot_general(a_ref[...].astype(BF16), b_ref[...].astype(BF16), dims,
                            preferred_element_type=F32)
        if acc_axis is None:
            o_ref[...] = r.astype(o_ref.dtype)
        else:
            acc = scratch[0]
            k = pl.program_id(acc_axis)

            @pl.when(k == 0)
            def _():
                acc[...] = r

            @pl.when(k > 0)
            def _():
                acc[...] += r

            @pl.when(k == n_acc - 1)
            def _():
                o_ref[...] = acc[...].astype(o_ref.dtype)

        if carry is not None:
            @pl.when(last)
            def _():
                for cp in _carried_copies(carry[0], src_ref, land_ref, *sems):
                    cp.wait()

    scratch = [] if acc_axis is None else [pltpu.VMEM(acc_shape, F32)]
    if carry is None:
        sem = tuple("arbitrary" if i == acc_axis else "parallel" for i in range(len(grid)))
        return pl.pallas_call(body, name=name, grid=grid, in_specs=[a_spec, b_spec], out_specs=out_spec,
                              out_shape=out_shape, scratch_shapes=scratch, compiler_params=_params(sem))(a, b)
    kind, src = carry
    land = jax.ShapeDtypeStruct((N_DEV,) + src.shape[-2:], src.dtype)
    hbm = pl.BlockSpec(memory_space=pl.ANY)
    scratch += [pltpu.SemaphoreType.DMA((N_DEV - 1,)), pltpu.SemaphoreType.DMA((N_DEV - 1,)), pltpu.SemaphoreType.DMA(())]
    return pl.pallas_call(body, name=name + "_" + kind, grid=grid, in_specs=[a_spec, b_spec, hbm],
                          out_specs=[out_spec, hbm], out_shape=[out_shape, land], scratch_shapes=scratch,
                          compiler_params=_params(("arbitrary",) * len(grid)))(a, b, src)


def _tile(n, t):
    if n <= t:
        return n
    for d in range(t - t % 8, 7, -8):
        if n % d == 0:
            return d
    raise ValueError((n, t))


@jax.custom_vjp
def linear_col(x, g):
    return _linear_col_fwd(x, g)[0]


def _col_fwd_call(x, g, carry=None):
    t, k = x.shape
    _, _, ns = g.shape
    tm = _tile(t, 1024)
    return _mm("col_fwd", x, g, grid=(t // tm, N_DEV),
               a_spec=pl.BlockSpec((tm, k), lambda i, j: (i, 0)),
               b_spec=pl.BlockSpec((None, k, ns), lambda i, j: (j, 0, 0)),
               out_spec=pl.BlockSpec((tm, ns), lambda i, j: (i, j)),
               out_shape=jax.ShapeDtypeStruct((t, N_DEV * ns), F32), dims=NN, carry=carry)


def _col_dx_call(dy, g, carry=None):
    _, k, ns = g.shape
    t = dy.shape[0]
    tm = _tile(t, 1024)
    return _mm("col_dx", dy, g, grid=(t // tm, N_DEV),
               a_spec=pl.BlockSpec((tm, ns), lambda i, j: (i, j)),
               b_spec=pl.BlockSpec((None, k, ns), lambda i, j: (j, 0, 0)),
               out_spec=pl.BlockSpec((tm, k), lambda i, j: (i, 0)),
               out_shape=jax.ShapeDtypeStruct((t, k), F32), dims=NT, acc_axis=1, acc_shape=(tm, k), carry=carry)


def _col_dw_call(x, dy, ns):
    t, k = x.shape
    tt = _tile(t, 1024)
    return _mm("col_dw", x, dy, grid=(N_DEV, t // tt),
               a_spec=pl.BlockSpec((tt, k), lambda j, s: (s, 0)),
               b_spec=pl.BlockSpec((tt, ns), lambda j, s: (s, j)),
               out_spec=pl.BlockSpec((None, k, ns), lambda j, s: (j, 0, 0)),
               out_shape=jax.ShapeDtypeStruct((N_DEV, k, ns), BF16), dims=TN, acc_axis=1, acc_shape=(k, ns))


def _linear_col_fwd(x, g):
    return _col_fwd_call(x, g), (x, g)


def _linear_col_bwd(res, dy):
    x, g = res
    return _col_dx_call(dy, g), _col_dw_call(x, dy, g.shape[2])


linear_col.defvjp(_linear_col_fwd, _linear_col_bwd)


@jax.custom_vjp
def linear_col_next(x, g, nxt):
    return _linear_col_next_fwd(x, g, nxt)[0]


def _linear_col_next_fwd(x, g, nxt):
    y, g_next = _col_fwd_call(x, g, carry=("gather", nxt.astype(BF16)))
    return (y, g_next), (x, g)


def _linear_col_next_bwd(res, cts):
    x, g = res
    dy, dg_next = cts
    dx, recv = _col_dx_call(dy, g, carry=("exchange", dg_next))
    return dx, _col_dw_call(x, dy, g.shape[2]), sum_slots(recv)


linear_col_next.defvjp(_linear_col_next_fwd, _linear_col_next_bwd)


@jax.custom_vjp
def linear(x, w):
    return _linear_fwd(x, w)[0]


def _lin_fwd_call(x, w, carry=None):
    t, k = x.shape
    _, n = w.shape
    tm, tk = _tile(t, 1024), _tile(k, 1024)
    return _mm("lin_fwd", x, w, grid=(t // tm, k // tk),
               a_spec=pl.BlockSpec((tm, tk), lambda i, s: (i, s)),
               b_spec=pl.BlockSpec((tk, n), lambda i, s: (s, 0)),
               out_spec=pl.BlockSpec((tm, n), lambda i, s: (i, 0)),
               out_shape=jax.ShapeDtypeStruct((t, n), F32), dims=NN, acc_axis=1, acc_shape=(tm, n), carry=carry)


def _lin_dx_call(dy, w, carry=None):
    k, n = w.shape
    t = dy.shape[0]
    tm, tk = _tile(t, 1024), _tile(k, 1024)
    return _mm("lin_dx", dy, w, grid=(t // tm, k // tk),
               a_spec=pl.BlockSpec((tm, n), lambda i, s: (i, 0)),
               b_spec=pl.BlockSpec((tk, n), lambda i, s: (s, 0)),
               out_spec=pl.BlockSpec((tm, tk), lambda i, s: (i, s)),
               out_shape=jax.ShapeDtypeStruct((t, k), F32), dims=NT, carry=carry)


def _lin_dw_call(x, dy):
    t, k = x.shape
    n = dy.shape[1]
    tk, tt = _tile(k, 1024), _tile(t, 1024)
    return _mm("lin_dw", x, dy, grid=(k // tk, t // tt),
               a_spec=pl.BlockSpec((tt, tk), lambda s, r: (r, s)),
               b_spec=pl.BlockSpec((tt, n), lambda s, r: (r, 0)),
               out_spec=pl.BlockSpec((tk, n), lambda s, r: (s, 0)),
               out_shape=jax.ShapeDtypeStruct((k, n), BF16), dims=TN, acc_axis=1, acc_shape=(tk, n))


def _linear_fwd(x, w):
    return _lin_fwd_call(x, w), (x, w)


def _linear_bwd(res, dy):
    x, w = res
    return _lin_dx_call(dy, w), _lin_dw_call(x, dy)


linear.defvjp(_linear_fwd, _linear_bwd)


@jax.custom_vjp
def linear_next(x, w, nxt):
    return _linear_next_fwd(x, w, nxt)[0]


def _linear_next_fwd(x, w, nxt):
    y, g_next = _lin_fwd_call(x, w, carry=("gather", nxt.astype(BF16)))
    return (y, g_next), (x, w)


def _linear_next_bwd(res, cts):
    x, w = res
    dy, dg_next = cts
    dx, recv = _lin_dx_call(dy, w, carry=("exchange", dg_next))
    return dx, _lin_dw_call(x, dy), sum_slots(recv)


linear_next.defvjp(_linear_next_fwd, _linear_next_bwd)


@jax.custom_vjp
def grouped_mm(x, w):
    return _grouped_fwd(x, w)[0]


def _grouped_fwd(x, w):
    t = x.shape[0]
    p, kin, kout = w.shape
    tm = _tile(t, 1024)
    y = _mm("grp_fwd", x, w, grid=(t // tm, p),
            a_spec=pl.BlockSpec((tm, kin), lambda i, q: (i, q)),
            b_spec=pl.BlockSpec((None, kin, kout), lambda i, q: (q, 0, 0)),
            out_spec=pl.BlockSpec((tm, kout), lambda i, q: (i, q)),
            out_shape=jax.ShapeDtypeStruct((t, p * kout), F32), dims=NN)
    return y, (x, w)


def _grouped_bwd(res, dy):
    x, w = res
    t = x.shape[0]
    p, kin, kout = w.shape
    tm = _tile(t, 1024)
    dx = _mm("grp_dx", dy, w, grid=(t // tm, p),
             a_spec=pl.BlockSpec((tm, kout), lambda i, q: (i, q)),
             b_spec=pl.BlockSpec((None, kin, kout), lambda i, q: (q, 0, 0)),
             out_spec=pl.BlockSpec((tm, kin), lambda i, q: (i, q)),
             out_shape=jax.ShapeDtypeStruct((t, p * kin), F32), dims=NT)
    dw = _mm("grp_dw", x, dy, grid=(p, t // tm),
             a_spec=pl.BlockSpec((tm, kin), lambda q, s: (s, q)),
             b_spec=pl.BlockSpec((tm, kout), lambda q, s: (s, q)),
             out_spec=pl.BlockSpec((None, kin, kout), lambda q, s: (q, 0, 0)),
             out_shape=jax.ShapeDtypeStruct((p, kin, kout), F32), dims=TN, acc_axis=1, acc_shape=(kin, kout))
    return dx, dw


grouped_mm.defvjp(_grouped_fwd, _grouped_bwd)


def _row_ts(widths, seq):
    per_row = 4 * sum(widths)
    ts = 512
    while ts > 8 and ts * per_row * 2 > 24 * 1024 * 1024:
        ts //= 2
    return min(ts, seq)


def make_rowop(name, f, n_row, n_batch, n_vec, f_bwd=None):
    n_in = n_row + n_batch + n_vec

    def in_specs(args, ts):
        specs = []
        for a in args[:n_row]:
            specs.append(pl.BlockSpec((None, ts, a.shape[2]), lambda b, s: (b, s, 0)))
        for a in args[n_row:n_row + n_batch]:
            specs.append(pl.BlockSpec((None, 1, a.shape[2]), lambda b, s: (b, 0, 0)))
        for a in args[n_row + n_batch:]:
            specs.append(pl.BlockSpec((1, a.shape[1]), lambda b, s: (0, 0)))
        return specs

    def out_struct(args, ts):
        blocks = [jax.ShapeDtypeStruct((ts, a.shape[2]), F32) for a in args[:n_row]]
        blocks += [jax.ShapeDtypeStruct((1, a.shape[2]), F32) for a in args[n_row:n_row + n_batch]]
        blocks += [jax.ShapeDtypeStruct((1, a.shape[1]), F32) for a in args[n_row + n_batch:]]
        return jax.eval_shape(f, *blocks)

    def run_fwd(args):
        bsz, seq = args[0].shape[:2]
        outs0 = out_struct(args, 8)
        widths = [a.shape[2] for a in args[:n_row]] + [o.shape[1] for o in outs0]
        ts = _row_ts(widths, seq)

        def body(*refs):
            outs = f(*[r[...] for r in refs[:n_in]])
            for o_ref, o in zip(refs[n_in:], outs):
                o_ref[...] = o

        return pl.pallas_call(
            body, name=name + "_fwd", grid=(bsz, seq // ts), in_specs=in_specs(args, ts),
            out_specs=[pl.BlockSpec((None, ts, o.shape[1]), lambda b, s: (b, s, 0)) for o in outs0],
            out_shape=[jax.ShapeDtypeStruct((bsz, seq, o.shape[1]), F32) for o in outs0],
            compiler_params=_params(("parallel", "parallel")))(*args)

    def run_bwd(args, cts):
        bsz, seq = args[0].shape[:2]
        widths = [a.shape[2] for a in args[:n_row]] * 2 + [c.shape[2] for c in cts]
        ts = _row_ts(widths, seq)
        n_ct = len(cts)

        def body(*refs):
            ins = [r[...] for r in refs[:n_in]]
            ct = [r[...] for r in refs[n_in:n_in + n_ct]]
            if f_bwd is None:
                _, vjp = jax.vjp(f, *ins)
                grads = vjp(tuple(ct))
            else:
                grads = f_bwd(ins, ct)
            g_refs = refs[n_in + n_ct:]
            b, s = pl.program_id(0), pl.program_id(1)
            for i in range(n_row):
                g_refs[i][...] = grads[i]
            for i in range(n_row, n_row + n_batch):
                @pl.when(s == 0)
                def _(i=i):
                    g_refs[i][...] = grads[i]

                @pl.when(s > 0)
                def _(i=i):
                    g_refs[i][...] += grads[i]
            for i in range(n_row + n_batch, n_in):
                first = jnp.logical_and(b == 0, s == 0)

                @pl.when(first)
                def _(i=i):
                    g_refs[i][...] = grads[i]

                @pl.when(jnp.logical_not(first))
                def _(i=i):
                    g_refs[i][...] += grads[i]

        ct_specs = [pl.BlockSpec((None, ts, c.shape[2]), lambda b, s: (b, s, 0)) for c in cts]
        return pl.pallas_call(
            body, name=name + "_bwd", grid=(bsz, seq // ts), in_specs=in_specs(args, ts) + ct_specs,
            out_specs=in_specs(args, ts), out_shape=[jax.ShapeDtypeStruct(a.shape, F32) for a in args],
            compiler_params=_params(("arbitrary", "arbitrary")))(*args, *cts)

    @jax.custom_vjp
    def op(*args):
        return tuple(run_fwd(args))

    def fwd(*args):
        return tuple(run_fwd(args)), args

    def bwd(args, cts):
        return tuple(run_bwd(args, list(cts)))

    op.defvjp(fwd, bwd)
    return op


def _modnorm_f(h, scale, shift, g):
    y = h * lax.rsqrt(jnp.mean(h * h, axis=-1, keepdims=True) + EPS) * g
    return (y * (1.0 + scale) + shift,)


def _gate_res_f(h, y, gate):
    return (h + gate * y,)


def _relu2_f(a):
    return (jnp.square(jnp.maximum(a, 0.0)),)


def _gelu_skip_f(y, u, d):
    return (jax.nn.gelu(y + d * u),)


def _glu_f(vg):
    n = vg.shape[1] // 2
    return (vg[:, :n] * jax.nn.sigmoid(vg[:, n:]),)


def _glu_b(ins, cts):
    (vg,), (ct,) = ins, cts
    n = vg.shape[1] // 2
    sg = jax.nn.sigmoid(vg[:, n:])
    return (jnp.concatenate([ct * sg, ct * vg[:, :n] * sg * (1.0 - sg)], axis=1),)


def _combine_f(o1, o2, o3, l1, l2, l3):
    m = jnp.maximum(jnp.maximum(l1, l2), l3)
    e1, e2, e3 = jnp.exp(l1 - m), jnp.exp(l2 - m), jnp.exp(l3 - m)
    return ((e1 * o1 + e2 * o2 + e3 * o3) / (e1 + e2 + e3),)


def _final_loss_f(h, target, g):
    y = h * lax.rsqrt(jnp.mean(h * h, axis=-1, keepdims=True) + EPS) * g
    return (0.5 * jnp.mean(jnp.square(y - target), axis=-1, keepdims=True),)


modnorm = make_rowop("modnorm", _modnorm_f, 1, 2, 1)
gate_res = make_rowop("gate_res", _gate_res_f, 2, 1, 0)
relu2 = make_rowop("relu2", _relu2_f, 1, 0, 0)
gelu_skip = make_rowop("gelu_skip", _gelu_skip_f, 2, 0, 1)
glu = make_rowop("glu", _glu_f, 1, 0, 0, _glu_b)
combine = make_rowop("combine", _combine_f, 6, 0, 0)
final_loss = make_rowop("final_loss", _final_loss_f, 2, 0, 1)


SCAN_TB = 64
HALF = 512


def _scan_fwd_call(bu, a_re, a_im):
    bsz, rows, width = bu.shape
    seq = rows // 8
    tb = _tile(seq, SCAN_TB)

    def body(bu_ref, ar_ref, ai_ref, o_ref, carry):
        @pl.when(pl.program_id(0) == 0)
        def _():
            carry[...] = jnp.zeros_like(carry)

        ar, ai = ar_ref[...], ai_ref[...]

        def step(t, st):
            r0 = pl.multiple_of(t * 8, 8)
            new = []
            for b in range(bsz):
                sre, sim = st[2 * b], st[2 * b + 1]
                nre = ar * sre - ai * sim + bu_ref[b, pl.ds(r0, 8), pl.ds(0, HALF)]
                nim = ar * sim + ai * sre + bu_ref[b, pl.ds(r0, 8), pl.ds(HALF, HALF)]
                o_ref[b, pl.ds(r0, 8), pl.ds(0, HALF)] = nre
                o_ref[b, pl.ds(r0, 8), pl.ds(HALF, HALF)] = nim
                new += [nre, nim]
            return tuple(new)

        fin = lax.fori_loop(0, tb, step, tuple(carry[k] for k in range(2 * bsz)), unroll=4)
        for k in range(2 * bsz):
            carry[k] = fin[k]

    blk = pl.BlockSpec((bsz, tb * 8, width), lambda i: (0, i, 0))
    vec = pl.BlockSpec((8, HALF), lambda i: (0, 0))
    return pl.pallas_call(body, name="s5_scan_fwd", grid=(seq // tb,), in_specs=[blk, vec, vec], out_specs=blk,
                          out_shape=jax.ShapeDtypeStruct(bu.shape, F32),
                          scratch_shapes=[pltpu.VMEM((2 * bsz, 8, HALF), F32)],
                          compiler_params=_params(("arbitrary",)))(bu, a_re, a_im)


def _scan_bwd_call(g, st, a_re, a_im):
    bsz, rows, width = g.shape
    seq = rows // 8
    tb = _tile(seq, SCAN_TB)
    nt = seq // tb

    def body(g_ref, st_ref, prev_ref, ar_ref, ai_ref, db_ref, dar_ref, dai_ref, carry):
        i = pl.program_id(0)

        @pl.when(i == 0)
        def _():
            carry[...] = jnp.zeros_like(carry)
            dar_ref[...] = jnp.zeros_like(dar_ref)
            dai_ref[...] = jnp.zeros_like(dai_ref)

        ar, ai = ar_ref[...], ai_ref[...]

        def lam_step(b, r0, lre, lim):
            nre = g_ref[b, pl.ds(r0, 8), pl.ds(0, HALF)] + ar * lre + ai * lim
            nim = g_ref[b, pl.ds(r0, 8), pl.ds(HALF, HALF)] - ai * lre + ar * lim
            db_ref[b, pl.ds(r0, 8), pl.ds(0, HALF)] = nre
            db_ref[b, pl.ds(r0, 8), pl.ds(HALF, HALF)] = nim
            return nre, nim

        def step(k, c):
            t = tb - 1 - k
            r0 = pl.multiple_of(t * 8, 8)
            p0 = pl.multiple_of(t * 8 - 8, 8)
            lam, dar, dai = list(c[:2 * bsz]), c[2 * bsz], c[2 * bsz + 1]
            for b in range(bsz):
                nre, nim = lam_step(b, r0, lam[2 * b], lam[2 * b + 1])
                pre = st_ref[b, pl.ds(p0, 8), pl.ds(0, HALF)]
                pim = st_ref[b, pl.ds(p0, 8), pl.ds(HALF, HALF)]
                dar = dar + nre * pre + nim * pim
                dai = dai + nim * pre - nre * pim
                lam[2 * b], lam[2 * b + 1] = nre, nim
            return tuple(lam) + (dar, dai)

        zero = jnp.zeros((8, HALF), F32)
        init = tuple(carry[k] for k in range(2 * bsz)) + (zero, zero)
        c = lax.fori_loop(0, tb - 1, step, init, unroll=4)
        lam, dar, dai = list(c[:2 * bsz]), c[2 * bsz], c[2 * bsz + 1]
        keep = jnp.where(i == nt - 1, 0.0, 1.0)
        for b in range(bsz):
            nre, nim = lam_step(b, 0, lam[2 * b], lam[2 * b + 1])
            pre = prev_ref[b, :, pl.ds(0, HALF)] * keep
            pim = prev_ref[b, :, pl.ds(HALF, HALF)] * keep
            dar = dar + nre * pre + nim * pim
            dai = dai + nim * pre - nre * pim
            carry[2 * b], carry[2 * b + 1] = nre, nim
        dar_ref[...] += dar
        dai_ref[...] += dai

    blk = pl.BlockSpec((bsz, tb * 8, width), lambda i: (0, nt - 1 - i, 0))
    prev = pl.BlockSpec((bsz, 8, width), lambda i: (0, jnp.maximum((nt - 1 - i) * tb - 1, 0), 0))
    vec = pl.BlockSpec((8, HALF), lambda i: (0, 0))
    return pl.pallas_call(
        body, name="s5_scan_bwd", grid=(nt,), in_specs=[blk, blk, prev, vec, vec], out_specs=[blk, vec, vec],
        out_shape=[jax.ShapeDtypeStruct(g.shape, F32), jax.ShapeDtypeStruct((8, HALF), F32),
                   jax.ShapeDtypeStruct((8, HALF), F32)],
        scratch_shapes=[pltpu.VMEM((2 * bsz, 8, HALF), F32)],
        compiler_params=_params(("arbitrary",)))(g, st, st, a_re, a_im)


@jax.custom_vjp
def s5_scan(bu, a_re, a_im):
    return _scan_fwd_call(bu, a_re, a_im)


def _s5_scan_fwd(bu, a_re, a_im):
    st = _scan_fwd_call(bu, a_re, a_im)
    return st, (st, a_re, a_im)


def _s5_scan_bwd(res, g):
    st, a_re, a_im = res
    return tuple(_scan_bwd_call(g, st, a_re, a_im))


s5_scan.defvjp(_s5_scan_fwd, _s5_scan_bwd)


SSM_TB = 64
PITCH = SSM_TB + 8
PIECES = 8
CH = 128
NCH = 2 * HALF // LANES
NRE = NCH // 2


def _slab(t):
    return pl.ds(t, PIECES, stride=PITCH)


def _put_rows(ref, b, q, val, tb):
    for j in range(NCH):
        ref[b, j, pl.ds(q * PITCH, tb), :] = val[:, j * LANES:(j + 1) * LANES]


def _get_rows(ref, q, bsz, tb):
    return jnp.concatenate(
        [jnp.concatenate([ref[b, j, pl.ds(q * PITCH, tb), :] for j in range(NCH)], axis=1) for b in range(bsz)], axis=0)


def _chunks(a):
    return [a[:, j * LANES:(j + 1) * LANES] for j in range(a.shape[1] // LANES)]


def _ssm_fwd_call(u, b_mat, c_mat, a_re, a_im):
    bsz, seq, d = u.shape
    tb = SSM_TB
    nt = seq // tb
    assert seq % tb == 0 and d == PIECES * CH

    def body(u_ref, b_ref, c_ref, ar_ref, ai_ref, y_ref, st_ref, end_ref, bu, carry):
        @pl.when(pl.program_id(0) == 0)
        def _():
            carry[...] = jnp.zeros_like(carry)

        u2 = u_ref[...].reshape(bsz * tb, d).astype(BF16)
        for q in range(PIECES):
            r = lax.dot_general(u2[:, q * CH:(q + 1) * CH], b_ref[q], NN, preferred_element_type=F32)
            for b in range(bsz):
                _put_rows(bu, b, q, r[b * tb:(b + 1) * tb], tb)
        ar, ai = _chunks(ar_ref[...]), _chunks(ai_ref[...])
        st = [[carry[b, j] for j in range(NCH)] for b in range(bsz)]
        for t in range(tb):
            for b in range(bsz):
                for j in range(NRE):
                    sre, sim = st[b][j], st[b][NRE + j]
                    nre = ar[j] * sre - ai[j] * sim + bu[b, j, _slab(t), :]
                    nim = ar[j] * sim + ai[j] * sre + bu[b, NRE + j, _slab(t), :]
                    st_ref[b, j, _slab(t), :] = nre
                    st_ref[b, NRE + j, _slab(t), :] = nim
                    st[b][j], st[b][NRE + j] = nre, nim
        for b in range(bsz):
            for j in range(NCH):
                carry[b, j] = st[b][j]
                end_ref[b, j] = st[b][j]
        for q in range(PIECES):
            s2 = _get_rows(st_ref, q, bsz, tb).astype(BF16)
            r = lax.dot_general(s2, c_ref[q], NN, preferred_element_type=F32)
            for b in range(bsz):
                y_ref[b, :, q * CH:(q + 1) * CH] = r[b * tb:(b + 1) * tb]

    rows = pl.BlockSpec((bsz, tb, d), lambda i: (0, i, 0))
    vec = pl.BlockSpec((PIECES, HALF), lambda i: (0, 0))
    return pl.pallas_call(
        body, name="s5_ssm_fwd", grid=(nt,),
        in_specs=[rows, pl.BlockSpec((PIECES, CH, 2 * HALF), lambda i: (0, 0, 0)),
                  pl.BlockSpec((PIECES, 2 * HALF, CH), lambda i: (0, 0, 0)), vec, vec],
        out_specs=[rows, pl.BlockSpec((bsz, None, NCH, PIECES * PITCH, LANES), lambda i: (0, i, 0, 0, 0)),
                   pl.BlockSpec((bsz, None, NCH, PIECES, LANES), lambda i: (0, i, 0, 0, 0))],
        out_shape=[jax.ShapeDtypeStruct((bsz, seq, d), F32),
                   jax.ShapeDtypeStruct((bsz, nt, NCH, PIECES * PITCH, LANES), F32),
                   jax.ShapeDtypeStruct((bsz, nt, NCH, PIECES, LANES), F32)],
        scratch_shapes=[pltpu.VMEM((bsz, NCH, PIECES * PITCH, LANES), F32), pltpu.VMEM((bsz, NCH, PIECES, LANES), F32)],
        compiler_params=_params(("arbitrary",)))(u, b_mat, c_mat, a_re, a_im)


def _ssm_bwd_call(dy, u, st, ends, b_mat, c_mat, a_re, a_im):
    bsz, seq, d = u.shape
    tb = SSM_TB
    nt = seq // tb

    def body(dy_ref, u_ref, st_ref, prev_ref, b_ref, c_ref, ar_ref, ai_ref, du_ref, db_hbm, dc_hbm, dar_ref, dai_ref,
             lam, carry, db_acc, dc_acc):
        i = pl.program_id(0)

        @pl.when(i == 0)
        def _():
            carry[...] = jnp.zeros_like(carry)
            db_acc[...] = jnp.zeros_like(db_acc)
            dc_acc[...] = jnp.zeros_like(dc_acc)
            dar_ref[...] = jnp.zeros_like(dar_ref)
            dai_ref[...] = jnp.zeros_like(dai_ref)

        dy2 = dy_ref[...].reshape(bsz * tb, d).astype(BF16)
        u2 = u_ref[...].reshape(bsz * tb, d).astype(BF16)
        for q in range(PIECES):
            dyq = dy2[:, q * CH:(q + 1) * CH]
            g = lax.dot_general(dyq, c_ref[q], NT, preferred_element_type=F32)
            for b in range(bsz):
                _put_rows(lam, b, q, g[b * tb:(b + 1) * tb], tb)
            s2 = _get_rows(st_ref, q, bsz, tb).astype(BF16)
            dc_acc[q] += lax.dot_general(s2, dyq, TN, preferred_element_type=F32)

        ar, ai = _chunks(ar_ref[...]), _chunks(ai_ref[...])
        keep = jnp.where(i == nt - 1, 0.0, 1.0)
        lm = [[carry[b, j] for j in range(NCH)] for b in range(bsz)]
        dar = [jnp.zeros((PIECES, LANES), F32) for _ in range(NRE)]
        dai = [jnp.zeros((PIECES, LANES), F32) for _ in range(NRE)]
        for t in range(tb - 1, -1, -1):
            for b in range(bsz):
                for j in range(NRE):
                    lre, lim = lm[b][j], lm[b][NRE + j]
                    nre = lam[b, j, _slab(t), :] + ar[j] * lre + ai[j] * lim
                    nim = lam[b, NRE + j, _slab(t), :] - ai[j] * lre + ar[j] * lim
                    lam[b, j, _slab(t), :] = nre
                    lam[b, NRE + j, _slab(t), :] = nim
                    if t > 0:
                        pre, pim = st_ref[b, j, _slab(t - 1), :], st_ref[b, NRE + j, _slab(t - 1), :]
                    else:
                        pre, pim = prev_ref[b, j] * keep, prev_ref[b, NRE + j] * keep
                    dar[j] = dar[j] + nre * pre + nim * pim
                    dai[j] = dai[j] + nim * pre - nre * pim
                    lm[b][j], lm[b][NRE + j] = nre, nim
        for b in range(bsz):
            for j in range(NCH):
                carry[b, j] = lm[b][j]
        dar_ref[...] += jnp.concatenate(dar, axis=1)
        dai_ref[...] += jnp.concatenate(dai, axis=1)

        for q in range(PIECES):
            l2 = _get_rows(lam, q, bsz, tb).astype(BF16)
            r = lax.dot_general(l2, b_ref[q], NT, preferred_element_type=F32)
            for b in range(bsz):
                du_ref[b, :, q * CH:(q + 1) * CH] = r[b * tb:(b + 1) * tb]
            db_acc[q] += lax.dot_general(u2[:, q * CH:(q + 1) * CH], l2, TN, preferred_element_type=F32)

        @pl.when(i == nt - 1)
        def _():
            pltpu.sync_copy(db_acc, db_hbm)
            pltpu.sync_copy(dc_acc, dc_hbm)

    rows = pl.BlockSpec((bsz, tb, d), lambda i: (0, nt - 1 - i, 0))
    vec = pl.BlockSpec((PIECES, HALF), lambda i: (0, 0))
    hbm = pl.BlockSpec(memory_space=pl.ANY)
    return pl.pallas_call(
        body, name="s5_ssm_bwd", grid=(nt,),
        in_specs=[rows, rows,
                  pl.BlockSpec((bsz, None, NCH, PIECES * PITCH, LANES), lambda i: (0, nt - 1 - i, 0, 0, 0)),
                  pl.BlockSpec((bsz, None, NCH, PIECES, LANES), lambda i: (0, jnp.maximum(nt - 2 - i, 0), 0, 0, 0)),
                  pl.BlockSpec((PIECES, CH, 2 * HALF), lambda i: (0, 0, 0)),
                  pl.BlockSpec((PIECES, 2 * HALF, CH), lambda i: (0, 0, 0)), vec, vec],
        out_specs=[rows, hbm, hbm, vec, vec],
        out_shape=[jax.ShapeDtypeStruct((bsz, seq, d), F32), jax.ShapeDtypeStruct((PIECES, CH, 2 * HALF), F32),
                   jax.ShapeDtypeStruct((PIECES, 2 * HALF, CH), F32), jax.ShapeDtypeStruct((PIECES, HALF), F32),
                   jax.ShapeDtypeStruct((PIECES, HALF), F32)],
        scratch_shapes=[pltpu.VMEM((bsz, NCH, PIECES * PITCH, LANES), F32), pltpu.VMEM((bsz, NCH, PIECES, LANES), F32),
                        pltpu.VMEM((PIECES, CH, 2 * HALF), F32), pltpu.VMEM((PIECES, 2 * HALF, CH), F32)],
        compiler_params=_params(("arbitrary",)))(dy, u, st, ends, b_mat, c_mat, a_re, a_im)


@jax.custom_vjp
def s5_ssm(u, b_mat, c_mat, a_re, a_im):
    return _s5_ssm_fwd(u, b_mat, c_mat, a_re, a_im)[0]


def _s5_ssm_fwd(u, b_mat, c_mat, a_re, a_im):
    b16, c16 = b_mat.astype(BF16), c_mat.astype(BF16)
    y, st, ends = _ssm_fwd_call(u, b16, c16, a_re, a_im)
    return y, (u, st, ends, b16, c16, a_re, a_im)


def _s5_ssm_bwd(res, dy):
    return tuple(_ssm_bwd_call(dy, *res))


s5_ssm.defvjp(_s5_ssm_fwd, _s5_ssm_bwd)


ATT_HW = 2 * HEAD
ATT_HB = 1024 // ATT_HW


def _branch_geometry(dil, seq):
    sub = seq // dil
    assert sub % QBLK == 0
    return sub // QBLK


def _drows(dil, r, start, size):
    if dil == 1:
        return pl.ds(start, size)
    return pl.ds(r + start * dil, size, stride=dil)


def _masks():
    qi = lax.broadcasted_iota(jnp.int32, (QBLK, 2 * QBLK), 0)
    kj = lax.broadcasted_iota(jnp.int32, (QBLK, 2 * QBLK), 1) - QBLK
    dist = qi - kj
    band = jnp.logical_and(dist >= 0, dist <= QBLK)
    ci = lax.broadcasted_iota(jnp.int32, (QBLK, QBLK), 0)
    cj = lax.broadcasted_iota(jnp.int32, (QBLK, QBLK), 1)
    return band, ci >= cj


def _attn_specs(i, seq):
    q_spec = pl.BlockSpec((None, seq, ATT_HW), lambda b, h: (b, 0, i * ATT_HB + h))
    k_spec = pl.BlockSpec((None, seq, ATT_HW), lambda b, h: (b, 0, i * ATT_HB + h))
    v_spec = pl.BlockSpec((None, seq, ATT_HW), lambda b, h: (b, 0, (3 + i) * ATT_HB + h))
    o_spec = pl.BlockSpec((None, seq, ATT_HW), lambda b, h: (b, 0, h))
    return q_spec, k_spec, v_spec, o_spec


def _attn_fwd_call(q_all, kv_all, i, dil):
    bsz, seq, _ = q_all.shape
    nb = _branch_geometry(dil, seq)
    scale = HEAD ** -0.5

    def body(q_ref, k_ref, v_ref, o_ref, l_ref):
        band, causal = _masks()

        def one(r, n, mask):
            qs = _drows(dil, r, n * QBLK, QBLK)
            ks = qs if n == 0 else _drows(dil, r, (n - 1) * QBLK, 2 * QBLK)
            q2 = q_ref[qs, :].astype(BF16)
            k2 = k_ref[ks, :].astype(BF16)
            v2 = v_ref[ks, :].astype(BF16)
            outs, lses = [], []
            for h in range(ATT_HW // HEAD):
                ls = slice(h * HEAD, (h + 1) * HEAD)
                s = lax.dot_general(q2[:, ls], k2[:, ls], NT, preferred_element_type=F32) * scale
                s = jnp.where(mask, s, NEG)
                m = jnp.max(s, axis=-1, keepdims=True)
                p = jnp.exp(s - m)
                den = jnp.sum(p, axis=-1, keepdims=True)
                outs.append(lax.dot_general(p.astype(BF16), v2[:, ls], NN, preferred_element_type=F32) / den)
                lses.append(jnp.broadcast_to(m + jnp.log(den), (QBLK, HEAD)))
            o_ref[qs, :] = jnp.concatenate(outs, axis=1)
            l_ref[qs, :] = jnp.concatenate(lses, axis=1)

        for r in range(dil):
            for n in range(nb):
                one(r, n, causal if n == 0 else band)

    q_spec, k_spec, v_spec, o_spec = _attn_specs(i, seq)
    shape = jax.ShapeDtypeStruct((bsz, seq, 1024), F32)
    return pl.pallas_call(
        body, name="attn_fwd_d%d" % dil, grid=(bsz, ATT_HB), in_specs=[q_spec, k_spec, v_spec],
        out_specs=[o_spec, o_spec], out_shape=[shape, shape],
        compiler_params=_params(("parallel", "parallel")))(q_all, kv_all, kv_all)


def _attn_bwd_call(q_all, kv_all, o, l, do, dl, i, dil, grads):
    bsz, seq, _ = q_all.shape
    nb = _branch_geometry(dil, seq)
    scale = HEAD ** -0.5
    first = grads is None

    def body(q_ref, k_ref, v_ref, o_ref, l_ref, do_ref, dl_ref, *rest):
        dq_ref, dk_ref, dv_ref = rest[-3:]
        band, causal = _masks()

        def one(r, n, mask):
            qs = _drows(dil, r, n * QBLK, QBLK)
            ks = qs if n == 0 else _drows(dil, r, (n - 1) * QBLK, 2 * QBLK)
            q2 = q_ref[qs, :].astype(BF16)
            k2 = k_ref[ks, :].astype(BF16)
            v2 = v_ref[ks, :].astype(BF16)
            l2, do2 = l_ref[qs, :], do_ref[qs, :]
            t2 = dl_ref[qs, :] - do2 * o_ref[qs, :]
            dqs, dks, dvs = [], [], []
            for h in range(ATT_HW // HEAD):
                ls = slice(h * HEAD, (h + 1) * HEAD)
                q, k, v = q2[:, ls], k2[:, ls], v2[:, ls]
                s = lax.dot_general(q, k, NT, preferred_element_type=F32) * scale
                s = jnp.where(mask, s, NEG)
                p = jnp.exp(s - l2[:, h * HEAD:h * HEAD + 1])
                row = jnp.sum(t2[:, ls], axis=-1, keepdims=True)
                d_ob = do2[:, ls].astype(BF16)
                dp = lax.dot_general(d_ob, v, NT, preferred_element_type=F32)
                ds = (p * (dp + row)).astype(BF16)
                dqs.append(lax.dot_general(ds, k, NN, preferred_element_type=F32) * scale)
                dks.append(lax.dot_general(ds, q, TN, preferred_element_type=F32) * scale)
                dvs.append(lax.dot_general(p.astype(BF16), d_ob, TN, preferred_element_type=F32))
            dq_ref[qs, :] = jnp.concatenate(dqs, axis=1)
            return jnp.concatenate(dks, axis=1), jnp.concatenate(dvs, axis=1)

        for r in range(dil):
            dk_cur, dv_cur = one(r, 0, causal)
            for n in range(1, nb):
                dk, dv = one(r, n, band)
                ks = _drows(dil, r, (n - 1) * QBLK, QBLK)
                dk_ref[ks, :] = dk_cur + dk[:QBLK]
                dv_ref[ks, :] = dv_cur + dv[:QBLK]
                dk_cur, dv_cur = dk[QBLK:], dv[QBLK:]
            ks = _drows(dil, r, (nb - 1) * QBLK, QBLK)
            dk_ref[ks, :] = dk_cur
            dv_ref[ks, :] = dv_cur

    q_spec, k_spec, v_spec, o_spec = _attn_specs(i, seq)
    in_specs = [q_spec, k_spec, v_spec, o_spec, o_spec, o_spec, o_spec]
    args = [q_all, kv_all, kv_all, o, l, do, dl]
    aliases = {}
    if not first:
        in_specs += [pl.BlockSpec(memory_space=pl.ANY)] * 3
        args += list(grads)
        aliases = {7: 0, 8: 1, 9: 2}
    shape = jax.ShapeDtypeStruct((bsz, seq, 3 * 1024), F32)
    return pl.pallas_call(
        body, name="attn_bwd_d%d" % dil, grid=(bsz, ATT_HB), in_specs=in_specs,
        out_specs=[q_spec, q_spec, q_spec], out_shape=[shape, shape, shape], input_output_aliases=aliases,
        compiler_params=_params(("parallel", "parallel")))(*args)


@jax.custom_vjp
def attn_branches(q_all, kv_all):
    return _attn_branches_fwd(q_all, kv_all)[0]


def _attn_branches_fwd(q_all, kv_all):
    outs = []
    for i, dil in enumerate(BRANCH_DIL):
        outs += list(_attn_fwd_call(q_all, kv_all, i, dil))
    return tuple(outs), (q_all, kv_all, tuple(outs))


def _attn_branches_bwd(res, cts):
    q_all, kv_all, outs = res
    grads = None
    for i, dil in enumerate(BRANCH_DIL):
        grads = _attn_bwd_call(q_all, kv_all, outs[2 * i], outs[2 * i + 1], cts[2 * i], cts[2 * i + 1], i, dil, grads)
    dq, dk, dv = grads
    return dq, jnp.concatenate([dk, dv], axis=-1)


attn_branches.defvjp(_attn_branches_fwd, _attn_branches_bwd)


def ada_fwd(c_all, w, b):
    n, d, cs = w.shape
    nb = c_all.shape[0]

    def body(c_ref, w_ref, b_ref, o_ref):
        a = jax.nn.silu(c_ref[...]).astype(BF16)
        o_ref[...] = lax.dot_general(a, w_ref[...].astype(BF16), NN, preferred_element_type=F32) + b_ref[...]

    return pl.pallas_call(
        body, name="ada_fwd", grid=(n,),
        in_specs=[pl.BlockSpec((nb, d), lambda i: (0, 0)), pl.BlockSpec((None, d, cs), lambda i: (i, 0, 0)),
                  pl.BlockSpec((None, 1, cs), lambda i: (i, 0, 0))],
        out_specs=pl.BlockSpec((None, nb, cs), lambda i: (i, 0, 0)),
        out_shape=jax.ShapeDtypeStruct((n, nb, cs), F32), compiler_params=_params(("parallel",)))(c_all, w, b)


def ada_bwd(c_all, dm):
    n, nb, cs = dm.shape
    d = c_all.shape[1]

    def body(c_ref, dm_ref, dw_ref, db_ref):
        a = jax.nn.silu(c_ref[...]).astype(BF16)
        g = dm_ref[...]
        dw_ref[...] = lax.dot_general(a, g.astype(BF16), TN, preferred_element_type=F32)
        db_ref[...] = jnp.sum(g, axis=0, keepdims=True)

    return pl.pallas_call(
        body, name="ada_bwd", grid=(n,),
        in_specs=[pl.BlockSpec((nb, d), lambda i: (0, 0)), pl.BlockSpec((None, nb, cs), lambda i: (i, 0, 0))],
        out_specs=[pl.BlockSpec((None, d, cs), lambda i: (i, 0, 0)), pl.BlockSpec((None, 1, cs), lambda i: (i, 0, 0))],
        out_shape=[jax.ShapeDtypeStruct((n, d, cs), F32), jax.ShapeDtypeStruct((n, 1, cs), F32)],
        compiler_params=_params(("parallel",)))(c_all, dm)


def all_gather(x, name):
    m, n = x.shape

    def body(x_ref, out_ref, send_sems, recv_sems, local_sem):
        px, py, pc = lax.axis_index("x"), lax.axis_index("y"), lax.axis_index("c")
        me, sibling = (px, py, pc), (px, py, 1 - pc)
        chips = [(1 - px, py), (px, 1 - py), (1 - px, 1 - py)]

        def rows(qx, qy, qc):
            return out_ref.at[pl.ds((4 * qx + 2 * qy + qc) * m, m), :]

        def copy(k, block, to, src=None):
            return pltpu.make_async_remote_copy(
                src_ref=rows(*block) if src is None else src, dst_ref=rows(*block),
                send_sem=send_sems.at[k], recv_sem=recv_sems.at[k], device_id=to,
                device_id_type=pl.DeviceIdType.MESH)

        mine = pltpu.make_async_copy(x_ref, rows(*me), local_sem)
        mine.start()
        first = [copy(0, me, sibling, src=x_ref)]
        first += [copy(1 + j, me, (*chip, pc), src=x_ref) for j, chip in enumerate(chips)]
        for cp in first:
            cp.start()
        passed = [copy(4 + j, (*chip, pc), sibling) for j, chip in enumerate(chips)]
        for j, chip in enumerate(chips):
            copy(1 + j, (*chip, pc), me).wait_recv()
            passed[j].start()
        copy(0, sibling, me).wait_recv()
        for j, chip in enumerate(chips):
            copy(4 + j, (*chip, 1 - pc), me).wait_recv()
        for cp in first + passed:
            cp.wait_send()
        mine.wait()

    out = pl.pallas_call(
        body, name=name, out_shape=jax.ShapeDtypeStruct((N_DEV * m, n), x.dtype),
        in_specs=[pl.BlockSpec(memory_space=pl.ANY)], out_specs=pl.BlockSpec(memory_space=pl.ANY),
        scratch_shapes=[pltpu.SemaphoreType.DMA((7,)), pltpu.SemaphoreType.DMA((7,)), pltpu.SemaphoreType.DMA(())],
    )(x)
    return out.reshape(N_DEV, m, n)


def exchange_partials(p, name):
    _, m, n = p.shape

    def body(p_ref, out_ref, send_sems, recv_sems, local_sem):
        px, py, pc = lax.axis_index("x"), lax.axis_index("y"), lax.axis_index("c")
        me = 4 * px + 2 * py + pc
        mine = pltpu.make_async_copy(p_ref.at[me], out_ref.at[me], local_sem)
        mine.start()
        copies = []
        for k in range(1, N_DEV):
            qx = 1 - px if k & 4 else px
            qy = 1 - py if k & 2 else py
            qc = 1 - pc if k & 1 else pc
            cp = pltpu.make_async_remote_copy(
                src_ref=p_ref.at[4 * qx + 2 * qy + qc], dst_ref=out_ref.at[me],
                send_sem=send_sems.at[k - 1], recv_sem=recv_sems.at[k - 1], device_id=(qx, qy, qc),
                device_id_type=pl.DeviceIdType.MESH)
            cp.start()
            copies.append(cp)
        for cp in copies:
            cp.wait()
        mine.wait()

    return pl.pallas_call(
        body, name=name, out_shape=jax.ShapeDtypeStruct(p.shape, p.dtype),
        in_specs=[pl.BlockSpec(memory_space=pl.ANY)], out_specs=pl.BlockSpec(memory_space=pl.ANY),
        scratch_shapes=[pltpu.SemaphoreType.DMA((7,)), pltpu.SemaphoreType.DMA((7,)), pltpu.SemaphoreType.DMA(())],
    )(p)


def sum_slots(g):
    _, r, n = g.shape
    tr = _tile(r, max(16, (256 * 1024) // n))

    def body(g_ref, o_ref):
        acc = g_ref[0].astype(F32)
        for s in range(1, N_DEV):
            acc = acc + g_ref[s].astype(F32)
        o_ref[...] = acc

    return pl.pallas_call(
        body, name="sum_slots", grid=(r // tr,), in_specs=[pl.BlockSpec((N_DEV, tr, n), lambda i: (0, i, 0))],
        out_specs=pl.BlockSpec((tr, n), lambda i: (i, 0)), out_shape=jax.ShapeDtypeStruct((r, n), F32),
        compiler_params=_params(("parallel",)))(g)


def adamw(w, m, v, g, name):
    nl, r, n = w.shape
    partials = g.ndim == 4
    tr = _tile(r, max(8, (256 * 1024) // n))
    c1 = 1.0 / (1.0 - ADAM_B1 ** ADAM_STEP)
    c2 = 1.0 / (1.0 - ADAM_B2 ** ADAM_STEP)

    def body(w_ref, m_ref, v_ref, g_ref, go_ref, d_ref, mo_ref, vo_ref):
        if partials:
            grad = g_ref[0].astype(F32)
            for s in range(1, N_DEV):
                grad = grad + g_ref[s].astype(F32)
        else:
            grad = g_ref[...]
        m_new = ADAM_B1 * m_ref[...] + (1.0 - ADAM_B1) * grad
        v_new = ADAM_B2 * v_ref[...] + (1.0 - ADAM_B2) * grad * grad
        go_ref[...] = grad
        mo_ref[...] = m_new
        vo_ref[...] = v_new
        d_ref[...] = -ADAM_LR * ((m_new * c1) / (jnp.sqrt(v_new * c2) + ADAM_EPS) + ADAM_WD * w_ref[...])

    spec = pl.BlockSpec((None, tr, n), lambda l, i: (l, i, 0))
    g_spec = pl.BlockSpec((None, N_DEV, tr, n), lambda l, i: (l, 0, i, 0)) if partials else spec
    shape = jax.ShapeDtypeStruct(w.shape, F32)
    return pl.pallas_call(
        body, name=name, grid=(nl, r // tr), in_specs=[spec, spec, spec, g_spec], out_specs=[spec] * 4,
        out_shape=[shape] * 4, compiler_params=_params(("parallel", "parallel")))(w, m, v, g)


def pack(arrays):
    flat = jnp.concatenate([a.reshape(-1).astype(F32) for a in arrays])
    rows = -(-flat.shape[0] // LANES)
    mult = 8 if rows <= 512 else 512
    rows = -(-rows // mult) * mult
    flat = jnp.pad(flat, (0, rows * LANES - flat.shape[0]))
    return flat.reshape(rows, LANES)


def unpack(slab, shapes):
    flat = slab.reshape(slab.shape[:-2] + (-1,))
    out, off = [], 0
    for s in shapes:
        size = math.prod(s)
        out.append(flat[..., off:off + size].reshape(flat.shape[:-1] + tuple(s)))
        off += size
    return out


def _s5_discretize(lam_re, lam_im, log_dt, b_re, b_im, c_re, c_im):
    dt = jnp.exp(log_dt)[:, None]
    xr, th = lam_re * dt, lam_im * dt
    er = jnp.exp(xr)
    a_re, a_im = er * jnp.cos(th), er * jnp.sin(th)
    am1 = jnp.expm1(xr) * jnp.cos(th) - 2.0 * jnp.square(jnp.sin(0.5 * th))
    den = lam_re * lam_re + lam_im * lam_im
    fr = (am1 * lam_re + a_im * lam_im) / den
    fi = (a_im * lam_re - am1 * lam_im) / den
    bb_re = fr[..., None] * b_re - fi[..., None] * b_im
    bb_im = fr[..., None] * b_im + fi[..., None] * b_re
    eye = jnp.eye(8, dtype=F32)

    def pack_b(bb):
        return jnp.einsum('qgpc,gh->qgchp', bb.reshape(8, 8, 64, 16), eye).reshape(8, 128, 512)

    def pack_c(cc):
        return jnp.einsum('qgcp,gh->qgphc', cc.reshape(8, 8, 16, 64), eye).reshape(8, 512, 128)

    b_mat = jnp.concatenate([pack_b(bb_re), pack_b(bb_im)], axis=-1)
    c_mat = jnp.concatenate([pack_c(c_re), pack_c(-c_im)], axis=1)
    return a_re.reshape(8, HALF), a_im.reshape(8, HALF), b_mat, c_mat


WEIGHT_ORDER = (("ssm_w_glu", 0, True), ("mlp_w1", 0, True), ("mlp_w2", 0, False),
                ("ssm_w_glu", 1, True), ("mlp_w1", 1, True), ("mlp_w2", 1, False), ("w_kv", None, True),
                ("attn_w_q", 0, True), ("attn_w_o", 0, False), ("mlp_w1", 2, True), ("mlp_w2", 2, False),
                ("attn_w_q", 1, True), ("attn_w_o", 1, False), ("mlp_w1", 3, True), ("mlp_w2", 3, False))


class _WeightChain:
    def __init__(self, first, upcoming):
        self.cur, self.upcoming, self.used = first, list(upcoming), 0

    def matmul(self, x, name):
        assert WEIGHT_ORDER[self.used][0] == name
        col = WEIGHT_ORDER[self.used][2]
        self.used += 1
        w = self.cur if col else self.cur.reshape(-1, self.cur.shape[-1])
        if not self.upcoming:
            return (linear_col if col else linear)(x, w)
        y, self.cur = (linear_col_next if col else linear_next)(x, w, self.upcoming.pop(0))
        return y


def _local_loss(diff, x, target, first_weight, shards):
    bsz, seq, d = x.shape
    t = bsz * seq
    rows3 = lambda a: a.reshape(bsz, seq, a.shape[-1])
    rows2 = lambda a: a.reshape(t, a.shape[-1])
    mods, kvmod, ln_g, ssm_d, kv_g, final_g = (diff[k] for k in ("mods", "kvmod", "ln_g", "ssm_d", "kv_g", "final_g"))
    chain = _WeightChain(first_weight, shards)
    h = x
    kv_all = None
    for layer in range(4):
        if layer == 2:
            shift, scale = kvmod[:, None, :d], kvmod[:, None, d:]
            (ukv,) = modnorm(h, scale, shift, kv_g[None])
            kv_all = rows3(chain.matmul(rows2(ukv), "w_kv"))
        mod = mods[2 * layer]
        shift, scale, gate = mod[:, None, :d], mod[:, None, d:2 * d], mod[:, None, 2 * d:]
        (u,) = modnorm(h, scale, shift, ln_g[layer, 0][None])
        if layer < 2:
            a_re, a_im, b_mat, c_mat = _s5_discretize(*(diff[k][layer] for k in (
                "ssm_lam_re", "ssm_lam_im", "ssm_log_dt", "ssm_b_re", "ssm_b_im", "ssm_c_re", "ssm_c_im")))
            y0 = s5_ssm(u, b_mat, c_mat, a_re, a_im)
            (z,) = gelu_skip(y0, u, ssm_d[layer][None])
            (y,) = glu(rows3(chain.matmul(rows2(z), "ssm_w_glu")))
        else:
            q_all = rows3(chain.matmul(rows2(u), "attn_w_q"))
            o1, l1, o2, l2, o3, l3 = attn_branches(q_all, kv_all)
            (o,) = combine(o1, o2, o3, l1, l2, l3)
            y = rows3(chain.matmul(rows2(o), "attn_w_o"))
        (h,) = gate_res(h, y, gate)
        mod = mods[2 * layer + 1]
        shift, scale, gate = mod[:, None, :d], mod[:, None, d:2 * d], mod[:, None, 2 * d:]
        (u,) = modnorm(h, scale, shift, ln_g[layer, 1][None])
        (act,) = relu2(rows3(chain.matmul(rows2(u), "mlp_w1")))
        y = rows3(chain.matmul(rows2(act), "mlp_w2"))
        (h,) = gate_res(h, y, gate)
    (row_loss,) = final_loss(h, target, final_g[None])
    return jnp.sum(row_loss)


SSM_NAMES = ("ssm_lam_re", "ssm_lam_im", "ssm_log_dt", "ssm_b_re", "ssm_b_im", "ssm_c_re", "ssm_c_im")
ARG_NAMES = ("x", "c", "ln_g", "ada_w", "ada_b") + SSM_NAMES + (
    "ssm_d", "ssm_w_glu", "kv_g", "kv_ada_w", "kv_ada_b", "w_kv", "attn_w_q", "attn_w_o", "mlp_w1", "mlp_w2", "final_g")
WEIGHT_NAMES = ARG_NAMES[2:]


def kernel(x, c, ln_g, ada_w, ada_b, ssm_lam_re, ssm_lam_im, ssm_log_dt, ssm_b_re, ssm_b_im, ssm_c_re, ssm_c_im, ssm_d, ssm_w_glu, kv_g, kv_ada_w, kv_ada_b, w_kv, attn_w_q, attn_w_o, mlp_w1, mlp_w2, final_g, loss_target, m_ln_g, m_ada_w, m_ada_b, m_ssm_lam_re, m_ssm_lam_im, m_ssm_log_dt, m_ssm_b_re, m_ssm_b_im, m_ssm_c_re, m_ssm_c_im, m_ssm_d, m_ssm_w_glu, m_kv_g, m_kv_ada_w, m_kv_ada_b, m_w_kv, m_attn_w_q, m_attn_w_o, m_mlp_w1, m_mlp_w2, m_final_g, v_ln_g, v_ada_w, v_ada_b, v_ssm_lam_re, v_ssm_lam_im, v_ssm_log_dt, v_ssm_b_re, v_ssm_b_im, v_ssm_c_re, v_ssm_c_im, v_ssm_d, v_ssm_w_glu, v_kv_g, v_kv_ada_w, v_kv_ada_b, v_w_kv, v_attn_w_q, v_attn_w_o, v_mlp_w1, v_mlp_w2, v_final_g):
    args = locals()
    w = {n: args[n] for n in WEIGHT_NAMES}
    mom = {n: args["m_" + n] for n in WEIGHT_NAMES}
    var = {n: args["v_" + n] for n in WEIGHT_NAMES}
    bsz, seq, d = x.shape
    me = 4 * lax.axis_index("x") + 2 * lax.axis_index("y") + lax.axis_index("c")

    small = all_gather(pack([c, ln_g, ssm_d]), "gather_small")
    c_parts, ln_parts, d_parts = unpack(small, [c.shape, ln_g.shape, ssm_d.shape])
    c_all = c_parts.reshape(N_DEV * bsz, d)
    ln_full = jnp.moveaxis(ln_parts, 0, 2).reshape(4, 2, d)
    d_full = jnp.moveaxis(d_parts, 0, 1).reshape(2, d)

    cs = ada_w.shape[-1]
    mod_cols = ada_fwd(c_all, ada_w.reshape(8, d, cs), ada_b.reshape(8, 1, cs))
    kcs = kv_ada_w.shape[-1]
    kv_cols = ada_fwd(c_all, kv_ada_w[None], jnp.zeros((1, 1, kcs), F32))
    mod_g = all_gather(mod_cols.reshape(8 * N_DEV * bsz, cs), "gather_mod")
    kv_g_all = all_gather(kv_cols.reshape(N_DEV * bsz, kcs), "gather_kvmod")
    mods_all = jnp.moveaxis(mod_g.reshape(N_DEV, 8, N_DEV * bsz, cs), 0, 2).reshape(8, N_DEV * bsz, N_DEV * cs)
    kvmod_all = jnp.moveaxis(kv_g_all, 0, 1).reshape(N_DEV * bsz, N_DEV * kcs) + kv_ada_b[None]
    mods = lax.dynamic_slice_in_dim(mods_all, me * bsz, bsz, axis=1)
    kvmod = lax.dynamic_slice_in_dim(kvmod_all, me * bsz, bsz, axis=0)

    shards = [w[n] if l is None else w[n][l] for n, l, _ in WEIGHT_ORDER]
    first_weight = all_gather(shards[0].astype(BF16), "gather_first_weight")

    diff = {"mods": mods, "kvmod": kvmod, "ln_g": ln_full, "ssm_d": d_full, "kv_g": kv_g, "final_g": final_g}
    diff.update({n: w[n] for n in SSM_NAMES})
    loss_local, (g_diff, grad_x, g_first, g_shards) = jax.value_and_grad(_local_loss, argnums=(0, 1, 3, 4))(
        diff, x, loss_target, first_weight, shards[1:])
    loss = lax.psum(loss_local, AXES)
    g_shards = [sum_slots(exchange_partials(g_first, "exchange_first_weight"))] + list(g_shards)

    dmod = all_gather(pack([g_diff["mods"], g_diff["kvmod"]]), "gather_dmod")
    dmods_p, dkv_p = unpack(dmod, [g_diff["mods"].shape, g_diff["kvmod"].shape])
    dmods_all = jnp.moveaxis(dmods_p, 0, 1).reshape(8, N_DEV * bsz, 3 * d)
    dkv_all = dkv_p.reshape(N_DEV * bsz, 2 * d)
    g_ada_w, g_ada_b = ada_bwd(c_all, lax.dynamic_slice_in_dim(dmods_all, me * cs, cs, axis=2))
    g_kv_ada_w, _ = ada_bwd(c_all, lax.dynamic_slice_in_dim(dkv_all, me * kcs, kcs, axis=1)[None])
    _, g_kv_ada_b = ada_bwd(c_all, dkv_all[None])

    small_names = ("ln_g", "ssm_d", "kv_g", "final_g") + SSM_NAMES
    partial = all_gather(pack([g_diff[n] for n in small_names]), "gather_small_grads")
    totals = unpack(sum_slots(partial), [g_diff[n].shape for n in small_names])
    g_small = dict(zip(small_names, totals))
    g_small["ln_g"] = lax.dynamic_slice_in_dim(g_small["ln_g"], me * ln_g.shape[-1], ln_g.shape[-1], axis=2)
    g_small["ssm_d"] = lax.dynamic_slice_in_dim(g_small["ssm_d"], me * ssm_d.shape[-1], ssm_d.shape[-1], axis=1)
    g_small["ada_b"] = g_ada_b.reshape(ada_b.shape)
    g_small["kv_ada_b"] = g_kv_ada_b.reshape(kv_ada_b.shape)

    out = {}

    def put(name, res, shape):
        for kind, a in zip(("grad_", "delta_", "new_m_", "new_v_"), res):
            out[kind + name] = a.reshape(shape)

    for name in ("ssm_w_glu", "w_kv", "attn_w_q", "attn_w_o", "mlp_w1", "mlp_w2"):
        grads = jnp.stack([g for g, (n, _, _) in zip(g_shards, WEIGHT_ORDER) if n == name])
        v3 = lambda a: a.reshape(grads.shape)
        put(name, adamw(v3(w[name]), v3(mom[name]), v3(var[name]), grads, "adamw_" + name), w[name].shape)
    v3 = lambda a: a.reshape(8, d, cs)
    put("ada_w", adamw(v3(ada_w), v3(m_ada_w), v3(v_ada_w), g_ada_w, "adamw_ada_w"), ada_w.shape)
    put("kv_ada_w", adamw(kv_ada_w[None], m_kv_ada_w[None], v_kv_ada_w[None], g_kv_ada_w, "adamw_kv_ada_w"), kv_ada_w.shape)
    names = small_names + ("ada_b", "kv_ada_b")
    res = adamw(pack([w[n] for n in names])[None], pack([mom[n] for n in names])[None],
                pack([var[n] for n in names])[None], pack([g_small[n] for n in names])[None], "adamw_small")
    for kind, slab in zip(("grad_", "delta_", "new_m_", "new_v_"), res):
        for n, a in zip(names, unpack(slab[0], [w[n].shape for n in names])):
            out[kind + n] = a

    result = [loss, grad_x]
    for kind in ("grad_", "delta_", "new_m_", "new_v_"):
        result += [out[kind + n] for n in WEIGHT_NAMES]
    return tuple(result)
```

```python
import functools
import math

import jax
import jax.numpy as jnp
from jax import lax
from jax.experimental import pallas as pl
from jax.experimental.pallas import tpu as pltpu

F32 = jnp.float32
BF16 = jnp.bfloat16
N_DEV = 8
AXES = ("x", "y", "c")
V7X_VMEM_LIMIT = 56 * 1024 * 1024
LANES = 128
EPS = 1e-6
NEG = -1e30
HEAD = 64
QBLK = 128
BRANCH_DIL = (1, 4, 16)
ADAM_LR, ADAM_B1, ADAM_B2, ADAM_EPS, ADAM_WD, ADAM_STEP = 0.001, 0.9, 0.999, 1e-08, 0.01, 10

NN = (((1,), (0,)), ((), ()))
NT = (((1,), (1,)), ((), ()))
TN = (((0,), (0,)), ((), ()))


def _params(sem):
    return pltpu.CompilerParams(dimension_semantics=sem, vmem_limit_bytes=V7X_VMEM_LIMIT)


def _peer(k):
    px, py, pc = lax.axis_index("x"), lax.axis_index("y"), lax.axis_index("c")
    qx = 1 - px if k & 4 else px
    qy = 1 - py if k & 2 else py
    qc = 1 - pc if k & 1 else pc
    return (qx, qy, qc), 4 * qx + 2 * qy + qc


NEAR, FAR = (0, 1, 2, 3), (4, 5, 6, 7)


def _exchange_copies(ks, src_ref, land_ref, send_sems, recv_sems):
    copies = []
    for i, k in enumerate(ks):
        peer, slot = _peer(k)
        if k == 0:
            copies.append(pltpu.make_async_copy(src_ref.at[slot], land_ref.at[i], send_sems.at[i]))
        else:
            copies.append(pltpu.make_async_remote_copy(
                src_ref=src_ref.at[slot], dst_ref=land_ref.at[i], send_sem=send_sems.at[i], recv_sem=recv_sems.at[i],
                device_id=peer, device_id_type=pl.DeviceIdType.MESH))
    return copies


def _gather_copies(src_ref, land_ref, send_sems, recv_sems):
    _, me = _peer(0)

    def copy(pair, slot, k, src=None):
        peer, _ = _peer(k)
        return pltpu.make_async_remote_copy(
            src_ref=land_ref.at[slot] if src is None else src, dst_ref=land_ref.at[slot],
            send_sem=send_sems.at[pair], recv_sem=recv_sems.at[pair], device_id=peer,
            device_id_type=pl.DeviceIdType.MESH)

    direct = [copy(0, me, 1, src_ref)] + [copy(1 + j, me, k, src_ref) for j, k in enumerate((2, 4, 6))]
    forwards = [copy(4 + j, _peer(k)[1], 1) for j, k in enumerate((2, 4, 6))]
    from_sibling = [copy(0, _peer(1)[1], 1)] + [copy(4 + j, _peer(k ^ 1)[1], 1) for j, k in enumerate((2, 4, 6))]
    return direct, forwards, from_sibling


def _mm(name, a, b, *, grid, a_spec, b_spec, out_spec, out_shape, dims, acc_axis=None, acc_shape=None, carry=None,
        a_pre=None, post=None):
    n_acc = grid[acc_axis] if acc_axis is not None else 1
    n_steps = math.prod(grid)
    n_ext = 0 if post is None else 1

    def finish(r, e_ref):
        return r if post is None else post[2](r, e_ref[...])

    def body(*refs):
        a_ref, b_ref = refs[:2]
        e_ref = refs[2] if post is not None else None
        if carry is None:
            o_ref = refs[2 + n_ext]
            scratch = refs[3 + n_ext:]
        else:
            src_ref, o_ref, land_ref = refs[2 + n_ext:5 + n_ext]
            scratch, sems = refs[5 + n_ext:-3], refs[-3:]
            step = functools.reduce(lambda s, i: s * grid[i] + pl.program_id(i), range(len(grid)), 0)

            @pl.when(step == 0)
            def _():
                if carry[0] == "gather":
                    pltpu.make_async_copy(src_ref, land_ref.at[_peer(0)[1]], sems[2]).start()
                    for cp in _gather_copies(src_ref, land_ref, sems[0], sems[1])[0]:
                        cp.start()
                else:
                    for cp in _exchange_copies(carry[2], src_ref, land_ref, sems[0], sems[1]):
                        cp.start()

            if carry[0] == "gather":
                @pl.when(step == (3 * n_steps) // 4)
                def _():
                    direct, forwards, _ = _gather_copies(src_ref, land_ref, sems[0], sems[1])
                    for cp, fw in zip(direct[1:], forwards):
                        cp.wait_recv()
                        fw.start()

        a_val = a_ref[...] if a_pre is None else a_pre(a_ref[...])
        r = lax.dot_general(a_val.astype(BF16), b_ref[...].astype(BF16), dims, preferred_element_type=F32)
        if acc_axis is None:
            o_ref[...] = finish(r, e_ref).astype(o_ref.dtype)
        else:
            acc = scratch[0]
            k = pl.program_id(acc_axis)

            @pl.when(k == 0)
            def _():
                acc[...] = r

            @pl.when(k > 0)
            def _():
                acc[...] += r

            @pl.when(k == n_acc - 1)
            def _():
                o_ref[...] = finish(acc[...], e_ref).astype(o_ref.dtype)

        if carry is not None:
            @pl.when(step == n_steps - 1)
            def _():
                if carry[0] == "gather":
                    direct, forwards, from_sibling = _gather_copies(src_ref, land_ref, sems[0], sems[1])
                    for cp in from_sibling:
                        cp.wait_recv()
                    for cp in direct + forwards:
                        cp.wait_send()
                    pltpu.make_async_copy(src_ref, land_ref.at[_peer(0)[1]], sems[2]).wait()
                else:
                    for cp in _exchange_copies(carry[2], src_ref, land_ref, sems[0], sems[1]):
                        cp.wait()

    scratch = [] if acc_axis is None else [pltpu.VMEM(acc_shape, F32)]
    in_specs, operands = [a_spec, b_spec], [a, b]
    if post is not None:
        in_specs.append(post[1])
        operands.append(post[0])
    if carry is None:
        sem = tuple("arbitrary" if i == acc_axis else "parallel" for i in range(len(grid)))
        return pl.pallas_call(body, name=name, grid=grid, in_specs=in_specs, out_specs=out_spec,
                              out_shape=out_shape, scratch_shapes=scratch, compiler_params=_params(sem))(*operands)
    kind, src = carry[:2]
    n_land = N_DEV if kind == "gather" else len(carry[2])
    n_sem = N_DEV - 1 if kind == "gather" else n_land
    land = jax.ShapeDtypeStruct((n_land,) + src.shape[-2:], src.dtype)
    hbm = pl.BlockSpec(memory_space=pl.ANY)
    scratch += [pltpu.SemaphoreType.DMA((n_sem,)), pltpu.SemaphoreType.DMA((n_sem,)), pltpu.SemaphoreType.DMA(())]
    return pl.pallas_call(body, name=name + "_" + kind, grid=grid, in_specs=in_specs + [hbm],
                          out_specs=[out_spec, hbm], out_shape=[out_shape, land], scratch_shapes=scratch,
                          compiler_params=_params(("arbitrary",) * len(grid)))(*operands, src)


def _tile(n, t):
    if n <= t:
        return n
    for d in range(t - t % 8, 7, -8):
        if n % d == 0:
            return d
    raise ValueError((n, t))


@jax.custom_vjp
def linear_col(x, g):
    return _linear_col_fwd(x, g)[0]


def _col_fwd_call(x, g, carry=None):
    t, k = x.shape
    _, _, ns = g.shape
    tm = _tile(t, 1024)
    return _mm("col_fwd", x, g, grid=(t // tm, N_DEV),
               a_spec=pl.BlockSpec((tm, k), lambda i, j: (i, 0)),
               b_spec=pl.BlockSpec((None, k, ns), lambda i, j: (j, 0, 0)),
               out_spec=pl.BlockSpec((tm, ns), lambda i, j: (i, j)),
               out_shape=jax.ShapeDtypeStruct((t, N_DEV * ns), F32), dims=NN, carry=carry)


def _col_dx_call(dy, g, carry=None):
    _, k, ns = g.shape
    t = dy.shape[0]
    tm = _tile(t, 1024)
    return _mm("col_dx", dy, g, grid=(t // tm, N_DEV),
               a_spec=pl.BlockSpec((tm, ns), lambda i, j: (i, j)),
               b_spec=pl.BlockSpec((None, k, ns), lambda i, j: (j, 0, 0)),
               out_spec=pl.BlockSpec((tm, k), lambda i, j: (i, 0)),
               out_shape=jax.ShapeDtypeStruct((t, k), F32), dims=NT, acc_axis=1, acc_shape=(tm, k), carry=carry)


def _col_dw_call(x, dy, ns, carry=None):
    t, k = x.shape
    tt = _tile(t, 1024)
    return _mm("col_dw", x, dy, grid=(N_DEV, t // tt),
               a_spec=pl.BlockSpec((tt, k), lambda j, s: (s, 0)),
               b_spec=pl.BlockSpec((tt, ns), lambda j, s: (s, j)),
               out_spec=pl.BlockSpec((None, k, ns), lambda j, s: (j, 0, 0)),
               out_shape=jax.ShapeDtypeStruct((N_DEV, k, ns), BF16), dims=TN, acc_axis=1, acc_shape=(k, ns),
               carry=carry)


def _linear_col_fwd(x, g):
    return _col_fwd_call(x, g), (x, g)


def _linear_col_bwd(res, dy):
    x, g = res
    return _col_dx_call(dy, g), _col_dw_call(x, dy, g.shape[2])


linear_col.defvjp(_linear_col_fwd, _linear_col_bwd)


@jax.custom_vjp
def linear_col_next(x, g, nxt):
    return _linear_col_next_fwd(x, g, nxt)[0]


def _linear_col_next_fwd(x, g, nxt):
    y, g_next = _col_fwd_call(x, g, carry=("gather", nxt.astype(BF16)))
    return (y, g_next), (x, g)


def _linear_col_next_bwd(res, cts):
    x, g = res
    dy, dg_next = cts
    dx, near = _col_dx_call(dy, g, carry=("exchange", dg_next, NEAR))
    dg, far = _col_dw_call(x, dy, g.shape[2], carry=("exchange", dg_next, FAR))
    return dx, dg, sum_slots(near, far)


linear_col_next.defvjp(_linear_col_next_fwd, _linear_col_next_bwd)


def _relu2(a):
    return jnp.square(jnp.maximum(a, 0.0))


def _relu2_grad(d_act, a):
    return d_act * (2.0 * jnp.maximum(a, 0.0))


def _lin_fwd_call(x, w, act, carry=None):
    t, k = x.shape
    _, n = w.shape
    tm, tk = _tile(t, 1024), _tile(k, 1024)
    return _mm("lin_fwd", x, w, grid=(t // tm, k // tk),
               a_spec=pl.BlockSpec((tm, tk), lambda i, s: (i, s)),
               b_spec=pl.BlockSpec((tk, n), lambda i, s: (s, 0)),
               out_spec=pl.BlockSpec((tm, n), lambda i, s: (i, 0)),
               out_shape=jax.ShapeDtypeStruct((t, n), F32), dims=NN, acc_axis=1, acc_shape=(tm, n), carry=carry,
               a_pre=_relu2 if act else None)


def _lin_dx_call(dy, w, x, act, carry=None):
    k, n = w.shape
    t = dy.shape[0]
    tm, tk = _tile(t, 1024), _tile(k, 1024)
    out_spec = pl.BlockSpec((tm, tk), lambda i, s: (i, s))
    return _mm("lin_dx", dy, w, grid=(t // tm, k // tk),
               a_spec=pl.BlockSpec((tm, n), lambda i, s: (i, 0)),
               b_spec=pl.BlockSpec((tk, n), lambda i, s: (s, 0)),
               out_spec=out_spec, out_shape=jax.ShapeDtypeStruct((t, k), F32), dims=NT, carry=carry,
               post=(x, out_spec, _relu2_grad) if act else None)


def _lin_dw_call(x, dy, act, carry=None):
    t, k = x.shape
    n = dy.shape[1]
    tk, tt = _tile(k, 1024), _tile(t, 1024)
    return _mm("lin_dw", x, dy, grid=(k // tk, t // tt),
               a_spec=pl.BlockSpec((tt, tk), lambda s, r: (r, s)),
               b_spec=pl.BlockSpec((tt, n), lambda s, r: (r, 0)),
               out_spec=pl.BlockSpec((tk, n), lambda s, r: (s, 0)),
               out_shape=jax.ShapeDtypeStruct((k, n), BF16), dims=TN, acc_axis=1, acc_shape=(tk, n), carry=carry,
               a_pre=_relu2 if act else None)


def _make_linear(act):
    @jax.custom_vjp
    def linear(x, w):
        return _lin_fwd_call(x, w, act)

    def fwd(x, w):
        return _lin_fwd_call(x, w, act), (x, w)

    def bwd(res, dy):
        x, w = res
        return _lin_dx_call(dy, w, x, act), _lin_dw_call(x, dy, act)

    linear.defvjp(fwd, bwd)

    @jax.custom_vjp
    def linear_next(x, w, nxt):
        return next_fwd(x, w, nxt)[0]

    def next_fwd(x, w, nxt):
        y, g_next = _lin_fwd_call(x, w, act, carry=("gather", nxt.astype(BF16)))
        return (y, g_next), (x, w)

    def next_bwd(res, cts):
        x, w = res
        dy, dg_next = cts
        dx, near = _lin_dx_call(dy, w, x, act, carry=("exchange", dg_next, NEAR))
        dw, far = _lin_dw_call(x, dy, act, carry=("exchange", dg_next, FAR))
        return dx, dw, sum_slots(near, far)

    linear_next.defvjp(next_fwd, next_bwd)
    return linear, linear_next


linear, linear_next = _make_linear(False)
linear_relu2, linear_relu2_next = _make_linear(True)


@jax.custom_vjp
def grouped_mm(x, w):
    return _grouped_fwd(x, w)[0]


def _grouped_fwd(x, w):
    t = x.shape[0]
    p, kin, kout = w.shape
    tm = _tile(t, 1024)
    y = _mm("grp_fwd", x, w, grid=(t // tm, p),
            a_spec=pl.BlockSpec((tm, kin), lambda i, q: (i, q)),
            b_spec=pl.BlockSpec((None, kin, kout), lambda i, q: (q, 0, 0)),
            out_spec=pl.BlockSpec((tm, kout), lambda i, q: (i, q)),
            out_shape=jax.ShapeDtypeStruct((t, p * kout), F32), dims=NN)
    return y, (x, w)


def _grouped_bwd(res, dy):
    x, w = res
    t = x.shape[0]
    p, kin, kout = w.shape
    tm = _tile(t, 1024)
    dx = _mm("grp_dx", dy, w, grid=(t // tm, p),
             a_spec=pl.BlockSpec((tm, kout), lambda i, q: (i, q)),
             b_spec=pl.BlockSpec((None, kin, kout), lambda i, q: (q, 0, 0)),
             out_spec=pl.BlockSpec((tm, kin), lambda i, q: (i, q)),
             out_shape=jax.ShapeDtypeStruct((t, p * kin), F32), dims=NT)
    dw = _mm("grp_dw", x, dy, grid=(p, t // tm),
             a_spec=pl.BlockSpec((tm, kin), lambda q, s: (s, q)),
             b_spec=pl.BlockSpec((tm, kout), lambda q, s: (s, q)),
             out_spec=pl.BlockSpec((None, kin, kout), lambda q, s: (q, 0, 0)),
             out_shape=jax.ShapeDtypeStruct((p, kin, kout), F32), dims=TN, acc_axis=1, acc_shape=(kin, kout))
    return dx, dw


grouped_mm.defvjp(_grouped_fwd, _grouped_bwd)


def _row_ts(widths, seq):
    per_row = 4 * sum(widths)
    ts = 512
    while ts > 8 and ts * per_row * 2 > 24 * 1024 * 1024:
        ts //= 2
    return min(ts, seq)


def make_rowop(name, f, n_row, n_batch, n_vec, f_bwd=None):
    n_in = n_row + n_batch + n_vec

    def in_specs(args, ts):
        specs = []
        for a in args[:n_row]:
            specs.append(pl.BlockSpec((None, ts, a.shape[2]), lambda b, s: (b, s, 0)))
        for a in args[n_row:n_row + n_batch]:
            specs.append(pl.BlockSpec((None, 1, a.shape[2]), lambda b, s: (b, 0, 0)))
        for a in args[n_row + n_batch:]:
            specs.append(pl.BlockSpec((1, a.shape[1]), lambda b, s: (0, 0)))
        return specs

    def out_struct(args, ts):
        blocks = [jax.ShapeDtypeStruct((ts, a.shape[2]), F32) for a in args[:n_row]]
        blocks += [jax.ShapeDtypeStruct((1, a.shape[2]), F32) for a in args[n_row:n_row + n_batch]]
        blocks += [jax.ShapeDtypeStruct((1, a.shape[1]), F32) for a in args[n_row + n_batch:]]
        return jax.eval_shape(f, *blocks)

    def run_fwd(args):
        bsz, seq = args[0].shape[:2]
        outs0 = out_struct(args, 8)
        widths = [a.shape[2] for a in args[:n_row]] + [o.shape[1] for o in outs0]
        ts = _row_ts(widths, seq)

        def body(*refs):
            outs = f(*[r[...] for r in refs[:n_in]])
            for o_ref, o in zip(refs[n_in:], outs):
                o_ref[...] = o

        return pl.pallas_call(
            body, name=name + "_fwd", grid=(bsz, seq // ts), in_specs=in_specs(args, ts),
            out_specs=[pl.BlockSpec((None, ts, o.shape[1]), lambda b, s: (b, s, 0)) for o in outs0],
            out_shape=[jax.ShapeDtypeStruct((bsz, seq, o.shape[1]), F32) for o in outs0],
            compiler_params=_params(("parallel", "parallel")))(*args)

    def run_bwd(args, cts):
        bsz, seq = args[0].shape[:2]
        widths = [a.shape[2] for a in args[:n_row]] * 2 + [c.shape[2] for c in cts]
        ts = _row_ts(widths, seq)
        n_ct = len(cts)

        def body(*refs):
            ins = [r[...] for r in refs[:n_in]]
            ct = [r[...] for r in refs[n_in:n_in + n_ct]]
            if f_bwd is None:
                _, vjp = jax.vjp(f, *ins)
                grads = vjp(tuple(ct))
            else:
                grads = f_bwd(ins, ct)
            g_refs = refs[n_in + n_ct:]
            b, s = pl.program_id(0), pl.program_id(1)
            for i in range(n_row):
                g_refs[i][...] = grads[i]
            for i in range(n_row, n_row + n_batch):
                @pl.when(s == 0)
                def _(i=i):
                    g_refs[i][...] = grads[i]

                @pl.when(s > 0)
                def _(i=i):
                    g_refs[i][...] += grads[i]
            for i in range(n_row + n_batch, n_in):
                first = jnp.logical_and(b == 0, s == 0)

                @pl.when(first)
                def _(i=i):
                    g_refs[i][...] = grads[i]

                @pl.when(jnp.logical_not(first))
                def _(i=i):
                    g_refs[i][...] += grads[i]

        ct_specs = [pl.BlockSpec((None, ts, c.shape[2]), lambda b, s: (b, s, 0)) for c in cts]
        return pl.pallas_call(
            body, name=name + "_bwd", grid=(bsz, seq // ts), in_specs=in_specs(args, ts) + ct_specs,
            out_specs=in_specs(args, ts), out_shape=[jax.ShapeDtypeStruct(a.shape, F32) for a in args],
            compiler_params=_params(("arbitrary", "arbitrary")))(*args, *cts)

    @jax.custom_vjp
    def op(*args):
        return tuple(run_fwd(args))

    def fwd(*args):
        return tuple(run_fwd(args)), args

    def bwd(args, cts):
        return tuple(run_bwd(args, list(cts)))

    op.defvjp(fwd, bwd)
    return op


def _modnorm_f(h, scale, shift, g):
    y = h * lax.rsqrt(jnp.mean(h * h, axis=-1, keepdims=True) + EPS) * g
    return (y * (1.0 + scale) + shift,)


def _gate_res_f(h, y, gate):
    return (h + gate * y,)


def _relu2_f(a):
    return (jnp.square(jnp.maximum(a, 0.0)),)


def _gelu_skip_f(y, u, d):
    return (jax.nn.gelu(y + d * u),)


def _glu_f(vg):
    n = vg.shape[1] // 2
    return (vg[:, :n] * jax.nn.sigmoid(vg[:, n:]),)


def _glu_b(ins, cts):
    (vg,), (ct,) = ins, cts
    n = vg.shape[1] // 2
    sg = jax.nn.sigmoid(vg[:, n:])
    return (jnp.concatenate([ct * sg, ct * vg[:, :n] * sg * (1.0 - sg)], axis=1),)


def _combine_f(o1, o2, o3, l1, l2, l3):
    m = jnp.maximum(jnp.maximum(l1, l2), l3)
    e1, e2, e3 = jnp.exp(l1 - m), jnp.exp(l2 - m), jnp.exp(l3 - m)
    return ((e1 * o1 + e2 * o2 + e3 * o3) / (e1 + e2 + e3),)


def _final_loss_f(h, target, g):
    y = h * lax.rsqrt(jnp.mean(h * h, axis=-1, keepdims=True) + EPS) * g
    return (0.5 * jnp.mean(jnp.square(y - target), axis=-1, keepdims=True),)


modnorm = make_rowop("modnorm", _modnorm_f, 1, 2, 1)
gate_res = make_rowop("gate_res", _gate_res_f, 2, 1, 0)
relu2 = make_rowop("relu2", _relu2_f, 1, 0, 0)
gelu_skip = make_rowop("gelu_skip", _gelu_skip_f, 2, 0, 1)
glu = make_rowop("glu", _glu_f, 1, 0, 0, _glu_b)
combine = make_rowop("combine", _combine_f, 6, 0, 0)
final_loss = make_rowop("final_loss", _final_loss_f, 2, 0, 1)


SCAN_TB = 64
HALF = 512


def _scan_fwd_call(bu, a_re, a_im):
    bsz, rows, width = bu.shape
    seq = rows // 8
    tb = _tile(seq, SCAN_TB)

    def body(bu_ref, ar_ref, ai_ref, o_ref, carry):
        @pl.when(pl.program_id(0) == 0)
        def _():
            carry[...] = jnp.zeros_like(carry)

        ar, ai = ar_ref[...], ai_ref[...]

        def step(t, st):
            r0 = pl.multiple_of(t * 8, 8)
            new = []
            for b in range(bsz):
                sre, sim = st[2 * b], st[2 * b + 1]
                nre = ar * sre - ai * sim + bu_ref[b, pl.ds(r0, 8), pl.ds(0, HALF)]
                nim = ar * sim + ai * sre + bu_ref[b, pl.ds(r0, 8), pl.ds(HALF, HALF)]
                o_ref[b, pl.ds(r0, 8), pl.ds(0, HALF)] = nre
                o_ref[b, pl.ds(r0, 8), pl.ds(HALF, HALF)] = nim
                new += [nre, nim]
            return tuple(new)

        fin = lax.fori_loop(0, tb, step, tuple(carry[k] for k in range(2 * bsz)), unroll=4)
        for k in range(2 * bsz):
            carry[k] = fin[k]

    blk = pl.BlockSpec((bsz, tb * 8, width), lambda i: (0, i, 0))
    vec = pl.BlockSpec((8, HALF), lambda i: (0, 0))
    return pl.pallas_call(body, name="s5_scan_fwd", grid=(seq // tb,), in_specs=[blk, vec, vec], out_specs=blk,
                          out_shape=jax.ShapeDtypeStruct(bu.shape, F32),
                          scratch_shapes=[pltpu.VMEM((2 * bsz, 8, HALF), F32)],
                          compiler_params=_params(("arbitrary",)))(bu, a_re, a_im)


def _scan_bwd_call(g, st, a_re, a_im):
    bsz, rows, width = g.shape
    seq = rows // 8
    tb = _tile(seq, SCAN_TB)
    nt = seq // tb

    def body(g_ref, st_ref, prev_ref, ar_ref, ai_ref, db_ref, dar_ref, dai_ref, carry):
        i = pl.program_id(0)

        @pl.when(i == 0)
        def _():
            carry[...] = jnp.zeros_like(carry)
            dar_ref[...] = jnp.zeros_like(dar_ref)
            dai_ref[...] = jnp.zeros_like(dai_ref)

        ar, ai = ar_ref[...], ai_ref[...]

        def lam_step(b, r0, lre, lim):
            nre = g_ref[b, pl.ds(r0, 8), pl.ds(0, HALF)] + ar * lre + ai * lim
            nim = g_ref[b, pl.ds(r0, 8), pl.ds(HALF, HALF)] - ai * lre + ar * lim
            db_ref[b, pl.ds(r0, 8), pl.ds(0, HALF)] = nre
            db_ref[b, pl.ds(r0, 8), pl.ds(HALF, HALF)] = nim
            return nre, nim

        def step(k, c):
            t = tb - 1 - k
            r0 = pl.multiple_of(t * 8, 8)
            p0 = pl.multiple_of(t * 8 - 8, 8)
            lam, dar, dai = list(c[:2 * bsz]), c[2 * bsz], c[2 * bsz + 1]
            for b in range(bsz):
                nre, nim = lam_step(b, r0, lam[2 * b], lam[2 * b + 1])
                pre = st_ref[b, pl.ds(p0, 8), pl.ds(0, HALF)]
                pim = st_ref[b, pl.ds(p0, 8), pl.ds(HALF, HALF)]
                dar = dar + nre * pre + nim * pim
                dai = dai + nim * pre - nre * pim
                lam[2 * b], lam[2 * b + 1] = nre, nim
            return tuple(lam) + (dar, dai)

        zero = jnp.zeros((8, HALF), F32)
        init = tuple(carry[k] for k in range(2 * bsz)) + (zero, zero)
        c = lax.fori_loop(0, tb - 1, step, init, unroll=4)
        lam, dar, dai = list(c[:2 * bsz]), c[2 * bsz], c[2 * bsz + 1]
        keep = jnp.where(i == nt - 1, 0.0, 1.0)
        for b in range(bsz):
            nre, nim = lam_step(b, 0, lam[2 * b], lam[2 * b + 1])
            pre = prev_ref[b, :, pl.ds(0, HALF)] * keep
            pim = prev_ref[b, :, pl.ds(HALF, HALF)] * keep
            dar = dar + nre * pre + nim * pim
            dai = dai + nim * pre - nre * pim
            carry[2 * b], carry[2 * b + 1] = nre, nim
        dar_ref[...] += dar
        dai_ref[...] += dai

    blk = pl.BlockSpec((bsz, tb * 8, width), lambda i: (0, nt - 1 - i, 0))
    prev = pl.BlockSpec((bsz, 8, width), lambda i: (0, jnp.maximum((nt - 1 - i) * tb - 1, 0), 0))
    vec = pl.BlockSpec((8, HALF), lambda i: (0, 0))
    return pl.pallas_call(
        body, name="s5_scan_bwd", grid=(nt,), in_specs=[blk, blk, prev, vec, vec], out_specs=[blk, vec, vec],
        out_shape=[jax.ShapeDtypeStruct(g.shape, F32), jax.ShapeDtypeStruct((8, HALF), F32),
                   jax.ShapeDtypeStruct((8, HALF), F32)],
        scratch_shapes=[pltpu.VMEM((2 * bsz, 8, HALF), F32)],
        compiler_params=_params(("arbitrary",)))(g, st, st, a_re, a_im)


@jax.custom_vjp
def s5_scan(bu, a_re, a_im):
    return _scan_fwd_call(bu, a_re, a_im)


def _s5_scan_fwd(bu, a_re, a_im):
    st = _scan_fwd_call(bu, a_re, a_im)
    return st, (st, a_re, a_im)


def _s5_scan_bwd(res, g):
    st, a_re, a_im = res
    return tuple(_scan_bwd_call(g, st, a_re, a_im))


s5_scan.defvjp(_s5_scan_fwd, _s5_scan_bwd)


SSM_TB = 64
PITCH = SSM_TB + 8
PIECES = 8
CH = 128
NCH = 2 * HALF // LANES
NRE = NCH // 2


def _slab(t):
    return pl.ds(t, PIECES, stride=PITCH)


def _put_rows(ref, b, q, val, tb):
    for j in range(NCH):
        ref[b, j, pl.ds(q * PITCH, tb), :] = val[:, j * LANES:(j + 1) * LANES]


def _get_rows(ref, q, bsz, tb):
    return jnp.concatenate(
        [jnp.concatenate([ref[b, j, pl.ds(q * PITCH, tb), :] for j in range(NCH)], axis=1) for b in range(bsz)], axis=0)


def _chunks(a):
    return [a[:, j * LANES:(j + 1) * LANES] for j in range(a.shape[1] // LANES)]


def _ssm_fwd_call(u, b_mat, c_mat, a_re, a_im):
    bsz, seq, d = u.shape
    tb = SSM_TB
    nt = seq // tb
    assert seq % tb == 0 and d == PIECES * CH

    def body(u_ref, b_ref, c_ref, ar_ref, ai_ref, y_ref, st_ref, end_ref, bu, carry):
        @pl.when(pl.program_id(0) == 0)
        def _():
            carry[...] = jnp.zeros_like(carry)

        u2 = u_ref[...].reshape(bsz * tb, d).astype(BF16)
        for q in range(PIECES):
            r = lax.dot_general(u2[:, q * CH:(q + 1) * CH], b_ref[q], NN, preferred_element_type=F32)
            for b in range(bsz):
                _put_rows(bu, b, q, r[b * tb:(b + 1) * tb], tb)
        ar, ai = _chunks(ar_ref[...]), _chunks(ai_ref[...])
        st = [[carry[b, j] for j in range(NCH)] for b in range(bsz)]
        for t in range(tb):
            for b in range(bsz):
                for j in range(NRE):
                    sre, sim = st[b][j], st[b][NRE + j]
                    nre = ar[j] * sre - ai[j] * sim + bu[b, j, _slab(t), :]
                    nim = ar[j] * sim + ai[j] * sre + bu[b, NRE + j, _slab(t), :]
                    st_ref[b, j, _slab(t), :] = nre
                    st_ref[b, NRE + j, _slab(t), :] = nim
                    st[b][j], st[b][NRE + j] = nre, nim
        for b in range(bsz):
            for j in range(NCH):
                carry[b, j] = st[b][j]
                end_ref[b, j] = st[b][j]
        for q in range(PIECES):
            s2 = _get_rows(st_ref, q, bsz, tb).astype(BF16)
            r = lax.dot_general(s2, c_ref[q], NN, preferred_element_type=F32)
            for b in range(bsz):
                y_ref[b, :, q * CH:(q + 1) * CH] = r[b * tb:(b + 1) * tb]

    rows = pl.BlockSpec((bsz, tb, d), lambda i: (0, i, 0))
    vec = pl.BlockSpec((PIECES, HALF), lambda i: (0, 0))
    return pl.pallas_call(
        body, name="s5_ssm_fwd", grid=(nt,),
        in_specs=[rows, pl.BlockSpec((PIECES, CH, 2 * HALF), lambda i: (0, 0, 0)),
                  pl.BlockSpec((PIECES, 2 * HALF, CH), lambda i: (0, 0, 0)), vec, vec],
        out_specs=[rows, pl.BlockSpec((bsz, None, NCH, PIECES * PITCH, LANES), lambda i: (0, i, 0, 0, 0)),
                   pl.BlockSpec((bsz, None, NCH, PIECES, LANES), lambda i: (0, i, 0, 0, 0))],
        out_shape=[jax.ShapeDtypeStruct((bsz, seq, d), F32),
                   jax.ShapeDtypeStruct((bsz, nt, NCH, PIECES * PITCH, LANES), F32),
                   jax.ShapeDtypeStruct((bsz, nt, NCH, PIECES, LANES), F32)],
        scratch_shapes=[pltpu.VMEM((bsz, NCH, PIECES * PITCH, LANES), F32), pltpu.VMEM((bsz, NCH, PIECES, LANES), F32)],
        compiler_params=_params(("arbitrary",)))(u, b_mat, c_mat, a_re, a_im)


def _ssm_bwd_call(dy, u, st, ends, b_mat, c_mat, a_re, a_im):
    bsz, seq, d = u.shape
    tb = SSM_TB
    nt = seq // tb

    def body(dy_ref, u_ref, st_ref, prev_ref, b_ref, c_ref, ar_ref, ai_ref, du_ref, db_hbm, dc_hbm, dar_ref, dai_ref,
             lam, carry, db_acc, dc_acc):
        i = pl.program_id(0)

        @pl.when(i == 0)
        def _():
            carry[...] = jnp.zeros_like(carry)
            db_acc[...] = jnp.zeros_like(db_acc)
            dc_acc[...] = jnp.zeros_like(dc_acc)
            dar_ref[...] = jnp.zeros_like(dar_ref)
            dai_ref[...] = jnp.zeros_like(dai_ref)

        dy2 = dy_ref[...].reshape(bsz * tb, d).astype(BF16)
        u2 = u_ref[...].reshape(bsz * tb, d).astype(BF16)
        for q in range(PIECES):
            dyq = dy2[:, q * CH:(q + 1) * CH]
            g = lax.dot_general(dyq, c_ref[q], NT, preferred_element_type=F32)
            for b in range(bsz):
                _put_rows(lam, b, q, g[b * tb:(b + 1) * tb], tb)
            s2 = _get_rows(st_ref, q, bsz, tb).astype(BF16)
            dc_acc[q] += lax.dot_general(s2, dyq, TN, preferred_element_type=F32)

        ar, ai = _chunks(ar_ref[...]), _chunks(ai_ref[...])
        keep = jnp.where(i == nt - 1, 0.0, 1.0)
        lm = [[carry[b, j] for j in range(NCH)] for b in range(bsz)]
        dar = [jnp.zeros((PIECES, LANES), F32) for _ in range(NRE)]
        dai = [jnp.zeros((PIECES, LANES), F32) for _ in range(NRE)]
        for t in range(tb - 1, -1, -1):
            for b in range(bsz):
                for j in range(NRE):
                    lre, lim = lm[b][j], lm[b][NRE + j]
                    nre = lam[b, j, _slab(t), :] + ar[j] * lre + ai[j] * lim
                    nim = lam[b, NRE + j, _slab(t), :] - ai[j] * lre + ar[j] * lim
                    lam[b, j, _slab(t), :] = nre
                    lam[b, NRE + j, _slab(t), :] = nim
                    if t > 0:
                        pre, pim = st_ref[b, j, _slab(t - 1), :], st_ref[b, NRE + j, _slab(t - 1), :]
                    else:
                        pre, pim = prev_ref[b, j] * keep, prev_ref[b, NRE + j] * keep
                    dar[j] = dar[j] + nre * pre + nim * pim
                    dai[j] = dai[j] + nim * pre - nre * pim
                    lm[b][j], lm[b][NRE + j] = nre, nim
        for b in range(bsz):
            for j in range(NCH):
                carry[b, j] = lm[b][j]
        dar_ref[...] += jnp.concatenate(dar, axis=1)
        dai_ref[...] += jnp.concatenate(dai, axis=1)

        for q in range(PIECES):
            l2 = _get_rows(lam, q, bsz, tb).astype(BF16)
            r = lax.dot_general(l2, b_ref[q], NT, preferred_element_type=F32)
            for b in range(bsz):
                du_ref[b, :, q * CH:(q + 1) * CH] = r[b * tb:(b + 1) * tb]
            db_acc[q] += lax.dot_general(u2[:, q * CH:(q + 1) * CH], l2, TN, preferred_element_type=F32)

        @pl.when(i == nt - 1)
        def _():
            pltpu.sync_copy(db_acc, db_hbm)
            pltpu.sync_copy(dc_acc, dc_hbm)

    rows = pl.BlockSpec((bsz, tb, d), lambda i: (0, nt - 1 - i, 0))
    vec = pl.BlockSpec((PIECES, HALF), lambda i: (0, 0))
    hbm = pl.BlockSpec(memory_space=pl.ANY)
    return pl.pallas_call(
        body, name="s5_ssm_bwd", grid=(nt,),
        in_specs=[rows, rows,
                  pl.BlockSpec((bsz, None, NCH, PIECES * PITCH, LANES), lambda i: (0, nt - 1 - i, 0, 0, 0)),
                  pl.BlockSpec((bsz, None, NCH, PIECES, LANES), lambda i: (0, jnp.maximum(nt - 2 - i, 0), 0, 0, 0)),
                  pl.BlockSpec((PIECES, CH, 2 * HALF), lambda i: (0, 0, 0)),
                  pl.BlockSpec((PIECES, 2 * HALF, CH), lambda i: (0, 0, 0)), vec, vec],
        out_specs=[rows, hbm, hbm, vec, vec],
        out_shape=[jax.ShapeDtypeStruct((bsz, seq, d), F32), jax.ShapeDtypeStruct((PIECES, CH, 2 * HALF), F32),
                   jax.ShapeDtypeStruct((PIECES, 2 * HALF, CH), F32), jax.ShapeDtypeStruct((PIECES, HALF), F32),
                   jax.ShapeDtypeStruct((PIECES, HALF), F32)],
        scratch_shapes=[pltpu.VMEM((bsz, NCH, PIECES * PITCH, LANES), F32), pltpu.VMEM((bsz, NCH, PIECES, LANES), F32),
                        pltpu.VMEM((PIECES, CH, 2 * HALF), F32), pltpu.VMEM((PIECES, 2 * HALF, CH), F32)],
        compiler_params=_params(("arbitrary",)))(dy, u, st, ends, b_mat, c_mat, a_re, a_im)


@jax.custom_vjp
def s5_ssm(u, b_mat, c_mat, a_re, a_im):
    return _s5_ssm_fwd(u, b_mat, c_mat, a_re, a_im)[0]


def _s5_ssm_fwd(u, b_mat, c_mat, a_re, a_im):
    b16, c16 = b_mat.astype(BF16), c_mat.astype(BF16)
    y, st, ends = _ssm_fwd_call(u, b16, c16, a_re, a_im)
    return y, (u, st, ends, b16, c16, a_re, a_im)


def _s5_ssm_bwd(res, dy):
    return tuple(_ssm_bwd_call(dy, *res))


s5_ssm.defvjp(_s5_ssm_fwd, _s5_ssm_bwd)


ATT_HW = 2 * HEAD
ATT_HB = 1024 // ATT_HW


def _branch_geometry(dil, seq):
    sub = seq // dil
    assert sub % QBLK == 0
    return sub // QBLK


def _drows(dil, r, start, size):
    if dil == 1:
        return pl.ds(start, size)
    return pl.ds(r + start * dil, size, stride=dil)


def _masks():
    qi = lax.broadcasted_iota(jnp.int32, (QBLK, 2 * QBLK), 0)
    kj = lax.broadcasted_iota(jnp.int32, (QBLK, 2 * QBLK), 1) - QBLK
    dist = qi - kj
    band = jnp.logical_and(dist >= 0, dist <= QBLK)
    ci = lax.broadcasted_iota(jnp.int32, (QBLK, QBLK), 0)
    cj = lax.broadcasted_iota(jnp.int32, (QBLK, QBLK), 1)
    return band, ci >= cj


def _attn_specs(i, seq):
    q_spec = pl.BlockSpec((None, seq, ATT_HW), lambda b, h: (b, 0, i * ATT_HB + h))
    k_spec = pl.BlockSpec((None, seq, ATT_HW), lambda b, h: (b, 0, i * ATT_HB + h))
    v_spec = pl.BlockSpec((None, seq, ATT_HW), lambda b, h: (b, 0, (3 + i) * ATT_HB + h))
    o_spec = pl.BlockSpec((None, seq, ATT_HW), lambda b, h: (b, 0, h))
    return q_spec, k_spec, v_spec, o_spec


def _attn_fwd_call(q_all, kv_all, i, dil):
    bsz, seq, _ = q_all.shape
    nb = _branch_geometry(dil, seq)
    scale = HEAD ** -0.5

    def body(q_ref, k_ref, v_ref, o_ref, l_ref):
        band, causal = _masks()

        def one(r, n, mask):
            qs = _drows(dil, r, n * QBLK, QBLK)
            ks = qs if n == 0 else _drows(dil, r, (n - 1) * QBLK, 2 * QBLK)
            q2 = q_ref[qs, :].astype(BF16)
            k2 = k_ref[ks, :].astype(BF16)
            v2 = v_ref[ks, :].astype(BF16)
            outs, lses = [], []
            for h in range(ATT_HW // HEAD):
                ls = slice(h * HEAD, (h + 1) * HEAD)
                s = lax.dot_general(q2[:, ls], k2[:, ls], NT, preferred_element_type=F32) * scale
                s = jnp.where(mask, s, NEG)
                m = jnp.max(s, axis=-1, keepdims=True)
                p = jnp.exp(s - m)
                den = jnp.sum(p, axis=-1, keepdims=True)
                outs.append(lax.dot_general(p.astype(BF16), v2[:, ls], NN, preferred_element_type=F32) / den)
                lses.append(jnp.broadcast_to(m + jnp.log(den), (QBLK, HEAD)))
            o_ref[qs, :] = jnp.concatenate(outs, axis=1)
            l_ref[qs, :] = jnp.concatenate(lses, axis=1)

        for r in range(dil):
            for n in range(nb):
                one(r, n, causal if n == 0 else band)

    q_spec, k_spec, v_spec, o_spec = _attn_specs(i, seq)
    shape = jax.ShapeDtypeStruct((bsz, seq, 1024), F32)
    return pl.pallas_call(
        body, name="attn_fwd_d%d" % dil, grid=(bsz, ATT_HB), in_specs=[q_spec, k_spec, v_spec],
        out_specs=[o_spec, o_spec], out_shape=[shape, shape],
        compiler_params=_params(("parallel", "parallel")))(q_all, kv_all, kv_all)


def _attn_bwd_call(q_all, kv_all, o, l, do, dl, i, dil, grads):
    bsz, seq, _ = q_all.shape
    nb = _branch_geometry(dil, seq)
    scale = HEAD ** -0.5
    first = grads is None

    def body(q_ref, k_ref, v_ref, o_ref, l_ref, do_ref, dl_ref, *rest):
        dq_ref, dk_ref, dv_ref = rest[-3:]
        band, causal = _masks()

        def one(r, n, mask):
            qs = _drows(dil, r, n * QBLK, QBLK)
            ks = qs if n == 0 else _drows(dil, r, (n - 1) * QBLK, 2 * QBLK)
            q2 = q_ref[qs, :].astype(BF16)
            k2 = k_ref[ks, :].astype(BF16)
            v2 = v_ref[ks, :].astype(BF16)
            l2, do2 = l_ref[qs, :], do_ref[qs, :]
            t2 = dl_ref[qs, :] - do2 * o_ref[qs, :]
            dqs, dks, dvs = [], [], []
            for h in range(ATT_HW // HEAD):
                ls = slice(h * HEAD, (h + 1) * HEAD)
                q, k, v = q2[:, ls], k2[:, ls], v2[:, ls]
                s = lax.dot_general(q, k, NT, preferred_element_type=F32) * scale
                s = jnp.where(mask, s, NEG)
                p = jnp.exp(s - l2[:, h * HEAD:h * HEAD + 1])
                row = jnp.sum(t2[:, ls], axis=-1, keepdims=True)
                d_ob = do2[:, ls].astype(BF16)
                dp = lax.dot_general(d_ob, v, NT, preferred_element_type=F32)
                ds = (p * (dp + row)).astype(BF16)
                dqs.append(lax.dot_general(ds, k, NN, preferred_element_type=F32) * scale)
                dks.append(lax.dot_general(ds, q, TN, preferred_element_type=F32) * scale)
                dvs.append(lax.dot_general(p.astype(BF16), d_ob, TN, preferred_element_type=F32))
            dq_ref[qs, :] = jnp.concatenate(dqs, axis=1)
            return jnp.concatenate(dks, axis=1), jnp.concatenate(dvs, axis=1)

        for r in range(dil):
            dk_cur, dv_cur = one(r, 0, causal)
            for n in range(1, nb):
                dk, dv = one(r, n, band)
                ks = _drows(dil, r, (n - 1) * QBLK, QBLK)
                dk_ref[ks, :] = dk_cur + dk[:QBLK]
                dv_ref[ks, :] = dv_cur + dv[:QBLK]
                dk_cur, dv_cur = dk[QBLK:], dv[QBLK:]
            ks = _drows(dil, r, (nb - 1) * QBLK, QBLK)
            dk_ref[ks, :] = dk_cur
            dv_ref[ks, :] = dv_cur

    q_spec, k_spec, v_spec, o_spec = _attn_specs(i, seq)
    in_specs = [q_spec, k_spec, v_spec, o_spec, o_spec, o_spec, o_spec]
    args = [q_all, kv_all, kv_all, o, l, do, dl]
    aliases = {}
    if not first:
        in_specs += [pl.BlockSpec(memory_space=pl.ANY)] * 3
        args += list(grads)
        aliases = {7: 0, 8: 1, 9: 2}
    shape = jax.ShapeDtypeStruct((bsz, seq, 3 * 1024), F32)
    return pl.pallas_call(
        body, name="attn_bwd_d%d" % dil, grid=(bsz, ATT_HB), in_specs=in_specs,
        out_specs=[q_spec, q_spec, q_spec], out_shape=[shape, shape, shape], input_output_aliases=aliases,
        compiler_params=_params(("parallel", "parallel")))(*args)


@jax.custom_vjp
def attn_branches(q_all, kv_all):
    return _attn_branches_fwd(q_all, kv_all)[0]


def _attn_branches_fwd(q_all, kv_all):
    outs = []
    for i, dil in enumerate(BRANCH_DIL):
        outs += list(_attn_fwd_call(q_all, kv_all, i, dil))
    return tuple(outs), (q_all, kv_all, tuple(outs))


def _attn_branches_bwd(res, cts):
    q_all, kv_all, outs = res
    grads = None
    for i, dil in enumerate(BRANCH_DIL):
        grads = _attn_bwd_call(q_all, kv_all, outs[2 * i], outs[2 * i + 1], cts[2 * i], cts[2 * i + 1], i, dil, grads)
    dq, dk, dv = grads
    return dq, jnp.concatenate([dk, dv], axis=-1)


attn_branches.defvjp(_attn_branches_fwd, _attn_branches_bwd)


def ada_fwd(c_all, w, b):
    n, d, cs = w.shape
    nb = c_all.shape[0]

    def body(c_ref, w_ref, b_ref, o_ref):
        a = jax.nn.silu(c_ref[...]).astype(BF16)
        o_ref[...] = lax.dot_general(a, w_ref[...].astype(BF16), NN, preferred_element_type=F32) + b_ref[...]

    return pl.pallas_call(
        body, name="ada_fwd", grid=(n,),
        in_specs=[pl.BlockSpec((nb, d), lambda i: (0, 0)), pl.BlockSpec((None, d, cs), lambda i: (i, 0, 0)),
                  pl.BlockSpec((None, 1, cs), lambda i: (i, 0, 0))],
        out_specs=pl.BlockSpec((None, nb, cs), lambda i: (i, 0, 0)),
        out_shape=jax.ShapeDtypeStruct((n, nb, cs), F32), compiler_params=_params(("parallel",)))(c_all, w, b)


def ada_bwd(c_all, dm):
    n, nb, cs = dm.shape
    d = c_all.shape[1]

    def body(c_ref, dm_ref, dw_ref, db_ref):
        a = jax.nn.silu(c_ref[...]).astype(BF16)
        g = dm_ref[...]
        dw_ref[...] = lax.dot_general(a, g.astype(BF16), TN, preferred_element_type=F32)
        db_ref[...] = jnp.sum(g, axis=0, keepdims=True)

    return pl.pallas_call(
        body, name="ada_bwd", grid=(n,),
        in_specs=[pl.BlockSpec((nb, d), lambda i: (0, 0)), pl.BlockSpec((None, nb, cs), lambda i: (i, 0, 0))],
        out_specs=[pl.BlockSpec((None, d, cs), lambda i: (i, 0, 0)), pl.BlockSpec((None, 1, cs), lambda i: (i, 0, 0))],
        out_shape=[jax.ShapeDtypeStruct((n, d, cs), F32), jax.ShapeDtypeStruct((n, 1, cs), F32)],
        compiler_params=_params(("parallel",)))(c_all, dm)


def all_gather(x, name):
    m, n = x.shape

    def body(x_ref, out_ref, send_sems, recv_sems, local_sem):
        px, py, pc = lax.axis_index("x"), lax.axis_index("y"), lax.axis_index("c")
        me, sibling = (px, py, pc), (px, py, 1 - pc)
        chips = [(1 - px, py), (px, 1 - py), (1 - px, 1 - py)]

        def rows(qx, qy, qc):
            return out_ref.at[pl.ds((4 * qx + 2 * qy + qc) * m, m), :]

        def copy(k, block, to, src=None):
            return pltpu.make_async_remote_copy(
                src_ref=rows(*block) if src is None else src, dst_ref=rows(*block),
                send_sem=send_sems.at[k], recv_sem=recv_sems.at[k], device_id=to,
                device_id_type=pl.DeviceIdType.MESH)

        mine = pltpu.make_async_copy(x_ref, rows(*me), local_sem)
        mine.start()
        first = [copy(0, me, sibling, src=x_ref)]
        first += [copy(1 + j, me, (*chip, pc), src=x_ref) for j, chip in enumerate(chips)]
        for cp in first:
            cp.start()
        passed = [copy(4 + j, (*chip, pc), sibling) for j, chip in enumerate(chips)]
        for j, chip in enumerate(chips):
            copy(1 + j, (*chip, pc), me).wait_recv()
            passed[j].start()
        copy(0, sibling, me).wait_recv()
        for j, chip in enumerate(chips):
            copy(4 + j, (*chip, 1 - pc), me).wait_recv()
        for cp in first + passed:
            cp.wait_send()
        mine.wait()

    out = pl.pallas_call(
        body, name=name, out_shape=jax.ShapeDtypeStruct((N_DEV * m, n), x.dtype),
        in_specs=[pl.BlockSpec(memory_space=pl.ANY)], out_specs=pl.BlockSpec(memory_space=pl.ANY),
        scratch_shapes=[pltpu.SemaphoreType.DMA((7,)), pltpu.SemaphoreType.DMA((7,)), pltpu.SemaphoreType.DMA(())],
    )(x)
    return out.reshape(N_DEV, m, n)


def exchange_partials(p, name):
    _, m, n = p.shape

    def body(p_ref, out_ref, send_sems, recv_sems, local_sem):
        px, py, pc = lax.axis_index("x"), lax.axis_index("y"), lax.axis_index("c")
        me = 4 * px + 2 * py + pc
        mine = pltpu.make_async_copy(p_ref.at[me], out_ref.at[me], local_sem)
        mine.start()
        copies = []
        for k in range(1, N_DEV):
            qx = 1 - px if k & 4 else px
            qy = 1 - py if k & 2 else py
            qc = 1 - pc if k & 1 else pc
            cp = pltpu.make_async_remote_copy(
                src_ref=p_ref.at[4 * qx + 2 * qy + qc], dst_ref=out_ref.at[me],
                send_sem=send_sems.at[k - 1], recv_sem=recv_sems.at[k - 1], device_id=(qx, qy, qc),
                device_id_type=pl.DeviceIdType.MESH)
            cp.start()
            copies.append(cp)
        for cp in copies:
            cp.wait()
        mine.wait()

    return pl.pallas_call(
        body, name=name, out_shape=jax.ShapeDtypeStruct(p.shape, p.dtype),
        in_specs=[pl.BlockSpec(memory_space=pl.ANY)], out_specs=pl.BlockSpec(memory_space=pl.ANY),
        scratch_shapes=[pltpu.SemaphoreType.DMA((7,)), pltpu.SemaphoreType.DMA((7,)), pltpu.SemaphoreType.DMA(())],
    )(p)


def sum_slots(*parts):
    _, r, n = parts[0].shape
    tr = _tile(r, max(16, (256 * 1024) // n))

    def body(*refs):
        acc = None
        for g_ref in refs[:-1]:
            for s in range(g_ref.shape[0]):
                term = g_ref[s].astype(F32)
                acc = term if acc is None else acc + term
        refs[-1][...] = acc

    return pl.pallas_call(
        body, name="sum_slots", grid=(r // tr,),
        in_specs=[pl.BlockSpec((p.shape[0], tr, n), lambda i: (0, i, 0)) for p in parts],
        out_specs=pl.BlockSpec((tr, n), lambda i: (i, 0)), out_shape=jax.ShapeDtypeStruct((r, n), F32),
        compiler_params=_params(("parallel",)))(*parts)


def adamw(w, m, v, g, name):
    nl, r, n = w.shape
    partials = g.ndim == 4
    tr = _tile(r, max(8, (256 * 1024) // n))
    c1 = 1.0 / (1.0 - ADAM_B1 ** ADAM_STEP)
    c2 = 1.0 / (1.0 - ADAM_B2 ** ADAM_STEP)

    def body(w_ref, m_ref, v_ref, g_ref, go_ref, d_ref, mo_ref, vo_ref):
        if partials:
            grad = g_ref[0].astype(F32)
            for s in range(1, N_DEV):
                grad = grad + g_ref[s].astype(F32)
        else:
            grad = g_ref[...]
        m_new = ADAM_B1 * m_ref[...] + (1.0 - ADAM_B1) * grad
        v_new = ADAM_B2 * v_ref[...] + (1.0 - ADAM_B2) * grad * grad
        go_ref[...] = grad
        mo_ref[...] = m_new
        vo_ref[...] = v_new
        d_ref[...] = -ADAM_LR * ((m_new * c1) / (jnp.sqrt(v_new * c2) + ADAM_EPS) + ADAM_WD * w_ref[...])

    spec = pl.BlockSpec((None, tr, n), lambda l, i: (l, i, 0))
    g_spec = pl.BlockSpec((None, N_DEV, tr, n), lambda l, i: (l, 0, i, 0)) if partials else spec
    shape = jax.ShapeDtypeStruct(w.shape, F32)
    return pl.pallas_call(
        body, name=name, grid=(nl, r // tr), in_specs=[spec, spec, spec, g_spec], out_specs=[spec] * 4,
        out_shape=[shape] * 4, compiler_params=_params(("parallel", "parallel")))(w, m, v, g)


def pack(arrays):
    flat = jnp.concatenate([a.reshape(-1).astype(F32) for a in arrays])
    rows = -(-flat.shape[0] // LANES)
    mult = 8 if rows <= 512 else 512
    rows = -(-rows // mult) * mult
    flat = jnp.pad(flat, (0, rows * LANES - flat.shape[0]))
    return flat.reshape(rows, LANES)


def unpack(slab, shapes):
    flat = slab.reshape(slab.shape[:-2] + (-1,))
    out, off = [], 0
    for s in shapes:
        size = math.prod(s)
        out.append(flat[..., off:off + size].reshape(flat.shape[:-1] + tuple(s)))
        off += size
    return out


def _s5_discretize(lam_re, lam_im, log_dt, b_re, b_im, c_re, c_im):
    dt = jnp.exp(log_dt)[:, None]
    xr, th = lam_re * dt, lam_im * dt
    er = jnp.exp(xr)
    a_re, a_im = er * jnp.cos(th), er * jnp.sin(th)
    am1 = jnp.expm1(xr) * jnp.cos(th) - 2.0 * jnp.square(jnp.sin(0.5 * th))
    den = lam_re * lam_re + lam_im * lam_im
    fr = (am1 * lam_re + a_im * lam_im) / den
    fi = (a_im * lam_re - am1 * lam_im) / den
    bb_re = fr[..., None] * b_re - fi[..., None] * b_im
    bb_im = fr[..., None] * b_im + fi[..., None] * b_re
    eye = jnp.eye(8, dtype=F32)

    def pack_b(bb):
        return jnp.einsum('qgpc,gh->qgchp', bb.reshape(8, 8, 64, 16), eye).reshape(8, 128, 512)

    def pack_c(cc):
        return jnp.einsum('qgcp,gh->qgphc', cc.reshape(8, 8, 16, 64), eye).reshape(8, 512, 128)

    b_mat = jnp.concatenate([pack_b(bb_re), pack_b(bb_im)], axis=-1)
    c_mat = jnp.concatenate([pack_c(c_re), pack_c(-c_im)], axis=1)
    return a_re.reshape(8, HALF), a_im.reshape(8, HALF), b_mat, c_mat


WEIGHT_ORDER = (("ssm_w_glu", 0, True), ("mlp_w1", 0, True), ("mlp_w2", 0, False),
                ("ssm_w_glu", 1, True), ("mlp_w1", 1, True), ("mlp_w2", 1, False), ("w_kv", None, True),
                ("attn_w_q", 0, True), ("attn_w_o", 0, False), ("mlp_w1", 2, True), ("mlp_w2", 2, False),
                ("attn_w_q", 1, True), ("attn_w_o", 1, False), ("mlp_w1", 3, True), ("mlp_w2", 3, False))


class _WeightChain:
    def __init__(self, first, upcoming):
        self.cur, self.upcoming, self.used = first, list(upcoming), 0

    def matmul(self, x, name, relu2_input=False):
        assert WEIGHT_ORDER[self.used][0] == name
        col = WEIGHT_ORDER[self.used][2]
        assert not (col and relu2_input)
        self.used += 1
        w = self.cur if col else self.cur.reshape(-1, self.cur.shape[-1])
        plain, with_next = (linear_col, linear_col_next) if col else (
            (linear_relu2, linear_relu2_next) if relu2_input else (linear, linear_next))
        if not self.upcoming:
            return plain(x, w)
        y, self.cur = with_next(x, w, self.upcoming.pop(0))
        return y


def _local_loss(diff, x, target, first_weight, shards):
    bsz, seq, d = x.shape
    t = bsz * seq
    rows3 = lambda a: a.reshape(bsz, seq, a.shape[-1])
    rows2 = lambda a: a.reshape(t, a.shape[-1])
    mods, kvmod, ln_g, ssm_d, kv_g, final_g = (diff[k] for k in ("mods", "kvmod", "ln_g", "ssm_d", "kv_g", "final_g"))
    chain = _WeightChain(first_weight, shards)
    h = x
    kv_all = None
    for layer in range(4):
        if layer == 2:
            shift, scale = kvmod[:, None, :d], kvmod[:, None, d:]
            (ukv,) = modnorm(h, scale, shift, kv_g[None])
            kv_all = rows3(chain.matmul(rows2(ukv), "w_kv"))
        mod = mods[2 * layer]
        shift, scale, gate = mod[:, None, :d], mod[:, None, d:2 * d], mod[:, None, 2 * d:]
        (u,) = modnorm(h, scale, shift, ln_g[layer, 0][None])
        if layer < 2:
            a_re, a_im, b_mat, c_mat = _s5_discretize(*(diff[k][layer] for k in (
                "ssm_lam_re", "ssm_lam_im", "ssm_log_dt", "ssm_b_re", "ssm_b_im", "ssm_c_re", "ssm_c_im")))
            y0 = s5_ssm(u, b_mat, c_mat, a_re, a_im)
            (z,) = gelu_skip(y0, u, ssm_d[layer][None])
            (y,) = glu(rows3(chain.matmul(rows2(z), "ssm_w_glu")))
        else:
            q_all = rows3(chain.matmul(rows2(u), "attn_w_q"))
            o1, l1, o2, l2, o3, l3 = attn_branches(q_all, kv_all)
            (o,) = combine(o1, o2, o3, l1, l2, l3)
            y = rows3(chain.matmul(rows2(o), "attn_w_o"))
        (h,) = gate_res(h, y, gate)
        mod = mods[2 * layer + 1]
        shift, scale, gate = mod[:, None, :d], mod[:, None, d:2 * d], mod[:, None, 2 * d:]
        (u,) = modnorm(h, scale, shift, ln_g[layer, 1][None])
        pre = chain.matmul(rows2(u), "mlp_w1")
        y = rows3(chain.matmul(pre, "mlp_w2", relu2_input=True))
        (h,) = gate_res(h, y, gate)
    (row_loss,) = final_loss(h, target, final_g[None])
    return jnp.sum(row_loss)


SSM_NAMES = ("ssm_lam_re", "ssm_lam_im", "ssm_log_dt", "ssm_b_re", "ssm_b_im", "ssm_c_re", "ssm_c_im")
ARG_NAMES = ("x", "c", "ln_g", "ada_w", "ada_b") + SSM_NAMES + (
    "ssm_d", "ssm_w_glu", "kv_g", "kv_ada_w", "kv_ada_b", "w_kv", "attn_w_q", "attn_w_o", "mlp_w1", "mlp_w2", "final_g")
WEIGHT_NAMES = ARG_NAMES[2:]


def kernel(x, c, ln_g, ada_w, ada_b, ssm_lam_re, ssm_lam_im, ssm_log_dt, ssm_b_re, ssm_b_im, ssm_c_re, ssm_c_im, ssm_d, ssm_w_glu, kv_g, kv_ada_w, kv_ada_b, w_kv, attn_w_q, attn_w_o, mlp_w1, mlp_w2, final_g, loss_target, m_ln_g, m_ada_w, m_ada_b, m_ssm_lam_re, m_ssm_lam_im, m_ssm_log_dt, m_ssm_b_re, m_ssm_b_im, m_ssm_c_re, m_ssm_c_im, m_ssm_d, m_ssm_w_glu, m_kv_g, m_kv_ada_w, m_kv_ada_b, m_w_kv, m_attn_w_q, m_attn_w_o, m_mlp_w1, m_mlp_w2, m_final_g, v_ln_g, v_ada_w, v_ada_b, v_ssm_lam_re, v_ssm_lam_im, v_ssm_log_dt, v_ssm_b_re, v_ssm_b_im, v_ssm_c_re, v_ssm_c_im, v_ssm_d, v_ssm_w_glu, v_kv_g, v_kv_ada_w, v_kv_ada_b, v_w_kv, v_attn_w_q, v_attn_w_o, v_mlp_w1, v_mlp_w2, v_final_g):
    args = locals()
    w = {n: args[n] for n in WEIGHT_NAMES}
    mom = {n: args["m_" + n] for n in WEIGHT_NAMES}
    var = {n: args["v_" + n] for n in WEIGHT_NAMES}
    bsz, seq, d = x.shape
    me = 4 * lax.axis_index("x") + 2 * lax.axis_index("y") + lax.axis_index("c")

    small = all_gather(pack([c, ln_g, ssm_d]), "gather_small")
    c_parts, ln_parts, d_parts = unpack(small, [c.shape, ln_g.shape, ssm_d.shape])
    c_all = c_parts.reshape(N_DEV * bsz, d)
    ln_full = jnp.moveaxis(ln_parts, 0, 2).reshape(4, 2, d)
    d_full = jnp.moveaxis(d_parts, 0, 1).reshape(2, d)

    cs = ada_w.shape[-1]
    mod_cols = ada_fwd(c_all, ada_w.reshape(8, d, cs), ada_b.reshape(8, 1, cs))
    kcs = kv_ada_w.shape[-1]
    kv_cols = ada_fwd(c_all, kv_ada_w[None], jnp.zeros((1, 1, kcs), F32))
    mod_g = all_gather(mod_cols.reshape(8 * N_DEV * bsz, cs), "gather_mod")
    kv_g_all = all_gather(kv_cols.reshape(N_DEV * bsz, kcs), "gather_kvmod")
    mods_all = jnp.moveaxis(mod_g.reshape(N_DEV, 8, N_DEV * bsz, cs), 0, 2).reshape(8, N_DEV * bsz, N_DEV * cs)
    kvmod_all = jnp.moveaxis(kv_g_all, 0, 1).reshape(N_DEV * bsz, N_DEV * kcs) + kv_ada_b[None]
    mods = lax.dynamic_slice_in_dim(mods_all, me * bsz, bsz, axis=1)
    kvmod = lax.dynamic_slice_in_dim(kvmod_all, me * bsz, bsz, axis=0)

    shards = [w[n] if l is None else w[n][l] for n, l, _ in WEIGHT_ORDER]
    first_weight = all_gather(shards[0].astype(BF16), "gather_first_weight")

    diff = {"mods": mods, "kvmod": kvmod, "ln_g": ln_full, "ssm_d": d_full, "kv_g": kv_g, "final_g": final_g}
    diff.update({n: w[n] for n in SSM_NAMES})
    loss_local, (g_diff, grad_x, g_first, g_shards) = jax.value_and_grad(_local_loss, argnums=(0, 1, 3, 4))(
        diff, x, loss_target, first_weight, shards[1:])
    loss = lax.psum(loss_local, AXES)
    g_shards = [sum_slots(exchange_partials(g_first, "exchange_first_weight"))] + list(g_shards)

    dmod = all_gather(pack([g_diff["mods"], g_diff["kvmod"]]), "gather_dmod")
    dmods_p, dkv_p = unpack(dmod, [g_diff["mods"].shape, g_diff["kvmod"].shape])
    dmods_all = jnp.moveaxis(dmods_p, 0, 1).reshape(8, N_DEV * bsz, 3 * d)
    dkv_all = dkv_p.reshape(N_DEV * bsz, 2 * d)
    g_ada_w, g_ada_b = ada_bwd(c_all, lax.dynamic_slice_in_dim(dmods_all, me * cs, cs, axis=2))
    g_kv_ada_w, _ = ada_bwd(c_all, lax.dynamic_slice_in_dim(dkv_all, me * kcs, kcs, axis=1)[None])
    _, g_kv_ada_b = ada_bwd(c_all, dkv_all[None])

    small_names = ("ln_g", "ssm_d", "kv_g", "final_g") + SSM_NAMES
    partial = all_gather(pack([g_diff[n] for n in small_names]), "gather_small_grads")
    totals = unpack(sum_slots(partial), [g_diff[n].shape for n in small_names])
    g_small = dict(zip(small_names, totals))
    g_small["ln_g"] = lax.dynamic_slice_in_dim(g_small["ln_g"], me * ln_g.shape[-1], ln_g.shape[-1], axis=2)
    g_small["ssm_d"] = lax.dynamic_slice_in_dim(g_small["ssm_d"], me * ssm_d.shape[-1], ssm_d.shape[-1], axis=1)
    g_small["ada_b"] = g_ada_b.reshape(ada_b.shape)
    g_small["kv_ada_b"] = g_kv_ada_b.reshape(kv_ada_b.shape)

    out = {}

    def put(name, res, shape):
        for kind, a in zip(("grad_", "delta_", "new_m_", "new_v_"), res):
            out[kind + name] = a.reshape(shape)

    for name in ("ssm_w_glu", "w_kv", "attn_w_q", "attn_w_o", "mlp_w1", "mlp_w2"):
        grads = jnp.stack([g for g, (n, _, _) in zip(g_shards, WEIGHT_ORDER) if n == name])
        v3 = lambda a: a.reshape(grads.shape)
        put(name, adamw(v3(w[name]), v3(mom[name]), v3(var[name]), grads, "adamw_" + name), w[name].shape)
    v3 = lambda a: a.reshape(8, d, cs)
    put("ada_w", adamw(v3(ada_w), v3(m_ada_w), v3(v_ada_w), g_ada_w, "adamw_ada_w"), ada_w.shape)
    put("kv_ada_w", adamw(kv_ada_w[None], m_kv_ada_w[None], v_kv_ada_w[None], g_kv_ada_w, "adamw_kv_ada_w"), kv_ada_w.shape)
    names = small_names + ("ada_b", "kv_ada_b")
    res = adamw(pack([w[n] for n in names])[None], pack([mom[n] for n in names])[None],
                pack([var[n] for n in names])[None], pack([g_small[n] for n in names])[None], "adamw_small")
    for kind, slab in zip(("grad_", "delta_", "new_m_", "new_v_"), res):
        for n, a in zip(names, unpack(slab[0], [w[n].shape for n in names])):
            out[kind + n] = a

    result = [loss, grad_x]
    for kind in ("grad_", "delta_", "new_m_", "new_v_"):
        result += [out[kind + n] for n in WEIGHT_NAMES]
    return tuple(result)
```

```python
import functools
import math

import jax
import jax.numpy as jnp
from jax import lax
from jax.experimental import pallas as pl
from jax.experimental.pallas import tpu as pltpu

F32 = jnp.float32
BF16 = jnp.bfloat16
N_DEV = 8
AXES = ("x", "y", "c")
V7X_VMEM_LIMIT = 56 * 1024 * 1024
LANES = 128
EPS = 1e-6
NEG = -1e30
HEAD = 64
QBLK = 128
BRANCH_DIL = (1, 4, 16)
ADAM_LR, ADAM_B1, ADAM_B2, ADAM_EPS, ADAM_WD, ADAM_STEP = 0.001, 0.9, 0.999, 1e-08, 0.01, 10

NN = (((1,), (0,)), ((), ()))
NT = (((1,), (1,)), ((), ()))
TN = (((0,), (0,)), ((), ()))


def _params(sem):
    return pltpu.CompilerParams(dimension_semantics=sem, vmem_limit_bytes=V7X_VMEM_LIMIT)


def _peer(k):
    px, py, pc = lax.axis_index("x"), lax.axis_index("y"), lax.axis_index("c")
    qx = 1 - px if k & 4 else px
    qy = 1 - py if k & 2 else py
    qc = 1 - pc if k & 1 else pc
    return (qx, qy, qc), 4 * qx + 2 * qy + qc


ROW_TILE = 2048
NEAR, FAR = (0, 1, 2, 3, 6), (4, 5, 7)


def _exchange_copies(ks, src_ref, land_ref, send_sems, recv_sems):
    copies = []
    for i, k in enumerate(ks):
        peer, slot = _peer(k)
        if k == 0:
            copies.append(pltpu.make_async_copy(src_ref.at[slot], land_ref.at[i], send_sems.at[i]))
        else:
            copies.append(pltpu.make_async_remote_copy(
                src_ref=src_ref.at[slot], dst_ref=land_ref.at[i], send_sem=send_sems.at[i], recv_sem=recv_sems.at[i],
                device_id=peer, device_id_type=pl.DeviceIdType.MESH))
    return copies


def _gather_copies(src_ref, land_ref, send_sems, recv_sems):
    _, me = _peer(0)

    def copy(pair, slot, k, src=None):
        peer, _ = _peer(k)
        return pltpu.make_async_remote_copy(
            src_ref=land_ref.at[slot] if src is None else src, dst_ref=land_ref.at[slot],
            send_sem=send_sems.at[pair], recv_sem=recv_sems.at[pair], device_id=peer,
            device_id_type=pl.DeviceIdType.MESH)

    direct = [copy(0, me, 1, src_ref)] + [copy(1 + j, me, k, src_ref) for j, k in enumerate((2, 4, 6))]
    forwards = [copy(4 + j, _peer(k)[1], 1) for j, k in enumerate((2, 4, 6))]
    from_sibling = [copy(0, _peer(1)[1], 1)] + [copy(4 + j, _peer(k ^ 1)[1], 1) for j, k in enumerate((2, 4, 6))]
    return direct, forwards, from_sibling


def _mm(name, a, b, *, grid, a_spec, b_spec, out_spec, out_shape, dims, acc_axis=None, acc_shape=None, carry=None,
        a_pre=None, post=None):
    n_acc = grid[acc_axis] if acc_axis is not None else 1
    n_steps = math.prod(grid)
    n_ext = 0 if post is None else 1

    def finish(r, e_ref):
        return r if post is None else post[2](r, e_ref[...])

    def body(*refs):
        a_ref, b_ref = refs[:2]
        e_ref = refs[2] if post is not None else None
        if carry is None:
            o_ref = refs[2 + n_ext]
            scratch = refs[3 + n_ext:]
        else:
            src_ref, o_ref, land_ref = refs[2 + n_ext:5 + n_ext]
            scratch, sems = refs[5 + n_ext:-3], refs[-3:]
            step = functools.reduce(lambda s, i: s * grid[i] + pl.program_id(i), range(len(grid)), 0)

            @pl.when(step == 0)
            def _():
                if carry[0] == "gather":
                    pltpu.make_async_copy(src_ref, land_ref.at[_peer(0)[1]], sems[2]).start()
                    for cp in _gather_copies(src_ref, land_ref, sems[0], sems[1])[0]:
                        cp.start()
                else:
                    for cp in _exchange_copies(carry[2], src_ref, land_ref, sems[0], sems[1]):
                        cp.start()

            if carry[0] == "gather":
                @pl.when(step == (3 * n_steps) // 4)
                def _():
                    direct, forwards, _ = _gather_copies(src_ref, land_ref, sems[0], sems[1])
                    for cp, fw in zip(direct[1:], forwards):
                        cp.wait_recv()
                        fw.start()

        a_val = a_ref[...] if a_pre is None else a_pre(a_ref[...])
        r = lax.dot_general(a_val.astype(BF16), b_ref[...].astype(BF16), dims, preferred_element_type=F32)
        if acc_axis is None:
            o_ref[...] = finish(r, e_ref).astype(o_ref.dtype)
        else:
            acc = scratch[0]
            k = pl.program_id(acc_axis)

            @pl.when(k == 0)
            def _():
                acc[...] = r

            @pl.when(k > 0)
            def _():
                acc[...] += r

            @pl.when(k == n_acc - 1)
            def _():
                o_ref[...] = finish(acc[...], e_ref).astype(o_ref.dtype)

        if carry is not None:
            @pl.when(step == n_steps - 1)
            def _():
                if carry[0] == "gather":
                    direct, forwards, from_sibling = _gather_copies(src_ref, land_ref, sems[0], sems[1])
                    for cp in from_sibling:
                        cp.wait_recv()
                    for cp in direct + forwards:
                        cp.wait_send()
                    pltpu.make_async_copy(src_ref, land_ref.at[_peer(0)[1]], sems[2]).wait()
                else:
                    for cp in _exchange_copies(carry[2], src_ref, land_ref, sems[0], sems[1]):
                        cp.wait()

    scratch = [] if acc_axis is None else [pltpu.VMEM(acc_shape, F32)]
    in_specs, operands = [a_spec, b_spec], [a, b]
    if post is not None:
        in_specs.append(post[1])
        operands.append(post[0])
    if carry is None:
        sem = tuple("arbitrary" if i == acc_axis else "parallel" for i in range(len(grid)))
        return pl.pallas_call(body, name=name, grid=grid, in_specs=in_specs, out_specs=out_spec,
                              out_shape=out_shape, scratch_shapes=scratch, compiler_params=_params(sem))(*operands)
    kind, src = carry[:2]
    n_land = N_DEV if kind == "gather" else len(carry[2])
    n_sem = N_DEV - 1 if kind == "gather" else n_land
    land = jax.ShapeDtypeStruct((n_land,) + src.shape[-2:], src.dtype)
    hbm = pl.BlockSpec(memory_space=pl.ANY)
    scratch += [pltpu.SemaphoreType.DMA((n_sem,)), pltpu.SemaphoreType.DMA((n_sem,)), pltpu.SemaphoreType.DMA(())]
    return pl.pallas_call(body, name=name + "_" + kind, grid=grid, in_specs=in_specs + [hbm],
                          out_specs=[out_spec, hbm], out_shape=[out_shape, land], scratch_shapes=scratch,
                          compiler_params=_params(("arbitrary",) * len(grid)))(*operands, src)


def _tile(n, t):
    if n <= t:
        return n
    for d in range(t - t % 8, 7, -8):
        if n % d == 0:
            return d
    raise ValueError((n, t))


@jax.custom_vjp
def linear_col(x, g):
    return _linear_col_fwd(x, g)[0]


def _col_fwd_call(x, g, carry=None):
    t, k = x.shape
    _, _, ns = g.shape
    tm = _tile(t, ROW_TILE)
    return _mm("col_fwd", x, g, grid=(t // tm, N_DEV),
               a_spec=pl.BlockSpec((tm, k), lambda i, j: (i, 0)),
               b_spec=pl.BlockSpec((None, k, ns), lambda i, j: (j, 0, 0)),
               out_spec=pl.BlockSpec((tm, ns), lambda i, j: (i, j)),
               out_shape=jax.ShapeDtypeStruct((t, N_DEV * ns), F32), dims=NN, carry=carry)


def _col_dx_call(dy, g, carry=None):
    _, k, ns = g.shape
    t = dy.shape[0]
    tm = _tile(t, ROW_TILE)
    return _mm("col_dx", dy, g, grid=(t // tm, N_DEV),
               a_spec=pl.BlockSpec((tm, ns), lambda i, j: (i, j)),
               b_spec=pl.BlockSpec((None, k, ns), lambda i, j: (j, 0, 0)),
               out_spec=pl.BlockSpec((tm, k), lambda i, j: (i, 0)),
               out_shape=jax.ShapeDtypeStruct((t, k), F32), dims=NT, acc_axis=1, acc_shape=(tm, k), carry=carry)


def _col_dw_call(x, dy, ns, carry=None):
    t, k = x.shape
    tt = _tile(t, ROW_TILE)
    return _mm("col_dw", x, dy, grid=(N_DEV, t // tt),
               a_spec=pl.BlockSpec((tt, k), lambda j, s: (s, 0)),
               b_spec=pl.BlockSpec((tt, ns), lambda j, s: (s, j)),
               out_spec=pl.BlockSpec((None, k, ns), lambda j, s: (j, 0, 0)),
               out_shape=jax.ShapeDtypeStruct((N_DEV, k, ns), BF16), dims=TN, acc_axis=1, acc_shape=(k, ns),
               carry=carry)


def _linear_col_fwd(x, g):
    return _col_fwd_call(x, g), (x, g)


def _linear_col_bwd(res, dy):
    x, g = res
    return _col_dx_call(dy, g), _col_dw_call(x, dy, g.shape[2])


linear_col.defvjp(_linear_col_fwd, _linear_col_bwd)


@jax.custom_vjp
def linear_col_next(x, g, nxt):
    return _linear_col_next_fwd(x, g, nxt)[0]


def _linear_col_next_fwd(x, g, nxt):
    y, g_next = _col_fwd_call(x, g, carry=("gather", nxt.astype(BF16)))
    return (y, g_next), (x, g)


def _linear_col_next_bwd(res, cts):
    x, g = res
    dy, dg_next = cts
    dx, near = _col_dx_call(dy, g, carry=("exchange", dg_next, NEAR))
    dg, far = _col_dw_call(x, dy, g.shape[2], carry=("exchange", dg_next, FAR))
    return dx, dg, sum_slots(near, far)


linear_col_next.defvjp(_linear_col_next_fwd, _linear_col_next_bwd)


def _relu2(a):
    return jnp.square(jnp.maximum(a, 0.0))


def _relu2_grad(d_act, a):
    return d_act * (2.0 * jnp.maximum(a, 0.0))


def _lin_fwd_call(x, w, act, carry=None):
    t, k = x.shape
    _, n = w.shape
    tm, tk = _tile(t, ROW_TILE), _tile(k, 1024)
    return _mm("lin_fwd", x, w, grid=(t // tm, k // tk),
               a_spec=pl.BlockSpec((tm, tk), lambda i, s: (i, s)),
               b_spec=pl.BlockSpec((tk, n), lambda i, s: (s, 0)),
               out_spec=pl.BlockSpec((tm, n), lambda i, s: (i, 0)),
               out_shape=jax.ShapeDtypeStruct((t, n), F32), dims=NN, acc_axis=1, acc_shape=(tm, n), carry=carry,
               a_pre=_relu2 if act else None)


def _lin_dx_call(dy, w, x, act, carry=None):
    k, n = w.shape
    t = dy.shape[0]
    tm, tk = _tile(t, 1024), _tile(k, 1024)
    out_spec = pl.BlockSpec((tm, tk), lambda i, s: (i, s))
    return _mm("lin_dx", dy, w, grid=(t // tm, k // tk),
               a_spec=pl.BlockSpec((tm, n), lambda i, s: (i, 0)),
               b_spec=pl.BlockSpec((tk, n), lambda i, s: (s, 0)),
               out_spec=out_spec, out_shape=jax.ShapeDtypeStruct((t, k), F32), dims=NT, carry=carry,
               post=(x, out_spec, _relu2_grad) if act else None)


def _lin_dw_call(x, dy, act, carry=None):
    t, k = x.shape
    n = dy.shape[1]
    tk, tt = _tile(k, 1024), _tile(t, ROW_TILE)
    return _mm("lin_dw", x, dy, grid=(k // tk, t // tt),
               a_spec=pl.BlockSpec((tt, tk), lambda s, r: (r, s)),
               b_spec=pl.BlockSpec((tt, n), lambda s, r: (r, 0)),
               out_spec=pl.BlockSpec((tk, n), lambda s, r: (s, 0)),
               out_shape=jax.ShapeDtypeStruct((k, n), BF16), dims=TN, acc_axis=1, acc_shape=(tk, n), carry=carry,
               a_pre=_relu2 if act else None)


def _make_linear(act):
    @jax.custom_vjp
    def linear(x, w):
        return _lin_fwd_call(x, w, act)

    def fwd(x, w):
        return _lin_fwd_call(x, w, act), (x, w)

    def bwd(res, dy):
        x, w = res
        return _lin_dx_call(dy, w, x, act), _lin_dw_call(x, dy, act)

    linear.defvjp(fwd, bwd)

    @jax.custom_vjp
    def linear_next(x, w, nxt):
        return next_fwd(x, w, nxt)[0]

    def next_fwd(x, w, nxt):
        y, g_next = _lin_fwd_call(x, w, act, carry=("gather", nxt.astype(BF16)))
        return (y, g_next), (x, w)

    def next_bwd(res, cts):
        x, w = res
        dy, dg_next = cts
        dx, near = _lin_dx_call(dy, w, x, act, carry=("exchange", dg_next, NEAR))
        dw, far = _lin_dw_call(x, dy, act, carry=("exchange", dg_next, FAR))
        return dx, dw, sum_slots(near, far)

    linear_next.defvjp(next_fwd, next_bwd)
    return linear, linear_next


linear, linear_next = _make_linear(False)
linear_relu2, linear_relu2_next = _make_linear(True)


@jax.custom_vjp
def grouped_mm(x, w):
    return _grouped_fwd(x, w)[0]


def _grouped_fwd(x, w):
    t = x.shape[0]
    p, kin, kout = w.shape
    tm = _tile(t, 1024)
    y = _mm("grp_fwd", x, w, grid=(t // tm, p),
            a_spec=pl.BlockSpec((tm, kin), lambda i, q: (i, q)),
            b_spec=pl.BlockSpec((None, kin, kout), lambda i, q: (q, 0, 0)),
            out_spec=pl.BlockSpec((tm, kout), lambda i, q: (i, q)),
            out_shape=jax.ShapeDtypeStruct((t, p * kout), F32), dims=NN)
    return y, (x, w)


def _grouped_bwd(res, dy):
    x, w = res
    t = x.shape[0]
    p, kin, kout = w.shape
    tm = _tile(t, 1024)
    dx = _mm("grp_dx", dy, w, grid=(t // tm, p),
             a_spec=pl.BlockSpec((tm, kout), lambda i, q: (i, q)),
             b_spec=pl.BlockSpec((None, kin, kout), lambda i, q: (q, 0, 0)),
             out_spec=pl.BlockSpec((tm, kin), lambda i, q: (i, q)),
             out_shape=jax.ShapeDtypeStruct((t, p * kin), F32), dims=NT)
    dw = _mm("grp_dw", x, dy, grid=(p, t // tm),
             a_spec=pl.BlockSpec((tm, kin), lambda q, s: (s, q)),
             b_spec=pl.BlockSpec((tm, kout), lambda q, s: (s, q)),
             out_spec=pl.BlockSpec((None, kin, kout), lambda q, s: (q, 0, 0)),
             out_shape=jax.ShapeDtypeStruct((p, kin, kout), F32), dims=TN, acc_axis=1, acc_shape=(kin, kout))
    return dx, dw


grouped_mm.defvjp(_grouped_fwd, _grouped_bwd)


def _row_ts(widths, seq):
    per_row = 4 * sum(widths)
    ts = 512
    while ts > 8 and ts * per_row * 2 > 24 * 1024 * 1024:
        ts //= 2
    return min(ts, seq)


def make_rowop(name, f, n_row, n_batch, n_vec, f_bwd=None):
    n_in = n_row + n_batch + n_vec

    def in_specs(args, ts):
        specs = []
        for a in args[:n_row]:
            specs.append(pl.BlockSpec((None, ts, a.shape[2]), lambda b, s: (b, s, 0)))
        for a in args[n_row:n_row + n_batch]:
            specs.append(pl.BlockSpec((None, 1, a.shape[2]), lambda b, s: (b, 0, 0)))
        for a in args[n_row + n_batch:]:
            specs.append(pl.BlockSpec((1, a.shape[1]), lambda b, s: (0, 0)))
        return specs

    def out_struct(args, ts):
        blocks = [jax.ShapeDtypeStruct((ts, a.shape[2]), F32) for a in args[:n_row]]
        blocks += [jax.ShapeDtypeStruct((1, a.shape[2]), F32) for a in args[n_row:n_row + n_batch]]
        blocks += [jax.ShapeDtypeStruct((1, a.shape[1]), F32) for a in args[n_row + n_batch:]]
        return jax.eval_shape(f, *blocks)

    def run_fwd(args):
        bsz, seq = args[0].shape[:2]
        outs0 = out_struct(args, 8)
        widths = [a.shape[2] for a in args[:n_row]] + [o.shape[1] for o in outs0]
        ts = _row_ts(widths, seq)

        def body(*refs):
            outs = f(*[r[...] for r in refs[:n_in]])
            for o_ref, o in zip(refs[n_in:], outs):
                o_ref[...] = o

        return pl.pallas_call(
            body, name=name + "_fwd", grid=(bsz, seq // ts), in_specs=in_specs(args, ts),
            out_specs=[pl.BlockSpec((None, ts, o.shape[1]), lambda b, s: (b, s, 0)) for o in outs0],
            out_shape=[jax.ShapeDtypeStruct((bsz, seq, o.shape[1]), F32) for o in outs0],
            compiler_params=_params(("parallel", "parallel")))(*args)

    def run_bwd(args, cts):
        bsz, seq = args[0].shape[:2]
        widths = [a.shape[2] for a in args[:n_row]] * 2 + [c.shape[2] for c in cts]
        ts = _row_ts(widths, seq)
        n_ct = len(cts)

        def body(*refs):
            ins = [r[...] for r in refs[:n_in]]
            ct = [r[...] for r in refs[n_in:n_in + n_ct]]
            if f_bwd is None:
                _, vjp = jax.vjp(f, *ins)
                grads = vjp(tuple(ct))
            else:
                grads = f_bwd(ins, ct)
            g_refs = refs[n_in + n_ct:]
            b, s = pl.program_id(0), pl.program_id(1)
            for i in range(n_row):
                g_refs[i][...] = grads[i]
            for i in range(n_row, n_row + n_batch):
                @pl.when(s == 0)
                def _(i=i):
                    g_refs[i][...] = grads[i]

                @pl.when(s > 0)
                def _(i=i):
                    g_refs[i][...] += grads[i]
            for i in range(n_row + n_batch, n_in):
                first = jnp.logical_and(b == 0, s == 0)

                @pl.when(first)
                def _(i=i):
                    g_refs[i][...] = grads[i]

                @pl.when(jnp.logical_not(first))
                def _(i=i):
                    g_refs[i][...] += grads[i]

        ct_specs = [pl.BlockSpec((None, ts, c.shape[2]), lambda b, s: (b, s, 0)) for c in cts]
        return pl.pallas_call(
            body, name=name + "_bwd", grid=(bsz, seq // ts), in_specs=in_specs(args, ts) + ct_specs,
            out_specs=in_specs(args, ts), out_shape=[jax.ShapeDtypeStruct(a.shape, F32) for a in args],
            compiler_params=_params(("arbitrary", "arbitrary")))(*args, *cts)

    @jax.custom_vjp
    def op(*args):
        return tuple(run_fwd(args))

    def fwd(*args):
        return tuple(run_fwd(args)), args

    def bwd(args, cts):
        return tuple(run_bwd(args, list(cts)))

    op.defvjp(fwd, bwd)
    return op


def _modnorm_f(h, scale, shift, g):
    y = h * lax.rsqrt(jnp.mean(h * h, axis=-1, keepdims=True) + EPS) * g
    return (y * (1.0 + scale) + shift,)


def _gate_res_f(h, y, gate):
    return (h + gate * y,)


def _res_modnorm_f(h, y, gate, scale, shift, g):
    h2 = h + gate * y
    return (h2,) + _modnorm_f(h2, scale, shift, g)


def _relu2_f(a):
    return (jnp.square(jnp.maximum(a, 0.0)),)


def _gelu_skip_f(y, u, d):
    return (jax.nn.gelu(y + d * u),)


def _glu_f(vg):
    n = vg.shape[1] // 2
    return (vg[:, :n] * jax.nn.sigmoid(vg[:, n:]),)


def _glu_b(ins, cts):
    (vg,), (ct,) = ins, cts
    n = vg.shape[1] // 2
    sg = jax.nn.sigmoid(vg[:, n:])
    return (jnp.concatenate([ct * sg, ct * vg[:, :n] * sg * (1.0 - sg)], axis=1),)


def _combine_f(o1, o2, o3, l1, l2, l3):
    m = jnp.maximum(jnp.maximum(l1, l2), l3)
    e1, e2, e3 = jnp.exp(l1 - m), jnp.exp(l2 - m), jnp.exp(l3 - m)
    return ((e1 * o1 + e2 * o2 + e3 * o3) / (e1 + e2 + e3),)


def _final_loss_f(h, target, g):
    y = h * lax.rsqrt(jnp.mean(h * h, axis=-1, keepdims=True) + EPS) * g
    return (0.5 * jnp.mean(jnp.square(y - target), axis=-1, keepdims=True),)


modnorm = make_rowop("modnorm", _modnorm_f, 1, 2, 1)
gate_res = make_rowop("gate_res", _gate_res_f, 2, 1, 0)
res_modnorm = make_rowop("res_modnorm", _res_modnorm_f, 2, 3, 1)
relu2 = make_rowop("relu2", _relu2_f, 1, 0, 0)
gelu_skip = make_rowop("gelu_skip", _gelu_skip_f, 2, 0, 1)
glu = make_rowop("glu", _glu_f, 1, 0, 0, _glu_b)
combine = make_rowop("combine", _combine_f, 6, 0, 0)
final_loss = make_rowop("final_loss", _final_loss_f, 2, 0, 1)


SCAN_TB = 64
HALF = 512


def _scan_fwd_call(bu, a_re, a_im):
    bsz, rows, width = bu.shape
    seq = rows // 8
    tb = _tile(seq, SCAN_TB)

    def body(bu_ref, ar_ref, ai_ref, o_ref, carry):
        @pl.when(pl.program_id(0) == 0)
        def _():
            carry[...] = jnp.zeros_like(carry)

        ar, ai = ar_ref[...], ai_ref[...]

        def step(t, st):
            r0 = pl.multiple_of(t * 8, 8)
            new = []
            for b in range(bsz):
                sre, sim = st[2 * b], st[2 * b + 1]
                nre = ar * sre - ai * sim + bu_ref[b, pl.ds(r0, 8), pl.ds(0, HALF)]
                nim = ar * sim + ai * sre + bu_ref[b, pl.ds(r0, 8), pl.ds(HALF, HALF)]
                o_ref[b, pl.ds(r0, 8), pl.ds(0, HALF)] = nre
                o_ref[b, pl.ds(r0, 8), pl.ds(HALF, HALF)] = nim
                new += [nre, nim]
            return tuple(new)

        fin = lax.fori_loop(0, tb, step, tuple(carry[k] for k in range(2 * bsz)), unroll=4)
        for k in range(2 * bsz):
            carry[k] = fin[k]

    blk = pl.BlockSpec((bsz, tb * 8, width), lambda i: (0, i, 0))
    vec = pl.BlockSpec((8, HALF), lambda i: (0, 0))
    return pl.pallas_call(body, name="s5_scan_fwd", grid=(seq // tb,), in_specs=[blk, vec, vec], out_specs=blk,
                          out_shape=jax.ShapeDtypeStruct(bu.shape, F32),
                          scratch_shapes=[pltpu.VMEM((2 * bsz, 8, HALF), F32)],
                          compiler_params=_params(("arbitrary",)))(bu, a_re, a_im)


def _scan_bwd_call(g, st, a_re, a_im):
    bsz, rows, width = g.shape
    seq = rows // 8
    tb = _tile(seq, SCAN_TB)
    nt = seq // tb

    def body(g_ref, st_ref, prev_ref, ar_ref, ai_ref, db_ref, dar_ref, dai_ref, carry):
        i = pl.program_id(0)

        @pl.when(i == 0)
        def _():
            carry[...] = jnp.zeros_like(carry)
            dar_ref[...] = jnp.zeros_like(dar_ref)
            dai_ref[...] = jnp.zeros_like(dai_ref)

        ar, ai = ar_ref[...], ai_ref[...]

        def lam_step(b, r0, lre, lim):
            nre = g_ref[b, pl.ds(r0, 8), pl.ds(0, HALF)] + ar * lre + ai * lim
            nim = g_ref[b, pl.ds(r0, 8), pl.ds(HALF, HALF)] - ai * lre + ar * lim
            db_ref[b, pl.ds(r0, 8), pl.ds(0, HALF)] = nre
            db_ref[b, pl.ds(r0, 8), pl.ds(HALF, HALF)] = nim
            return nre, nim

        def step(k, c):
            t = tb - 1 - k
            r0 = pl.multiple_of(t * 8, 8)
            p0 = pl.multiple_of(t * 8 - 8, 8)
            lam, dar, dai = list(c[:2 * bsz]), c[2 * bsz], c[2 * bsz + 1]
            for b in range(bsz):
                nre, nim = lam_step(b, r0, lam[2 * b], lam[2 * b + 1])
                pre = st_ref[b, pl.ds(p0, 8), pl.ds(0, HALF)]
                pim = st_ref[b, pl.ds(p0, 8), pl.ds(HALF, HALF)]
                dar = dar + nre * pre + nim * pim
                dai = dai + nim * pre - nre * pim
                lam[2 * b], lam[2 * b + 1] = nre, nim
            return tuple(lam) + (dar, dai)

        zero = jnp.zeros((8, HALF), F32)
        init = tuple(carry[k] for k in range(2 * bsz)) + (zero, zero)
        c = lax.fori_loop(0, tb - 1, step, init, unroll=4)
        lam, dar, dai = list(c[:2 * bsz]), c[2 * bsz], c[2 * bsz + 1]
        keep = jnp.where(i == nt - 1, 0.0, 1.0)
        for b in range(bsz):
            nre, nim = lam_step(b, 0, lam[2 * b], lam[2 * b + 1])
            pre = prev_ref[b, :, pl.ds(0, HALF)] * keep
            pim = prev_ref[b, :, pl.ds(HALF, HALF)] * keep
            dar = dar + nre * pre + nim * pim
            dai = dai + nim * pre - nre * pim
            carry[2 * b], carry[2 * b + 1] = nre, nim
        dar_ref[...] += dar
        dai_ref[...] += dai

    blk = pl.BlockSpec((bsz, tb * 8, width), lambda i: (0, nt - 1 - i, 0))
    prev = pl.BlockSpec((bsz, 8, width), lambda i: (0, jnp.maximum((nt - 1 - i) * tb - 1, 0), 0))
    vec = pl.BlockSpec((8, HALF), lambda i: (0, 0))
    return pl.pallas_call(
        body, name="s5_scan_bwd", grid=(nt,), in_specs=[blk, blk, prev, vec, vec], out_specs=[blk, vec, vec],
        out_shape=[jax.ShapeDtypeStruct(g.shape, F32), jax.ShapeDtypeStruct((8, HALF), F32),
                   jax.ShapeDtypeStruct((8, HALF), F32)],
        scratch_shapes=[pltpu.VMEM((2 * bsz, 8, HALF), F32)],
        compiler_params=_params(("arbitrary",)))(g, st, st, a_re, a_im)


@jax.custom_vjp
def s5_scan(bu, a_re, a_im):
    return _scan_fwd_call(bu, a_re, a_im)


def _s5_scan_fwd(bu, a_re, a_im):
    st = _scan_fwd_call(bu, a_re, a_im)
    return st, (st, a_re, a_im)


def _s5_scan_bwd(res, g):
    st, a_re, a_im = res
    return tuple(_scan_bwd_call(g, st, a_re, a_im))


s5_scan.defvjp(_s5_scan_fwd, _s5_scan_bwd)


SSM_TB = 64
PITCH = SSM_TB + 8
PIECES = 8
CH = 128
NCH = 2 * HALF // LANES
NRE = NCH // 2


def _slab(t):
    return pl.ds(t, PIECES, stride=PITCH)


def _put_rows(ref, b, q, val, tb):
    for j in range(NCH):
        ref[b, j, pl.ds(q * PITCH, tb), :] = val[:, j * LANES:(j + 1) * LANES]


def _get_rows(ref, q, bsz, tb):
    return jnp.concatenate(
        [jnp.concatenate([ref[b, j, pl.ds(q * PITCH, tb), :] for j in range(NCH)], axis=1) for b in range(bsz)], axis=0)


def _chunks(a):
    return [a[:, j * LANES:(j + 1) * LANES] for j in range(a.shape[1] // LANES)]


def _ssm_fwd_call(u, b_mat, c_mat, a_re, a_im):
    bsz, seq, d = u.shape
    tb = SSM_TB
    nt = seq // tb
    assert seq % tb == 0 and d == PIECES * CH

    def body(u_ref, b_ref, c_ref, ar_ref, ai_ref, y_ref, st_ref, end_ref, bu, carry):
        @pl.when(pl.program_id(0) == 0)
        def _():
            carry[...] = jnp.zeros_like(carry)

        u2 = u_ref[...].reshape(bsz * tb, d).astype(BF16)
        for q in range(PIECES):
            r = lax.dot_general(u2[:, q * CH:(q + 1) * CH], b_ref[q], NN, preferred_element_type=F32)
            for b in range(bsz):
                _put_rows(bu, b, q, r[b * tb:(b + 1) * tb], tb)
        ar, ai = _chunks(ar_ref[...]), _chunks(ai_ref[...])
        st = [[carry[b, j] for j in range(NCH)] for b in range(bsz)]
        for t in range(tb):
            for b in range(bsz):
                for j in range(NRE):
                    sre, sim = st[b][j], st[b][NRE + j]
                    nre = ar[j] * sre - ai[j] * sim + bu[b, j, _slab(t), :]
                    nim = ar[j] * sim + ai[j] * sre + bu[b, NRE + j, _slab(t), :]
                    st_ref[b, j, _slab(t), :] = nre
                    st_ref[b, NRE + j, _slab(t), :] = nim
                    st[b][j], st[b][NRE + j] = nre, nim
        for b in range(bsz):
            for j in range(NCH):
                carry[b, j] = st[b][j]
                end_ref[b, j] = st[b][j]
        for q in range(PIECES):
            s2 = _get_rows(st_ref, q, bsz, tb).astype(BF16)
            r = lax.dot_general(s2, c_ref[q], NN, preferred_element_type=F32)
            for b in range(bsz):
                y_ref[b, :, q * CH:(q + 1) * CH] = r[b * tb:(b + 1) * tb]

    rows = pl.BlockSpec((bsz, tb, d), lambda i: (0, i, 0))
    vec = pl.BlockSpec((PIECES, HALF), lambda i: (0, 0))
    return pl.pallas_call(
        body, name="s5_ssm_fwd", grid=(nt,),
        in_specs=[rows, pl.BlockSpec((PIECES, CH, 2 * HALF), lambda i: (0, 0, 0)),
                  pl.BlockSpec((PIECES, 2 * HALF, CH), lambda i: (0, 0, 0)), vec, vec],
        out_specs=[rows, pl.BlockSpec((bsz, None, NCH, PIECES * PITCH, LANES), lambda i: (0, i, 0, 0, 0)),
                   pl.BlockSpec((bsz, None, NCH, PIECES, LANES), lambda i: (0, i, 0, 0, 0))],
        out_shape=[jax.ShapeDtypeStruct((bsz, seq, d), F32),
                   jax.ShapeDtypeStruct((bsz, nt, NCH, PIECES * PITCH, LANES), F32),
                   jax.ShapeDtypeStruct((bsz, nt, NCH, PIECES, LANES), F32)],
        scratch_shapes=[pltpu.VMEM((bsz, NCH, PIECES * PITCH, LANES), F32), pltpu.VMEM((bsz, NCH, PIECES, LANES), F32)],
        compiler_params=_params(("arbitrary",)))(u, b_mat, c_mat, a_re, a_im)


def _ssm_bwd_call(dy, u, st, ends, b_mat, c_mat, a_re, a_im):
    bsz, seq, d = u.shape
    tb = SSM_TB
    nt = seq // tb

    def body(dy_ref, u_ref, st_ref, prev_ref, b_ref, c_ref, ar_ref, ai_ref, du_ref, db_hbm, dc_hbm, dar_ref, dai_ref,
             lam, carry, db_acc, dc_acc):
        i = pl.program_id(0)

        @pl.when(i == 0)
        def _():
            carry[...] = jnp.zeros_like(carry)
            db_acc[...] = jnp.zeros_like(db_acc)
            dc_acc[...] = jnp.zeros_like(dc_acc)
            dar_ref[...] = jnp.zeros_like(dar_ref)
            dai_ref[...] = jnp.zeros_like(dai_ref)

        dy2 = dy_ref[...].reshape(bsz * tb, d).astype(BF16)
        u2 = u_ref[...].reshape(bsz * tb, d).astype(BF16)
        for q in range(PIECES):
            dyq = dy2[:, q * CH:(q + 1) * CH]
            g = lax.dot_general(dyq, c_ref[q], NT, preferred_element_type=F32)
            for b in range(bsz):
                _put_rows(lam, b, q, g[b * tb:(b + 1) * tb], tb)
            s2 = _get_rows(st_ref, q, bsz, tb).astype(BF16)
            dc_acc[q] += lax.dot_general(s2, dyq, TN, preferred_element_type=F32)

        ar, ai = _chunks(ar_ref[...]), _chunks(ai_ref[...])
        keep = jnp.where(i == nt - 1, 0.0, 1.0)
        lm = [[carry[b, j] for j in range(NCH)] for b in range(bsz)]
        dar = [jnp.zeros((PIECES, LANES), F32) for _ in range(NRE)]
        dai = [jnp.zeros((PIECES, LANES), F32) for _ in range(NRE)]
        for t in range(tb - 1, -1, -1):
            for b in range(bsz):
                for j in range(NRE):
                    lre, lim = lm[b][j], lm[b][NRE + j]
                    nre = lam[b, j, _slab(t), :] + ar[j] * lre + ai[j] * lim
                    nim = lam[b, NRE + j, _slab(t), :] - ai[j] * lre + ar[j] * lim
                    lam[b, j, _slab(t), :] = nre
                    lam[b, NRE + j, _slab(t), :] = nim
                    if t > 0:
                        pre, pim = st_ref[b, j, _slab(t - 1), :], st_ref[b, NRE + j, _slab(t - 1), :]
                    else:
                        pre, pim = prev_ref[b, j] * keep, prev_ref[b, NRE + j] * keep
                    dar[j] = dar[j] + nre * pre + nim * pim
                    dai[j] = dai[j] + nim * pre - nre * pim
                    lm[b][j], lm[b][NRE + j] = nre, nim
        for b in range(bsz):
            for j in range(NCH):
                carry[b, j] = lm[b][j]
        dar_ref[...] += jnp.concatenate(dar, axis=1)
        dai_ref[...] += jnp.concatenate(dai, axis=1)

        for q in range(PIECES):
            l2 = _get_rows(lam, q, bsz, tb).astype(BF16)
            r = lax.dot_general(l2, b_ref[q], NT, preferred_element_type=F32)
            for b in range(bsz):
                du_ref[b, :, q * CH:(q + 1) * CH] = r[b * tb:(b + 1) * tb]
            db_acc[q] += lax.dot_general(u2[:, q * CH:(q + 1) * CH], l2, TN, preferred_element_type=F32)

        @pl.when(i == nt - 1)
        def _():
            pltpu.sync_copy(db_acc, db_hbm)
            pltpu.sync_copy(dc_acc, dc_hbm)

    rows = pl.BlockSpec((bsz, tb, d), lambda i: (0, nt - 1 - i, 0))
    vec = pl.BlockSpec((PIECES, HALF), lambda i: (0, 0))
    hbm = pl.BlockSpec(memory_space=pl.ANY)
    return pl.pallas_call(
        body, name="s5_ssm_bwd", grid=(nt,),
        in_specs=[rows, rows,
                  pl.BlockSpec((bsz, None, NCH, PIECES * PITCH, LANES), lambda i: (0, nt - 1 - i, 0, 0, 0)),
                  pl.BlockSpec((bsz, None, NCH, PIECES, LANES), lambda i: (0, jnp.maximum(nt - 2 - i, 0), 0, 0, 0)),
                  pl.BlockSpec((PIECES, CH, 2 * HALF), lambda i: (0, 0, 0)),
                  pl.BlockSpec((PIECES, 2 * HALF, CH), lambda i: (0, 0, 0)), vec, vec],
        out_specs=[rows, hbm, hbm, vec, vec],
        out_shape=[jax.ShapeDtypeStruct((bsz, seq, d), F32), jax.ShapeDtypeStruct((PIECES, CH, 2 * HALF), F32),
                   jax.ShapeDtypeStruct((PIECES, 2 * HALF, CH), F32), jax.ShapeDtypeStruct((PIECES, HALF), F32),
                   jax.ShapeDtypeStruct((PIECES, HALF), F32)],
        scratch_shapes=[pltpu.VMEM((bsz, NCH, PIECES * PITCH, LANES), F32), pltpu.VMEM((bsz, NCH, PIECES, LANES), F32),
                        pltpu.VMEM((PIECES, CH, 2 * HALF), F32), pltpu.VMEM((PIECES, 2 * HALF, CH), F32)],
        compiler_params=_params(("arbitrary",)))(dy, u, st, ends, b_mat, c_mat, a_re, a_im)


@jax.custom_vjp
def s5_ssm(u, b_mat, c_mat, a_re, a_im):
    return _s5_ssm_fwd(u, b_mat, c_mat, a_re, a_im)[0]


def _s5_ssm_fwd(u, b_mat, c_mat, a_re, a_im):
    b16, c16 = b_mat.astype(BF16), c_mat.astype(BF16)
    y, st, ends = _ssm_fwd_call(u, b16, c16, a_re, a_im)
    return y, (u, st, ends, b16, c16, a_re, a_im)


def _s5_ssm_bwd(res, dy):
    return tuple(_ssm_bwd_call(dy, *res))


s5_ssm.defvjp(_s5_ssm_fwd, _s5_ssm_bwd)


ATT_HW = 2 * HEAD
ATT_HB = 1024 // ATT_HW


def _branch_geometry(dil, seq):
    sub = seq // dil
    assert sub % QBLK == 0
    return sub // QBLK


def _drows(dil, r, start, size):
    if dil == 1:
        return pl.ds(start, size)
    return pl.ds(r + start * dil, size, stride=dil)


def _masks():
    qi = lax.broadcasted_iota(jnp.int32, (QBLK, 2 * QBLK), 0)
    kj = lax.broadcasted_iota(jnp.int32, (QBLK, 2 * QBLK), 1) - QBLK
    dist = qi - kj
    band = jnp.logical_and(dist >= 0, dist <= QBLK)
    ci = lax.broadcasted_iota(jnp.int32, (QBLK, QBLK), 0)
    cj = lax.broadcasted_iota(jnp.int32, (QBLK, QBLK), 1)
    return band, ci >= cj


def _attn_specs(i, seq):
    q_spec = pl.BlockSpec((None, seq, ATT_HW), lambda b, h: (b, 0, i * ATT_HB + h))
    k_spec = pl.BlockSpec((None, seq, ATT_HW), lambda b, h: (b, 0, i * ATT_HB + h))
    v_spec = pl.BlockSpec((None, seq, ATT_HW), lambda b, h: (b, 0, (3 + i) * ATT_HB + h))
    o_spec = pl.BlockSpec((None, seq, ATT_HW), lambda b, h: (b, 0, h))
    return q_spec, k_spec, v_spec, o_spec


def _attn_fwd_call(q_all, kv_all, i, dil):
    bsz, seq, _ = q_all.shape
    nb = _branch_geometry(dil, seq)
    scale = HEAD ** -0.5

    def body(q_ref, k_ref, v_ref, o_ref, l_ref):
        band, causal = _masks()

        def one(r, n, mask):
            qs = _drows(dil, r, n * QBLK, QBLK)
            ks = qs if n == 0 else _drows(dil, r, (n - 1) * QBLK, 2 * QBLK)
            q2 = q_ref[qs, :].astype(BF16)
            k2 = k_ref[ks, :].astype(BF16)
            v2 = v_ref[ks, :].astype(BF16)
            outs, lses = [], []
            for h in range(ATT_HW // HEAD):
                ls = slice(h * HEAD, (h + 1) * HEAD)
                s = lax.dot_general(q2[:, ls], k2[:, ls], NT, preferred_element_type=F32) * scale
                s = jnp.where(mask, s, NEG)
                m = jnp.max(s, axis=-1, keepdims=True)
                p = jnp.exp(s - m)
                den = jnp.sum(p, axis=-1, keepdims=True)
                outs.append(lax.dot_general(p.astype(BF16), v2[:, ls], NN, preferred_element_type=F32) / den)
                lses.append(jnp.broadcast_to(m + jnp.log(den), (QBLK, HEAD)))
            o_ref[qs, :] = jnp.concatenate(outs, axis=1)
            l_ref[qs, :] = jnp.concatenate(lses, axis=1)

        for r in range(dil):
            for n in range(nb):
                one(r, n, causal if n == 0 else band)

    q_spec, k_spec, v_spec, o_spec = _attn_specs(i, seq)
    shape = jax.ShapeDtypeStruct((bsz, seq, 1024), F32)
    return pl.pallas_call(
        body, name="attn_fwd_d%d" % dil, grid=(bsz, ATT_HB), in_specs=[q_spec, k_spec, v_spec],
        out_specs=[o_spec, o_spec], out_shape=[shape, shape],
        compiler_params=_params(("parallel", "parallel")))(q_all, kv_all, kv_all)


def _attn_bwd_call(q_all, kv_all, o, l, do, dl, i, dil, grads):
    bsz, seq, _ = q_all.shape
    nb = _branch_geometry(dil, seq)
    scale = HEAD ** -0.5
    first = grads is None

    def body(q_ref, k_ref, v_ref, o_ref, l_ref, do_ref, dl_ref, *rest):
        dq_ref, dk_ref, dv_ref = rest[-3:]
        band, causal = _masks()

        def one(r, n, mask):
            qs = _drows(dil, r, n * QBLK, QBLK)
            ks = qs if n == 0 else _drows(dil, r, (n - 1) * QBLK, 2 * QBLK)
            q2 = q_ref[qs, :].astype(BF16)
            k2 = k_ref[ks, :].astype(BF16)
            v2 = v_ref[ks, :].astype(BF16)
            l2, do2 = l_ref[qs, :], do_ref[qs, :]
            t2 = dl_ref[qs, :] - do2 * o_ref[qs, :]
            dqs, dks, dvs = [], [], []
            for h in range(ATT_HW // HEAD):
                ls = slice(h * HEAD, (h + 1) * HEAD)
                q, k, v = q2[:, ls], k2[:, ls], v2[:, ls]
                s = lax.dot_general(q, k, NT, preferred_element_type=F32) * scale
                s = jnp.where(mask, s, NEG)
                p = jnp.exp(s - l2[:, h * HEAD:h * HEAD + 1])
                row = jnp.sum(t2[:, ls], axis=-1, keepdims=True)
                d_ob = do2[:, ls].astype(BF16)
                dp = lax.dot_general(d_ob, v, NT, preferred_element_type=F32)
                ds = (p * (dp + row)).astype(BF16)
                dqs.append(lax.dot_general(ds, k, NN, preferred_element_type=F32) * scale)
                dks.append(lax.dot_general(ds, q, TN, preferred_element_type=F32) * scale)
                dvs.append(lax.dot_general(p.astype(BF16), d_ob, TN, preferred_element_type=F32))
            dq_ref[qs, :] = jnp.concatenate(dqs, axis=1)
            return jnp.concatenate(dks, axis=1), jnp.concatenate(dvs, axis=1)

        for r in range(dil):
            dk_cur, dv_cur = one(r, 0, causal)
            for n in range(1, nb):
                dk, dv = one(r, n, band)
                ks = _drows(dil, r, (n - 1) * QBLK, QBLK)
                dk_ref[ks, :] = dk_cur + dk[:QBLK]
                dv_ref[ks, :] = dv_cur + dv[:QBLK]
                dk_cur, dv_cur = dk[QBLK:], dv[QBLK:]
            ks = _drows(dil, r, (nb - 1) * QBLK, QBLK)
            dk_ref[ks, :] = dk_cur
            dv_ref[ks, :] = dv_cur

    q_spec, k_spec, v_spec, o_spec = _attn_specs(i, seq)
    in_specs = [q_spec, k_spec, v_spec, o_spec, o_spec, o_spec, o_spec]
    args = [q_all, kv_all, kv_all, o, l, do, dl]
    aliases = {}
    if not first:
        in_specs += [pl.BlockSpec(memory_space=pl.ANY)] * 3
        args += list(grads)
        aliases = {7: 0, 8: 1, 9: 2}
    shape = jax.ShapeDtypeStruct((bsz, seq, 3 * 1024), F32)
    return pl.pallas_call(
        body, name="attn_bwd_d%d" % dil, grid=(bsz, ATT_HB), in_specs=in_specs,
        out_specs=[q_spec, q_spec, q_spec], out_shape=[shape, shape, shape], input_output_aliases=aliases,
        compiler_params=_params(("parallel", "parallel")))(*args)


@jax.custom_vjp
def attn_branches(q_all, kv_all):
    return _attn_branches_fwd(q_all, kv_all)[0]


def _attn_branches_fwd(q_all, kv_all):
    outs = []
    for i, dil in enumerate(BRANCH_DIL):
        outs += list(_attn_fwd_call(q_all, kv_all, i, dil))
    return tuple(outs), (q_all, kv_all, tuple(outs))


def _attn_branches_bwd(res, cts):
    q_all, kv_all, outs = res
    grads = None
    for i, dil in enumerate(BRANCH_DIL):
        grads = _attn_bwd_call(q_all, kv_all, outs[2 * i], outs[2 * i + 1], cts[2 * i], cts[2 * i + 1], i, dil, grads)
    dq, dk, dv = grads
    return dq, jnp.concatenate([dk, dv], axis=-1)


attn_branches.defvjp(_attn_branches_fwd, _attn_branches_bwd)


def ada_fwd(c_all, w, b):
    n, d, cs = w.shape
    nb = c_all.shape[0]

    def body(c_ref, w_ref, b_ref, o_ref):
        a = jax.nn.silu(c_ref[...]).astype(BF16)
        o_ref[...] = lax.dot_general(a, w_ref[...].astype(BF16), NN, preferred_element_type=F32) + b_ref[...]

    return pl.pallas_call(
        body, name="ada_fwd", grid=(n,),
        in_specs=[pl.BlockSpec((nb, d), lambda i: (0, 0)), pl.BlockSpec((None, d, cs), lambda i: (i, 0, 0)),
                  pl.BlockSpec((None, 1, cs), lambda i: (i, 0, 0))],
        out_specs=pl.BlockSpec((None, nb, cs), lambda i: (i, 0, 0)),
        out_shape=jax.ShapeDtypeStruct((n, nb, cs), F32), compiler_params=_params(("parallel",)))(c_all, w, b)


def ada_bwd(c_all, dm):
    n, nb, cs = dm.shape
    d = c_all.shape[1]

    def body(c_ref, dm_ref, dw_ref, db_ref):
        a = jax.nn.silu(c_ref[...]).astype(BF16)
        g = dm_ref[...]
        dw_ref[...] = lax.dot_general(a, g.astype(BF16), TN, preferred_element_type=F32)
        db_ref[...] = jnp.sum(g, axis=0, keepdims=True)

    return pl.pallas_call(
        body, name="ada_bwd", grid=(n,),
        in_specs=[pl.BlockSpec((nb, d), lambda i: (0, 0)), pl.BlockSpec((None, nb, cs), lambda i: (i, 0, 0))],
        out_specs=[pl.BlockSpec((None, d, cs), lambda i: (i, 0, 0)), pl.BlockSpec((None, 1, cs), lambda i: (i, 0, 0))],
        out_shape=[jax.ShapeDtypeStruct((n, d, cs), F32), jax.ShapeDtypeStruct((n, 1, cs), F32)],
        compiler_params=_params(("parallel",)))(c_all, dm)


def all_gather(x, name):
    m, n = x.shape

    def body(x_ref, out_ref, send_sems, recv_sems, local_sem):
        px, py, pc = lax.axis_index("x"), lax.axis_index("y"), lax.axis_index("c")
        me, sibling = (px, py, pc), (px, py, 1 - pc)
        chips = [(1 - px, py), (px, 1 - py), (1 - px, 1 - py)]

        def rows(qx, qy, qc):
            return out_ref.at[pl.ds((4 * qx + 2 * qy + qc) * m, m), :]

        def copy(k, block, to, src=None):
            return pltpu.make_async_remote_copy(
                src_ref=rows(*block) if src is None else src, dst_ref=rows(*block),
                send_sem=send_sems.at[k], recv_sem=recv_sems.at[k], device_id=to,
                device_id_type=pl.DeviceIdType.MESH)

        mine = pltpu.make_async_copy(x_ref, rows(*me), local_sem)
        mine.start()
        first = [copy(0, me, sibling, src=x_ref)]
        first += [copy(1 + j, me, (*chip, pc), src=x_ref) for j, chip in enumerate(chips)]
        for cp in first:
            cp.start()
        passed = [copy(4 + j, (*chip, pc), sibling) for j, chip in enumerate(chips)]
        for j, chip in enumerate(chips):
            copy(1 + j, (*chip, pc), me).wait_recv()
            passed[j].start()
        copy(0, sibling, me).wait_recv()
        for j, chip in enumerate(chips):
            copy(4 + j, (*chip, 1 - pc), me).wait_recv()
        for cp in first + passed:
            cp.wait_send()
        mine.wait()

    out = pl.pallas_call(
        body, name=name, out_shape=jax.ShapeDtypeStruct((N_DEV * m, n), x.dtype),
        in_specs=[pl.BlockSpec(memory_space=pl.ANY)], out_specs=pl.BlockSpec(memory_space=pl.ANY),
        scratch_shapes=[pltpu.SemaphoreType.DMA((7,)), pltpu.SemaphoreType.DMA((7,)), pltpu.SemaphoreType.DMA(())],
    )(x)
    return out.reshape(N_DEV, m, n)


def exchange_partials(p, name):
    _, m, n = p.shape

    def body(p_ref, out_ref, send_sems, recv_sems, local_sem):
        px, py, pc = lax.axis_index("x"), lax.axis_index("y"), lax.axis_index("c")
        me = 4 * px + 2 * py + pc
        mine = pltpu.make_async_copy(p_ref.at[me], out_ref.at[me], local_sem)
        mine.start()
        copies = []
        for k in range(1, N_DEV):
            qx = 1 - px if k & 4 else px
            qy = 1 - py if k & 2 else py
            qc = 1 - pc if k & 1 else pc
            cp = pltpu.make_async_remote_copy(
                src_ref=p_ref.at[4 * qx + 2 * qy + qc], dst_ref=out_ref.at[me],
                send_sem=send_sems.at[k - 1], recv_sem=recv_sems.at[k - 1], device_id=(qx, qy, qc),
                device_id_type=pl.DeviceIdType.MESH)
            cp.start()
            copies.append(cp)
        for cp in copies:
            cp.wait()
        mine.wait()

    return pl.pallas_call(
        body, name=name, out_shape=jax.ShapeDtypeStruct(p.shape, p.dtype),
        in_specs=[pl.BlockSpec(memory_space=pl.ANY)], out_specs=pl.BlockSpec(memory_space=pl.ANY),
        scratch_shapes=[pltpu.SemaphoreType.DMA((7,)), pltpu.SemaphoreType.DMA((7,)), pltpu.SemaphoreType.DMA(())],
    )(p)


def sum_slots(*parts):
    _, r, n = parts[0].shape
    tr = _tile(r, max(16, (256 * 1024) // n))

    def body(*refs):
        acc = None
        for g_ref in refs[:-1]:
            for s in range(g_ref.shape[0]):
                term = g_ref[s].astype(F32)
                acc = term if acc is None else acc + term
        refs[-1][...] = acc

    return pl.pallas_call(
        body, name="sum_slots", grid=(r // tr,),
        in_specs=[pl.BlockSpec((p.shape[0], tr, n), lambda i: (0, i, 0)) for p in parts],
        out_specs=pl.BlockSpec((tr, n), lambda i: (i, 0)), out_shape=jax.ShapeDtypeStruct((r, n), F32),
        compiler_params=_params(("parallel",)))(*parts)


def adamw(w, m, v, g, name):
    nl, r, n = w.shape
    partials = g.ndim == 4
    tr = _tile(r, max(8, (256 * 1024) // n))
    c1 = 1.0 / (1.0 - ADAM_B1 ** ADAM_STEP)
    c2 = 1.0 / (1.0 - ADAM_B2 ** ADAM_STEP)

    def body(w_ref, m_ref, v_ref, g_ref, go_ref, d_ref, mo_ref, vo_ref):
        if partials:
            grad = g_ref[0].astype(F32)
            for s in range(1, N_DEV):
                grad = grad + g_ref[s].astype(F32)
        else:
            grad = g_ref[...]
        m_new = ADAM_B1 * m_ref[...] + (1.0 - ADAM_B1) * grad
        v_new = ADAM_B2 * v_ref[...] + (1.0 - ADAM_B2) * grad * grad
        go_ref[...] = grad
        mo_ref[...] = m_new
        vo_ref[...] = v_new
        d_ref[...] = -ADAM_LR * ((m_new * c1) / (jnp.sqrt(v_new * c2) + ADAM_EPS) + ADAM_WD * w_ref[...])

    spec = pl.BlockSpec((None, tr, n), lambda l, i: (l, i, 0))
    g_spec = pl.BlockSpec((None, N_DEV, tr, n), lambda l, i: (l, 0, i, 0)) if partials else spec
    shape = jax.ShapeDtypeStruct(w.shape, F32)
    return pl.pallas_call(
        body, name=name, grid=(nl, r // tr), in_specs=[spec, spec, spec, g_spec], out_specs=[spec] * 4,
        out_shape=[shape] * 4, compiler_params=_params(("parallel", "parallel")))(w, m, v, g)


def pack(arrays):
    flat = jnp.concatenate([a.reshape(-1).astype(F32) for a in arrays])
    rows = -(-flat.shape[0] // LANES)
    mult = 8 if rows <= 512 else 512
    rows = -(-rows // mult) * mult
    flat = jnp.pad(flat, (0, rows * LANES - flat.shape[0]))
    return flat.reshape(rows, LANES)


def unpack(slab, shapes):
    flat = slab.reshape(slab.shape[:-2] + (-1,))
    out, off = [], 0
    for s in shapes:
        size = math.prod(s)
        out.append(flat[..., off:off + size].reshape(flat.shape[:-1] + tuple(s)))
        off += size
    return out


def _s5_discretize(lam_re, lam_im, log_dt, b_re, b_im, c_re, c_im):
    dt = jnp.exp(log_dt)[:, None]
    xr, th = lam_re * dt, lam_im * dt
    er = jnp.exp(xr)
    a_re, a_im = er * jnp.cos(th), er * jnp.sin(th)
    am1 = jnp.expm1(xr) * jnp.cos(th) - 2.0 * jnp.square(jnp.sin(0.5 * th))
    den = lam_re * lam_re + lam_im * lam_im
    fr = (am1 * lam_re + a_im * lam_im) / den
    fi = (a_im * lam_re - am1 * lam_im) / den
    bb_re = fr[..., None] * b_re - fi[..., None] * b_im
    bb_im = fr[..., None] * b_im + fi[..., None] * b_re
    eye = jnp.eye(8, dtype=F32)

    def pack_b(bb):
        return jnp.einsum('qgpc,gh->qgchp', bb.reshape(8, 8, 64, 16), eye).reshape(8, 128, 512)

    def pack_c(cc):
        return jnp.einsum('qgcp,gh->qgphc', cc.reshape(8, 8, 16, 64), eye).reshape(8, 512, 128)

    b_mat = jnp.concatenate([pack_b(bb_re), pack_b(bb_im)], axis=-1)
    c_mat = jnp.concatenate([pack_c(c_re), pack_c(-c_im)], axis=1)
    return a_re.reshape(8, HALF), a_im.reshape(8, HALF), b_mat, c_mat


WEIGHT_ORDER = (("ssm_w_glu", 0, True), ("mlp_w1", 0, True), ("mlp_w2", 0, False),
                ("ssm_w_glu", 1, True), ("mlp_w1", 1, True), ("mlp_w2", 1, False), ("w_kv", None, True),
                ("attn_w_q", 0, True), ("attn_w_o", 0, False), ("mlp_w1", 2, True), ("mlp_w2", 2, False),
                ("attn_w_q", 1, True), ("attn_w_o", 1, False), ("mlp_w1", 3, True), ("mlp_w2", 3, False))


class _WeightChain:
    def __init__(self, first, upcoming):
        self.cur, self.upcoming, self.used = first, list(upcoming), 0

    def matmul(self, x, name, relu2_input=False):
        assert WEIGHT_ORDER[self.used][0] == name
        col = WEIGHT_ORDER[self.used][2]
        assert not (col and relu2_input)
        self.used += 1
        w = self.cur if col else self.cur.reshape(-1, self.cur.shape[-1])
        plain, with_next = (linear_col, linear_col_next) if col else (
            (linear_relu2, linear_relu2_next) if relu2_input else (linear, linear_next))
        if not self.upcoming:
            return plain(x, w)
        y, self.cur = with_next(x, w, self.upcoming.pop(0))
        return y


def _local_loss(diff, x, target, first_weight, shards):
    bsz, seq, d = x.shape
    t = bsz * seq
    rows3 = lambda a: a.reshape(bsz, seq, a.shape[-1])
    rows2 = lambda a: a.reshape(t, a.shape[-1])
    mods, kvmod, ln_g, ssm_d, kv_g, final_g = (diff[k] for k in ("mods", "kvmod", "ln_g", "ssm_d", "kv_g", "final_g"))
    chain = _WeightChain(first_weight, shards)

    def chunks(sub):
        mod = mods[sub]
        return mod[:, None, :d], mod[:, None, d:2 * d], mod[:, None, 2 * d:], ln_g[sub // 2, sub % 2][None]

    h = x
    kv_all = None
    shift, scale, gate, gain = chunks(0)
    (u,) = modnorm(h, scale, shift, gain)
    for layer in range(4):
        if layer == 2:
            (ukv,) = modnorm(h, kvmod[:, None, d:], kvmod[:, None, :d], kv_g[None])
            kv_all = rows3(chain.matmul(rows2(ukv), "w_kv"))
        if layer < 2:
            a_re, a_im, b_mat, c_mat = _s5_discretize(*(diff[k][layer] for k in (
                "ssm_lam_re", "ssm_lam_im", "ssm_log_dt", "ssm_b_re", "ssm_b_im", "ssm_c_re", "ssm_c_im")))
            y0 = s5_ssm(u, b_mat, c_mat, a_re, a_im)
            (z,) = gelu_skip(y0, u, ssm_d[layer][None])
            (y,) = glu(rows3(chain.matmul(rows2(z), "ssm_w_glu")))
        else:
            q_all = rows3(chain.matmul(rows2(u), "attn_w_q"))
            o1, l1, o2, l2, o3, l3 = attn_branches(q_all, kv_all)
            (o,) = combine(o1, o2, o3, l1, l2, l3)
            y = rows3(chain.matmul(rows2(o), "attn_w_o"))
        shift, scale, next_gate, gain = chunks(2 * layer + 1)
        h, u = res_modnorm(h, y, gate, scale, shift, gain)
        gate = next_gate
        pre = chain.matmul(rows2(u), "mlp_w1")
        y = rows3(chain.matmul(pre, "mlp_w2", relu2_input=True))
        if layer < 3:
            shift, scale, next_gate, gain = chunks(2 * layer + 2)
            h, u = res_modnorm(h, y, gate, scale, shift, gain)
            gate = next_gate
        else:
            (h,) = gate_res(h, y, gate)
    (row_loss,) = final_loss(h, target, final_g[None])
    return jnp.sum(row_loss)


SSM_NAMES = ("ssm_lam_re", "ssm_lam_im", "ssm_log_dt", "ssm_b_re", "ssm_b_im", "ssm_c_re", "ssm_c_im")
ARG_NAMES = ("x", "c", "ln_g", "ada_w", "ada_b") + SSM_NAMES + (
    "ssm_d", "ssm_w_glu", "kv_g", "kv_ada_w", "kv_ada_b", "w_kv", "attn_w_q", "attn_w_o", "mlp_w1", "mlp_w2", "final_g")
WEIGHT_NAMES = ARG_NAMES[2:]


def kernel(x, c, ln_g, ada_w, ada_b, ssm_lam_re, ssm_lam_im, ssm_log_dt, ssm_b_re, ssm_b_im, ssm_c_re, ssm_c_im, ssm_d, ssm_w_glu, kv_g, kv_ada_w, kv_ada_b, w_kv, attn_w_q, attn_w_o, mlp_w1, mlp_w2, final_g, loss_target, m_ln_g, m_ada_w, m_ada_b, m_ssm_lam_re, m_ssm_lam_im, m_ssm_log_dt, m_ssm_b_re, m_ssm_b_im, m_ssm_c_re, m_ssm_c_im, m_ssm_d, m_ssm_w_glu, m_kv_g, m_kv_ada_w, m_kv_ada_b, m_w_kv, m_attn_w_q, m_attn_w_o, m_mlp_w1, m_mlp_w2, m_final_g, v_ln_g, v_ada_w, v_ada_b, v_ssm_lam_re, v_ssm_lam_im, v_ssm_log_dt, v_ssm_b_re, v_ssm_b_im, v_ssm_c_re, v_ssm_c_im, v_ssm_d, v_ssm_w_glu, v_kv_g, v_kv_ada_w, v_kv_ada_b, v_w_kv, v_attn_w_q, v_attn_w_o, v_mlp_w1, v_mlp_w2, v_final_g):
    args = locals()
    w = {n: args[n] for n in WEIGHT_NAMES}
    mom = {n: args["m_" + n] for n in WEIGHT_NAMES}
    var = {n: args["v_" + n] for n in WEIGHT_NAMES}
    bsz, seq, d = x.shape
    me = 4 * lax.axis_index("x") + 2 * lax.axis_index("y") + lax.axis_index("c")

    small = all_gather(pack([c, ln_g, ssm_d]), "gather_small")
    c_parts, ln_parts, d_parts = unpack(small, [c.shape, ln_g.shape, ssm_d.shape])
    c_all = c_parts.reshape(N_DEV * bsz, d)
    ln_full = jnp.moveaxis(ln_parts, 0, 2).reshape(4, 2, d)
    d_full = jnp.moveaxis(d_parts, 0, 1).reshape(2, d)

    cs = ada_w.shape[-1]
    mod_cols = ada_fwd(c_all, ada_w.reshape(8, d, cs), ada_b.reshape(8, 1, cs))
    kcs = kv_ada_w.shape[-1]
    kv_cols = ada_fwd(c_all, kv_ada_w[None], jnp.zeros((1, 1, kcs), F32))
    mod_g = all_gather(mod_cols.reshape(8 * N_DEV * bsz, cs), "gather_mod")
    kv_g_all = all_gather(kv_cols.reshape(N_DEV * bsz, kcs), "gather_kvmod")
    mods_all = jnp.moveaxis(mod_g.reshape(N_DEV, 8, N_DEV * bsz, cs), 0, 2).reshape(8, N_DEV * bsz, N_DEV * cs)
    kvmod_all = jnp.moveaxis(kv_g_all, 0, 1).reshape(N_DEV * bsz, N_DEV * kcs) + kv_ada_b[None]
    mods = lax.dynamic_slice_in_dim(mods_all, me * bsz, bsz, axis=1)
    kvmod = lax.dynamic_slice_in_dim(kvmod_all, me * bsz, bsz, axis=0)

    shards = [w[n] if l is None else w[n][l] for n, l, _ in WEIGHT_ORDER]
    first_weight = all_gather(shards[0].astype(BF16), "gather_first_weight")

    diff = {"mods": mods, "kvmod": kvmod, "ln_g": ln_full, "ssm_d": d_full, "kv_g": kv_g, "final_g": final_g}
    diff.update({n: w[n] for n in SSM_NAMES})
    loss_local, (g_diff, grad_x, g_first, g_shards) = jax.value_and_grad(_local_loss, argnums=(0, 1, 3, 4))(
        diff, x, loss_target, first_weight, shards[1:])
    loss = lax.psum(loss_local, AXES)
    g_shards = [sum_slots(exchange_partials(g_first, "exchange_first_weight"))] + list(g_shards)

    dmod = all_gather(pack([g_diff["mods"], g_diff["kvmod"]]), "gather_dmod")
    dmods_p, dkv_p = unpack(dmod, [g_diff["mods"].shape, g_diff["kvmod"].shape])
    dmods_all = jnp.moveaxis(dmods_p, 0, 1).reshape(8, N_DEV * bsz, 3 * d)
    dkv_all = dkv_p.reshape(N_DEV * bsz, 2 * d)
    g_ada_w, g_ada_b = ada_bwd(c_all, lax.dynamic_slice_in_dim(dmods_all, me * cs, cs, axis=2))
    g_kv_ada_w, _ = ada_bwd(c_all, lax.dynamic_slice_in_dim(dkv_all, me * kcs, kcs, axis=1)[None])
    _, g_kv_ada_b = ada_bwd(c_all, dkv_all[None])

    small_names = ("ln_g", "ssm_d", "kv_g", "final_g") + SSM_NAMES
    partial = all_gather(pack([g_diff[n] for n in small_names]), "gather_small_grads")
    totals = unpack(sum_slots(partial), [g_diff[n].shape for n in small_names])
    g_small = dict(zip(small_names, totals))
    g_small["ln_g"] = lax.dynamic_slice_in_dim(g_small["ln_g"], me * ln_g.shape[-1], ln_g.shape[-1], axis=2)
    g_small["ssm_d"] = lax.dynamic_slice_in_dim(g_small["ssm_d"], me * ssm_d.shape[-1], ssm_d.shape[-1], axis=1)
    g_small["ada_b"] = g_ada_b.reshape(ada_b.shape)
    g_small["kv_ada_b"] = g_kv_ada_b.reshape(kv_ada_b.shape)

    out = {}

    def put(name, res, shape):
        for kind, a in zip(("grad_", "delta_", "new_m_", "new_v_"), res):
            out[kind + name] = a.reshape(shape)

    for name in ("ssm_w_glu", "w_kv", "attn_w_q", "attn_w_o", "mlp_w1", "mlp_w2"):
        grads = jnp.stack([g for g, (n, _, _) in zip(g_shards, WEIGHT_ORDER) if n == name])
        v3 = lambda a: a.reshape(grads.shape)
        put(name, adamw(v3(w[name]), v3(mom[name]), v3(var[name]), grads, "adamw_" + name), w[name].shape)
    v3 = lambda a: a.reshape(8, d, cs)
    put("ada_w", adamw(v3(ada_w), v3(m_ada_w), v3(v_ada_w), g_ada_w, "adamw_ada_w"), ada_w.shape)
    put("kv_ada_w", adamw(kv_ada_w[None], m_kv_ada_w[None], v_kv_ada_w[None], g_kv_ada_w, "adamw_kv_ada_w"), kv_ada_w.shape)
    names = small_names + ("ada_b", "kv_ada_b")
    res = adamw(pack([w[n] for n in names])[None], pack([mom[n] for n in names])[None],
                pack([var[n] for n in names])[None], pack([g_small[n] for n in names])[None], "adamw_small")
    for kind, slab in zip(("grad_", "delta_", "new_m_", "new_v_"), res):
        for n, a in zip(names, unpack(slab[0], [w[n].shape for n in names])):
            out[kind + n] = a

    result = [loss, grad_x]
    for kind in ("grad_", "delta_", "new_m_", "new_v_"):
        result += [out[kind + n] for n in WEIGHT_NAMES]
    return tuple(result)
```

```python
import functools
import math

import jax
import jax.numpy as jnp
from jax import lax
from jax.experimental import pallas as pl
from jax.experimental.pallas import tpu as pltpu

F32 = jnp.float32
BF16 = jnp.bfloat16
N_DEV = 8
AXES = ("x", "y", "c")
V7X_VMEM_LIMIT = 56 * 1024 * 1024
LANES = 128
EPS = 1e-6
NEG = -1e30
HEAD = 64
QBLK = 128
BRANCH_DIL = (1, 4, 16)
ADAM_LR, ADAM_B1, ADAM_B2, ADAM_EPS, ADAM_WD, ADAM_STEP = 0.001, 0.9, 0.999, 1e-08, 0.01, 10

NN = (((1,), (0,)), ((), ()))
NT = (((1,), (1,)), ((), ()))
TN = (((0,), (0,)), ((), ()))


def _params(sem):
    return pltpu.CompilerParams(dimension_semantics=sem, vmem_limit_bytes=V7X_VMEM_LIMIT)


def _peer(k):
    px, py, pc = lax.axis_index("x"), lax.axis_index("y"), lax.axis_index("c")
    qx = 1 - px if k & 4 else px
    qy = 1 - py if k & 2 else py
    qc = 1 - pc if k & 1 else pc
    return (qx, qy, qc), 4 * qx + 2 * qy + qc


ROW_TILE = 2048
NEAR, FAR = (0, 1, 2, 3, 6), (4, 5, 7)


def _exchange_copies(ks, src_ref, land_ref, send_sems, recv_sems):
    copies = []
    for i, k in enumerate(ks):
        peer, slot = _peer(k)
        if k == 0:
            copies.append(pltpu.make_async_copy(src_ref.at[slot], land_ref.at[i], send_sems.at[i]))
        else:
            copies.append(pltpu.make_async_remote_copy(
                src_ref=src_ref.at[slot], dst_ref=land_ref.at[i], send_sem=send_sems.at[i], recv_sem=recv_sems.at[i],
                device_id=peer, device_id_type=pl.DeviceIdType.MESH))
    return copies


def _gather_copies(src_ref, land_ref, send_sems, recv_sems):
    _, me = _peer(0)

    def copy(pair, slot, k, src=None):
        peer, _ = _peer(k)
        return pltpu.make_async_remote_copy(
            src_ref=land_ref.at[slot] if src is None else src, dst_ref=land_ref.at[slot],
            send_sem=send_sems.at[pair], recv_sem=recv_sems.at[pair], device_id=peer,
            device_id_type=pl.DeviceIdType.MESH)

    direct = [copy(0, me, 1, src_ref)] + [copy(1 + j, me, k, src_ref) for j, k in enumerate((2, 4, 6))]
    forwards = [copy(4 + j, _peer(k)[1], 1) for j, k in enumerate((2, 4, 6))]
    from_sibling = [copy(0, _peer(1)[1], 1)] + [copy(4 + j, _peer(k ^ 1)[1], 1) for j, k in enumerate((2, 4, 6))]
    return direct, forwards, from_sibling


def _mm(name, a, b, *, grid, a_spec, b_spec, out_spec, out_shape, dims, acc_axis=None, acc_shape=None, carry=None,
        a_pre=None, post=None, cache=None):
    n_acc = grid[acc_axis] if acc_axis is not None else 1
    n_steps = math.prod(grid)
    n_ext = 0 if post is None else 1

    def finish(r, e_ref):
        return r if post is None else post[2](r, e_ref[...])

    def as_bf16(ref, pre, cache_ref):
        def convert():
            return (ref[...] if pre is None else pre(ref[...])).astype(BF16)

        if cache_ref is None:
            return convert()
        outer, inner = pl.program_id(0), pl.program_id(1)
        if cache[1] == "inner":
            @pl.when(inner == 0)
            def _():
                cache_ref[...] = convert()

            return cache_ref[...]

        @pl.when(outer == 0)
        def _():
            cache_ref[inner] = convert()

        return cache_ref[inner]

    def body(*refs):
        a_ref, b_ref = refs[:2]
        e_ref = refs[2] if post is not None else None
        if carry is None:
            o_ref = refs[2 + n_ext]
            scratch = list(refs[3 + n_ext:])
        else:
            src_ref, o_ref, land_ref = refs[2 + n_ext:5 + n_ext]
            scratch, sems = list(refs[5 + n_ext:-3]), refs[-3:]
            step = functools.reduce(lambda s, i: s * grid[i] + pl.program_id(i), range(len(grid)), 0)
        cache_ref = scratch.pop() if cache is not None else None

        if carry is not None:
            @pl.when(step == 0)
            def _():
                if carry[0] == "gather":
                    pltpu.make_async_copy(src_ref, land_ref.at[_peer(0)[1]], sems[2]).start()
                    for cp in _gather_copies(src_ref, land_ref, sems[0], sems[1])[0]:
                        cp.start()
                else:
                    for cp in _exchange_copies(carry[2], src_ref, land_ref, sems[0], sems[1]):
                        cp.start()

            if carry[0] == "gather":
                @pl.when(step == (3 * n_steps) // 4)
                def _():
                    direct, forwards, _ = _gather_copies(src_ref, land_ref, sems[0], sems[1])
                    for cp, fw in zip(direct[1:], forwards):
                        cp.wait_recv()
                        fw.start()

        a_val = as_bf16(a_ref, a_pre, cache_ref if cache is not None and cache[0] == "a" else None)
        b_val = as_bf16(b_ref, None, cache_ref if cache is not None and cache[0] == "b" else None)
        r = lax.dot_general(a_val, b_val, dims, preferred_element_type=F32)
        if acc_axis is None:
            o_ref[...] = finish(r, e_ref).astype(o_ref.dtype)
        else:
            acc = scratch[0]
            k = pl.program_id(acc_axis)

            @pl.when(k == 0)
            def _():
                acc[...] = r

            @pl.when(k > 0)
            def _():
                acc[...] += r

            @pl.when(k == n_acc - 1)
            def _():
                o_ref[...] = finish(acc[...], e_ref).astype(o_ref.dtype)

        if carry is not None:
            @pl.when(step == n_steps - 1)
            def _():
                if carry[0] == "gather":
                    direct, forwards, from_sibling = _gather_copies(src_ref, land_ref, sems[0], sems[1])
                    for cp in from_sibling:
                        cp.wait_recv()
                    for cp in direct + forwards:
                        cp.wait_send()
                    pltpu.make_async_copy(src_ref, land_ref.at[_peer(0)[1]], sems[2]).wait()
                else:
                    for cp in _exchange_copies(carry[2], src_ref, land_ref, sems[0], sems[1]):
                        cp.wait()

    scratch = [] if acc_axis is None else [pltpu.VMEM(acc_shape, F32)]
    if cache is not None:
        block = tuple(s for s in (a_spec if cache[0] == "a" else b_spec).block_shape if s is not None)
        scratch.append(pltpu.VMEM(block if cache[1] == "inner" else (grid[1],) + block, BF16))
    in_specs, operands = [a_spec, b_spec], [a, b]
    if post is not None:
        in_specs.append(post[1])
        operands.append(post[0])
    if carry is None:
        sem = tuple("arbitrary" if i == acc_axis or cache is not None else "parallel" for i in range(len(grid)))
        return pl.pallas_call(body, name=name, grid=grid, in_specs=in_specs, out_specs=out_spec,
                              out_shape=out_shape, scratch_shapes=scratch, compiler_params=_params(sem))(*operands)
    kind, src = carry[:2]
    n_land = N_DEV if kind == "gather" else len(carry[2])
    n_sem = N_DEV - 1 if kind == "gather" else n_land
    land = jax.ShapeDtypeStruct((n_land,) + src.shape[-2:], src.dtype)
    hbm = pl.BlockSpec(memory_space=pl.ANY)
    scratch += [pltpu.SemaphoreType.DMA((n_sem,)), pltpu.SemaphoreType.DMA((n_sem,)), pltpu.SemaphoreType.DMA(())]
    return pl.pallas_call(body, name=name + "_" + kind, grid=grid, in_specs=in_specs + [hbm],
                          out_specs=[out_spec, hbm], out_shape=[out_shape, land], scratch_shapes=scratch,
                          compiler_params=_params(("arbitrary",) * len(grid)))(*operands, src)


def _tile(n, t):
    if n <= t:
        return n
    for d in range(t - t % 8, 7, -8):
        if n % d == 0:
            return d
    raise ValueError((n, t))


@jax.custom_vjp
def linear_col(x, g):
    return _linear_col_fwd(x, g)[0]


def _col_fwd_call(x, g, carry=None):
    t, k = x.shape
    _, _, ns = g.shape
    tm = _tile(t, ROW_TILE)
    return _mm("col_fwd", x, g, grid=(t // tm, N_DEV),
               a_spec=pl.BlockSpec((tm, k), lambda i, j: (i, 0)),
               b_spec=pl.BlockSpec((None, k, ns), lambda i, j: (j, 0, 0)),
               out_spec=pl.BlockSpec((tm, ns), lambda i, j: (i, j)),
               out_shape=jax.ShapeDtypeStruct((t, N_DEV * ns), F32), dims=NN, carry=carry, cache=("a", "inner"))


def _col_dx_call(dy, g, carry=None):
    _, k, ns = g.shape
    t = dy.shape[0]
    tm = _tile(t, ROW_TILE)
    return _mm("col_dx", dy, g, grid=(t // tm, N_DEV),
               a_spec=pl.BlockSpec((tm, ns), lambda i, j: (i, j)),
               b_spec=pl.BlockSpec((None, k, ns), lambda i, j: (j, 0, 0)),
               out_spec=pl.BlockSpec((tm, k), lambda i, j: (i, 0)),
               out_shape=jax.ShapeDtypeStruct((t, k), F32), dims=NT, acc_axis=1, acc_shape=(tm, k), carry=carry)


def _col_dw_call(x, dy, ns, carry=None):
    t, k = x.shape
    tt = _tile(t, ROW_TILE)
    last = t // tt - 1
    return _mm("col_dw", x, dy, grid=(N_DEV, t // tt), cache=("a", "outer"),
               a_spec=pl.BlockSpec((tt, k), lambda j, s: (jnp.where(j == 0, s, last), 0)),
               b_spec=pl.BlockSpec((tt, ns), lambda j, s: (s, j)),
               out_spec=pl.BlockSpec((None, k, ns), lambda j, s: (j, 0, 0)),
               out_shape=jax.ShapeDtypeStruct((N_DEV, k, ns), BF16), dims=TN, acc_axis=1, acc_shape=(k, ns),
               carry=carry)


def _linear_col_fwd(x, g):
    return _col_fwd_call(x, g), (x, g)


def _linear_col_bwd(res, dy):
    x, g = res
    return _col_dx_call(dy, g), _col_dw_call(x, dy, g.shape[2])


linear_col.defvjp(_linear_col_fwd, _linear_col_bwd)


@jax.custom_vjp
def linear_col_next(x, g, nxt):
    return _linear_col_next_fwd(x, g, nxt)[0]


def _linear_col_next_fwd(x, g, nxt):
    y, g_next = _col_fwd_call(x, g, carry=("gather", nxt.astype(BF16)))
    return (y, g_next), (x, g)


def _linear_col_next_bwd(res, cts):
    x, g = res
    dy, dg_next = cts
    dx, near = _col_dx_call(dy, g, carry=("exchange", dg_next, NEAR))
    dg, far = _col_dw_call(x, dy, g.shape[2], carry=("exchange", dg_next, FAR))
    return dx, dg, sum_slots(near, far)


linear_col_next.defvjp(_linear_col_next_fwd, _linear_col_next_bwd)


def _relu2(a):
    return jnp.square(jnp.maximum(a, 0.0))


def _relu2_grad(d_act, a):
    return d_act * (2.0 * jnp.maximum(a, 0.0))


def _lin_fwd_call(x, w, act, carry=None):
    t, k = x.shape
    _, n = w.shape
    tm, tk = _tile(t, ROW_TILE), _tile(k, 1024)
    return _mm("lin_fwd", x, w, grid=(t // tm, k // tk),
               a_spec=pl.BlockSpec((tm, tk), lambda i, s: (i, s)),
               b_spec=pl.BlockSpec((tk, n), lambda i, s: (s, 0)),
               out_spec=pl.BlockSpec((tm, n), lambda i, s: (i, 0)),
               out_shape=jax.ShapeDtypeStruct((t, n), F32), dims=NN, acc_axis=1, acc_shape=(tm, n), carry=carry,
               a_pre=_relu2 if act else None)


def _lin_dx_call(dy, w, x, act, carry=None):
    k, n = w.shape
    t = dy.shape[0]
    tm, tk = _tile(t, 1024), _tile(k, 1024)
    out_spec = pl.BlockSpec((tm, tk), lambda i, s: (i, s))
    return _mm("lin_dx", dy, w, grid=(t // tm, k // tk),
               a_spec=pl.BlockSpec((tm, n), lambda i, s: (i, 0)),
               b_spec=pl.BlockSpec((tk, n), lambda i, s: (s, 0)),
               out_spec=out_spec, out_shape=jax.ShapeDtypeStruct((t, k), F32), dims=NT, carry=carry,
               post=(x, out_spec, _relu2_grad) if act else None, cache=("a", "inner"))


def _lin_dw_call(x, dy, act, carry=None):
    t, k = x.shape
    n = dy.shape[1]
    tk, tt = _tile(k, 1024), _tile(t, ROW_TILE)
    last = t // tt - 1
    return _mm("lin_dw", x, dy, grid=(k // tk, t // tt), cache=("b", "outer"),
               a_spec=pl.BlockSpec((tt, tk), lambda s, r: (r, s)),
               b_spec=pl.BlockSpec((tt, n), lambda s, r: (jnp.where(s == 0, r, last), 0)),
               out_spec=pl.BlockSpec((tk, n), lambda s, r: (s, 0)),
               out_shape=jax.ShapeDtypeStruct((k, n), BF16), dims=TN, acc_axis=1, acc_shape=(tk, n), carry=carry,
               a_pre=_relu2 if act else None)


def _make_linear(act):
    @jax.custom_vjp
    def linear(x, w):
        return _lin_fwd_call(x, w, act)

    def fwd(x, w):
        return _lin_fwd_call(x, w, act), (x, w)

    def bwd(res, dy):
        x, w = res
        return _lin_dx_call(dy, w, x, act), _lin_dw_call(x, dy, act)

    linear.defvjp(fwd, bwd)

    @jax.custom_vjp
    def linear_next(x, w, nxt):
        return next_fwd(x, w, nxt)[0]

    def next_fwd(x, w, nxt):
        y, g_next = _lin_fwd_call(x, w, act, carry=("gather", nxt.astype(BF16)))
        return (y, g_next), (x, w)

    def next_bwd(res, cts):
        x, w = res
        dy, dg_next = cts
        dx, near = _lin_dx_call(dy, w, x, act, carry=("exchange", dg_next, NEAR))
        dw, far = _lin_dw_call(x, dy, act, carry=("exchange", dg_next, FAR))
        return dx, dw, sum_slots(near, far)

    linear_next.defvjp(next_fwd, next_bwd)
    return linear, linear_next


linear, linear_next = _make_linear(False)
linear_relu2, linear_relu2_next = _make_linear(True)


@jax.custom_vjp
def grouped_mm(x, w):
    return _grouped_fwd(x, w)[0]


def _grouped_fwd(x, w):
    t = x.shape[0]
    p, kin, kout = w.shape
    tm = _tile(t, 1024)
    y = _mm("grp_fwd", x, w, grid=(t // tm, p),
            a_spec=pl.BlockSpec((tm, kin), lambda i, q: (i, q)),
            b_spec=pl.BlockSpec((None, kin, kout), lambda i, q: (q, 0, 0)),
            out_spec=pl.BlockSpec((tm, kout), lambda i, q: (i, q)),
            out_shape=jax.ShapeDtypeStruct((t, p * kout), F32), dims=NN)
    return y, (x, w)


def _grouped_bwd(res, dy):
    x, w = res
    t = x.shape[0]
    p, kin, kout = w.shape
    tm = _tile(t, 1024)
    dx = _mm("grp_dx", dy, w, grid=(t // tm, p),
             a_spec=pl.BlockSpec((tm, kout), lambda i, q: (i, q)),
             b_spec=pl.BlockSpec((None, kin, kout), lambda i, q: (q, 0, 0)),
             out_spec=pl.BlockSpec((tm, kin), lambda i, q: (i, q)),
             out_shape=jax.ShapeDtypeStruct((t, p * kin), F32), dims=NT)
    dw = _mm("grp_dw", x, dy, grid=(p, t // tm),
             a_spec=pl.BlockSpec((tm, kin), lambda q, s: (s, q)),
             b_spec=pl.BlockSpec((tm, kout), lambda q, s: (s, q)),
             out_spec=pl.BlockSpec((None, kin, kout), lambda q, s: (q, 0, 0)),
             out_shape=jax.ShapeDtypeStruct((p, kin, kout), F32), dims=TN, acc_axis=1, acc_shape=(kin, kout))
    return dx, dw


grouped_mm.defvjp(_grouped_fwd, _grouped_bwd)


def _row_ts(widths, seq):
    per_row = 4 * sum(widths)
    ts = 512
    while ts > 8 and ts * per_row * 2 > 24 * 1024 * 1024:
        ts //= 2
    return min(ts, seq)


def make_rowop(name, f, n_row, n_batch, n_vec, f_bwd=None):
    n_in = n_row + n_batch + n_vec

    def in_specs(args, ts):
        specs = []
        for a in args[:n_row]:
            specs.append(pl.BlockSpec((None, ts, a.shape[2]), lambda b, s: (b, s, 0)))
        for a in args[n_row:n_row + n_batch]:
            specs.append(pl.BlockSpec((None, 1, a.shape[2]), lambda b, s: (b, 0, 0)))
        for a in args[n_row + n_batch:]:
            specs.append(pl.BlockSpec((1, a.shape[1]), lambda b, s: (0, 0)))
        return specs

    def out_struct(args, ts):
        blocks = [jax.ShapeDtypeStruct((ts, a.shape[2]), F32) for a in args[:n_row]]
        blocks += [jax.ShapeDtypeStruct((1, a.shape[2]), F32) for a in args[n_row:n_row + n_batch]]
        blocks += [jax.ShapeDtypeStruct((1, a.shape[1]), F32) for a in args[n_row + n_batch:]]
        return jax.eval_shape(f, *blocks)

    def run_fwd(args):
        bsz, seq = args[0].shape[:2]
        outs0 = out_struct(args, 8)
        widths = [a.shape[2] for a in args[:n_row]] + [o.shape[1] for o in outs0]
        ts = _row_ts(widths, seq)

        def body(*refs):
            outs = f(*[r[...] for r in refs[:n_in]])
            for o_ref, o in zip(refs[n_in:], outs):
                o_ref[...] = o

        return pl.pallas_call(
            body, name=name + "_fwd", grid=(bsz, seq // ts), in_specs=in_specs(args, ts),
            out_specs=[pl.BlockSpec((None, ts, o.shape[1]), lambda b, s: (b, s, 0)) for o in outs0],
            out_shape=[jax.ShapeDtypeStruct((bsz, seq, o.shape[1]), F32) for o in outs0],
            compiler_params=_params(("parallel", "parallel")))(*args)

    def run_bwd(args, cts):
        bsz, seq = args[0].shape[:2]
        widths = [a.shape[2] for a in args[:n_row]] * 2 + [c.shape[2] for c in cts]
        ts = _row_ts(widths, seq)
        n_ct = len(cts)

        def body(*refs):
            ins = [r[...] for r in refs[:n_in]]
            ct = [r[...] for r in refs[n_in:n_in + n_ct]]
            if f_bwd is None:
                _, vjp = jax.vjp(f, *ins)
                grads = vjp(tuple(ct))
            else:
                grads = f_bwd(ins, ct)
            g_refs = refs[n_in + n_ct:]
            b, s = pl.program_id(0), pl.program_id(1)
            for i in range(n_row):
                g_refs[i][...] = grads[i]
            for i in range(n_row, n_row + n_batch):
                @pl.when(s == 0)
                def _(i=i):
                    g_refs[i][...] = grads[i]

                @pl.when(s > 0)
                def _(i=i):
                    g_refs[i][...] += grads[i]
            for i in range(n_row + n_batch, n_in):
                first = jnp.logical_and(b == 0, s == 0)

                @pl.when(first)
                def _(i=i):
                    g_refs[i][...] = grads[i]

                @pl.when(jnp.logical_not(first))
                def _(i=i):
                    g_refs[i][...] += grads[i]

        ct_specs = [pl.BlockSpec((None, ts, c.shape[2]), lambda b, s: (b, s, 0)) for c in cts]
        return pl.pallas_call(
            body, name=name + "_bwd", grid=(bsz, seq // ts), in_specs=in_specs(args, ts) + ct_specs,
            out_specs=in_specs(args, ts), out_shape=[jax.ShapeDtypeStruct(a.shape, F32) for a in args],
            compiler_params=_params(("arbitrary", "arbitrary")))(*args, *cts)

    @jax.custom_vjp
    def op(*args):
        return tuple(run_fwd(args))

    def fwd(*args):
        return tuple(run_fwd(args)), args

    def bwd(args, cts):
        return tuple(run_bwd(args, list(cts)))

    op.defvjp(fwd, bwd)
    return op


def _modnorm_f(h, scale, shift, g):
    y = h * lax.rsqrt(jnp.mean(h * h, axis=-1, keepdims=True) + EPS) * g
    return (y * (1.0 + scale) + shift,)


def _gate_res_f(h, y, gate):
    return (h + gate * y,)


def _res_modnorm_f(h, y, gate, scale, shift, g):
    h2 = h + gate * y
    return (h2,) + _modnorm_f(h2, scale, shift, g)


def _relu2_f(a):
    return (jnp.square(jnp.maximum(a, 0.0)),)


def _gelu_skip_f(y, u, d):
    return (jax.nn.gelu(y + d * u),)


def _glu_f(vg):
    n = vg.shape[1] // 2
    return (vg[:, :n] * jax.nn.sigmoid(vg[:, n:]),)


def _glu_b(ins, cts):
    (vg,), (ct,) = ins, cts
    n = vg.shape[1] // 2
    sg = jax.nn.sigmoid(vg[:, n:])
    return (jnp.concatenate([ct * sg, ct * vg[:, :n] * sg * (1.0 - sg)], axis=1),)


def _combine_f(o1, o2, o3, l1, l2, l3):
    m = jnp.maximum(jnp.maximum(l1, l2), l3)
    e1, e2, e3 = jnp.exp(l1 - m), jnp.exp(l2 - m), jnp.exp(l3 - m)
    return ((e1 * o1 + e2 * o2 + e3 * o3) / (e1 + e2 + e3),)


def _final_loss_f(h, target, g):
    y = h * lax.rsqrt(jnp.mean(h * h, axis=-1, keepdims=True) + EPS) * g
    return (0.5 * jnp.mean(jnp.square(y - target), axis=-1, keepdims=True),)


modnorm = make_rowop("modnorm", _modnorm_f, 1, 2, 1)
gate_res = make_rowop("gate_res", _gate_res_f, 2, 1, 0)
res_modnorm = make_rowop("res_modnorm", _res_modnorm_f, 2, 3, 1)
relu2 = make_rowop("relu2", _relu2_f, 1, 0, 0)
gelu_skip = make_rowop("gelu_skip", _gelu_skip_f, 2, 0, 1)
glu = make_rowop("glu", _glu_f, 1, 0, 0, _glu_b)
combine = make_rowop("combine", _combine_f, 6, 0, 0)
final_loss = make_rowop("final_loss", _final_loss_f, 2, 0, 1)


SCAN_TB = 64
HALF = 512


def _scan_fwd_call(bu, a_re, a_im):
    bsz, rows, width = bu.shape
    seq = rows // 8
    tb = _tile(seq, SCAN_TB)

    def body(bu_ref, ar_ref, ai_ref, o_ref, carry):
        @pl.when(pl.program_id(0) == 0)
        def _():
            carry[...] = jnp.zeros_like(carry)

        ar, ai = ar_ref[...], ai_ref[...]

        def step(t, st):
            r0 = pl.multiple_of(t * 8, 8)
            new = []
            for b in range(bsz):
                sre, sim = st[2 * b], st[2 * b + 1]
                nre = ar * sre - ai * sim + bu_ref[b, pl.ds(r0, 8), pl.ds(0, HALF)]
                nim = ar * sim + ai * sre + bu_ref[b, pl.ds(r0, 8), pl.ds(HALF, HALF)]
                o_ref[b, pl.ds(r0, 8), pl.ds(0, HALF)] = nre
                o_ref[b, pl.ds(r0, 8), pl.ds(HALF, HALF)] = nim
                new += [nre, nim]
            return tuple(new)

        fin = lax.fori_loop(0, tb, step, tuple(carry[k] for k in range(2 * bsz)), unroll=4)
        for k in range(2 * bsz):
            carry[k] = fin[k]

    blk = pl.BlockSpec((bsz, tb * 8, width), lambda i: (0, i, 0))
    vec = pl.BlockSpec((8, HALF), lambda i: (0, 0))
    return pl.pallas_call(body, name="s5_scan_fwd", grid=(seq // tb,), in_specs=[blk, vec, vec], out_specs=blk,
                          out_shape=jax.ShapeDtypeStruct(bu.shape, F32),
                          scratch_shapes=[pltpu.VMEM((2 * bsz, 8, HALF), F32)],
                          compiler_params=_params(("arbitrary",)))(bu, a_re, a_im)


def _scan_bwd_call(g, st, a_re, a_im):
    bsz, rows, width = g.shape
    seq = rows // 8
    tb = _tile(seq, SCAN_TB)
    nt = seq // tb

    def body(g_ref, st_ref, prev_ref, ar_ref, ai_ref, db_ref, dar_ref, dai_ref, carry):
        i = pl.program_id(0)

        @pl.when(i == 0)
        def _():
            carry[...] = jnp.zeros_like(carry)
            dar_ref[...] = jnp.zeros_like(dar_ref)
            dai_ref[...] = jnp.zeros_like(dai_ref)

        ar, ai = ar_ref[...], ai_ref[...]

        def lam_step(b, r0, lre, lim):
            nre = g_ref[b, pl.ds(r0, 8), pl.ds(0, HALF)] + ar * lre + ai * lim
            nim = g_ref[b, pl.ds(r0, 8), pl.ds(HALF, HALF)] - ai * lre + ar * lim
            db_ref[b, pl.ds(r0, 8), pl.ds(0, HALF)] = nre
            db_ref[b, pl.ds(r0, 8), pl.ds(HALF, HALF)] = nim
            return nre, nim

        def step(k, c):
            t = tb - 1 - k
            r0 = pl.multiple_of(t * 8, 8)
            p0 = pl.multiple_of(t * 8 - 8, 8)
            lam, dar, dai = list(c[:2 * bsz]), c[2 * bsz], c[2 * bsz + 1]
            for b in range(bsz):
                nre, nim = lam_step(b, r0, lam[2 * b], lam[2 * b + 1])
                pre = st_ref[b, pl.ds(p0, 8), pl.ds(0, HALF)]
                pim = st_ref[b, pl.ds(p0, 8), pl.ds(HALF, HALF)]
                dar = dar + nre * pre + nim * pim
                dai = dai + nim * pre - nre * pim
                lam[2 * b], lam[2 * b + 1] = nre, nim
            return tuple(lam) + (dar, dai)

        zero = jnp.zeros((8, HALF), F32)
        init = tuple(carry[k] for k in range(2 * bsz)) + (zero, zero)
        c = lax.fori_loop(0, tb - 1, step, init, unroll=4)
        lam, dar, dai = list(c[:2 * bsz]), c[2 * bsz], c[2 * bsz + 1]
        keep = jnp.where(i == nt - 1, 0.0, 1.0)
        for b in range(bsz):
            nre, nim = lam_step(b, 0, lam[2 * b], lam[2 * b + 1])
            pre = prev_ref[b, :, pl.ds(0, HALF)] * keep
            pim = prev_ref[b, :, pl.ds(HALF, HALF)] * keep
            dar = dar + nre * pre + nim * pim
            dai = dai + nim * pre - nre * pim
            carry[2 * b], carry[2 * b + 1] = nre, nim
        dar_ref[...] += dar
        dai_ref[...] += dai

    blk = pl.BlockSpec((bsz, tb * 8, width), lambda i: (0, nt - 1 - i, 0))
    prev = pl.BlockSpec((bsz, 8, width), lambda i: (0, jnp.maximum((nt - 1 - i) * tb - 1, 0), 0))
    vec = pl.BlockSpec((8, HALF), lambda i: (0, 0))
    return pl.pallas_call(
        body, name="s5_scan_bwd", grid=(nt,), in_specs=[blk, blk, prev, vec, vec], out_specs=[blk, vec, vec],
        out_shape=[jax.ShapeDtypeStruct(g.shape, F32), jax.ShapeDtypeStruct((8, HALF), F32),
                   jax.ShapeDtypeStruct((8, HALF), F32)],
        scratch_shapes=[pltpu.VMEM((2 * bsz, 8, HALF), F32)],
        compiler_params=_params(("arbitrary",)))(g, st, st, a_re, a_im)


@jax.custom_vjp
def s5_scan(bu, a_re, a_im):
    return _scan_fwd_call(bu, a_re, a_im)


def _s5_scan_fwd(bu, a_re, a_im):
    st = _scan_fwd_call(bu, a_re, a_im)
    return st, (st, a_re, a_im)


def _s5_scan_bwd(res, g):
    st, a_re, a_im = res
    return tuple(_scan_bwd_call(g, st, a_re, a_im))


s5_scan.defvjp(_s5_scan_fwd, _s5_scan_bwd)


SSM_TB = 64
PITCH = SSM_TB + 8
PIECES = 8
CH = 128
NCH = 2 * HALF // LANES
NRE = NCH // 2


def _slab(t):
    return pl.ds(t, PIECES, stride=PITCH)


def _put_rows(ref, b, q, val, tb):
    for j in range(NCH):
        ref[b, j, pl.ds(q * PITCH, tb), :] = val[:, j * LANES:(j + 1) * LANES]


def _get_rows(ref, q, bsz, tb):
    return jnp.concatenate(
        [jnp.concatenate([ref[b, j, pl.ds(q * PITCH, tb), :] for j in range(NCH)], axis=1) for b in range(bsz)], axis=0)


def _chunks(a):
    return [a[:, j * LANES:(j + 1) * LANES] for j in range(a.shape[1] // LANES)]


def _ssm_fwd_call(u, b_mat, c_mat, a_re, a_im):
    bsz, seq, d = u.shape
    tb = SSM_TB
    nt = seq // tb
    assert seq % tb == 0 and d == PIECES * CH

    def body(u_ref, b_ref, c_ref, ar_ref, ai_ref, y_ref, st_ref, end_ref, bu, carry):
        @pl.when(pl.program_id(0) == 0)
        def _():
            carry[...] = jnp.zeros_like(carry)

        u2 = u_ref[...].reshape(bsz * tb, d).astype(BF16)
        for q in range(PIECES):
            r = lax.dot_general(u2[:, q * CH:(q + 1) * CH], b_ref[q], NN, preferred_element_type=F32)
            for b in range(bsz):
                _put_rows(bu, b, q, r[b * tb:(b + 1) * tb], tb)
        ar, ai = _chunks(ar_ref[...]), _chunks(ai_ref[...])
        st = [[carry[b, j] for j in range(NCH)] for b in range(bsz)]
        for t in range(tb):
            for b in range(bsz):
                for j in range(NRE):
                    sre, sim = st[b][j], st[b][NRE + j]
                    nre = ar[j] * sre - ai[j] * sim + bu[b, j, _slab(t), :]
                    nim = ar[j] * sim + ai[j] * sre + bu[b, NRE + j, _slab(t), :]
                    st_ref[b, j, _slab(t), :] = nre
                    st_ref[b, NRE + j, _slab(t), :] = nim
                    st[b][j], st[b][NRE + j] = nre, nim
        for b in range(bsz):
            for j in range(NCH):
                carry[b, j] = st[b][j]
                end_ref[b, j] = st[b][j]
        for q in range(PIECES):
            s2 = _get_rows(st_ref, q, bsz, tb).astype(BF16)
            r = lax.dot_general(s2, c_ref[q], NN, preferred_element_type=F32)
            for b in range(bsz):
                y_ref[b, :, q * CH:(q + 1) * CH] = r[b * tb:(b + 1) * tb]

    rows = pl.BlockSpec((bsz, tb, d), lambda i: (0, i, 0))
    vec = pl.BlockSpec((PIECES, HALF), lambda i: (0, 0))
    return pl.pallas_call(
        body, name="s5_ssm_fwd", grid=(nt,),
        in_specs=[rows, pl.BlockSpec((PIECES, CH, 2 * HALF), lambda i: (0, 0, 0)),
                  pl.BlockSpec((PIECES, 2 * HALF, CH), lambda i: (0, 0, 0)), vec, vec],
        out_specs=[rows, pl.BlockSpec((bsz, None, NCH, PIECES * PITCH, LANES), lambda i: (0, i, 0, 0, 0)),
                   pl.BlockSpec((bsz, None, NCH, PIECES, LANES), lambda i: (0, i, 0, 0, 0))],
        out_shape=[jax.ShapeDtypeStruct((bsz, seq, d), F32),
                   jax.ShapeDtypeStruct((bsz, nt, NCH, PIECES * PITCH, LANES), F32),
                   jax.ShapeDtypeStruct((bsz, nt, NCH, PIECES, LANES), F32)],
        scratch_shapes=[pltpu.VMEM((bsz, NCH, PIECES * PITCH, LANES), F32), pltpu.VMEM((bsz, NCH, PIECES, LANES), F32)],
        compiler_params=_params(("arbitrary",)))(u, b_mat, c_mat, a_re, a_im)


def _ssm_bwd_call(dy, u, st, ends, b_mat, c_mat, a_re, a_im):
    bsz, seq, d = u.shape
    tb = SSM_TB
    nt = seq // tb

    def body(dy_ref, u_ref, st_ref, prev_ref, b_ref, c_ref, ar_ref, ai_ref, du_ref, db_hbm, dc_hbm, dar_ref, dai_ref,
             lam, carry, db_acc, dc_acc):
        i = pl.program_id(0)

        @pl.when(i == 0)
        def _():
            carry[...] = jnp.zeros_like(carry)
            db_acc[...] = jnp.zeros_like(db_acc)
            dc_acc[...] = jnp.zeros_like(dc_acc)
            dar_ref[...] = jnp.zeros_like(dar_ref)
            dai_ref[...] = jnp.zeros_like(dai_ref)

        dy2 = dy_ref[...].reshape(bsz * tb, d).astype(BF16)
        u2 = u_ref[...].reshape(bsz * tb, d).astype(BF16)
        for q in range(PIECES):
            dyq = dy2[:, q * CH:(q + 1) * CH]
            g = lax.dot_general(dyq, c_ref[q], NT, preferred_element_type=F32)
            for b in range(bsz):
                _put_rows(lam, b, q, g[b * tb:(b + 1) * tb], tb)
            s2 = _get_rows(st_ref, q, bsz, tb).astype(BF16)
            dc_acc[q] += lax.dot_general(s2, dyq, TN, preferred_element_type=F32)

        ar, ai = _chunks(ar_ref[...]), _chunks(ai_ref[...])
        keep = jnp.where(i == nt - 1, 0.0, 1.0)
        lm = [[carry[b, j] for j in range(NCH)] for b in range(bsz)]
        dar = [jnp.zeros((PIECES, LANES), F32) for _ in range(NRE)]
        dai = [jnp.zeros((PIECES, LANES), F32) for _ in range(NRE)]
        for t in range(tb - 1, -1, -1):
            for b in range(bsz):
                for j in range(NRE):
                    lre, lim = lm[b][j], lm[b][NRE + j]
                    nre = lam[b, j, _slab(t), :] + ar[j] * lre + ai[j] * lim
                    nim = lam[b, NRE + j, _slab(t), :] - ai[j] * lre + ar[j] * lim
                    lam[b, j, _slab(t), :] = nre
                    lam[b, NRE + j, _slab(t), :] = nim
                    if t > 0:
                        pre, pim = st_ref[b, j, _slab(t - 1), :], st_ref[b, NRE + j, _slab(t - 1), :]
                    else:
                        pre, pim = prev_ref[b, j] * keep, prev_ref[b, NRE + j] * keep
                    dar[j] = dar[j] + nre * pre + nim * pim
                    dai[j] = dai[j] + nim * pre - nre * pim
                    lm[b][j], lm[b][NRE + j] = nre, nim
        for b in range(bsz):
            for j in range(NCH):
                carry[b, j] = lm[b][j]
        dar_ref[...] += jnp.concatenate(dar, axis=1)
        dai_ref[...] += jnp.concatenate(dai, axis=1)

        for q in range(PIECES):
            l2 = _get_rows(lam, q, bsz, tb).astype(BF16)
            r = lax.dot_general(l2, b_ref[q], NT, preferred_element_type=F32)
            for b in range(bsz):
                du_ref[b, :, q * CH:(q + 1) * CH] = r[b * tb:(b + 1) * tb]
            db_acc[q] += lax.dot_general(u2[:, q * CH:(q + 1) * CH], l2, TN, preferred_element_type=F32)

        @pl.when(i == nt - 1)
        def _():
            pltpu.sync_copy(db_acc, db_hbm)
            pltpu.sync_copy(dc_acc, dc_hbm)

    rows = pl.BlockSpec((bsz, tb, d), lambda i: (0, nt - 1 - i, 0))
    vec = pl.BlockSpec((PIECES, HALF), lambda i: (0, 0))
    hbm = pl.BlockSpec(memory_space=pl.ANY)
    return pl.pallas_call(
        body, name="s5_ssm_bwd", grid=(nt,),
        in_specs=[rows, rows,
                  pl.BlockSpec((bsz, None, NCH, PIECES * PITCH, LANES), lambda i: (0, nt - 1 - i, 0, 0, 0)),
                  pl.BlockSpec((bsz, None, NCH, PIECES, LANES), lambda i: (0, jnp.maximum(nt - 2 - i, 0), 0, 0, 0)),
                  pl.BlockSpec((PIECES, CH, 2 * HALF), lambda i: (0, 0, 0)),
                  pl.BlockSpec((PIECES, 2 * HALF, CH), lambda i: (0, 0, 0)), vec, vec],
        out_specs=[rows, hbm, hbm, vec, vec],
        out_shape=[jax.ShapeDtypeStruct((bsz, seq, d), F32), jax.ShapeDtypeStruct((PIECES, CH, 2 * HALF), F32),
                   jax.ShapeDtypeStruct((PIECES, 2 * HALF, CH), F32), jax.ShapeDtypeStruct((PIECES, HALF), F32),
                   jax.ShapeDtypeStruct((PIECES, HALF), F32)],
        scratch_shapes=[pltpu.VMEM((bsz, NCH, PIECES * PITCH, LANES), F32), pltpu.VMEM((bsz, NCH, PIECES, LANES), F32),
                        pltpu.VMEM((PIECES, CH, 2 * HALF), F32), pltpu.VMEM((PIECES, 2 * HALF, CH), F32)],
        compiler_params=_params(("arbitrary",)))(dy, u, st, ends, b_mat, c_mat, a_re, a_im)


@jax.custom_vjp
def s5_ssm(u, b_mat, c_mat, a_re, a_im):
    return _s5_ssm_fwd(u, b_mat, c_mat, a_re, a_im)[0]


def _s5_ssm_fwd(u, b_mat, c_mat, a_re, a_im):
    b16, c16 = b_mat.astype(BF16), c_mat.astype(BF16)
    y, st, ends = _ssm_fwd_call(u, b16, c16, a_re, a_im)
    return y, (u, st, ends, b16, c16, a_re, a_im)


def _s5_ssm_bwd(res, dy):
    return tuple(_ssm_bwd_call(dy, *res))


s5_ssm.defvjp(_s5_ssm_fwd, _s5_ssm_bwd)


ATT_HW = 2 * HEAD
ATT_HB = 1024 // ATT_HW


def _branch_geometry(dil, seq):
    sub = seq // dil
    assert sub % QBLK == 0
    return sub // QBLK


def _drows(dil, r, start, size):
    if dil == 1:
        return pl.ds(start, size)
    return pl.ds(r + start * dil, size, stride=dil)


def _masks():
    qi = lax.broadcasted_iota(jnp.int32, (QBLK, 2 * QBLK), 0)
    kj = lax.broadcasted_iota(jnp.int32, (QBLK, 2 * QBLK), 1) - QBLK
    dist = qi - kj
    band = jnp.logical_and(dist >= 0, dist <= QBLK)
    ci = lax.broadcasted_iota(jnp.int32, (QBLK, QBLK), 0)
    cj = lax.broadcasted_iota(jnp.int32, (QBLK, QBLK), 1)
    return band, ci >= cj


def _attn_specs(i, seq):
    q_spec = pl.BlockSpec((None, seq, ATT_HW), lambda b, h: (b, 0, i * ATT_HB + h))
    k_spec = pl.BlockSpec((None, seq, ATT_HW), lambda b, h: (b, 0, i * ATT_HB + h))
    v_spec = pl.BlockSpec((None, seq, ATT_HW), lambda b, h: (b, 0, (3 + i) * ATT_HB + h))
    o_spec = pl.BlockSpec((None, seq, ATT_HW), lambda b, h: (b, 0, h))
    return q_spec, k_spec, v_spec, o_spec


def _attn_fwd_call(q_all, kv_all, i, dil):
    bsz, seq, _ = q_all.shape
    nb = _branch_geometry(dil, seq)
    scale = HEAD ** -0.5

    def body(q_ref, k_ref, v_ref, o_ref, l_ref):
        band, causal = _masks()

        def one(r, n, mask):
            qs = _drows(dil, r, n * QBLK, QBLK)
            ks = qs if n == 0 else _drows(dil, r, (n - 1) * QBLK, 2 * QBLK)
            q2 = q_ref[qs, :].astype(BF16)
            k2 = k_ref[ks, :].astype(BF16)
            v2 = v_ref[ks, :].astype(BF16)
            outs, lses = [], []
            for h in range(ATT_HW // HEAD):
                ls = slice(h * HEAD, (h + 1) * HEAD)
                s = lax.dot_general(q2[:, ls], k2[:, ls], NT, preferred_element_type=F32) * scale
                s = jnp.where(mask, s, NEG)
                m = jnp.max(s, axis=-1, keepdims=True)
                p = jnp.exp(s - m)
                den = jnp.sum(p, axis=-1, keepdims=True)
                outs.append(lax.dot_general(p.astype(BF16), v2[:, ls], NN, preferred_element_type=F32) / den)
                lses.append(jnp.broadcast_to(m + jnp.log(den), (QBLK, HEAD)))
            o_ref[qs, :] = jnp.concatenate(outs, axis=1)
            l_ref[qs, :] = jnp.concatenate(lses, axis=1)

        for r in range(dil):
            for n in range(nb):
                one(r, n, causal if n == 0 else band)

    q_spec, k_spec, v_spec, o_spec = _attn_specs(i, seq)
    shape = jax.ShapeDtypeStruct((bsz, seq, 1024), F32)
    return pl.pallas_call(
        body, name="attn_fwd_d%d" % dil, grid=(bsz, ATT_HB), in_specs=[q_spec, k_spec, v_spec],
        out_specs=[o_spec, o_spec], out_shape=[shape, shape],
        compiler_params=_params(("parallel", "parallel")))(q_all, kv_all, kv_all)


def _attn_bwd_call(q_all, kv_all, o, l, do, dl, i, dil, grads):
    bsz, seq, _ = q_all.shape
    nb = _branch_geometry(dil, seq)
    scale = HEAD ** -0.5
    first = grads is None

    def body(q_ref, k_ref, v_ref, o_ref, l_ref, do_ref, dl_ref, *rest):
        dq_ref, dk_ref, dv_ref = rest[-3:]
        band, causal = _masks()

        def one(r, n, mask):
            qs = _drows(dil, r, n * QBLK, QBLK)
            ks = qs if n == 0 else _drows(dil, r, (n - 1) * QBLK, 2 * QBLK)
            q2 = q_ref[qs, :].astype(BF16)
            k2 = k_ref[ks, :].astype(BF16)
            v2 = v_ref[ks, :].astype(BF16)
            l2, do2 = l_ref[qs, :], do_ref[qs, :]
            t2 = dl_ref[qs, :] - do2 * o_ref[qs, :]
            dqs, dks, dvs = [], [], []
            for h in range(ATT_HW // HEAD):
                ls = slice(h * HEAD, (h + 1) * HEAD)
                q, k, v = q2[:, ls], k2[:, ls], v2[:, ls]
                s = lax.dot_general(q, k, NT, preferred_element_type=F32) * scale
                s = jnp.where(mask, s, NEG)
                p = jnp.exp(s - l2[:, h * HEAD:h * HEAD + 1])
                row = jnp.sum(t2[:, ls], axis=-1, keepdims=True)
                d_ob = do2[:, ls].astype(BF16)
                dp = lax.dot_general(d_ob, v, NT, preferred_element_type=F32)
                ds = (p * (dp + row)).astype(BF16)
                dqs.append(lax.dot_general(ds, k, NN, preferred_element_type=F32) * scale)
                dks.append(lax.dot_general(ds, q, TN, preferred_element_type=F32) * scale)
                dvs.append(lax.dot_general(p.astype(BF16), d_ob, TN, preferred_element_type=F32))
            dq_ref[qs, :] = jnp.concatenate(dqs, axis=1)
            return jnp.concatenate(dks, axis=1), jnp.concatenate(dvs, axis=1)

        for r in range(dil):
            dk_cur, dv_cur = one(r, 0, causal)
            for n in range(1, nb):
                dk, dv = one(r, n, band)
                ks = _drows(dil, r, (n - 1) * QBLK, QBLK)
                dk_ref[ks, :] = dk_cur + dk[:QBLK]
                dv_ref[ks, :] = dv_cur + dv[:QBLK]
                dk_cur, dv_cur = dk[QBLK:], dv[QBLK:]
            ks = _drows(dil, r, (nb - 1) * QBLK, QBLK)
            dk_ref[ks, :] = dk_cur
            dv_ref[ks, :] = dv_cur

    q_spec, k_spec, v_spec, o_spec = _attn_specs(i, seq)
    in_specs = [q_spec, k_spec, v_spec, o_spec, o_spec, o_spec, o_spec]
    args = [q_all, kv_all, kv_all, o, l, do, dl]
    aliases = {}
    if not first:
        in_specs += [pl.BlockSpec(memory_space=pl.ANY)] * 3
        args += list(grads)
        aliases = {7: 0, 8: 1, 9: 2}
    shape = jax.ShapeDtypeStruct((bsz, seq, 3 * 1024), F32)
    return pl.pallas_call(
        body, name="attn_bwd_d%d" % dil, grid=(bsz, ATT_HB), in_specs=in_specs,
        out_specs=[q_spec, q_spec, q_spec], out_shape=[shape, shape, shape], input_output_aliases=aliases,
        compiler_params=_params(("parallel", "parallel")))(*args)


@jax.custom_vjp
def attn_branches(q_all, kv_all):
    return _attn_branches_fwd(q_all, kv_all)[0]


def _attn_branches_fwd(q_all, kv_all):
    outs = []
    for i, dil in enumerate(BRANCH_DIL):
        outs += list(_attn_fwd_call(q_all, kv_all, i, dil))
    return tuple(outs), (q_all, kv_all, tuple(outs))


def _attn_branches_bwd(res, cts):
    q_all, kv_all, outs = res
    grads = None
    for i, dil in enumerate(BRANCH_DIL):
        grads = _attn_bwd_call(q_all, kv_all, outs[2 * i], outs[2 * i + 1], cts[2 * i], cts[2 * i + 1], i, dil, grads)
    dq, dk, dv = grads
    return dq, jnp.concatenate([dk, dv], axis=-1)


attn_branches.defvjp(_attn_branches_fwd, _attn_branches_bwd)


def ada_fwd(c_all, w, b):
    n, d, cs = w.shape
    nb = c_all.shape[0]

    def body(c_ref, w_ref, b_ref, o_ref):
        a = jax.nn.silu(c_ref[...]).astype(BF16)
        o_ref[...] = lax.dot_general(a, w_ref[...].astype(BF16), NN, preferred_element_type=F32) + b_ref[...]

    return pl.pallas_call(
        body, name="ada_fwd", grid=(n,),
        in_specs=[pl.BlockSpec((nb, d), lambda i: (0, 0)), pl.BlockSpec((None, d, cs), lambda i: (i, 0, 0)),
                  pl.BlockSpec((None, 1, cs), lambda i: (i, 0, 0))],
        out_specs=pl.BlockSpec((None, nb, cs), lambda i: (i, 0, 0)),
        out_shape=jax.ShapeDtypeStruct((n, nb, cs), F32), compiler_params=_params(("parallel",)))(c_all, w, b)


def ada_bwd(c_all, dm):
    n, nb, cs = dm.shape
    d = c_all.shape[1]

    def body(c_ref, dm_ref, dw_ref, db_ref):
        a = jax.nn.silu(c_ref[...]).astype(BF16)
        g = dm_ref[...]
        dw_ref[...] = lax.dot_general(a, g.astype(BF16), TN, preferred_element_type=F32)
        db_ref[...] = jnp.sum(g, axis=0, keepdims=True)

    return pl.pallas_call(
        body, name="ada_bwd", grid=(n,),
        in_specs=[pl.BlockSpec((nb, d), lambda i: (0, 0)), pl.BlockSpec((None, nb, cs), lambda i: (i, 0, 0))],
        out_specs=[pl.BlockSpec((None, d, cs), lambda i: (i, 0, 0)), pl.BlockSpec((None, 1, cs), lambda i: (i, 0, 0))],
        out_shape=[jax.ShapeDtypeStruct((n, d, cs), F32), jax.ShapeDtypeStruct((n, 1, cs), F32)],
        compiler_params=_params(("parallel",)))(c_all, dm)


def all_gather(x, name):
    m, n = x.shape

    def body(x_ref, out_ref, send_sems, recv_sems, local_sem):
        px, py, pc = lax.axis_index("x"), lax.axis_index("y"), lax.axis_index("c")
        me, sibling = (px, py, pc), (px, py, 1 - pc)
        chips = [(1 - px, py), (px, 1 - py), (1 - px, 1 - py)]

        def rows(qx, qy, qc):
            return out_ref.at[pl.ds((4 * qx + 2 * qy + qc) * m, m), :]

        def copy(k, block, to, src=None):
            return pltpu.make_async_remote_copy(
                src_ref=rows(*block) if src is None else src, dst_ref=rows(*block),
                send_sem=send_sems.at[k], recv_sem=recv_sems.at[k], device_id=to,
                device_id_type=pl.DeviceIdType.MESH)

        mine = pltpu.make_async_copy(x_ref, rows(*me), local_sem)
        mine.start()
        first = [copy(0, me, sibling, src=x_ref)]
        first += [copy(1 + j, me, (*chip, pc), src=x_ref) for j, chip in enumerate(chips)]
        for cp in first:
            cp.start()
        passed = [copy(4 + j, (*chip, pc), sibling) for j, chip in enumerate(chips)]
        for j, chip in enumerate(chips):
            copy(1 + j, (*chip, pc), me).wait_recv()
            passed[j].start()
        copy(0, sibling, me).wait_recv()
        for j, chip in enumerate(chips):
            copy(4 + j, (*chip, 1 - pc), me).wait_recv()
        for cp in first + passed:
            cp.wait_send()
        mine.wait()

    out = pl.pallas_call(
        body, name=name, out_shape=jax.ShapeDtypeStruct((N_DEV * m, n), x.dtype),
        in_specs=[pl.BlockSpec(memory_space=pl.ANY)], out_specs=pl.BlockSpec(memory_space=pl.ANY),
        scratch_shapes=[pltpu.SemaphoreType.DMA((7,)), pltpu.SemaphoreType.DMA((7,)), pltpu.SemaphoreType.DMA(())],
    )(x)
    return out.reshape(N_DEV, m, n)


def exchange_partials(p, name):
    _, m, n = p.shape

    def body(p_ref, out_ref, send_sems, recv_sems, local_sem):
        px, py, pc = lax.axis_index("x"), lax.axis_index("y"), lax.axis_index("c")
        me = 4 * px + 2 * py + pc
        mine = pltpu.make_async_copy(p_ref.at[me], out_ref.at[me], local_sem)
        mine.start()
        copies = []
        for k in range(1, N_DEV):
            qx = 1 - px if k & 4 else px
            qy = 1 - py if k & 2 else py
            qc = 1 - pc if k & 1 else pc
            cp = pltpu.make_async_remote_copy(
                src_ref=p_ref.at[4 * qx + 2 * qy + qc], dst_ref=out_ref.at[me],
                send_sem=send_sems.at[k - 1], recv_sem=recv_sems.at[k - 1], device_id=(qx, qy, qc),
                device_id_type=pl.DeviceIdType.MESH)
            cp.start()
            copies.append(cp)
        for cp in copies:
            cp.wait()
        mine.wait()

    return pl.pallas_call(
        body, name=name, out_shape=jax.ShapeDtypeStruct(p.shape, p.dtype),
        in_specs=[pl.BlockSpec(memory_space=pl.ANY)], out_specs=pl.BlockSpec(memory_space=pl.ANY),
        scratch_shapes=[pltpu.SemaphoreType.DMA((7,)), pltpu.SemaphoreType.DMA((7,)), pltpu.SemaphoreType.DMA(())],
    )(p)


def sum_slots(*parts):
    _, r, n = parts[0].shape
    tr = _tile(r, max(16, (256 * 1024) // n))

    def body(*refs):
        acc = None
        for g_ref in refs[:-1]:
            for s in range(g_ref.shape[0]):
                term = g_ref[s].astype(F32)
                acc = term if acc is None else acc + term
        refs[-1][...] = acc

    return pl.pallas_call(
        body, name="sum_slots", grid=(r // tr,),
        in_specs=[pl.BlockSpec((p.shape[0], tr, n), lambda i: (0, i, 0)) for p in parts],
        out_specs=pl.BlockSpec((tr, n), lambda i: (i, 0)), out_shape=jax.ShapeDtypeStruct((r, n), F32),
        compiler_params=_params(("parallel",)))(*parts)


def adamw(w, m, v, g, name):
    nl, r, n = w.shape
    partials = g.ndim == 4
    tr = _tile(r, max(8, (256 * 1024) // n))
    c1 = 1.0 / (1.0 - ADAM_B1 ** ADAM_STEP)
    c2 = 1.0 / (1.0 - ADAM_B2 ** ADAM_STEP)

    def body(w_ref, m_ref, v_ref, g_ref, go_ref, d_ref, mo_ref, vo_ref):
        if partials:
            grad = g_ref[0].astype(F32)
            for s in range(1, N_DEV):
                grad = grad + g_ref[s].astype(F32)
        else:
            grad = g_ref[...]
        m_new = ADAM_B1 * m_ref[...] + (1.0 - ADAM_B1) * grad
        v_new = ADAM_B2 * v_ref[...] + (1.0 - ADAM_B2) * grad * grad
        go_ref[...] = grad
        mo_ref[...] = m_new
        vo_ref[...] = v_new
        d_ref[...] = -ADAM_LR * ((m_new * c1) / (jnp.sqrt(v_new * c2) + ADAM_EPS) + ADAM_WD * w_ref[...])

    spec = pl.BlockSpec((None, tr, n), lambda l, i: (l, i, 0))
    g_spec = pl.BlockSpec((None, N_DEV, tr, n), lambda l, i: (l, 0, i, 0)) if partials else spec
    shape = jax.ShapeDtypeStruct(w.shape, F32)
    return pl.pallas_call(
        body, name=name, grid=(nl, r // tr), in_specs=[spec, spec, spec, g_spec], out_specs=[spec] * 4,
        out_shape=[shape] * 4, compiler_params=_params(("parallel", "parallel")))(w, m, v, g)


def pack(arrays):
    flat = jnp.concatenate([a.reshape(-1).astype(F32) for a in arrays])
    rows = -(-flat.shape[0] // LANES)
    mult = 8 if rows <= 512 else 512
    rows = -(-rows // mult) * mult
    flat = jnp.pad(flat, (0, rows * LANES - flat.shape[0]))
    return flat.reshape(rows, LANES)


def unpack(slab, shapes):
    flat = slab.reshape(slab.shape[:-2] + (-1,))
    out, off = [], 0
    for s in shapes:
        size = math.prod(s)
        out.append(flat[..., off:off + size].reshape(flat.shape[:-1] + tuple(s)))
        off += size
    return out


def _s5_discretize(lam_re, lam_im, log_dt, b_re, b_im, c_re, c_im):
    dt = jnp.exp(log_dt)[:, None]
    xr, th = lam_re * dt, lam_im * dt
    er = jnp.exp(xr)
    a_re, a_im = er * jnp.cos(th), er * jnp.sin(th)
    am1 = jnp.expm1(xr) * jnp.cos(th) - 2.0 * jnp.square(jnp.sin(0.5 * th))
    den = lam_re * lam_re + lam_im * lam_im
    fr = (am1 * lam_re + a_im * lam_im) / den
    fi = (a_im * lam_re - am1 * lam_im) / den
    bb_re = fr[..., None] * b_re - fi[..., None] * b_im
    bb_im = fr[..., None] * b_im + fi[..., None] * b_re
    eye = jnp.eye(8, dtype=F32)

    def pack_b(bb):
        return jnp.einsum('qgpc,gh->qgchp', bb.reshape(8, 8, 64, 16), eye).reshape(8, 128, 512)

    def pack_c(cc):
        return jnp.einsum('qgcp,gh->qgphc', cc.reshape(8, 8, 16, 64), eye).reshape(8, 512, 128)

    b_mat = jnp.concatenate([pack_b(bb_re), pack_b(bb_im)], axis=-1)
    c_mat = jnp.concatenate([pack_c(c_re), pack_c(-c_im)], axis=1)
    return a_re.reshape(8, HALF), a_im.reshape(8, HALF), b_mat, c_mat


WEIGHT_ORDER = (("ssm_w_glu", 0, True), ("mlp_w1", 0, True), ("mlp_w2", 0, False),
                ("ssm_w_glu", 1, True), ("mlp_w1", 1, True), ("mlp_w2", 1, False), ("w_kv", None, True),
                ("attn_w_q", 0, True), ("attn_w_o", 0, False), ("mlp_w1", 2, True), ("mlp_w2", 2, False),
                ("attn_w_q", 1, True), ("attn_w_o", 1, False), ("mlp_w1", 3, True), ("mlp_w2", 3, False))


class _WeightChain:
    def __init__(self, first, upcoming):
        self.cur, self.upcoming, self.used = first, list(upcoming), 0

    def matmul(self, x, name, relu2_input=False):
        assert WEIGHT_ORDER[self.used][0] == name
        col = WEIGHT_ORDER[self.used][2]
        assert not (col and relu2_input)
        self.used += 1
        w = self.cur if col else self.cur.reshape(-1, self.cur.shape[-1])
        plain, with_next = (linear_col, linear_col_next) if col else (
            (linear_relu2, linear_relu2_next) if relu2_input else (linear, linear_next))
        if not self.upcoming:
            return plain(x, w)
        y, self.cur = with_next(x, w, self.upcoming.pop(0))
        return y


def _local_loss(diff, x, target, first_weight, shards):
    bsz, seq, d = x.shape
    t = bsz * seq
    rows3 = lambda a: a.reshape(bsz, seq, a.shape[-1])
    rows2 = lambda a: a.reshape(t, a.shape[-1])
    mods, kvmod, ln_g, ssm_d, kv_g, final_g = (diff[k] for k in ("mods", "kvmod", "ln_g", "ssm_d", "kv_g", "final_g"))
    chain = _WeightChain(first_weight, shards)

    def chunks(sub):
        mod = mods[sub]
        return mod[:, None, :d], mod[:, None, d:2 * d], mod[:, None, 2 * d:], ln_g[sub // 2, sub % 2][None]

    h = x
    kv_all = None
    shift, scale, gate, gain = chunks(0)
    (u,) = modnorm(h, scale, shift, gain)
    for layer in range(4):
        if layer == 2:
            (ukv,) = modnorm(h, kvmod[:, None, d:], kvmod[:, None, :d], kv_g[None])
            kv_all = rows3(chain.matmul(rows2(ukv), "w_kv"))
        if layer < 2:
            a_re, a_im, b_mat, c_mat = _s5_discretize(*(diff[k][layer] for k in (
                "ssm_lam_re", "ssm_lam_im", "ssm_log_dt", "ssm_b_re", "ssm_b_im", "ssm_c_re", "ssm_c_im")))
            y0 = s5_ssm(u, b_mat, c_mat, a_re, a_im)
            (z,) = gelu_skip(y0, u, ssm_d[layer][None])
            (y,) = glu(rows3(chain.matmul(rows2(z), "ssm_w_glu")))
        else:
            q_all = rows3(chain.matmul(rows2(u), "attn_w_q"))
            o1, l1, o2, l2, o3, l3 = attn_branches(q_all, kv_all)
            (o,) = combine(o1, o2, o3, l1, l2, l3)
            y = rows3(chain.matmul(rows2(o), "attn_w_o"))
        shift, scale, next_gate, gain = chunks(2 * layer + 1)
        h, u = res_modnorm(h, y, gate, scale, shift, gain)
        gate = next_gate
        pre = chain.matmul(rows2(u), "mlp_w1")
        y = rows3(chain.matmul(pre, "mlp_w2", relu2_input=True))
        if layer < 3:
            shift, scale, next_gate, gain = chunks(2 * layer + 2)
            h, u = res_modnorm(h, y, gate, scale, shift, gain)
            gate = next_gate
        else:
            (h,) = gate_res(h, y, gate)
    (row_loss,) = final_loss(h, target, final_g[None])
    return jnp.sum(row_loss)


SSM_NAMES = ("ssm_lam_re", "ssm_lam_im", "ssm_log_dt", "ssm_b_re", "ssm_b_im", "ssm_c_re", "ssm_c_im")
ARG_NAMES = ("x", "c", "ln_g", "ada_w", "ada_b") + SSM_NAMES + (
    "ssm_d", "ssm_w_glu", "kv_g", "kv_ada_w", "kv_ada_b", "w_kv", "attn_w_q", "attn_w_o", "mlp_w1", "mlp_w2", "final_g")
WEIGHT_NAMES = ARG_NAMES[2:]


def kernel(x, c, ln_g, ada_w, ada_b, ssm_lam_re, ssm_lam_im, ssm_log_dt, ssm_b_re, ssm_b_im, ssm_c_re, ssm_c_im, ssm_d, ssm_w_glu, kv_g, kv_ada_w, kv_ada_b, w_kv, attn_w_q, attn_w_o, mlp_w1, mlp_w2, final_g, loss_target, m_ln_g, m_ada_w, m_ada_b, m_ssm_lam_re, m_ssm_lam_im, m_ssm_log_dt, m_ssm_b_re, m_ssm_b_im, m_ssm_c_re, m_ssm_c_im, m_ssm_d, m_ssm_w_glu, m_kv_g, m_kv_ada_w, m_kv_ada_b, m_w_kv, m_attn_w_q, m_attn_w_o, m_mlp_w1, m_mlp_w2, m_final_g, v_ln_g, v_ada_w, v_ada_b, v_ssm_lam_re, v_ssm_lam_im, v_ssm_log_dt, v_ssm_b_re, v_ssm_b_im, v_ssm_c_re, v_ssm_c_im, v_ssm_d, v_ssm_w_glu, v_kv_g, v_kv_ada_w, v_kv_ada_b, v_w_kv, v_attn_w_q, v_attn_w_o, v_mlp_w1, v_mlp_w2, v_final_g):
    args = locals()
    w = {n: args[n] for n in WEIGHT_NAMES}
    mom = {n: args["m_" + n] for n in WEIGHT_NAMES}
    var = {n: args["v_" + n] for n in WEIGHT_NAMES}
    bsz, seq, d = x.shape
    me = 4 * lax.axis_index("x") + 2 * lax.axis_index("y") + lax.axis_index("c")

    small = all_gather(pack([c, ln_g, ssm_d]), "gather_small")
    c_parts, ln_parts, d_parts = unpack(small, [c.shape, ln_g.shape, ssm_d.shape])
    c_all = c_parts.reshape(N_DEV * bsz, d)
    ln_full = jnp.moveaxis(ln_parts, 0, 2).reshape(4, 2, d)
    d_full = jnp.moveaxis(d_parts, 0, 1).reshape(2, d)

    cs = ada_w.shape[-1]
    mod_cols = ada_fwd(c_all, ada_w.reshape(8, d, cs), ada_b.reshape(8, 1, cs))
    kcs = kv_ada_w.shape[-1]
    kv_cols = ada_fwd(c_all, kv_ada_w[None], jnp.zeros((1, 1, kcs), F32))
    mod_g = all_gather(mod_cols.reshape(8 * N_DEV * bsz, cs), "gather_mod")
    kv_g_all = all_gather(kv_cols.reshape(N_DEV * bsz, kcs), "gather_kvmod")
    mods_all = jnp.moveaxis(mod_g.reshape(N_DEV, 8, N_DEV * bsz, cs), 0, 2).reshape(8, N_DEV * bsz, N_DEV * cs)
    kvmod_all = jnp.moveaxis(kv_g_all, 0, 1).reshape(N_DEV * bsz, N_DEV * kcs) + kv_ada_b[None]
    mods = lax.dynamic_slice_in_dim(mods_all, me * bsz, bsz, axis=1)
    kvmod = lax.dynamic_slice_in_dim(kvmod_all, me * bsz, bsz, axis=0)

    shards = [w[n] if l is None else w[n][l] for n, l, _ in WEIGHT_ORDER]
    first_weight = all_gather(shards[0].astype(BF16), "gather_first_weight")

    diff = {"mods": mods, "kvmod": kvmod, "ln_g": ln_full, "ssm_d": d_full, "kv_g": kv_g, "final_g": final_g}
    diff.update({n: w[n] for n in SSM_NAMES})
    loss_local, (g_diff, grad_x, g_first, g_shards) = jax.value_and_grad(_local_loss, argnums=(0, 1, 3, 4))(
        diff, x, loss_target, first_weight, shards[1:])
    loss = lax.psum(loss_local, AXES)
    g_shards = [sum_slots(exchange_partials(g_first, "exchange_first_weight"))] + list(g_shards)

    dmod = all_gather(pack([g_diff["mods"], g_diff["kvmod"]]), "gather_dmod")
    dmods_p, dkv_p = unpack(dmod, [g_diff["mods"].shape, g_diff["kvmod"].shape])
    dmods_all = jnp.moveaxis(dmods_p, 0, 1).reshape(8, N_DEV * bsz, 3 * d)
    dkv_all = dkv_p.reshape(N_DEV * bsz, 2 * d)
    g_ada_w, g_ada_b = ada_bwd(c_all, lax.dynamic_slice_in_dim(dmods_all, me * cs, cs, axis=2))
    g_kv_ada_w, _ = ada_bwd(c_all, lax.dynamic_slice_in_dim(dkv_all, me * kcs, kcs, axis=1)[None])
    _, g_kv_ada_b = ada_bwd(c_all, dkv_all[None])

    small_names = ("ln_g", "ssm_d", "kv_g", "final_g") + SSM_NAMES
    partial = all_gather(pack([g_diff[n] for n in small_names]), "gather_small_grads")
    totals = unpack(sum_slots(partial), [g_diff[n].shape for n in small_names])
    g_small = dict(zip(small_names, totals))
    g_small["ln_g"] = lax.dynamic_slice_in_dim(g_small["ln_g"], me * ln_g.shape[-1], ln_g.shape[-1], axis=2)
    g_small["ssm_d"] = lax.dynamic_slice_in_dim(g_small["ssm_d"], me * ssm_d.shape[-1], ssm_d.shape[-1], axis=1)
    g_small["ada_b"] = g_ada_b.reshape(ada_b.shape)
    g_small["kv_ada_b"] = g_kv_ada_b.reshape(kv_ada_b.shape)

    out = {}

    def put(name, res, shape):
        for kind, a in zip(("grad_", "delta_", "new_m_", "new_v_"), res):
            out[kind + name] = a.reshape(shape)

    for name in ("ssm_w_glu", "w_kv", "attn_w_q", "attn_w_o", "mlp_w1", "mlp_w2"):
        grads = jnp.stack([g for g, (n, _, _) in zip(g_shards, WEIGHT_ORDER) if n == name])
        v3 = lambda a: a.reshape(grads.shape)
        put(name, adamw(v3(w[name]), v3(mom[name]), v3(var[name]), grads, "adamw_" + name), w[name].shape)
    v3 = lambda a: a.reshape(8, d, cs)
    put("ada_w", adamw(v3(ada_w), v3(m_ada_w), v3(v_ada_w), g_ada_w, "adamw_ada_w"), ada_w.shape)
    put("kv_ada_w", adamw(kv_ada_w[None], m_kv_ada_w[None], v_kv_ada_w[None], g_kv_ada_w, "adamw_kv_ada_w"), kv_ada_w.shape)
    names = small_names + ("ada_b", "kv_ada_b")
    res = adamw(pack([w[n] for n in names])[None], pack([mom[n] for n in names])[None],
                pack([var[n] for n in names])[None], pack([g_small[n] for n in names])[None], "adamw_small")
    for kind, slab in zip(("grad_", "delta_", "new_m_", "new_v_"), res):
        for n, a in zip(names, unpack(slab[0], [w[n].shape for n in names])):
            out[kind + n] = a

    result = [loss, grad_x]
    for kind in ("grad_", "delta_", "new_m_", "new_v_"):
        result += [out[kind + n] for n in WEIGHT_NAMES]
    return tuple(result)
```

```python
import functools
import math

import jax
import jax.numpy as jnp
from jax import lax
from jax.experimental import pallas as pl
from jax.experimental.pallas import tpu as pltpu

F32 = jnp.float32
BF16 = jnp.bfloat16
N_DEV = 8
AXES = ("x", "y", "c")
V7X_VMEM_LIMIT = 56 * 1024 * 1024
LANES = 128
EPS = 1e-6
NEG = -1e30
HEAD = 64
QBLK = 128
BRANCH_DIL = (1, 4, 16)
ADAM_LR, ADAM_B1, ADAM_B2, ADAM_EPS, ADAM_WD, ADAM_STEP = 0.001, 0.9, 0.999, 1e-08, 0.01, 10

NN = (((1,), (0,)), ((), ()))
NT = (((1,), (1,)), ((), ()))
TN = (((0,), (0,)), ((), ()))


def _params(sem):
    return pltpu.CompilerParams(dimension_semantics=sem, vmem_limit_bytes=V7X_VMEM_LIMIT)


def _peer(k):
    px, py, pc = lax.axis_index("x"), lax.axis_index("y"), lax.axis_index("c")
    qx = 1 - px if k & 4 else px
    qy = 1 - py if k & 2 else py
    qc = 1 - pc if k & 1 else pc
    return (qx, qy, qc), 4 * qx + 2 * qy + qc


ROW_TILE = 2048
NEAR, FAR = (0, 1, 2, 3, 6), (4, 5, 7)


def _exchange_copies(ks, src_ref, land_ref, send_sems, recv_sems):
    copies = []
    for i, k in enumerate(ks):
        peer, slot = _peer(k)
        if k == 0:
            copies.append(pltpu.make_async_copy(src_ref.at[slot], land_ref.at[i], send_sems.at[i]))
        else:
            copies.append(pltpu.make_async_remote_copy(
                src_ref=src_ref.at[slot], dst_ref=land_ref.at[i], send_sem=send_sems.at[i], recv_sem=recv_sems.at[i],
                device_id=peer, device_id_type=pl.DeviceIdType.MESH))
    return copies


def _gather_copies(src_ref, land_ref, send_sems, recv_sems):
    _, me = _peer(0)

    def copy(pair, slot, k, src=None):
        peer, _ = _peer(k)
        return pltpu.make_async_remote_copy(
            src_ref=land_ref.at[slot] if src is None else src, dst_ref=land_ref.at[slot],
            send_sem=send_sems.at[pair], recv_sem=recv_sems.at[pair], device_id=peer,
            device_id_type=pl.DeviceIdType.MESH)

    direct = [copy(0, me, 1, src_ref)] + [copy(1 + j, me, k, src_ref) for j, k in enumerate((2, 4, 6))]
    forwards = [copy(4 + j, _peer(k)[1], 1) for j, k in enumerate((2, 4, 6))]
    from_sibling = [copy(0, _peer(1)[1], 1)] + [copy(4 + j, _peer(k ^ 1)[1], 1) for j, k in enumerate((2, 4, 6))]
    return direct, forwards, from_sibling


def _mm(name, a, b, *, grid, a_spec, b_spec, out_spec, out_shape, dims, acc_axis=None, acc_shape=None, carry=None,
        a_pre=None, post=None, cache=None):
    n_acc = grid[acc_axis] if acc_axis is not None else 1
    n_steps = math.prod(grid)
    n_ext = 0 if post is None else 1

    def finish(r, e_ref):
        return r if post is None else post[2](r, e_ref[...])

    def as_bf16(ref, pre, cache_ref):
        def convert():
            return (ref[...] if pre is None else pre(ref[...])).astype(BF16)

        if cache_ref is None:
            return convert()
        outer, inner = pl.program_id(0), pl.program_id(1)
        if cache[1] == "inner":
            @pl.when(inner == 0)
            def _():
                cache_ref[...] = convert()

            return cache_ref[...]

        @pl.when(outer == 0)
        def _():
            cache_ref[inner] = convert()

        return cache_ref[inner]

    def body(*refs):
        a_ref, b_ref = refs[:2]
        e_ref = refs[2] if post is not None else None
        if carry is None:
            o_ref = refs[2 + n_ext]
            scratch = list(refs[3 + n_ext:])
        else:
            src_ref, o_ref, land_ref = refs[2 + n_ext:5 + n_ext]
            scratch, sems = list(refs[5 + n_ext:-3]), refs[-3:]
            step = functools.reduce(lambda s, i: s * grid[i] + pl.program_id(i), range(len(grid)), 0)
        cache_ref = scratch.pop() if cache is not None else None

        if carry is not None:
            @pl.when(step == 0)
            def _():
                if carry[0] == "gather":
                    pltpu.make_async_copy(src_ref, land_ref.at[_peer(0)[1]], sems[2]).start()
                    for cp in _gather_copies(src_ref, land_ref, sems[0], sems[1])[0]:
                        cp.start()
                else:
                    for cp in _exchange_copies(carry[2], src_ref, land_ref, sems[0], sems[1]):
                        cp.start()

            if carry[0] == "gather":
                @pl.when(step == (3 * n_steps) // 4)
                def _():
                    direct, forwards, _ = _gather_copies(src_ref, land_ref, sems[0], sems[1])
                    for cp, fw in zip(direct[1:], forwards):
                        cp.wait_recv()
                        fw.start()

        a_val = as_bf16(a_ref, a_pre, cache_ref if cache is not None and cache[0] == "a" else None)
        b_val = as_bf16(b_ref, None, cache_ref if cache is not None and cache[0] == "b" else None)
        r = lax.dot_general(a_val, b_val, dims, preferred_element_type=F32)
        if acc_axis is None:
            o_ref[...] = finish(r, e_ref).astype(o_ref.dtype)
        else:
            acc = scratch[0]
            k = pl.program_id(acc_axis)

            @pl.when(k == 0)
            def _():
                acc[...] = r

            @pl.when(k > 0)
            def _():
                acc[...] += r

            @pl.when(k == n_acc - 1)
            def _():
                o_ref[...] = finish(acc[...], e_ref).astype(o_ref.dtype)

        if carry is not None:
            @pl.when(step == n_steps - 1)
            def _():
                if carry[0] == "gather":
                    direct, forwards, from_sibling = _gather_copies(src_ref, land_ref, sems[0], sems[1])
                    for cp in from_sibling:
                        cp.wait_recv()
                    for cp in direct + forwards:
                        cp.wait_send()
                    pltpu.make_async_copy(src_ref, land_ref.at[_peer(0)[1]], sems[2]).wait()
                else:
                    for cp in _exchange_copies(carry[2], src_ref, land_ref, sems[0], sems[1]):
                        cp.wait()

    scratch = [] if acc_axis is None else [pltpu.VMEM(acc_shape, F32)]
    if cache is not None:
        block = tuple(s for s in (a_spec if cache[0] == "a" else b_spec).block_shape if s is not None)
        scratch.append(pltpu.VMEM(block if cache[1] == "inner" else (grid[1],) + block, BF16))
    in_specs, operands = [a_spec, b_spec], [a, b]
    if post is not None:
        in_specs.append(post[1])
        operands.append(post[0])
    if carry is None:
        sem = tuple("arbitrary" if i == acc_axis or cache is not None else "parallel" for i in range(len(grid)))
        return pl.pallas_call(body, name=name, grid=grid, in_specs=in_specs, out_specs=out_spec,
                              out_shape=out_shape, scratch_shapes=scratch, compiler_params=_params(sem))(*operands)
    kind, src = carry[:2]
    n_land = N_DEV if kind == "gather" else len(carry[2])
    n_sem = N_DEV - 1 if kind == "gather" else n_land
    land = jax.ShapeDtypeStruct((n_land,) + src.shape[-2:], src.dtype)
    hbm = pl.BlockSpec(memory_space=pl.ANY)
    scratch += [pltpu.SemaphoreType.DMA((n_sem,)), pltpu.SemaphoreType.DMA((n_sem,)), pltpu.SemaphoreType.DMA(())]
    return pl.pallas_call(body, name=name + "_" + kind, grid=grid, in_specs=in_specs + [hbm],
                          out_specs=[out_spec, hbm], out_shape=[out_shape, land], scratch_shapes=scratch,
                          compiler_params=_params(("arbitrary",) * len(grid)))(*operands, src)


def _tile(n, t):
    if n <= t:
        return n
    for d in range(t - t % 8, 7, -8):
        if n % d == 0:
            return d
    raise ValueError((n, t))


def _col_fwd_call(x, g, out_dtype, carry=None):
    t, k = x.shape
    _, _, ns = g.shape
    tm = _tile(t, ROW_TILE)
    return _mm("col_fwd", x, g, grid=(t // tm, N_DEV),
               a_spec=pl.BlockSpec((tm, k), lambda i, j: (i, 0)),
               b_spec=pl.BlockSpec((None, k, ns), lambda i, j: (j, 0, 0)),
               out_spec=pl.BlockSpec((tm, ns), lambda i, j: (i, j)),
               out_shape=jax.ShapeDtypeStruct((t, N_DEV * ns), out_dtype), dims=NN, carry=carry,
               cache=("a", "inner"))


def _col_dx_call(dy, g, carry=None):
    _, k, ns = g.shape
    t = dy.shape[0]
    tm = _tile(t, ROW_TILE)
    return _mm("col_dx", dy, g, grid=(t // tm, N_DEV),
               a_spec=pl.BlockSpec((tm, ns), lambda i, j: (i, j)),
               b_spec=pl.BlockSpec((None, k, ns), lambda i, j: (j, 0, 0)),
               out_spec=pl.BlockSpec((tm, k), lambda i, j: (i, 0)),
               out_shape=jax.ShapeDtypeStruct((t, k), F32), dims=NT, acc_axis=1, acc_shape=(tm, k), carry=carry)


def _col_dw_call(x, dy, ns, carry=None):
    t, k = x.shape
    tt = _tile(t, ROW_TILE)
    last = t // tt - 1
    return _mm("col_dw", x, dy, grid=(N_DEV, t // tt), cache=("a", "outer"),
               a_spec=pl.BlockSpec((tt, k), lambda j, s: (jnp.where(j == 0, s, last), 0)),
               b_spec=pl.BlockSpec((tt, ns), lambda j, s: (s, j)),
               out_spec=pl.BlockSpec((None, k, ns), lambda j, s: (j, 0, 0)),
               out_shape=jax.ShapeDtypeStruct((N_DEV, k, ns), BF16), dims=TN, acc_axis=1, acc_shape=(k, ns),
               carry=carry)


def _make_linear_col(out_dtype):
    @jax.custom_vjp
    def linear_col(x, g):
        return _col_fwd_call(x, g, out_dtype)

    def fwd(x, g):
        return _col_fwd_call(x, g, out_dtype), (x, g)

    def bwd(res, dy):
        x, g = res
        return _col_dx_call(dy, g), _col_dw_call(x, dy, g.shape[2])

    linear_col.defvjp(fwd, bwd)

    @jax.custom_vjp
    def linear_col_next(x, g, nxt):
        return next_fwd(x, g, nxt)[0]

    def next_fwd(x, g, nxt):
        y, g_next = _col_fwd_call(x, g, out_dtype, carry=("gather", nxt.astype(BF16)))
        return (y, g_next), (x, g)

    def next_bwd(res, cts):
        x, g = res
        dy, dg_next = cts
        dx, near = _col_dx_call(dy, g, carry=("exchange", dg_next, NEAR))
        dg, far = _col_dw_call(x, dy, g.shape[2], carry=("exchange", dg_next, FAR))
        return dx, dg, sum_slots(near, far)

    linear_col_next.defvjp(next_fwd, next_bwd)
    return linear_col, linear_col_next


linear_col, linear_col_next = _make_linear_col(F32)
linear_col_bf16, linear_col_bf16_next = _make_linear_col(BF16)


def _relu2(a):
    return jnp.square(jnp.maximum(a.astype(F32), 0.0))


def _relu2_grad(d_act, a):
    return d_act * (2.0 * jnp.maximum(a.astype(F32), 0.0))


def _lin_fwd_call(x, w, act, carry=None):
    t, k = x.shape
    _, n = w.shape
    tm, tk = _tile(t, ROW_TILE), _tile(k, 1024)
    return _mm("lin_fwd", x, w, grid=(t // tm, k // tk),
               a_spec=pl.BlockSpec((tm, tk), lambda i, s: (i, s)),
               b_spec=pl.BlockSpec((tk, n), lambda i, s: (s, 0)),
               out_spec=pl.BlockSpec((tm, n), lambda i, s: (i, 0)),
               out_shape=jax.ShapeDtypeStruct((t, n), F32), dims=NN, acc_axis=1, acc_shape=(tm, n), carry=carry,
               a_pre=_relu2 if act else None)


def _lin_dx_call(dy, w, x, act, carry=None):
    k, n = w.shape
    t = dy.shape[0]
    tm, tk = _tile(t, 1024), _tile(k, 1024)
    out_spec = pl.BlockSpec((tm, tk), lambda i, s: (i, s))
    return _mm("lin_dx", dy, w, grid=(t // tm, k // tk),
               a_spec=pl.BlockSpec((tm, n), lambda i, s: (i, 0)),
               b_spec=pl.BlockSpec((tk, n), lambda i, s: (s, 0)),
               out_spec=out_spec, out_shape=jax.ShapeDtypeStruct((t, k), x.dtype), dims=NT, carry=carry,
               post=(x, out_spec, _relu2_grad) if act else None, cache=("a", "inner"))


def _lin_dw_call(x, dy, act, carry=None):
    t, k = x.shape
    n = dy.shape[1]
    tk, tt = _tile(k, 1024), _tile(t, ROW_TILE)
    last = t // tt - 1
    return _mm("lin_dw", x, dy, grid=(k // tk, t // tt), cache=("b", "outer"),
               a_spec=pl.BlockSpec((tt, tk), lambda s, r: (r, s)),
               b_spec=pl.BlockSpec((tt, n), lambda s, r: (jnp.where(s == 0, r, last), 0)),
               out_spec=pl.BlockSpec((tk, n), lambda s, r: (s, 0)),
               out_shape=jax.ShapeDtypeStruct((k, n), BF16), dims=TN, acc_axis=1, acc_shape=(tk, n), carry=carry,
               a_pre=_relu2 if act else None)


def _make_linear(act):
    @jax.custom_vjp
    def linear(x, w):
        return _lin_fwd_call(x, w, act)

    def fwd(x, w):
        return _lin_fwd_call(x, w, act), (x, w)

    def bwd(res, dy):
        x, w = res
        return _lin_dx_call(dy, w, x, act), _lin_dw_call(x, dy, act)

    linear.defvjp(fwd, bwd)

    @jax.custom_vjp
    def linear_next(x, w, nxt):
        return next_fwd(x, w, nxt)[0]

    def next_fwd(x, w, nxt):
        y, g_next = _lin_fwd_call(x, w, act, carry=("gather", nxt.astype(BF16)))
        return (y, g_next), (x, w)

    def next_bwd(res, cts):
        x, w = res
        dy, dg_next = cts
        dx, near = _lin_dx_call(dy, w, x, act, carry=("exchange", dg_next, NEAR))
        dw, far = _lin_dw_call(x, dy, act, carry=("exchange", dg_next, FAR))
        return dx, dw, sum_slots(near, far)

    linear_next.defvjp(next_fwd, next_bwd)
    return linear, linear_next


linear, linear_next = _make_linear(False)
linear_relu2, linear_relu2_next = _make_linear(True)


@jax.custom_vjp
def grouped_mm(x, w):
    return _grouped_fwd(x, w)[0]


def _grouped_fwd(x, w):
    t = x.shape[0]
    p, kin, kout = w.shape
    tm = _tile(t, 1024)
    y = _mm("grp_fwd", x, w, grid=(t // tm, p),
            a_spec=pl.BlockSpec((tm, kin), lambda i, q: (i, q)),
            b_spec=pl.BlockSpec((None, kin, kout), lambda i, q: (q, 0, 0)),
            out_spec=pl.BlockSpec((tm, kout), lambda i, q: (i, q)),
            out_shape=jax.ShapeDtypeStruct((t, p * kout), F32), dims=NN)
    return y, (x, w)


def _grouped_bwd(res, dy):
    x, w = res
    t = x.shape[0]
    p, kin, kout = w.shape
    tm = _tile(t, 1024)
    dx = _mm("grp_dx", dy, w, grid=(t // tm, p),
             a_spec=pl.BlockSpec((tm, kout), lambda i, q: (i, q)),
             b_spec=pl.BlockSpec((None, kin, kout), lambda i, q: (q, 0, 0)),
             out_spec=pl.BlockSpec((tm, kin), lambda i, q: (i, q)),
             out_shape=jax.ShapeDtypeStruct((t, p * kin), F32), dims=NT)
    dw = _mm("grp_dw", x, dy, grid=(p, t // tm),
             a_spec=pl.BlockSpec((tm, kin), lambda q, s: (s, q)),
             b_spec=pl.BlockSpec((tm, kout), lambda q, s: (s, q)),
             out_spec=pl.BlockSpec((None, kin, kout), lambda q, s: (q, 0, 0)),
             out_shape=jax.ShapeDtypeStruct((p, kin, kout), F32), dims=TN, acc_axis=1, acc_shape=(kin, kout))
    return dx, dw


grouped_mm.defvjp(_grouped_fwd, _grouped_bwd)


def _row_ts(widths, seq):
    per_row = 4 * sum(widths)
    ts = 512
    while ts > 8 and ts * per_row * 2 > 24 * 1024 * 1024:
        ts //= 2
    return min(ts, seq)


def make_rowop(name, f, n_row, n_batch, n_vec, f_bwd=None):
    n_in = n_row + n_batch + n_vec

    def in_specs(args, ts):
        specs = []
        for a in args[:n_row]:
            specs.append(pl.BlockSpec((None, ts, a.shape[2]), lambda b, s: (b, s, 0)))
        for a in args[n_row:n_row + n_batch]:
            specs.append(pl.BlockSpec((None, 1, a.shape[2]), lambda b, s: (b, 0, 0)))
        for a in args[n_row + n_batch:]:
            specs.append(pl.BlockSpec((1, a.shape[1]), lambda b, s: (0, 0)))
        return specs

    def out_struct(args, ts):
        blocks = [jax.ShapeDtypeStruct((ts, a.shape[2]), F32) for a in args[:n_row]]
        blocks += [jax.ShapeDtypeStruct((1, a.shape[2]), F32) for a in args[n_row:n_row + n_batch]]
        blocks += [jax.ShapeDtypeStruct((1, a.shape[1]), F32) for a in args[n_row + n_batch:]]
        return jax.eval_shape(f, *blocks)

    def run_fwd(args):
        bsz, seq = args[0].shape[:2]
        outs0 = out_struct(args, 8)
        widths = [a.shape[2] for a in args[:n_row]] + [o.shape[1] for o in outs0]
        ts = _row_ts(widths, seq)

        def body(*refs):
            outs = f(*[r[...] for r in refs[:n_in]])
            for o_ref, o in zip(refs[n_in:], outs):
                o_ref[...] = o

        return pl.pallas_call(
            body, name=name + "_fwd", grid=(bsz, seq // ts), in_specs=in_specs(args, ts),
            out_specs=[pl.BlockSpec((None, ts, o.shape[1]), lambda b, s: (b, s, 0)) for o in outs0],
            out_shape=[jax.ShapeDtypeStruct((bsz, seq, o.shape[1]), F32) for o in outs0],
            compiler_params=_params(("parallel", "parallel")))(*args)

    def run_bwd(args, cts):
        bsz, seq = args[0].shape[:2]
        widths = [a.shape[2] for a in args[:n_row]] * 2 + [c.shape[2] for c in cts]
        ts = _row_ts(widths, seq)
        n_ct = len(cts)

        def body(*refs):
            ins = [r[...] for r in refs[:n_in]]
            ct = [r[...] for r in refs[n_in:n_in + n_ct]]
            if f_bwd is None:
                _, vjp = jax.vjp(f, *ins)
                grads = vjp(tuple(ct))
            else:
                grads = f_bwd(ins, ct)
            g_refs = refs[n_in + n_ct:]
            b, s = pl.program_id(0), pl.program_id(1)
            for i in range(n_row):
                g_refs[i][...] = grads[i]
            for i in range(n_row, n_row + n_batch):
                @pl.when(s == 0)
                def _(i=i):
                    g_refs[i][...] = grads[i]

                @pl.when(s > 0)
                def _(i=i):
                    g_refs[i][...] += grads[i]
            for i in range(n_row + n_batch, n_in):
                first = jnp.logical_and(b == 0, s == 0)

                @pl.when(first)
                def _(i=i):
                    g_refs[i][...] = grads[i]

                @pl.when(jnp.logical_not(first))
                def _(i=i):
                    g_refs[i][...] += grads[i]

        ct_specs = [pl.BlockSpec((None, ts, c.shape[2]), lambda b, s: (b, s, 0)) for c in cts]
        return pl.pallas_call(
            body, name=name + "_bwd", grid=(bsz, seq // ts), in_specs=in_specs(args, ts) + ct_specs,
            out_specs=in_specs(args, ts), out_shape=[jax.ShapeDtypeStruct(a.shape, F32) for a in args],
            compiler_params=_params(("arbitrary", "arbitrary")))(*args, *cts)

    @jax.custom_vjp
    def op(*args):
        return tuple(run_fwd(args))

    def fwd(*args):
        return tuple(run_fwd(args)), args

    def bwd(args, cts):
        return tuple(run_bwd(args, list(cts)))

    op.defvjp(fwd, bwd)
    return op


def _modnorm_f(h, scale, shift, g):
    y = h * lax.rsqrt(jnp.mean(h * h, axis=-1, keepdims=True) + EPS) * g
    return (y * (1.0 + scale) + shift,)


def _gate_res_f(h, y, gate):
    return (h + gate * y,)


def _res_modnorm_f(h, y, gate, scale, shift, g):
    h2 = h + gate * y
    return (h2,) + _modnorm_f(h2, scale, shift, g)


def _relu2_f(a):
    return (jnp.square(jnp.maximum(a, 0.0)),)


def _gelu_skip_f(y, u, d):
    return (jax.nn.gelu(y + d * u),)


def _glu_f(vg):
    n = vg.shape[1] // 2
    return (vg[:, :n] * jax.nn.sigmoid(vg[:, n:]),)


def _glu_b(ins, cts):
    (vg,), (ct,) = ins, cts
    n = vg.shape[1] // 2
    sg = jax.nn.sigmoid(vg[:, n:])
    return (jnp.concatenate([ct * sg, ct * vg[:, :n] * sg * (1.0 - sg)], axis=1),)


def _combine_f(o1, o2, o3, l1, l2, l3):
    m = jnp.maximum(jnp.maximum(l1, l2), l3)
    e1, e2, e3 = jnp.exp(l1 - m), jnp.exp(l2 - m), jnp.exp(l3 - m)
    return ((e1 * o1 + e2 * o2 + e3 * o3) / (e1 + e2 + e3),)


def _final_loss_f(h, target, g):
    y = h * lax.rsqrt(jnp.mean(h * h, axis=-1, keepdims=True) + EPS) * g
    return (0.5 * jnp.mean(jnp.square(y - target), axis=-1, keepdims=True),)


modnorm = make_rowop("modnorm", _modnorm_f, 1, 2, 1)
gate_res = make_rowop("gate_res", _gate_res_f, 2, 1, 0)
res_modnorm = make_rowop("res_modnorm", _res_modnorm_f, 2, 3, 1)
relu2 = make_rowop("relu2", _relu2_f, 1, 0, 0)
gelu_skip = make_rowop("gelu_skip", _gelu_skip_f, 2, 0, 1)
glu = make_rowop("glu", _glu_f, 1, 0, 0, _glu_b)
combine = make_rowop("combine", _combine_f, 6, 0, 0)
final_loss = make_rowop("final_loss", _final_loss_f, 2, 0, 1)


SCAN_TB = 64
HALF = 512


def _scan_fwd_call(bu, a_re, a_im):
    bsz, rows, width = bu.shape
    seq = rows // 8
    tb = _tile(seq, SCAN_TB)

    def body(bu_ref, ar_ref, ai_ref, o_ref, carry):
        @pl.when(pl.program_id(0) == 0)
        def _():
            carry[...] = jnp.zeros_like(carry)

        ar, ai = ar_ref[...], ai_ref[...]

        def step(t, st):
            r0 = pl.multiple_of(t * 8, 8)
            new = []
            for b in range(bsz):
                sre, sim = st[2 * b], st[2 * b + 1]
                nre = ar * sre - ai * sim + bu_ref[b, pl.ds(r0, 8), pl.ds(0, HALF)]
                nim = ar * sim + ai * sre + bu_ref[b, pl.ds(r0, 8), pl.ds(HALF, HALF)]
                o_ref[b, pl.ds(r0, 8), pl.ds(0, HALF)] = nre
                o_ref[b, pl.ds(r0, 8), pl.ds(HALF, HALF)] = nim
                new += [nre, nim]
            return tuple(new)

        fin = lax.fori_loop(0, tb, step, tuple(carry[k] for k in range(2 * bsz)), unroll=4)
        for k in range(2 * bsz):
            carry[k] = fin[k]

    blk = pl.BlockSpec((bsz, tb * 8, width), lambda i: (0, i, 0))
    vec = pl.BlockSpec((8, HALF), lambda i: (0, 0))
    return pl.pallas_call(body, name="s5_scan_fwd", grid=(seq // tb,), in_specs=[blk, vec, vec], out_specs=blk,
                          out_shape=jax.ShapeDtypeStruct(bu.shape, F32),
                          scratch_shapes=[pltpu.VMEM((2 * bsz, 8, HALF), F32)],
                          compiler_params=_params(("arbitrary",)))(bu, a_re, a_im)


def _scan_bwd_call(g, st, a_re, a_im):
    bsz, rows, width = g.shape
    seq = rows // 8
    tb = _tile(seq, SCAN_TB)
    nt = seq // tb

    def body(g_ref, st_ref, prev_ref, ar_ref, ai_ref, db_ref, dar_ref, dai_ref, carry):
        i = pl.program_id(0)

        @pl.when(i == 0)
        def _():
            carry[...] = jnp.zeros_like(carry)
            dar_ref[...] = jnp.zeros_like(dar_ref)
            dai_ref[...] = jnp.zeros_like(dai_ref)

        ar, ai = ar_ref[...], ai_ref[...]

        def lam_step(b, r0, lre, lim):
            nre = g_ref[b, pl.ds(r0, 8), pl.ds(0, HALF)] + ar * lre + ai * lim
            nim = g_ref[b, pl.ds(r0, 8), pl.ds(HALF, HALF)] - ai * lre + ar * lim
            db_ref[b, pl.ds(r0, 8), pl.ds(0, HALF)] = nre
            db_ref[b, pl.ds(r0, 8), pl.ds(HALF, HALF)] = nim
            return nre, nim

        def step(k, c):
            t = tb - 1 - k
            r0 = pl.multiple_of(t * 8, 8)
            p0 = pl.multiple_of(t * 8 - 8, 8)
            lam, dar, dai = list(c[:2 * bsz]), c[2 * bsz], c[2 * bsz + 1]
            for b in range(bsz):
                nre, nim = lam_step(b, r0, lam[2 * b], lam[2 * b + 1])
                pre = st_ref[b, pl.ds(p0, 8), pl.ds(0, HALF)]
                pim = st_ref[b, pl.ds(p0, 8), pl.ds(HALF, HALF)]
                dar = dar + nre * pre + nim * pim
                dai = dai + nim * pre - nre * pim
                lam[2 * b], lam[2 * b + 1] = nre, nim
            return tuple(lam) + (dar, dai)

        zero = jnp.zeros((8, HALF), F32)
        init = tuple(carry[k] for k in range(2 * bsz)) + (zero, zero)
        c = lax.fori_loop(0, tb - 1, step, init, unroll=4)
        lam, dar, dai = list(c[:2 * bsz]), c[2 * bsz], c[2 * bsz + 1]
        keep = jnp.where(i == nt - 1, 0.0, 1.0)
        for b in range(bsz):
            nre, nim = lam_step(b, 0, lam[2 * b], lam[2 * b + 1])
            pre = prev_ref[b, :, pl.ds(0, HALF)] * keep
            pim = prev_ref[b, :, pl.ds(HALF, HALF)] * keep
            dar = dar + nre * pre + nim * pim
            dai = dai + nim * pre - nre * pim
            carry[2 * b], carry[2 * b + 1] = nre, nim
        dar_ref[...] += dar
        dai_ref[...] += dai

    blk = pl.BlockSpec((bsz, tb * 8, width), lambda i: (0, nt - 1 - i, 0))
    prev = pl.BlockSpec((bsz, 8, width), lambda i: (0, jnp.maximum((nt - 1 - i) * tb - 1, 0), 0))
    vec = pl.BlockSpec((8, HALF), lambda i: (0, 0))
    return pl.pallas_call(
        body, name="s5_scan_bwd", grid=(nt,), in_specs=[blk, blk, prev, vec, vec], out_specs=[blk, vec, vec],
        out_shape=[jax.ShapeDtypeStruct(g.shape, F32), jax.ShapeDtypeStruct((8, HALF), F32),
                   jax.ShapeDtypeStruct((8, HALF), F32)],
        scratch_shapes=[pltpu.VMEM((2 * bsz, 8, HALF), F32)],
        compiler_params=_params(("arbitrary",)))(g, st, st, a_re, a_im)


@jax.custom_vjp
def s5_scan(bu, a_re, a_im):
    return _scan_fwd_call(bu, a_re, a_im)


def _s5_scan_fwd(bu, a_re, a_im):
    st = _scan_fwd_call(bu, a_re, a_im)
    return st, (st, a_re, a_im)


def _s5_scan_bwd(res, g):
    st, a_re, a_im = res
    return tuple(_scan_bwd_call(g, st, a_re, a_im))


s5_scan.defvjp(_s5_scan_fwd, _s5_scan_bwd)


SSM_TB = 64
PITCH = SSM_TB + 8
PIECES = 8
CH = 128
NCH = 2 * HALF // LANES
NRE = NCH // 2


def _slab(t):
    return pl.ds(t, PIECES, stride=PITCH)


def _put_rows(ref, b, q, val, tb):
    for j in range(NCH):
        ref[b, j, pl.ds(q * PITCH, tb), :] = val[:, j * LANES:(j + 1) * LANES]


def _get_rows(ref, q, bsz, tb):
    return jnp.concatenate(
        [jnp.concatenate([ref[b, j, pl.ds(q * PITCH, tb), :] for j in range(NCH)], axis=1) for b in range(bsz)], axis=0)


def _chunks(a):
    return [a[:, j * LANES:(j + 1) * LANES] for j in range(a.shape[1] // LANES)]


def _ssm_fwd_call(u, b_mat, c_mat, a_re, a_im):
    bsz, seq, d = u.shape
    tb = SSM_TB
    nt = seq // tb
    assert seq % tb == 0 and d == PIECES * CH

    def body(u_ref, b_ref, c_ref, ar_ref, ai_ref, y_ref, st_ref, end_ref, bu, carry):
        @pl.when(pl.program_id(0) == 0)
        def _():
            carry[...] = jnp.zeros_like(carry)

        u2 = u_ref[...].reshape(bsz * tb, d).astype(BF16)
        for q in range(PIECES):
            r = lax.dot_general(u2[:, q * CH:(q + 1) * CH], b_ref[q], NN, preferred_element_type=F32)
            for b in range(bsz):
                _put_rows(bu, b, q, r[b * tb:(b + 1) * tb], tb)
        ar, ai = _chunks(ar_ref[...]), _chunks(ai_ref[...])
        st = [[carry[b, j] for j in range(NCH)] for b in range(bsz)]
        for t in range(tb):
            for b in range(bsz):
                for j in range(NRE):
                    sre, sim = st[b][j], st[b][NRE + j]
                    nre = ar[j] * sre - ai[j] * sim + bu[b, j, _slab(t), :]
                    nim = ar[j] * sim + ai[j] * sre + bu[b, NRE + j, _slab(t), :]
                    st_ref[b, j, _slab(t), :] = nre
                    st_ref[b, NRE + j, _slab(t), :] = nim
                    st[b][j], st[b][NRE + j] = nre, nim
        for b in range(bsz):
            for j in range(NCH):
                carry[b, j] = st[b][j]
                end_ref[b, j] = st[b][j]
        for q in range(PIECES):
            s2 = _get_rows(st_ref, q, bsz, tb).astype(BF16)
            r = lax.dot_general(s2, c_ref[q], NN, preferred_element_type=F32)
            for b in range(bsz):
                y_ref[b, :, q * CH:(q + 1) * CH] = r[b * tb:(b + 1) * tb]

    rows = pl.BlockSpec((bsz, tb, d), lambda i: (0, i, 0))
    vec = pl.BlockSpec((PIECES, HALF), lambda i: (0, 0))
    return pl.pallas_call(
        body, name="s5_ssm_fwd", grid=(nt,),
        in_specs=[rows, pl.BlockSpec((PIECES, CH, 2 * HALF), lambda i: (0, 0, 0)),
                  pl.BlockSpec((PIECES, 2 * HALF, CH), lambda i: (0, 0, 0)), vec, vec],
        out_specs=[rows, pl.BlockSpec((bsz, None, NCH, PIECES * PITCH, LANES), lambda i: (0, i, 0, 0, 0)),
                   pl.BlockSpec((bsz, None, NCH, PIECES, LANES), lambda i: (0, i, 0, 0, 0))],
        out_shape=[jax.ShapeDtypeStruct((bsz, seq, d), F32),
                   jax.ShapeDtypeStruct((bsz, nt, NCH, PIECES * PITCH, LANES), F32),
                   jax.ShapeDtypeStruct((bsz, nt, NCH, PIECES, LANES), F32)],
        scratch_shapes=[pltpu.VMEM((bsz, NCH, PIECES * PITCH, LANES), F32), pltpu.VMEM((bsz, NCH, PIECES, LANES), F32)],
        compiler_params=_params(("arbitrary",)))(u, b_mat, c_mat, a_re, a_im)


def _ssm_bwd_call(dy, u, st, ends, b_mat, c_mat, a_re, a_im):
    bsz, seq, d = u.shape
    tb = SSM_TB
    nt = seq // tb

    def body(dy_ref, u_ref, st_ref, prev_ref, b_ref, c_ref, ar_ref, ai_ref, du_ref, db_hbm, dc_hbm, dar_ref, dai_ref,
             lam, carry, db_acc, dc_acc):
        i = pl.program_id(0)

        @pl.when(i == 0)
        def _():
            carry[...] = jnp.zeros_like(carry)
            db_acc[...] = jnp.zeros_like(db_acc)
            dc_acc[...] = jnp.zeros_like(dc_acc)
            dar_ref[...] = jnp.zeros_like(dar_ref)
            dai_ref[...] = jnp.zeros_like(dai_ref)

        dy2 = dy_ref[...].reshape(bsz * tb, d).astype(BF16)
        u2 = u_ref[...].reshape(bsz * tb, d).astype(BF16)
        for q in range(PIECES):
            dyq = dy2[:, q * CH:(q + 1) * CH]
            g = lax.dot_general(dyq, c_ref[q], NT, preferred_element_type=F32)
            for b in range(bsz):
                _put_rows(lam, b, q, g[b * tb:(b + 1) * tb], tb)
            s2 = _get_rows(st_ref, q, bsz, tb).astype(BF16)
            dc_acc[q] += lax.dot_general(s2, dyq, TN, preferred_element_type=F32)

        ar, ai = _chunks(ar_ref[...]), _chunks(ai_ref[...])
        keep = jnp.where(i == nt - 1, 0.0, 1.0)
        lm = [[carry[b, j] for j in range(NCH)] for b in range(bsz)]
        dar = [jnp.zeros((PIECES, LANES), F32) for _ in range(NRE)]
        dai = [jnp.zeros((PIECES, LANES), F32) for _ in range(NRE)]
        for t in range(tb - 1, -1, -1):
            for b in range(bsz):
                for j in range(NRE):
                    lre, lim = lm[b][j], lm[b][NRE + j]
                    nre = lam[b, j, _slab(t), :] + ar[j] * lre + ai[j] * lim
                    nim = lam[b, NRE + j, _slab(t), :] - ai[j] * lre + ar[j] * lim
                    lam[b, j, _slab(t), :] = nre
                    lam[b, NRE + j, _slab(t), :] = nim
                    if t > 0:
                        pre, pim = st_ref[b, j, _slab(t - 1), :], st_ref[b, NRE + j, _slab(t - 1), :]
                    else:
                        pre, pim = prev_ref[b, j] * keep, prev_ref[b, NRE + j] * keep
                    dar[j] = dar[j] + nre * pre + nim * pim
                    dai[j] = dai[j] + nim * pre - nre * pim
                    lm[b][j], lm[b][NRE + j] = nre, nim
        for b in range(bsz):
            for j in range(NCH):
                carry[b, j] = lm[b][j]
        dar_ref[...] += jnp.concatenate(dar, axis=1)
        dai_ref[...] += jnp.concatenate(dai, axis=1)

        for q in range(PIECES):
            l2 = _get_rows(lam, q, bsz, tb).astype(BF16)
            r = lax.dot_general(l2, b_ref[q], NT, preferred_element_type=F32)
            for b in range(bsz):
                du_ref[b, :, q * CH:(q + 1) * CH] = r[b * tb:(b + 1) * tb]
            db_acc[q] += lax.dot_general(u2[:, q * CH:(q + 1) * CH], l2, TN, preferred_element_type=F32)

        @pl.when(i == nt - 1)
        def _():
            pltpu.sync_copy(db_acc, db_hbm)
            pltpu.sync_copy(dc_acc, dc_hbm)

    rows = pl.BlockSpec((bsz, tb, d), lambda i: (0, nt - 1 - i, 0))
    vec = pl.BlockSpec((PIECES, HALF), lambda i: (0, 0))
    hbm = pl.BlockSpec(memory_space=pl.ANY)
    return pl.pallas_call(
        body, name="s5_ssm_bwd", grid=(nt,),
        in_specs=[rows, rows,
                  pl.BlockSpec((bsz, None, NCH, PIECES * PITCH, LANES), lambda i: (0, nt - 1 - i, 0, 0, 0)),
                  pl.BlockSpec((bsz, None, NCH, PIECES, LANES), lambda i: (0, jnp.maximum(nt - 2 - i, 0), 0, 0, 0)),
                  pl.BlockSpec((PIECES, CH, 2 * HALF), lambda i: (0, 0, 0)),
                  pl.BlockSpec((PIECES, 2 * HALF, CH), lambda i: (0, 0, 0)), vec, vec],
        out_specs=[rows, hbm, hbm, vec, vec],
        out_shape=[jax.ShapeDtypeStruct((bsz, seq, d), F32), jax.ShapeDtypeStruct((PIECES, CH, 2 * HALF), F32),
                   jax.ShapeDtypeStruct((PIECES, 2 * HALF, CH), F32), jax.ShapeDtypeStruct((PIECES, HALF), F32),
                   jax.ShapeDtypeStruct((PIECES, HALF), F32)],
        scratch_shapes=[pltpu.VMEM((bsz, NCH, PIECES * PITCH, LANES), F32), pltpu.VMEM((bsz, NCH, PIECES, LANES), F32),
                        pltpu.VMEM((PIECES, CH, 2 * HALF), F32), pltpu.VMEM((PIECES, 2 * HALF, CH), F32)],
        compiler_params=_params(("arbitrary",)))(dy, u, st, ends, b_mat, c_mat, a_re, a_im)


@jax.custom_vjp
def s5_ssm(u, b_mat, c_mat, a_re, a_im):
    return _s5_ssm_fwd(u, b_mat, c_mat, a_re, a_im)[0]


def _s5_ssm_fwd(u, b_mat, c_mat, a_re, a_im):
    b16, c16 = b_mat.astype(BF16), c_mat.astype(BF16)
    y, st, ends = _ssm_fwd_call(u, b16, c16, a_re, a_im)
    return y, (u, st, ends, b16, c16, a_re, a_im)


def _s5_ssm_bwd(res, dy):
    return tuple(_ssm_bwd_call(dy, *res))


s5_ssm.defvjp(_s5_ssm_fwd, _s5_ssm_bwd)


ATT_HW = 2 * HEAD
ATT_HB = 1024 // ATT_HW


def _branch_geometry(dil, seq):
    sub = seq // dil
    assert sub % QBLK == 0
    return sub // QBLK


def _drows(dil, r, start, size):
    if dil == 1:
        return pl.ds(start, size)
    return pl.ds(r + start * dil, size, stride=dil)


def _masks():
    qi = lax.broadcasted_iota(jnp.int32, (QBLK, 2 * QBLK), 0)
    kj = lax.broadcasted_iota(jnp.int32, (QBLK, 2 * QBLK), 1) - QBLK
    dist = qi - kj
    band = jnp.logical_and(dist >= 0, dist <= QBLK)
    ci = lax.broadcasted_iota(jnp.int32, (QBLK, QBLK), 0)
    cj = lax.broadcasted_iota(jnp.int32, (QBLK, QBLK), 1)
    return band, ci >= cj


def _attn_specs(i, seq):
    q_spec = pl.BlockSpec((None, seq, ATT_HW), lambda b, h: (b, 0, i * ATT_HB + h))
    k_spec = pl.BlockSpec((None, seq, ATT_HW), lambda b, h: (b, 0, i * ATT_HB + h))
    v_spec = pl.BlockSpec((None, seq, ATT_HW), lambda b, h: (b, 0, (3 + i) * ATT_HB + h))
    o_spec = pl.BlockSpec((None, seq, ATT_HW), lambda b, h: (b, 0, h))
    return q_spec, k_spec, v_spec, o_spec


def _attn_fwd_call(q_all, kv_all, i, dil):
    bsz, seq, _ = q_all.shape
    nb = _branch_geometry(dil, seq)
    scale = HEAD ** -0.5

    def body(q_ref, k_ref, v_ref, o_ref, l_ref):
        band, causal = _masks()

        def one(r, n, mask):
            qs = _drows(dil, r, n * QBLK, QBLK)
            ks = qs if n == 0 else _drows(dil, r, (n - 1) * QBLK, 2 * QBLK)
            q2 = q_ref[qs, :].astype(BF16)
            k2 = k_ref[ks, :].astype(BF16)
            v2 = v_ref[ks, :].astype(BF16)
            outs, lses = [], []
            for h in range(ATT_HW // HEAD):
                ls = slice(h * HEAD, (h + 1) * HEAD)
                s = lax.dot_general(q2[:, ls], k2[:, ls], NT, preferred_element_type=F32) * scale
                s = jnp.where(mask, s, NEG)
                m = jnp.max(s, axis=-1, keepdims=True)
                p = jnp.exp(s - m)
                den = jnp.sum(p, axis=-1, keepdims=True)
                outs.append(lax.dot_general(p.astype(BF16), v2[:, ls], NN, preferred_element_type=F32) / den)
                lses.append(jnp.broadcast_to(m + jnp.log(den), (QBLK, HEAD)))
            o_ref[qs, :] = jnp.concatenate(outs, axis=1)
            l_ref[qs, :] = jnp.concatenate(lses, axis=1)

        for r in range(dil):
            for n in range(nb):
                one(r, n, causal if n == 0 else band)

    q_spec, k_spec, v_spec, o_spec = _attn_specs(i, seq)
    shape = jax.ShapeDtypeStruct((bsz, seq, 1024), F32)
    return pl.pallas_call(
        body, name="attn_fwd_d%d" % dil, grid=(bsz, ATT_HB), in_specs=[q_spec, k_spec, v_spec],
        out_specs=[o_spec, o_spec], out_shape=[shape, shape],
        compiler_params=_params(("parallel", "parallel")))(q_all, kv_all, kv_all)


def _attn_bwd_call(q_all, kv_all, o, l, do, dl, i, dil, grads):
    bsz, seq, _ = q_all.shape
    nb = _branch_geometry(dil, seq)
    scale = HEAD ** -0.5
    first = grads is None

    def body(q_ref, k_ref, v_ref, o_ref, l_ref, do_ref, dl_ref, *rest):
        dq_ref, dk_ref, dv_ref = rest[-3:]
        band, causal = _masks()

        def one(r, n, mask):
            qs = _drows(dil, r, n * QBLK, QBLK)
            ks = qs if n == 0 else _drows(dil, r, (n - 1) * QBLK, 2 * QBLK)
            q2 = q_ref[qs, :].astype(BF16)
            k2 = k_ref[ks, :].astype(BF16)
            v2 = v_ref[ks, :].astype(BF16)
            l2, do2 = l_ref[qs, :], do_ref[qs, :]
            t2 = dl_ref[qs, :] - do2 * o_ref[qs, :]
            dqs, dks, dvs = [], [], []
            for h in range(ATT_HW // HEAD):
                ls = slice(h * HEAD, (h + 1) * HEAD)
                q, k, v = q2[:, ls], k2[:, ls], v2[:, ls]
                s = lax.dot_general(q, k, NT, preferred_element_type=F32) * scale
                s = jnp.where(mask, s, NEG)
                p = jnp.exp(s - l2[:, h * HEAD:h * HEAD + 1])
                row = jnp.sum(t2[:, ls], axis=-1, keepdims=True)
                d_ob = do2[:, ls].astype(BF16)
                dp = lax.dot_general(d_ob, v, NT, preferred_element_type=F32)
                ds = (p * (dp + row)).astype(BF16)
                dqs.append(lax.dot_general(ds, k, NN, preferred_element_type=F32) * scale)
                dks.append(lax.dot_general(ds, q, TN, preferred_element_type=F32) * scale)
                dvs.append(lax.dot_general(p.astype(BF16), d_ob, TN, preferred_element_type=F32))
            dq_ref[qs, :] = jnp.concatenate(dqs, axis=1)
            return jnp.concatenate(dks, axis=1), jnp.concatenate(dvs, axis=1)

        for r in range(dil):
            dk_cur, dv_cur = one(r, 0, causal)
            for n in range(1, nb):
                dk, dv = one(r, n, band)
                ks = _drows(dil, r, (n - 1) * QBLK, QBLK)
                dk_ref[ks, :] = dk_cur + dk[:QBLK]
                dv_ref[ks, :] = dv_cur + dv[:QBLK]
                dk_cur, dv_cur = dk[QBLK:], dv[QBLK:]
            ks = _drows(dil, r, (nb - 1) * QBLK, QBLK)
            dk_ref[ks, :] = dk_cur
            dv_ref[ks, :] = dv_cur

    q_spec, k_spec, v_spec, o_spec = _attn_specs(i, seq)
    in_specs = [q_spec, k_spec, v_spec, o_spec, o_spec, o_spec, o_spec]
    args = [q_all, kv_all, kv_all, o, l, do, dl]
    aliases = {}
    if not first:
        in_specs += [pl.BlockSpec(memory_space=pl.ANY)] * 3
        args += list(grads)
        aliases = {7: 0, 8: 1, 9: 2}
    shape = jax.ShapeDtypeStruct((bsz, seq, 3 * 1024), F32)
    return pl.pallas_call(
        body, name="attn_bwd_d%d" % dil, grid=(bsz, ATT_HB), in_specs=in_specs,
        out_specs=[q_spec, q_spec, q_spec], out_shape=[shape, shape, shape], input_output_aliases=aliases,
        compiler_params=_params(("parallel", "parallel")))(*args)


@jax.custom_vjp
def attn_branches(q_all, kv_all):
    return _attn_branches_fwd(q_all, kv_all)[0]


def _attn_branches_fwd(q_all, kv_all):
    outs = []
    for i, dil in enumerate(BRANCH_DIL):
        outs += list(_attn_fwd_call(q_all, kv_all, i, dil))
    return tuple(outs), (q_all, kv_all, tuple(outs))


def _attn_branches_bwd(res, cts):
    q_all, kv_all, outs = res
    grads = None
    for i, dil in enumerate(BRANCH_DIL):
        grads = _attn_bwd_call(q_all, kv_all, outs[2 * i], outs[2 * i + 1], cts[2 * i], cts[2 * i + 1], i, dil, grads)
    dq, dk, dv = grads
    return dq, jnp.concatenate([dk, dv], axis=-1)


attn_branches.defvjp(_attn_branches_fwd, _attn_branches_bwd)


def ada_fwd(c_all, w, b):
    n, d, cs = w.shape
    nb = c_all.shape[0]

    def body(c_ref, w_ref, b_ref, o_ref):
        a = jax.nn.silu(c_ref[...]).astype(BF16)
        o_ref[...] = lax.dot_general(a, w_ref[...].astype(BF16), NN, preferred_element_type=F32) + b_ref[...]

    return pl.pallas_call(
        body, name="ada_fwd", grid=(n,),
        in_specs=[pl.BlockSpec((nb, d), lambda i: (0, 0)), pl.BlockSpec((None, d, cs), lambda i: (i, 0, 0)),
                  pl.BlockSpec((None, 1, cs), lambda i: (i, 0, 0))],
        out_specs=pl.BlockSpec((None, nb, cs), lambda i: (i, 0, 0)),
        out_shape=jax.ShapeDtypeStruct((n, nb, cs), F32), compiler_params=_params(("parallel",)))(c_all, w, b)


def ada_bwd(c_all, dm):
    n, nb, cs = dm.shape
    d = c_all.shape[1]

    def body(c_ref, dm_ref, dw_ref, db_ref):
        a = jax.nn.silu(c_ref[...]).astype(BF16)
        g = dm_ref[...]
        dw_ref[...] = lax.dot_general(a, g.astype(BF16), TN, preferred_element_type=F32)
        db_ref[...] = jnp.sum(g, axis=0, keepdims=True)

    return pl.pallas_call(
        body, name="ada_bwd", grid=(n,),
        in_specs=[pl.BlockSpec((nb, d), lambda i: (0, 0)), pl.BlockSpec((None, nb, cs), lambda i: (i, 0, 0))],
        out_specs=[pl.BlockSpec((None, d, cs), lambda i: (i, 0, 0)), pl.BlockSpec((None, 1, cs), lambda i: (i, 0, 0))],
        out_shape=[jax.ShapeDtypeStruct((n, d, cs), F32), jax.ShapeDtypeStruct((n, 1, cs), F32)],
        compiler_params=_params(("parallel",)))(c_all, dm)


def all_gather(x, name):
    m, n = x.shape

    def body(x_ref, out_ref, send_sems, recv_sems, local_sem):
        px, py, pc = lax.axis_index("x"), lax.axis_index("y"), lax.axis_index("c")
        me, sibling = (px, py, pc), (px, py, 1 - pc)
        chips = [(1 - px, py), (px, 1 - py), (1 - px, 1 - py)]

        def rows(qx, qy, qc):
            return out_ref.at[pl.ds((4 * qx + 2 * qy + qc) * m, m), :]

        def copy(k, block, to, src=None):
            return pltpu.make_async_remote_copy(
                src_ref=rows(*block) if src is None else src, dst_ref=rows(*block),
                send_sem=send_sems.at[k], recv_sem=recv_sems.at[k], device_id=to,
                device_id_type=pl.DeviceIdType.MESH)

        mine = pltpu.make_async_copy(x_ref, rows(*me), local_sem)
        mine.start()
        first = [copy(0, me, sibling, src=x_ref)]
        first += [copy(1 + j, me, (*chip, pc), src=x_ref) for j, chip in enumerate(chips)]
        for cp in first:
            cp.start()
        passed = [copy(4 + j, (*chip, pc), sibling) for j, chip in enumerate(chips)]
        for j, chip in enumerate(chips):
            copy(1 + j, (*chip, pc), me).wait_recv()
            passed[j].start()
        copy(0, sibling, me).wait_recv()
        for j, chip in enumerate(chips):
            copy(4 + j, (*chip, 1 - pc), me).wait_recv()
        for cp in first + passed:
            cp.wait_send()
        mine.wait()

    out = pl.pallas_call(
        body, name=name, out_shape=jax.ShapeDtypeStruct((N_DEV * m, n), x.dtype),
        in_specs=[pl.BlockSpec(memory_space=pl.ANY)], out_specs=pl.BlockSpec(memory_space=pl.ANY),
        scratch_shapes=[pltpu.SemaphoreType.DMA((7,)), pltpu.SemaphoreType.DMA((7,)), pltpu.SemaphoreType.DMA(())],
    )(x)
    return out.reshape(N_DEV, m, n)


def exchange_partials(p, name):
    _, m, n = p.shape

    def body(p_ref, out_ref, send_sems, recv_sems, local_sem):
        px, py, pc = lax.axis_index("x"), lax.axis_index("y"), lax.axis_index("c")
        me = 4 * px + 2 * py + pc
        mine = pltpu.make_async_copy(p_ref.at[me], out_ref.at[me], local_sem)
        mine.start()
        copies = []
        for k in range(1, N_DEV):
            qx = 1 - px if k & 4 else px
            qy = 1 - py if k & 2 else py
            qc = 1 - pc if k & 1 else pc
            cp = pltpu.make_async_remote_copy(
                src_ref=p_ref.at[4 * qx + 2 * qy + qc], dst_ref=out_ref.at[me],
                send_sem=send_sems.at[k - 1], recv_sem=recv_sems.at[k - 1], device_id=(qx, qy, qc),
                device_id_type=pl.DeviceIdType.MESH)
            cp.start()
            copies.append(cp)
        for cp in copies:
            cp.wait()
        mine.wait()

    return pl.pallas_call(
        body, name=name, out_shape=jax.ShapeDtypeStruct(p.shape, p.dtype),
        in_specs=[pl.BlockSpec(memory_space=pl.ANY)], out_specs=pl.BlockSpec(memory_space=pl.ANY),
        scratch_shapes=[pltpu.SemaphoreType.DMA((7,)), pltpu.SemaphoreType.DMA((7,)), pltpu.SemaphoreType.DMA(())],
    )(p)


def sum_slots(*parts):
    _, r, n = parts[0].shape
    tr = _tile(r, max(16, (256 * 1024) // n))

    def body(*refs):
        acc = None
        for g_ref in refs[:-1]:
            for s in range(g_ref.shape[0]):
                term = g_ref[s].astype(F32)
                acc = term if acc is None else acc + term
        refs[-1][...] = acc

    return pl.pallas_call(
        body, name="sum_slots", grid=(r // tr,),
        in_specs=[pl.BlockSpec((p.shape[0], tr, n), lambda i: (0, i, 0)) for p in parts],
        out_specs=pl.BlockSpec((tr, n), lambda i: (i, 0)), out_shape=jax.ShapeDtypeStruct((r, n), F32),
        compiler_params=_params(("parallel",)))(*parts)


def adamw(w, m, v, g, name):
    nl, r, n = w.shape
    partials = g.ndim == 4
    tr = _tile(r, max(8, (256 * 1024) // n))
    c1 = 1.0 / (1.0 - ADAM_B1 ** ADAM_STEP)
    c2 = 1.0 / (1.0 - ADAM_B2 ** ADAM_STEP)

    def body(w_ref, m_ref, v_ref, g_ref, go_ref, d_ref, mo_ref, vo_ref):
        if partials:
            grad = g_ref[0].astype(F32)
            for s in range(1, N_DEV):
                grad = grad + g_ref[s].astype(F32)
        else:
            grad = g_ref[...]
        m_new = ADAM_B1 * m_ref[...] + (1.0 - ADAM_B1) * grad
        v_new = ADAM_B2 * v_ref[...] + (1.0 - ADAM_B2) * grad * grad
        go_ref[...] = grad
        mo_ref[...] = m_new
        vo_ref[...] = v_new
        d_ref[...] = -ADAM_LR * ((m_new * c1) / (jnp.sqrt(v_new * c2) + ADAM_EPS) + ADAM_WD * w_ref[...])

    spec = pl.BlockSpec((None, tr, n), lambda l, i: (l, i, 0))
    g_spec = pl.BlockSpec((None, N_DEV, tr, n), lambda l, i: (l, 0, i, 0)) if partials else spec
    shape = jax.ShapeDtypeStruct(w.shape, F32)
    return pl.pallas_call(
        body, name=name, grid=(nl, r // tr), in_specs=[spec, spec, spec, g_spec], out_specs=[spec] * 4,
        out_shape=[shape] * 4, compiler_params=_params(("parallel", "parallel")))(w, m, v, g)


def pack(arrays):
    flat = jnp.concatenate([a.reshape(-1).astype(F32) for a in arrays])
    rows = -(-flat.shape[0] // LANES)
    mult = 8 if rows <= 512 else 512
    rows = -(-rows // mult) * mult
    flat = jnp.pad(flat, (0, rows * LANES - flat.shape[0]))
    return flat.reshape(rows, LANES)


def unpack(slab, shapes):
    flat = slab.reshape(slab.shape[:-2] + (-1,))
    out, off = [], 0
    for s in shapes:
        size = math.prod(s)
        out.append(flat[..., off:off + size].reshape(flat.shape[:-1] + tuple(s)))
        off += size
    return out


def _s5_discretize(lam_re, lam_im, log_dt, b_re, b_im, c_re, c_im):
    dt = jnp.exp(log_dt)[:, None]
    xr, th = lam_re * dt, lam_im * dt
    er = jnp.exp(xr)
    a_re, a_im = er * jnp.cos(th), er * jnp.sin(th)
    am1 = jnp.expm1(xr) * jnp.cos(th) - 2.0 * jnp.square(jnp.sin(0.5 * th))
    den = lam_re * lam_re + lam_im * lam_im
    fr = (am1 * lam_re + a_im * lam_im) / den
    fi = (a_im * lam_re - am1 * lam_im) / den
    bb_re = fr[..., None] * b_re - fi[..., None] * b_im
    bb_im = fr[..., None] * b_im + fi[..., None] * b_re
    eye = jnp.eye(8, dtype=F32)

    def pack_b(bb):
        return jnp.einsum('qgpc,gh->qgchp', bb.reshape(8, 8, 64, 16), eye).reshape(8, 128, 512)

    def pack_c(cc):
        return jnp.einsum('qgcp,gh->qgphc', cc.reshape(8, 8, 16, 64), eye).reshape(8, 512, 128)

    b_mat = jnp.concatenate([pack_b(bb_re), pack_b(bb_im)], axis=-1)
    c_mat = jnp.concatenate([pack_c(c_re), pack_c(-c_im)], axis=1)
    return a_re.reshape(8, HALF), a_im.reshape(8, HALF), b_mat, c_mat


WEIGHT_ORDER = (("ssm_w_glu", 0, True), ("mlp_w1", 0, True), ("mlp_w2", 0, False),
                ("ssm_w_glu", 1, True), ("mlp_w1", 1, True), ("mlp_w2", 1, False), ("w_kv", None, True),
                ("attn_w_q", 0, True), ("attn_w_o", 0, False), ("mlp_w1", 2, True), ("mlp_w2", 2, False),
                ("attn_w_q", 1, True), ("attn_w_o", 1, False), ("mlp_w1", 3, True), ("mlp_w2", 3, False))


class _WeightChain:
    def __init__(self, first, upcoming):
        self.cur, self.upcoming, self.used = first, list(upcoming), 0

    def matmul(self, x, name, relu2_input=False, bf16_output=False):
        assert WEIGHT_ORDER[self.used][0] == name
        col = WEIGHT_ORDER[self.used][2]
        assert not (col and relu2_input) and (col or not bf16_output)
        self.used += 1
        w = self.cur if col else self.cur.reshape(-1, self.cur.shape[-1])
        if col:
            plain, with_next = (linear_col_bf16, linear_col_bf16_next) if bf16_output else (linear_col, linear_col_next)
        else:
            plain, with_next = (linear_relu2, linear_relu2_next) if relu2_input else (linear, linear_next)
        if not self.upcoming:
            return plain(x, w)
        y, self.cur = with_next(x, w, self.upcoming.pop(0))
        return y


def _local_loss(diff, x, target, first_weight, shards):
    bsz, seq, d = x.shape
    t = bsz * seq
    rows3 = lambda a: a.reshape(bsz, seq, a.shape[-1])
    rows2 = lambda a: a.reshape(t, a.shape[-1])
    mods, kvmod, ln_g, ssm_d, kv_g, final_g = (diff[k] for k in ("mods", "kvmod", "ln_g", "ssm_d", "kv_g", "final_g"))
    chain = _WeightChain(first_weight, shards)

    def chunks(sub):
        mod = mods[sub]
        return mod[:, None, :d], mod[:, None, d:2 * d], mod[:, None, 2 * d:], ln_g[sub // 2, sub % 2][None]

    h = x
    kv_all = None
    shift, scale, gate, gain = chunks(0)
    (u,) = modnorm(h, scale, shift, gain)
    for layer in range(4):
        if layer == 2:
            (ukv,) = modnorm(h, kvmod[:, None, d:], kvmod[:, None, :d], kv_g[None])
            kv_all = rows3(chain.matmul(rows2(ukv), "w_kv"))
        if layer < 2:
            a_re, a_im, b_mat, c_mat = _s5_discretize(*(diff[k][layer] for k in (
                "ssm_lam_re", "ssm_lam_im", "ssm_log_dt", "ssm_b_re", "ssm_b_im", "ssm_c_re", "ssm_c_im")))
            y0 = s5_ssm(u, b_mat, c_mat, a_re, a_im)
            (z,) = gelu_skip(y0, u, ssm_d[layer][None])
            (y,) = glu(rows3(chain.matmul(rows2(z), "ssm_w_glu")))
        else:
            q_all = rows3(chain.matmul(rows2(u), "attn_w_q"))
            o1, l1, o2, l2, o3, l3 = attn_branches(q_all, kv_all)
            (o,) = combine(o1, o2, o3, l1, l2, l3)
            y = rows3(chain.matmul(rows2(o), "attn_w_o"))
        shift, scale, next_gate, gain = chunks(2 * layer + 1)
        h, u = res_modnorm(h, y, gate, scale, shift, gain)
        gate = next_gate
        pre = chain.matmul(rows2(u), "mlp_w1", bf16_output=True)
        y = rows3(chain.matmul(pre, "mlp_w2", relu2_input=True))
        if layer < 3:
            shift, scale, next_gate, gain = chunks(2 * layer + 2)
            h, u = res_modnorm(h, y, gate, scale, shift, gain)
            gate = next_gate
        else:
            (h,) = gate_res(h, y, gate)
    (row_loss,) = final_loss(h, target, final_g[None])
    return jnp.sum(row_loss)


SSM_NAMES = ("ssm_lam_re", "ssm_lam_im", "ssm_log_dt", "ssm_b_re", "ssm_b_im", "ssm_c_re", "ssm_c_im")
ARG_NAMES = ("x", "c", "ln_g", "ada_w", "ada_b") + SSM_NAMES + (
    "ssm_d", "ssm_w_glu", "kv_g", "kv_ada_w", "kv_ada_b", "w_kv", "attn_w_q", "attn_w_o", "mlp_w1", "mlp_w2", "final_g")
WEIGHT_NAMES = ARG_NAMES[2:]


def kernel(x, c, ln_g, ada_w, ada_b, ssm_lam_re, ssm_lam_im, ssm_log_dt, ssm_b_re, ssm_b_im, ssm_c_re, ssm_c_im, ssm_d, ssm_w_glu, kv_g, kv_ada_w, kv_ada_b, w_kv, attn_w_q, attn_w_o, mlp_w1, mlp_w2, final_g, loss_target, m_ln_g, m_ada_w, m_ada_b, m_ssm_lam_re, m_ssm_lam_im, m_ssm_log_dt, m_ssm_b_re, m_ssm_b_im, m_ssm_c_re, m_ssm_c_im, m_ssm_d, m_ssm_w_glu, m_kv_g, m_kv_ada_w, m_kv_ada_b, m_w_kv, m_attn_w_q, m_attn_w_o, m_mlp_w1, m_mlp_w2, m_final_g, v_ln_g, v_ada_w, v_ada_b, v_ssm_lam_re, v_ssm_lam_im, v_ssm_log_dt, v_ssm_b_re, v_ssm_b_im, v_ssm_c_re, v_ssm_c_im, v_ssm_d, v_ssm_w_glu, v_kv_g, v_kv_ada_w, v_kv_ada_b, v_w_kv, v_attn_w_q, v_attn_w_o, v_mlp_w1, v_mlp_w2, v_final_g):
    args = locals()
    w = {n: args[n] for n in WEIGHT_NAMES}
    mom = {n: args["m_" + n] for n in WEIGHT_NAMES}
    var = {n: args["v_" + n] for n in WEIGHT_NAMES}
    bsz, seq, d = x.shape
    me = 4 * lax.axis_index("x") + 2 * lax.axis_index("y") + lax.axis_index("c")

    small = all_gather(pack([c, ln_g, ssm_d]), "gather_small")
    c_parts, ln_parts, d_parts = unpack(small, [c.shape, ln_g.shape, ssm_d.shape])
    c_all = c_parts.reshape(N_DEV * bsz, d)
    ln_full = jnp.moveaxis(ln_parts, 0, 2).reshape(4, 2, d)
    d_full = jnp.moveaxis(d_parts, 0, 1).reshape(2, d)

    cs = ada_w.shape[-1]
    mod_cols = ada_fwd(c_all, ada_w.reshape(8, d, cs), ada_b.reshape(8, 1, cs))
    kcs = kv_ada_w.shape[-1]
    kv_cols = ada_fwd(c_all, kv_ada_w[None], jnp.zeros((1, 1, kcs), F32))
    mod_g = all_gather(mod_cols.reshape(8 * N_DEV * bsz, cs), "gather_mod")
    kv_g_all = all_gather(kv_cols.reshape(N_DEV * bsz, kcs), "gather_kvmod")
    mods_all = jnp.moveaxis(mod_g.reshape(N_DEV, 8, N_DEV * bsz, cs), 0, 2).reshape(8, N_DEV * bsz, N_DEV * cs)
    kvmod_all = jnp.moveaxis(kv_g_all, 0, 1).reshape(N_DEV * bsz, N_DEV * kcs) + kv_ada_b[None]
    mods = lax.dynamic_slice_in_dim(mods_all, me * bsz, bsz, axis=1)
    kvmod = lax.dynamic_slice_in_dim(kvmod_all, me * bsz, bsz, axis=0)

    shards = [w[n] if l is None else w[n][l] for n, l, _ in WEIGHT_ORDER]
    first_weight = all_gather(shards[0].astype(BF16), "gather_first_weight")

    diff = {"mods": mods, "kvmod": kvmod, "ln_g": ln_full, "ssm_d": d_full, "kv_g": kv_g, "final_g": final_g}
    diff.update({n: w[n] for n in SSM_NAMES})
    loss_local, (g_diff, grad_x, g_first, g_shards) = jax.value_and_grad(_local_loss, argnums=(0, 1, 3, 4))(
        diff, x, loss_target, first_weight, shards[1:])
    loss = lax.psum(loss_local, AXES)
    g_shards = [sum_slots(exchange_partials(g_first, "exchange_first_weight"))] + list(g_shards)

    dmod = all_gather(pack([g_diff["mods"], g_diff["kvmod"]]), "gather_dmod")
    dmods_p, dkv_p = unpack(dmod, [g_diff["mods"].shape, g_diff["kvmod"].shape])
    dmods_all = jnp.moveaxis(dmods_p, 0, 1).reshape(8, N_DEV * bsz, 3 * d)
    dkv_all = dkv_p.reshape(N_DEV * bsz, 2 * d)
    g_ada_w, g_ada_b = ada_bwd(c_all, lax.dynamic_slice_in_dim(dmods_all, me * cs, cs, axis=2))
    g_kv_ada_w, _ = ada_bwd(c_all, lax.dynamic_slice_in_dim(dkv_all, me * kcs, kcs, axis=1)[None])
    _, g_kv_ada_b = ada_bwd(c_all, dkv_all[None])

    small_names = ("ln_g", "ssm_d", "kv_g", "final_g") + SSM_NAMES
    partial = all_gather(pack([g_diff[n] for n in small_names]), "gather_small_grads")
    totals = unpack(sum_slots(partial), [g_diff[n].shape for n in small_names])
    g_small = dict(zip(small_names, totals))
    g_small["ln_g"] = lax.dynamic_slice_in_dim(g_small["ln_g"], me * ln_g.shape[-1], ln_g.shape[-1], axis=2)
    g_small["ssm_d"] = lax.dynamic_slice_in_dim(g_small["ssm_d"], me * ssm_d.shape[-1], ssm_d.shape[-1], axis=1)
    g_small["ada_b"] = g_ada_b.reshape(ada_b.shape)
    g_small["kv_ada_b"] = g_kv_ada_b.reshape(kv_ada_b.shape)

    out = {}

    def put(name, res, shape):
        for kind, a in zip(("grad_", "delta_", "new_m_", "new_v_"), res):
            out[kind + name] = a.reshape(shape)

    for name in ("ssm_w_glu", "w_kv", "attn_w_q", "attn_w_o", "mlp_w1", "mlp_w2"):
        grads = jnp.stack([g for g, (n, _, _) in zip(g_shards, WEIGHT_ORDER) if n == name])
        v3 = lambda a: a.reshape(grads.shape)
        put(name, adamw(v3(w[name]), v3(mom[name]), v3(var[name]), grads, "adamw_" + name), w[name].shape)
    v3 = lambda a: a.reshape(8, d, cs)
    put("ada_w", adamw(v3(ada_w), v3(m_ada_w), v3(v_ada_w), g_ada_w, "adamw_ada_w"), ada_w.shape)
    put("kv_ada_w", adamw(kv_ada_w[None], m_kv_ada_w[None], v_kv_ada_w[None], g_kv_ada_w, "adamw_kv_ada_w"), kv_ada_w.shape)
    names = small_names + ("ada_b", "kv_ada_b")
    res = adamw(pack([w[n] for n in names])[None], pack([mom[n] for n in names])[None],
                pack([var[n] for n in names])[None], pack([g_small[n] for n in names])[None], "adamw_small")
    for kind, slab in zip(("grad_", "delta_", "new_m_", "new_v_"), res):
        for n, a in zip(names, unpack(slab[0], [w[n].shape for n in names])):
            out[kind + n] = a

    result = [loss, grad_x]
    for kind in ("grad_", "delta_", "new_m_", "new_v_"):
        result += [out[kind + n] for n in WEIGHT_NAMES]
    return tuple(result)
```

```python
import functools
import math

import jax
import jax.numpy as jnp
from jax import lax
from jax.experimental import pallas as pl
from jax.experimental.pallas import tpu as pltpu

F32 = jnp.float32
BF16 = jnp.bfloat16
N_DEV = 8
AXES = ("x", "y", "c")
V7X_VMEM_LIMIT = 56 * 1024 * 1024
LANES = 128
EPS = 1e-6
NEG = -1e30
HEAD = 64
QBLK = 128
BRANCH_DIL = (1, 4, 16)
ADAM_LR, ADAM_B1, ADAM_B2, ADAM_EPS, ADAM_WD, ADAM_STEP = 0.001, 0.9, 0.999, 1e-08, 0.01, 10

NN = (((1,), (0,)), ((), ()))
NT = (((1,), (1,)), ((), ()))
TN = (((0,), (0,)), ((), ()))


def _params(sem):
    return pltpu.CompilerParams(dimension_semantics=sem, vmem_limit_bytes=V7X_VMEM_LIMIT)


def _peer(k):
    px, py, pc = lax.axis_index("x"), lax.axis_index("y"), lax.axis_index("c")
    qx = 1 - px if k & 4 else px
    qy = 1 - py if k & 2 else py
    qc = 1 - pc if k & 1 else pc
    return (qx, qy, qc), 4 * qx + 2 * qy + qc


ROW_TILE = 2048
NEAR, FAR = (0, 1, 2, 3, 6), (4, 5, 7)


def _exchange_copies(ks, src_ref, land_ref, send_sems, recv_sems):
    copies = []
    for i, k in enumerate(ks):
        peer, slot = _peer(k)
        if k == 0:
            copies.append(pltpu.make_async_copy(src_ref.at[slot], land_ref.at[i], send_sems.at[i]))
        else:
            copies.append(pltpu.make_async_remote_copy(
                src_ref=src_ref.at[slot], dst_ref=land_ref.at[i], send_sem=send_sems.at[i], recv_sem=recv_sems.at[i],
                device_id=peer, device_id_type=pl.DeviceIdType.MESH))
    return copies


def _gather_copies(src_ref, land_ref, send_sems, recv_sems):
    _, me = _peer(0)

    def copy(pair, slot, k, src=None):
        peer, _ = _peer(k)
        return pltpu.make_async_remote_copy(
            src_ref=land_ref.at[slot] if src is None else src, dst_ref=land_ref.at[slot],
            send_sem=send_sems.at[pair], recv_sem=recv_sems.at[pair], device_id=peer,
            device_id_type=pl.DeviceIdType.MESH)

    direct = [copy(0, me, 1, src_ref)] + [copy(1 + j, me, k, src_ref) for j, k in enumerate((2, 4, 6))]
    forwards = [copy(4 + j, _peer(k)[1], 1) for j, k in enumerate((2, 4, 6))]
    from_sibling = [copy(0, _peer(1)[1], 1)] + [copy(4 + j, _peer(k ^ 1)[1], 1) for j, k in enumerate((2, 4, 6))]
    return direct, forwards, from_sibling


def _mm(name, a, b, *, grid, a_spec, b_spec, out_spec, out_shape, dims, acc_axis=None, acc_shape=None, carry=None,
        a_pre=None, post=None, cache=None):
    n_acc = grid[acc_axis] if acc_axis is not None else 1
    n_steps = math.prod(grid)
    n_ext = 0 if post is None else 1

    def finish(r, e_ref):
        return r if post is None else post[2](r, e_ref[...])

    def as_bf16(ref, pre, cache_ref):
        def convert():
            return (ref[...] if pre is None else pre(ref[...])).astype(BF16)

        if cache_ref is None:
            return convert()
        outer, inner = pl.program_id(0), pl.program_id(1)
        if cache[1] == "inner":
            @pl.when(inner == 0)
            def _():
                cache_ref[...] = convert()

            return cache_ref[...]

        @pl.when(outer == 0)
        def _():
            cache_ref[inner] = convert()

        return cache_ref[inner]

    def body(*refs):
        a_ref, b_ref = refs[:2]
        e_ref = refs[2] if post is not None else None
        if carry is None:
            o_ref = refs[2 + n_ext]
            scratch = list(refs[3 + n_ext:])
        else:
            src_ref, o_ref, land_ref = refs[2 + n_ext:5 + n_ext]
            scratch, sems = list(refs[5 + n_ext:-3]), refs[-3:]
            step = functools.reduce(lambda s, i: s * grid[i] + pl.program_id(i), range(len(grid)), 0)
        cache_ref = scratch.pop() if cache is not None else None

        if carry is not None:
            @pl.when(step == 0)
            def _():
                if carry[0] == "gather":
                    pltpu.make_async_copy(src_ref, land_ref.at[_peer(0)[1]], sems[2]).start()
                    for cp in _gather_copies(src_ref, land_ref, sems[0], sems[1])[0]:
                        cp.start()
                else:
                    for cp in _exchange_copies(carry[2], src_ref, land_ref, sems[0], sems[1]):
                        cp.start()

            if carry[0] == "gather":
                @pl.when(step == (3 * n_steps) // 4)
                def _():
                    direct, forwards, _ = _gather_copies(src_ref, land_ref, sems[0], sems[1])
                    for cp, fw in zip(direct[1:], forwards):
                        cp.wait_recv()
                        fw.start()

        a_val = as_bf16(a_ref, a_pre, cache_ref if cache is not None and cache[0] == "a" else None)
        b_val = as_bf16(b_ref, None, cache_ref if cache is not None and cache[0] == "b" else None)
        r = lax.dot_general(a_val, b_val, dims, preferred_element_type=F32)
        if acc_axis is None:
            o_ref[...] = finish(r, e_ref).astype(o_ref.dtype)
        else:
            acc = scratch[0]
            k = pl.program_id(acc_axis)

            @pl.when(k == 0)
            def _():
                acc[...] = r

            @pl.when(k > 0)
            def _():
                acc[...] += r

            @pl.when(k == n_acc - 1)
            def _():
                o_ref[...] = finish(acc[...], e_ref).astype(o_ref.dtype)

        if carry is not None:
            @pl.when(step == n_steps - 1)
            def _():
                if carry[0] == "gather":
                    direct, forwards, from_sibling = _gather_copies(src_ref, land_ref, sems[0], sems[1])
                    for cp in from_sibling:
                        cp.wait_recv()
                    for cp in direct + forwards:
                        cp.wait_send()
                    pltpu.make_async_copy(src_ref, land_ref.at[_peer(0)[1]], sems[2]).wait()
                else:
                    for cp in _exchange_copies(carry[2], src_ref, land_ref, sems[0], sems[1]):
                        cp.wait()

    scratch = [] if acc_axis is None else [pltpu.VMEM(acc_shape, F32)]
    if cache is not None:
        block = tuple(s for s in (a_spec if cache[0] == "a" else b_spec).block_shape if s is not None)
        scratch.append(pltpu.VMEM(block if cache[1] == "inner" else (grid[1],) + block, BF16))
    in_specs, operands = [a_spec, b_spec], [a, b]
    if post is not None:
        in_specs.append(post[1])
        operands.append(post[0])
    if carry is None:
        sem = tuple("arbitrary" if i == acc_axis or cache is not None else "parallel" for i in range(len(grid)))
        return pl.pallas_call(body, name=name, grid=grid, in_specs=in_specs, out_specs=out_spec,
                              out_shape=out_shape, scratch_shapes=scratch, compiler_params=_params(sem))(*operands)
    kind, src = carry[:2]
    n_land = N_DEV if kind == "gather" else len(carry[2])
    n_sem = N_DEV - 1 if kind == "gather" else n_land
    land = jax.ShapeDtypeStruct((n_land,) + src.shape[-2:], src.dtype)
    hbm = pl.BlockSpec(memory_space=pl.ANY)
    scratch += [pltpu.SemaphoreType.DMA((n_sem,)), pltpu.SemaphoreType.DMA((n_sem,)), pltpu.SemaphoreType.DMA(())]
    return pl.pallas_call(body, name=name + "_" + kind, grid=grid, in_specs=in_specs + [hbm],
                          out_specs=[out_spec, hbm], out_shape=[out_shape, land], scratch_shapes=scratch,
                          compiler_params=_params(("arbitrary",) * len(grid)))(*operands, src)


def _tile(n, t):
    if n <= t:
        return n
    for d in range(t - t % 8, 7, -8):
        if n % d == 0:
            return d
    raise ValueError((n, t))


def _col_fwd_call(x, g, out_dtype, carry=None):
    t, k = x.shape
    _, _, ns = g.shape
    tm = _tile(t, ROW_TILE)
    return _mm("col_fwd", x, g, grid=(t // tm, N_DEV),
               a_spec=pl.BlockSpec((tm, k), lambda i, j: (i, 0)),
               b_spec=pl.BlockSpec((None, k, ns), lambda i, j: (j, 0, 0)),
               out_spec=pl.BlockSpec((tm, ns), lambda i, j: (i, j)),
               out_shape=jax.ShapeDtypeStruct((t, N_DEV * ns), out_dtype), dims=NN, carry=carry,
               cache=("a", "inner"))


def _col_dx_call(dy, g, carry=None):
    _, k, ns = g.shape
    t = dy.shape[0]
    tm = _tile(t, ROW_TILE)
    return _mm("col_dx", dy, g, grid=(t // tm, N_DEV),
               a_spec=pl.BlockSpec((tm, ns), lambda i, j: (i, j)),
               b_spec=pl.BlockSpec((None, k, ns), lambda i, j: (j, 0, 0)),
               out_spec=pl.BlockSpec((tm, k), lambda i, j: (i, 0)),
               out_shape=jax.ShapeDtypeStruct((t, k), F32), dims=NT, acc_axis=1, acc_shape=(tm, k), carry=carry)


def _col_dw_call(x, dy, ns, carry=None):
    t, k = x.shape
    tt = _tile(t, ROW_TILE)
    last = t // tt - 1
    return _mm("col_dw", x, dy, grid=(N_DEV, t // tt), cache=("a", "outer"),
               a_spec=pl.BlockSpec((tt, k), lambda j, s: (jnp.where(j == 0, s, last), 0)),
               b_spec=pl.BlockSpec((tt, ns), lambda j, s: (s, j)),
               out_spec=pl.BlockSpec((None, k, ns), lambda j, s: (j, 0, 0)),
               out_shape=jax.ShapeDtypeStruct((N_DEV, k, ns), BF16), dims=TN, acc_axis=1, acc_shape=(k, ns),
               carry=carry)


def _make_linear_col(out_dtype):
    @jax.custom_vjp
    def linear_col(x, g):
        return _col_fwd_call(x, g, out_dtype)

    def fwd(x, g):
        return _col_fwd_call(x, g, out_dtype), (x, g)

    def bwd(res, dy):
        x, g = res
        return _col_dx_call(dy, g), _col_dw_call(x, dy, g.shape[2])

    linear_col.defvjp(fwd, bwd)

    @jax.custom_vjp
    def linear_col_next(x, g, nxt):
        return next_fwd(x, g, nxt)[0]

    def next_fwd(x, g, nxt):
        y, g_next = _col_fwd_call(x, g, out_dtype, carry=("gather", nxt.astype(BF16)))
        return (y, g_next), (x, g)

    def next_bwd(res, cts):
        x, g = res
        dy, dg_next = cts
        dx, near = _col_dx_call(dy, g, carry=("exchange", dg_next, NEAR))
        dg, far = _col_dw_call(x, dy, g.shape[2], carry=("exchange", dg_next, FAR))
        return dx, dg, sum_slots(near, far)

    linear_col_next.defvjp(next_fwd, next_bwd)
    return linear_col, linear_col_next


linear_col, linear_col_next = _make_linear_col(F32)
linear_col_bf16, linear_col_bf16_next = _make_linear_col(BF16)


def _relu2(a):
    return jnp.square(jnp.maximum(a.astype(F32), 0.0))


def _relu2_grad(d_act, a):
    return d_act * (2.0 * jnp.maximum(a.astype(F32), 0.0))


def _lin_fwd_call(x, w, act, carry=None):
    t, k = x.shape
    _, n = w.shape
    tm, tk = _tile(t, ROW_TILE), _tile(k, 1024)
    return _mm("lin_fwd", x, w, grid=(t // tm, k // tk),
               a_spec=pl.BlockSpec((tm, tk), lambda i, s: (i, s)),
               b_spec=pl.BlockSpec((tk, n), lambda i, s: (s, 0)),
               out_spec=pl.BlockSpec((tm, n), lambda i, s: (i, 0)),
               out_shape=jax.ShapeDtypeStruct((t, n), F32), dims=NN, acc_axis=1, acc_shape=(tm, n), carry=carry,
               a_pre=_relu2 if act else None)


def _lin_dx_call(dy, w, x, act, carry=None):
    k, n = w.shape
    t = dy.shape[0]
    tm, tk = _tile(t, 1024), _tile(k, 1024)
    out_spec = pl.BlockSpec((tm, tk), lambda i, s: (i, s))
    return _mm("lin_dx", dy, w, grid=(t // tm, k // tk),
               a_spec=pl.BlockSpec((tm, n), lambda i, s: (i, 0)),
               b_spec=pl.BlockSpec((tk, n), lambda i, s: (s, 0)),
               out_spec=out_spec, out_shape=jax.ShapeDtypeStruct((t, k), x.dtype), dims=NT, carry=carry,
               post=(x, out_spec, _relu2_grad) if act else None, cache=("a", "inner"))


def _lin_dw_call(x, dy, act, carry=None):
    t, k = x.shape
    n = dy.shape[1]
    tk, tt = _tile(k, 1024), _tile(t, ROW_TILE)
    last = t // tt - 1
    return _mm("lin_dw", x, dy, grid=(k // tk, t // tt), cache=("b", "outer"),
               a_spec=pl.BlockSpec((tt, tk), lambda s, r: (r, s)),
               b_spec=pl.BlockSpec((tt, n), lambda s, r: (jnp.where(s == 0, r, last), 0)),
               out_spec=pl.BlockSpec((tk, n), lambda s, r: (s, 0)),
               out_shape=jax.ShapeDtypeStruct((k, n), BF16), dims=TN, acc_axis=1, acc_shape=(tk, n), carry=carry,
               a_pre=_relu2 if act else None)


def _make_linear(act):
    @jax.custom_vjp
    def linear(x, w):
        return _lin_fwd_call(x, w, act)

    def fwd(x, w):
        return _lin_fwd_call(x, w, act), (x, w)

    def bwd(res, dy):
        x, w = res
        return _lin_dx_call(dy, w, x, act), _lin_dw_call(x, dy, act)

    linear.defvjp(fwd, bwd)

    @jax.custom_vjp
    def linear_next(x, w, nxt):
        return next_fwd(x, w, nxt)[0]

    def next_fwd(x, w, nxt):
        y, g_next = _lin_fwd_call(x, w, act, carry=("gather", nxt.astype(BF16)))
        return (y, g_next), (x, w)

    def next_bwd(res, cts):
        x, w = res
        dy, dg_next = cts
        dx, near = _lin_dx_call(dy, w, x, act, carry=("exchange", dg_next, NEAR))
        dw, far = _lin_dw_call(x, dy, act, carry=("exchange", dg_next, FAR))
        return dx, dw, sum_slots(near, far)

    linear_next.defvjp(next_fwd, next_bwd)
    return linear, linear_next


linear, linear_next = _make_linear(False)
linear_relu2, linear_relu2_next = _make_linear(True)


@jax.custom_vjp
def grouped_mm(x, w):
    return _grouped_fwd(x, w)[0]


def _grouped_fwd(x, w):
    t = x.shape[0]
    p, kin, kout = w.shape
    tm = _tile(t, 1024)
    y = _mm("grp_fwd", x, w, grid=(t // tm, p),
            a_spec=pl.BlockSpec((tm, kin), lambda i, q: (i, q)),
            b_spec=pl.BlockSpec((None, kin, kout), lambda i, q: (q, 0, 0)),
            out_spec=pl.BlockSpec((tm, kout), lambda i, q: (i, q)),
            out_shape=jax.ShapeDtypeStruct((t, p * kout), F32), dims=NN)
    return y, (x, w)


def _grouped_bwd(res, dy):
    x, w = res
    t = x.shape[0]
    p, kin, kout = w.shape
    tm = _tile(t, 1024)
    dx = _mm("grp_dx", dy, w, grid=(t // tm, p),
             a_spec=pl.BlockSpec((tm, kout), lambda i, q: (i, q)),
             b_spec=pl.BlockSpec((None, kin, kout), lambda i, q: (q, 0, 0)),
             out_spec=pl.BlockSpec((tm, kin), lambda i, q: (i, q)),
             out_shape=jax.ShapeDtypeStruct((t, p * kin), F32), dims=NT)
    dw = _mm("grp_dw", x, dy, grid=(p, t // tm),
             a_spec=pl.BlockSpec((tm, kin), lambda q, s: (s, q)),
             b_spec=pl.BlockSpec((tm, kout), lambda q, s: (s, q)),
             out_spec=pl.BlockSpec((None, kin, kout), lambda q, s: (q, 0, 0)),
             out_shape=jax.ShapeDtypeStruct((p, kin, kout), F32), dims=TN, acc_axis=1, acc_shape=(kin, kout))
    return dx, dw


grouped_mm.defvjp(_grouped_fwd, _grouped_bwd)


def _row_ts(widths, seq):
    per_row = 4 * sum(widths)
    ts = 512
    while ts > 8 and ts * per_row * 2 > 24 * 1024 * 1024:
        ts //= 2
    return min(ts, seq)


def make_rowop(name, f, n_row, n_batch, n_vec, f_bwd=None):
    n_in = n_row + n_batch + n_vec

    def in_specs(args, ts):
        specs = []
        for a in args[:n_row]:
            specs.append(pl.BlockSpec((None, ts, a.shape[2]), lambda b, s: (b, s, 0)))
        for a in args[n_row:n_row + n_batch]:
            specs.append(pl.BlockSpec((None, 1, a.shape[2]), lambda b, s: (b, 0, 0)))
        for a in args[n_row + n_batch:]:
            specs.append(pl.BlockSpec((1, a.shape[1]), lambda b, s: (0, 0)))
        return specs

    def out_struct(args, ts):
        blocks = [jax.ShapeDtypeStruct((ts, a.shape[2]), F32) for a in args[:n_row]]
        blocks += [jax.ShapeDtypeStruct((1, a.shape[2]), F32) for a in args[n_row:n_row + n_batch]]
        blocks += [jax.ShapeDtypeStruct((1, a.shape[1]), F32) for a in args[n_row + n_batch:]]
        return jax.eval_shape(f, *blocks)

    def run_fwd(args):
        bsz, seq = args[0].shape[:2]
        outs0 = out_struct(args, 8)
        widths = [a.shape[2] for a in args[:n_row]] + [o.shape[1] for o in outs0]
        ts = _row_ts(widths, seq)

        def body(*refs):
            outs = f(*[r[...] for r in refs[:n_in]])
            for o_ref, o in zip(refs[n_in:], outs):
                o_ref[...] = o

        return pl.pallas_call(
            body, name=name + "_fwd", grid=(bsz, seq // ts), in_specs=in_specs(args, ts),
            out_specs=[pl.BlockSpec((None, ts, o.shape[1]), lambda b, s: (b, s, 0)) for o in outs0],
            out_shape=[jax.ShapeDtypeStruct((bsz, seq, o.shape[1]), F32) for o in outs0],
            compiler_params=_params(("parallel", "parallel")))(*args)

    def run_bwd(args, cts):
        bsz, seq = args[0].shape[:2]
        widths = [a.shape[2] for a in args[:n_row]] * 2 + [c.shape[2] for c in cts]
        ts = _row_ts(widths, seq)
        n_ct = len(cts)

        def body(*refs):
            ins = [r[...] for r in refs[:n_in]]
            ct = [r[...] for r in refs[n_in:n_in + n_ct]]
            if f_bwd is None:
                _, vjp = jax.vjp(f, *ins)
                grads = vjp(tuple(ct))
            else:
                grads = f_bwd(ins, ct)
            g_refs = refs[n_in + n_ct:]
            b, s = pl.program_id(0), pl.program_id(1)
            for i in range(n_row):
                g_refs[i][...] = grads[i]
            for i in range(n_row, n_row + n_batch):
                @pl.when(s == 0)
                def _(i=i):
                    g_refs[i][...] = grads[i]

                @pl.when(s > 0)
                def _(i=i):
                    g_refs[i][...] += grads[i]
            for i in range(n_row + n_batch, n_in):
                first = jnp.logical_and(b == 0, s == 0)

                @pl.when(first)
                def _(i=i):
                    g_refs[i][...] = grads[i]

                @pl.when(jnp.logical_not(first))
                def _(i=i):
                    g_refs[i][...] += grads[i]

        ct_specs = [pl.BlockSpec((None, ts, c.shape[2]), lambda b, s: (b, s, 0)) for c in cts]
        return pl.pallas_call(
            body, name=name + "_bwd", grid=(bsz, seq // ts), in_specs=in_specs(args, ts) + ct_specs,
            out_specs=in_specs(args, ts), out_shape=[jax.ShapeDtypeStruct(a.shape, F32) for a in args],
            compiler_params=_params(("arbitrary", "arbitrary")))(*args, *cts)

    @jax.custom_vjp
    def op(*args):
        return tuple(run_fwd(args))

    def fwd(*args):
        return tuple(run_fwd(args)), args

    def bwd(args, cts):
        return tuple(run_bwd(args, list(cts)))

    op.defvjp(fwd, bwd)
    return op


def _modnorm_f(h, scale, shift, g):
    y = h * lax.rsqrt(jnp.mean(h * h, axis=-1, keepdims=True) + EPS) * g
    return (y * (1.0 + scale) + shift,)


def _gate_res_f(h, y, gate):
    return (h + gate * y,)


def _res_modnorm_f(h, y, gate, scale, shift, g):
    h2 = h + gate * y
    return (h2,) + _modnorm_f(h2, scale, shift, g)


def _relu2_f(a):
    return (jnp.square(jnp.maximum(a, 0.0)),)


def _gelu_skip_f(y, u, d):
    return (jax.nn.gelu(y + d * u),)


def _glu_f(vg):
    n = vg.shape[1] // 2
    return (vg[:, :n] * jax.nn.sigmoid(vg[:, n:]),)


def _glu_b(ins, cts):
    (vg,), (ct,) = ins, cts
    n = vg.shape[1] // 2
    sg = jax.nn.sigmoid(vg[:, n:])
    return (jnp.concatenate([ct * sg, ct * vg[:, :n] * sg * (1.0 - sg)], axis=1),)


def _combine_f(o1, o2, o3, l1, l2, l3):
    m = jnp.maximum(jnp.maximum(l1, l2), l3)
    e1, e2, e3 = jnp.exp(l1 - m), jnp.exp(l2 - m), jnp.exp(l3 - m)
    return ((e1 * o1 + e2 * o2 + e3 * o3) / (e1 + e2 + e3),)


def _final_loss_f(h, target, g):
    y = h * lax.rsqrt(jnp.mean(h * h, axis=-1, keepdims=True) + EPS) * g
    return (0.5 * jnp.mean(jnp.square(y - target), axis=-1, keepdims=True),)


modnorm = make_rowop("modnorm", _modnorm_f, 1, 2, 1)
gate_res = make_rowop("gate_res", _gate_res_f, 2, 1, 0)
res_modnorm = make_rowop("res_modnorm", _res_modnorm_f, 2, 3, 1)
relu2 = make_rowop("relu2", _relu2_f, 1, 0, 0)
gelu_skip = make_rowop("gelu_skip", _gelu_skip_f, 2, 0, 1)
glu = make_rowop("glu", _glu_f, 1, 0, 0, _glu_b)
combine = make_rowop("combine", _combine_f, 6, 0, 0)
final_loss = make_rowop("final_loss", _final_loss_f, 2, 0, 1)


SCAN_TB = 64
HALF = 512


def _scan_fwd_call(bu, a_re, a_im):
    bsz, rows, width = bu.shape
    seq = rows // 8
    tb = _tile(seq, SCAN_TB)

    def body(bu_ref, ar_ref, ai_ref, o_ref, carry):
        @pl.when(pl.program_id(0) == 0)
        def _():
            carry[...] = jnp.zeros_like(carry)

        ar, ai = ar_ref[...], ai_ref[...]

        def step(t, st):
            r0 = pl.multiple_of(t * 8, 8)
            new = []
            for b in range(bsz):
                sre, sim = st[2 * b], st[2 * b + 1]
                nre = ar * sre - ai * sim + bu_ref[b, pl.ds(r0, 8), pl.ds(0, HALF)]
                nim = ar * sim + ai * sre + bu_ref[b, pl.ds(r0, 8), pl.ds(HALF, HALF)]
                o_ref[b, pl.ds(r0, 8), pl.ds(0, HALF)] = nre
                o_ref[b, pl.ds(r0, 8), pl.ds(HALF, HALF)] = nim
                new += [nre, nim]
            return tuple(new)

        fin = lax.fori_loop(0, tb, step, tuple(carry[k] for k in range(2 * bsz)), unroll=4)
        for k in range(2 * bsz):
            carry[k] = fin[k]

    blk = pl.BlockSpec((bsz, tb * 8, width), lambda i: (0, i, 0))
    vec = pl.BlockSpec((8, HALF), lambda i: (0, 0))
    return pl.pallas_call(body, name="s5_scan_fwd", grid=(seq // tb,), in_specs=[blk, vec, vec], out_specs=blk,
                          out_shape=jax.ShapeDtypeStruct(bu.shape, F32),
                          scratch_shapes=[pltpu.VMEM((2 * bsz, 8, HALF), F32)],
                          compiler_params=_params(("arbitrary",)))(bu, a_re, a_im)


def _scan_bwd_call(g, st, a_re, a_im):
    bsz, rows, width = g.shape
    seq = rows // 8
    tb = _tile(seq, SCAN_TB)
    nt = seq // tb

    def body(g_ref, st_ref, prev_ref, ar_ref, ai_ref, db_ref, dar_ref, dai_ref, carry):
        i = pl.program_id(0)

        @pl.when(i == 0)
        def _():
            carry[...] = jnp.zeros_like(carry)
            dar_ref[...] = jnp.zeros_like(dar_ref)
            dai_ref[...] = jnp.zeros_like(dai_ref)

        ar, ai = ar_ref[...], ai_ref[...]

        def lam_step(b, r0, lre, lim):
            nre = g_ref[b, pl.ds(r0, 8), pl.ds(0, HALF)] + ar * lre + ai * lim
            nim = g_ref[b, pl.ds(r0, 8), pl.ds(HALF, HALF)] - ai * lre + ar * lim
            db_ref[b, pl.ds(r0, 8), pl.ds(0, HALF)] = nre
            db_ref[b, pl.ds(r0, 8), pl.ds(HALF, HALF)] = nim
            return nre, nim

        def step(k, c):
            t = tb - 1 - k
            r0 = pl.multiple_of(t * 8, 8)
            p0 = pl.multiple_of(t * 8 - 8, 8)
            lam, dar, dai = list(c[:2 * bsz]), c[2 * bsz], c[2 * bsz + 1]
            for b in range(bsz):
                nre, nim = lam_step(b, r0, lam[2 * b], lam[2 * b + 1])
                pre = st_ref[b, pl.ds(p0, 8), pl.ds(0, HALF)]
                pim = st_ref[b, pl.ds(p0, 8), pl.ds(HALF, HALF)]
                dar = dar + nre * pre + nim * pim
                dai = dai + nim * pre - nre * pim
                lam[2 * b], lam[2 * b + 1] = nre, nim
            return tuple(lam) + (dar, dai)

        zero = jnp.zeros((8, HALF), F32)
        init = tuple(carry[k] for k in range(2 * bsz)) + (zero, zero)
        c = lax.fori_loop(0, tb - 1, step, init, unroll=4)
        lam, dar, dai = list(c[:2 * bsz]), c[2 * bsz], c[2 * bsz + 1]
        keep = jnp.where(i == nt - 1, 0.0, 1.0)
        for b in range(bsz):
            nre, nim = lam_step(b, 0, lam[2 * b], lam[2 * b + 1])
            pre = prev_ref[b, :, pl.ds(0, HALF)] * keep
            pim = prev_ref[b, :, pl.ds(HALF, HALF)] * keep
            dar = dar + nre * pre + nim * pim
            dai = dai + nim * pre - nre * pim
            carry[2 * b], carry[2 * b + 1] = nre, nim
        dar_ref[...] += dar
        dai_ref[...] += dai

    blk = pl.BlockSpec((bsz, tb * 8, width), lambda i: (0, nt - 1 - i, 0))
    prev = pl.BlockSpec((bsz, 8, width), lambda i: (0, jnp.maximum((nt - 1 - i) * tb - 1, 0), 0))
    vec = pl.BlockSpec((8, HALF), lambda i: (0, 0))
    return pl.pallas_call(
        body, name="s5_scan_bwd", grid=(nt,), in_specs=[blk, blk, prev, vec, vec], out_specs=[blk, vec, vec],
        out_shape=[jax.ShapeDtypeStruct(g.shape, F32), jax.ShapeDtypeStruct((8, HALF), F32),
                   jax.ShapeDtypeStruct((8, HALF), F32)],
        scratch_shapes=[pltpu.VMEM((2 * bsz, 8, HALF), F32)],
        compiler_params=_params(("arbitrary",)))(g, st, st, a_re, a_im)


@jax.custom_vjp
def s5_scan(bu, a_re, a_im):
    return _scan_fwd_call(bu, a_re, a_im)


def _s5_scan_fwd(bu, a_re, a_im):
    st = _scan_fwd_call(bu, a_re, a_im)
    return st, (st, a_re, a_im)


def _s5_scan_bwd(res, g):
    st, a_re, a_im = res
    return tuple(_scan_bwd_call(g, st, a_re, a_im))


s5_scan.defvjp(_s5_scan_fwd, _s5_scan_bwd)


SSM_TB = 64
PITCH = SSM_TB + 8
PIECES = 8
CH = 128
NCH = 2 * HALF // LANES
NRE = NCH // 2


def _slab(t):
    return pl.ds(t, PIECES, stride=PITCH)


def _put_rows(ref, b, q, val, tb):
    for j in range(NCH):
        ref[b, j, pl.ds(q * PITCH, tb), :] = val[:, j * LANES:(j + 1) * LANES]


def _get_rows(ref, q, bsz, tb):
    return jnp.concatenate(
        [jnp.concatenate([ref[b, j, pl.ds(q * PITCH, tb), :] for j in range(NCH)], axis=1) for b in range(bsz)], axis=0)


def _chunks(a):
    return [a[:, j * LANES:(j + 1) * LANES] for j in range(a.shape[1] // LANES)]


def _ssm_fwd_call(u, b_mat, c_mat, a_re, a_im):
    bsz, seq, d = u.shape
    tb = SSM_TB
    nt = seq // tb
    assert seq % tb == 0 and d == PIECES * CH

    def body(u_ref, b_ref, c_ref, ar_ref, ai_ref, y_ref, st_ref, end_ref, bu, carry):
        @pl.when(pl.program_id(0) == 0)
        def _():
            carry[...] = jnp.zeros_like(carry)

        u2 = u_ref[...].reshape(bsz * tb, d).astype(BF16)
        for q in range(PIECES):
            r = lax.dot_general(u2[:, q * CH:(q + 1) * CH], b_ref[q], NN, preferred_element_type=F32)
            for b in range(bsz):
                _put_rows(bu, b, q, r[b * tb:(b + 1) * tb], tb)
        ar, ai = _chunks(ar_ref[...]), _chunks(ai_ref[...])
        st = [[carry[b, j] for j in range(NCH)] for b in range(bsz)]
        for t in range(tb):
            for b in range(bsz):
                for j in range(NRE):
                    sre, sim = st[b][j], st[b][NRE + j]
                    nre = ar[j] * sre - ai[j] * sim + bu[b, j, _slab(t), :]
                    nim = ar[j] * sim + ai[j] * sre + bu[b, NRE + j, _slab(t), :]
                    st_ref[b, j, _slab(t), :] = nre
                    st_ref[b, NRE + j, _slab(t), :] = nim
                    st[b][j], st[b][NRE + j] = nre, nim
        for b in range(bsz):
            for j in range(NCH):
                carry[b, j] = st[b][j]
                end_ref[b, j] = st[b][j]
        for q in range(PIECES):
            s2 = _get_rows(st_ref, q, bsz, tb).astype(BF16)
            r = lax.dot_general(s2, c_ref[q], NN, preferred_element_type=F32)
            for b in range(bsz):
                y_ref[b, :, q * CH:(q + 1) * CH] = r[b * tb:(b + 1) * tb]

    rows = pl.BlockSpec((bsz, tb, d), lambda i: (0, i, 0))
    vec = pl.BlockSpec((PIECES, HALF), lambda i: (0, 0))
    return pl.pallas_call(
        body, name="s5_ssm_fwd", grid=(nt,),
        in_specs=[rows, pl.BlockSpec((PIECES, CH, 2 * HALF), lambda i: (0, 0, 0)),
                  pl.BlockSpec((PIECES, 2 * HALF, CH), lambda i: (0, 0, 0)), vec, vec],
        out_specs=[rows, pl.BlockSpec((bsz, None, NCH, PIECES * PITCH, LANES), lambda i: (0, i, 0, 0, 0)),
                   pl.BlockSpec((bsz, None, NCH, PIECES, LANES), lambda i: (0, i, 0, 0, 0))],
        out_shape=[jax.ShapeDtypeStruct((bsz, seq, d), F32),
                   jax.ShapeDtypeStruct((bsz, nt, NCH, PIECES * PITCH, LANES), F32),
                   jax.ShapeDtypeStruct((bsz, nt, NCH, PIECES, LANES), F32)],
        scratch_shapes=[pltpu.VMEM((bsz, NCH, PIECES * PITCH, LANES), F32), pltpu.VMEM((bsz, NCH, PIECES, LANES), F32)],
        compiler_params=_params(("arbitrary",)))(u, b_mat, c_mat, a_re, a_im)


def _ssm_bwd_call(dy, u, st, ends, b_mat, c_mat, a_re, a_im):
    bsz, seq, d = u.shape
    tb = SSM_TB
    nt = seq // tb

    def body(dy_ref, u_ref, st_ref, prev_ref, b_ref, c_ref, ar_ref, ai_ref, du_ref, db_hbm, dc_hbm, dar_ref, dai_ref,
             lam, carry, db_acc, dc_acc):
        i = pl.program_id(0)

        @pl.when(i == 0)
        def _():
            carry[...] = jnp.zeros_like(carry)
            db_acc[...] = jnp.zeros_like(db_acc)
            dc_acc[...] = jnp.zeros_like(dc_acc)
            dar_ref[...] = jnp.zeros_like(dar_ref)
            dai_ref[...] = jnp.zeros_like(dai_ref)

        dy2 = dy_ref[...].reshape(bsz * tb, d).astype(BF16)
        u2 = u_ref[...].reshape(bsz * tb, d).astype(BF16)
        for q in range(PIECES):
            dyq = dy2[:, q * CH:(q + 1) * CH]
            g = lax.dot_general(dyq, c_ref[q], NT, preferred_element_type=F32)
            for b in range(bsz):
                _put_rows(lam, b, q, g[b * tb:(b + 1) * tb], tb)
            s2 = _get_rows(st_ref, q, bsz, tb).astype(BF16)
            dc_acc[q] += lax.dot_general(s2, dyq, TN, preferred_element_type=F32)

        ar, ai = _chunks(ar_ref[...]), _chunks(ai_ref[...])
        keep = jnp.where(i == nt - 1, 0.0, 1.0)
        lm = [[carry[b, j] for j in range(NCH)] for b in range(bsz)]
        dar = [jnp.zeros((PIECES, LANES), F32) for _ in range(NRE)]
        dai = [jnp.zeros((PIECES, LANES), F32) for _ in range(NRE)]
        for t in range(tb - 1, -1, -1):
            for b in range(bsz):
                for j in range(NRE):
                    lre, lim = lm[b][j], lm[b][NRE + j]
                    nre = lam[b, j, _slab(t), :] + ar[j] * lre + ai[j] * lim
                    nim = lam[b, NRE + j, _slab(t), :] - ai[j] * lre + ar[j] * lim
                    lam[b, j, _slab(t), :] = nre
                    lam[b, NRE + j, _slab(t), :] = nim
                    if t > 0:
                        pre, pim = st_ref[b, j, _slab(t - 1), :], st_ref[b, NRE + j, _slab(t - 1), :]
                    else:
                        pre, pim = prev_ref[b, j] * keep, prev_ref[b, NRE + j] * keep
                    dar[j] = dar[j] + nre * pre + nim * pim
                    dai[j] = dai[j] + nim * pre - nre * pim
                    lm[b][j], lm[b][NRE + j] = nre, nim
        for b in range(bsz):
            for j in range(NCH):
                carry[b, j] = lm[b][j]
        dar_ref[...] += jnp.concatenate(dar, axis=1)
        dai_ref[...] += jnp.concatenate(dai, axis=1)

        for q in range(PIECES):
            l2 = _get_rows(lam, q, bsz, tb).astype(BF16)
            r = lax.dot_general(l2, b_ref[q], NT, preferred_element_type=F32)
            for b in range(bsz):
                du_ref[b, :, q * CH:(q + 1) * CH] = r[b * tb:(b + 1) * tb]
            db_acc[q] += lax.dot_general(u2[:, q * CH:(q + 1) * CH], l2, TN, preferred_element_type=F32)

        @pl.when(i == nt - 1)
        def _():
            pltpu.sync_copy(db_acc, db_hbm)
            pltpu.sync_copy(dc_acc, dc_hbm)

    rows = pl.BlockSpec((bsz, tb, d), lambda i: (0, nt - 1 - i, 0))
    vec = pl.BlockSpec((PIECES, HALF), lambda i: (0, 0))
    hbm = pl.BlockSpec(memory_space=pl.ANY)
    return pl.pallas_call(
        body, name="s5_ssm_bwd", grid=(nt,),
        in_specs=[rows, rows,
                  pl.BlockSpec((bsz, None, NCH, PIECES * PITCH, LANES), lambda i: (0, nt - 1 - i, 0, 0, 0)),
                  pl.BlockSpec((bsz, None, NCH, PIECES, LANES), lambda i: (0, jnp.maximum(nt - 2 - i, 0), 0, 0, 0)),
                  pl.BlockSpec((PIECES, CH, 2 * HALF), lambda i: (0, 0, 0)),
                  pl.BlockSpec((PIECES, 2 * HALF, CH), lambda i: (0, 0, 0)), vec, vec],
        out_specs=[rows, hbm, hbm, vec, vec],
        out_shape=[jax.ShapeDtypeStruct((bsz, seq, d), F32), jax.ShapeDtypeStruct((PIECES, CH, 2 * HALF), F32),
                   jax.ShapeDtypeStruct((PIECES, 2 * HALF, CH), F32), jax.ShapeDtypeStruct((PIECES, HALF), F32),
                   jax.ShapeDtypeStruct((PIECES, HALF), F32)],
        scratch_shapes=[pltpu.VMEM((bsz, NCH, PIECES * PITCH, LANES), F32), pltpu.VMEM((bsz, NCH, PIECES, LANES), F32),
                        pltpu.VMEM((PIECES, CH, 2 * HALF), F32), pltpu.VMEM((PIECES, 2 * HALF, CH), F32)],
        compiler_params=_params(("arbitrary",)))(dy, u, st, ends, b_mat, c_mat, a_re, a_im)


@jax.custom_vjp
def s5_ssm(u, b_mat, c_mat, a_re, a_im):
    return _s5_ssm_fwd(u, b_mat, c_mat, a_re, a_im)[0]


def _s5_ssm_fwd(u, b_mat, c_mat, a_re, a_im):
    b16, c16 = b_mat.astype(BF16), c_mat.astype(BF16)
    y, st, ends = _ssm_fwd_call(u, b16, c16, a_re, a_im)
    return y, (u, st, ends, b16, c16, a_re, a_im)


def _s5_ssm_bwd(res, dy):
    return tuple(_ssm_bwd_call(dy, *res))


s5_ssm.defvjp(_s5_ssm_fwd, _s5_ssm_bwd)


ATT_HW = 2 * HEAD
ATT_HB = 1024 // ATT_HW


def _branch_geometry(dil, seq):
    sub = seq // dil
    assert sub % QBLK == 0
    return sub // QBLK


def _drows(dil, r, start, size):
    if dil == 1:
        return pl.ds(start, size)
    return pl.ds(r + start * dil, size, stride=dil)


def _masks():
    qi = lax.broadcasted_iota(jnp.int32, (QBLK, 2 * QBLK), 0)
    kj = lax.broadcasted_iota(jnp.int32, (QBLK, 2 * QBLK), 1) - QBLK
    dist = qi - kj
    band = jnp.logical_and(dist >= 0, dist <= QBLK)
    ci = lax.broadcasted_iota(jnp.int32, (QBLK, QBLK), 0)
    cj = lax.broadcasted_iota(jnp.int32, (QBLK, QBLK), 1)
    return band, ci >= cj


MAX_STRIDE = 4


class _Rows:
    def __init__(self, ref, dil, stage=None):
        self.ref, self.dil, self.stage = ref, dil, stage

    def _at(self, r, start, size):
        if self.stage is None:
            return self.ref, _drows(self.dil, r, start, size)
        step = self.dil // MAX_STRIDE
        return self.stage.at[r % MAX_STRIDE], pl.ds(r // MAX_STRIDE + start * step, size, stride=step)

    def get(self, r, start, size):
        ref, rows = self._at(r, start, size)
        return ref[rows, :]

    def put(self, r, start, size, val):
        ref, rows = self._at(r, start, size)
        ref[rows, :] = val

    def _parts(self):
        n = self.ref.shape[0] // MAX_STRIDE
        return [(a, pl.ds(a, n, stride=MAX_STRIDE)) for a in range(MAX_STRIDE)]

    def fill(self):
        if self.stage is not None:
            for a, rows in self._parts():
                self.stage[a] = self.ref[rows, :]

    def flush(self):
        if self.stage is not None:
            for a, rows in self._parts():
                self.ref[rows, :] = self.stage[a]


def _stage_shapes(dil, seq, n):
    return [pltpu.VMEM((MAX_STRIDE, seq // MAX_STRIDE, ATT_HW), F32)] * n if dil > MAX_STRIDE else []


def _head_lanes():
    lane = lax.broadcasted_iota(jnp.int32, (QBLK, ATT_HW), 1)
    return [jnp.logical_and(lane >= h * HEAD, lane < (h + 1) * HEAD) for h in range(ATT_HW // HEAD)]


def _attn_specs(i, seq):
    q_spec = pl.BlockSpec((None, seq, ATT_HW), lambda b, h: (b, 0, i * ATT_HB + h))
    k_spec = pl.BlockSpec((None, seq, ATT_HW), lambda b, h: (b, 0, i * ATT_HB + h))
    v_spec = pl.BlockSpec((None, seq, ATT_HW), lambda b, h: (b, 0, (3 + i) * ATT_HB + h))
    o_spec = pl.BlockSpec((None, seq, ATT_HW), lambda b, h: (b, 0, h))
    return q_spec, k_spec, v_spec, o_spec


def _attn_fwd_call(q_all, kv_all, i, dil):
    bsz, seq, _ = q_all.shape
    nb = _branch_geometry(dil, seq)
    scale = HEAD ** -0.5

    def body(*refs):
        stages = refs[5:] if dil > MAX_STRIDE else (None,) * 5
        q_rows, k_rows, v_rows, o_rows, l_rows = (_Rows(ref, dil, st) for ref, st in zip(refs[:5], stages))
        band, causal = _masks()
        head_lanes = _head_lanes()
        for rows in (q_rows, k_rows, v_rows):
            rows.fill()

        def one(r, n, mask):
            k0, kn = (0, QBLK) if n == 0 else ((n - 1) * QBLK, 2 * QBLK)
            q2 = q_rows.get(r, n * QBLK, QBLK).astype(BF16)
            k2 = k_rows.get(r, k0, kn).astype(BF16)
            v2 = v_rows.get(r, k0, kn).astype(BF16)
            o, lse = None, None
            for h in range(ATT_HW // HEAD):
                s = lax.dot_general(jnp.where(head_lanes[h], q2, 0), k2, NT, preferred_element_type=F32) * scale
                s = jnp.where(mask, s, NEG)
                m = jnp.max(s, axis=-1, keepdims=True)
                p = jnp.exp(s - m)
                den = jnp.sum(p, axis=-1, keepdims=True)
                o_h = lax.dot_general(p.astype(BF16), v2, NN, preferred_element_type=F32) / den
                lse_h = jnp.broadcast_to(m + jnp.log(den), (QBLK, ATT_HW))
                o = o_h if o is None else jnp.where(head_lanes[h], o_h, o)
                lse = lse_h if lse is None else jnp.where(head_lanes[h], lse_h, lse)
            o_rows.put(r, n * QBLK, QBLK, o)
            l_rows.put(r, n * QBLK, QBLK, lse)

        for r in range(dil):
            for n in range(nb):
                one(r, n, causal if n == 0 else band)
        o_rows.flush()
        l_rows.flush()

    q_spec, k_spec, v_spec, o_spec = _attn_specs(i, seq)
    shape = jax.ShapeDtypeStruct((bsz, seq, 1024), F32)
    return pl.pallas_call(
        body, name="attn_fwd_d%d" % dil, grid=(bsz, ATT_HB), in_specs=[q_spec, k_spec, v_spec],
        out_specs=[o_spec, o_spec], out_shape=[shape, shape], scratch_shapes=_stage_shapes(dil, seq, 5),
        compiler_params=_params(("parallel", "parallel")))(q_all, kv_all, kv_all)


def _attn_bwd_call(q_all, kv_all, o, l, do, dl, i, dil, grads):
    bsz, seq, _ = q_all.shape
    nb = _branch_geometry(dil, seq)
    scale = HEAD ** -0.5
    first = grads is None

    def body(*refs):
        n_pass = 0 if first else 3
        blocks = list(refs[:7]) + list(refs[7 + n_pass:10 + n_pass])
        stages = refs[10 + n_pass:] if dil > MAX_STRIDE else (None,) * 10
        q_rows, k_rows, v_rows, o_rows, l_rows, do_rows, dl_rows, dq_rows, dk_rows, dv_rows = (
            _Rows(ref, dil, st) for ref, st in zip(blocks, stages))
        band, causal = _masks()
        head_lanes = _head_lanes()
        for rows in (q_rows, k_rows, v_rows, o_rows, l_rows, do_rows, dl_rows):
            rows.fill()

        def one(r, n, mask):
            q0 = n * QBLK
            k0, kn = (0, QBLK) if n == 0 else ((n - 1) * QBLK, 2 * QBLK)
            q2 = q_rows.get(r, q0, QBLK).astype(BF16)
            k2 = k_rows.get(r, k0, kn).astype(BF16)
            v2 = v_rows.get(r, k0, kn).astype(BF16)
            l2, do2 = l_rows.get(r, q0, QBLK), do_rows.get(r, q0, QBLK)
            t2 = dl_rows.get(r, q0, QBLK) - do2 * o_rows.get(r, q0, QBLK)
            do2 = do2.astype(BF16)
            dq, dk, dv = None, None, None
            for h in range(ATT_HW // HEAD):
                q = jnp.where(head_lanes[h], q2, 0)
                d_ob = jnp.where(head_lanes[h], do2, 0)
                s = lax.dot_general(q, k2, NT, preferred_element_type=F32) * scale
                s = jnp.where(mask, s, NEG)
                lse = jnp.max(jnp.where(head_lanes[h], l2, NEG), axis=-1, keepdims=True)
                p = jnp.exp(s - lse)
                row = jnp.sum(jnp.where(head_lanes[h], t2, 0.0), axis=-1, keepdims=True)
                dp = lax.dot_general(d_ob, v2, NT, preferred_element_type=F32)
                ds = (p * (dp + row)).astype(BF16)
                dq_h = lax.dot_general(ds, k2, NN, preferred_element_type=F32) * scale
                dk_h = lax.dot_general(ds, q, TN, preferred_element_type=F32) * scale
                dv_h = lax.dot_general(p.astype(BF16), d_ob, TN, preferred_element_type=F32)
                dq = dq_h if dq is None else jnp.where(head_lanes[h], dq_h, dq)
                dk = dk_h if dk is None else dk + dk_h
                dv = dv_h if dv is None else dv + dv_h
            dq_rows.put(r, q0, QBLK, dq)
            return dk, dv

        for r in range(dil):
            dk_cur, dv_cur = one(r, 0, causal)
            for n in range(1, nb):
                dk, dv = one(r, n, band)
                dk_rows.put(r, (n - 1) * QBLK, QBLK, dk_cur + dk[:QBLK])
                dv_rows.put(r, (n - 1) * QBLK, QBLK, dv_cur + dv[:QBLK])
                dk_cur, dv_cur = dk[QBLK:], dv[QBLK:]
            dk_rows.put(r, (nb - 1) * QBLK, QBLK, dk_cur)
            dv_rows.put(r, (nb - 1) * QBLK, QBLK, dv_cur)
        for rows in (dq_rows, dk_rows, dv_rows):
            rows.flush()

    q_spec, k_spec, v_spec, o_spec = _attn_specs(i, seq)
    in_specs = [q_spec, k_spec, v_spec, o_spec, o_spec, o_spec, o_spec]
    args = [q_all, kv_all, kv_all, o, l, do, dl]
    aliases = {}
    if not first:
        in_specs += [pl.BlockSpec(memory_space=pl.ANY)] * 3
        args += list(grads)
        aliases = {7: 0, 8: 1, 9: 2}
    shape = jax.ShapeDtypeStruct((bsz, seq, 3 * 1024), F32)
    return pl.pallas_call(
        body, name="attn_bwd_d%d" % dil, grid=(bsz, ATT_HB), in_specs=in_specs,
        out_specs=[q_spec, q_spec, q_spec], out_shape=[shape, shape, shape], input_output_aliases=aliases,
        scratch_shapes=_stage_shapes(dil, seq, 10), compiler_params=_params(("parallel", "parallel")))(*args)


@jax.custom_vjp
def attn_branches(q_all, kv_all):
    return _attn_branches_fwd(q_all, kv_all)[0]


def _attn_branches_fwd(q_all, kv_all):
    outs = []
    for i, dil in enumerate(BRANCH_DIL):
        outs += list(_attn_fwd_call(q_all, kv_all, i, dil))
    return tuple(outs), (q_all, kv_all, tuple(outs))


def _attn_branches_bwd(res, cts):
    q_all, kv_all, outs = res
    grads = None
    for i, dil in enumerate(BRANCH_DIL):
        grads = _attn_bwd_call(q_all, kv_all, outs[2 * i], outs[2 * i + 1], cts[2 * i], cts[2 * i + 1], i, dil, grads)
    dq, dk, dv = grads
    return dq, jnp.concatenate([dk, dv], axis=-1)


attn_branches.defvjp(_attn_branches_fwd, _attn_branches_bwd)


def ada_fwd(c_all, w, b):
    n, d, cs = w.shape
    nb = c_all.shape[0]

    def body(c_ref, w_ref, b_ref, o_ref):
        a = jax.nn.silu(c_ref[...]).astype(BF16)
        o_ref[...] = lax.dot_general(a, w_ref[...].astype(BF16), NN, preferred_element_type=F32) + b_ref[...]

    return pl.pallas_call(
        body, name="ada_fwd", grid=(n,),
        in_specs=[pl.BlockSpec((nb, d), lambda i: (0, 0)), pl.BlockSpec((None, d, cs), lambda i: (i, 0, 0)),
                  pl.BlockSpec((None, 1, cs), lambda i: (i, 0, 0))],
        out_specs=pl.BlockSpec((None, nb, cs), lambda i: (i, 0, 0)),
        out_shape=jax.ShapeDtypeStruct((n, nb, cs), F32), compiler_params=_params(("parallel",)))(c_all, w, b)


def ada_bwd(c_all, dm):
    n, nb, cs = dm.shape
    d = c_all.shape[1]

    def body(c_ref, dm_ref, dw_ref, db_ref):
        a = jax.nn.silu(c_ref[...]).astype(BF16)
        g = dm_ref[...]
        dw_ref[...] = lax.dot_general(a, g.astype(BF16), TN, preferred_element_type=F32)
        db_ref[...] = jnp.sum(g, axis=0, keepdims=True)

    return pl.pallas_call(
        body, name="ada_bwd", grid=(n,),
        in_specs=[pl.BlockSpec((nb, d), lambda i: (0, 0)), pl.BlockSpec((None, nb, cs), lambda i: (i, 0, 0))],
        out_specs=[pl.BlockSpec((None, d, cs), lambda i: (i, 0, 0)), pl.BlockSpec((None, 1, cs), lambda i: (i, 0, 0))],
        out_shape=[jax.ShapeDtypeStruct((n, d, cs), F32), jax.ShapeDtypeStruct((n, 1, cs), F32)],
        compiler_params=_params(("parallel",)))(c_all, dm)


def all_gather(x, name):
    m, n = x.shape

    def body(x_ref, out_ref, send_sems, recv_sems, local_sem):
        px, py, pc = lax.axis_index("x"), lax.axis_index("y"), lax.axis_index("c")
        me, sibling = (px, py, pc), (px, py, 1 - pc)
        chips = [(1 - px, py), (px, 1 - py), (1 - px, 1 - py)]

        def rows(qx, qy, qc):
            return out_ref.at[pl.ds((4 * qx + 2 * qy + qc) * m, m), :]

        def copy(k, block, to, src=None):
            return pltpu.make_async_remote_copy(
                src_ref=rows(*block) if src is None else src, dst_ref=rows(*block),
                send_sem=send_sems.at[k], recv_sem=recv_sems.at[k], device_id=to,
                device_id_type=pl.DeviceIdType.MESH)

        mine = pltpu.make_async_copy(x_ref, rows(*me), local_sem)
        mine.start()
        first = [copy(0, me, sibling, src=x_ref)]
        first += [copy(1 + j, me, (*chip, pc), src=x_ref) for j, chip in enumerate(chips)]
        for cp in first:
            cp.start()
        passed = [copy(4 + j, (*chip, pc), sibling) for j, chip in enumerate(chips)]
        for j, chip in enumerate(chips):
            copy(1 + j, (*chip, pc), me).wait_recv()
            passed[j].start()
        copy(0, sibling, me).wait_recv()
        for j, chip in enumerate(chips):
            copy(4 + j, (*chip, 1 - pc), me).wait_recv()
        for cp in first + passed:
            cp.wait_send()
        mine.wait()

    out = pl.pallas_call(
        body, name=name, out_shape=jax.ShapeDtypeStruct((N_DEV * m, n), x.dtype),
        in_specs=[pl.BlockSpec(memory_space=pl.ANY)], out_specs=pl.BlockSpec(memory_space=pl.ANY),
        scratch_shapes=[pltpu.SemaphoreType.DMA((7,)), pltpu.SemaphoreType.DMA((7,)), pltpu.SemaphoreType.DMA(())],
    )(x)
    return out.reshape(N_DEV, m, n)


def exchange_partials(p, name):
    _, m, n = p.shape

    def body(p_ref, out_ref, send_sems, recv_sems, local_sem):
        px, py, pc = lax.axis_index("x"), lax.axis_index("y"), lax.axis_index("c")
        me = 4 * px + 2 * py + pc
        mine = pltpu.make_async_copy(p_ref.at[me], out_ref.at[me], local_sem)
        mine.start()
        copies = []
        for k in range(1, N_DEV):
            qx = 1 - px if k & 4 else px
            qy = 1 - py if k & 2 else py
            qc = 1 - pc if k & 1 else pc
            cp = pltpu.make_async_remote_copy(
                src_ref=p_ref.at[4 * qx + 2 * qy + qc], dst_ref=out_ref.at[me],
                send_sem=send_sems.at[k - 1], recv_sem=recv_sems.at[k - 1], device_id=(qx, qy, qc),
                device_id_type=pl.DeviceIdType.MESH)
            cp.start()
            copies.append(cp)
        for cp in copies:
            cp.wait()
        mine.wait()

    return pl.pallas_call(
        body, name=name, out_shape=jax.ShapeDtypeStruct(p.shape, p.dtype),
        in_specs=[pl.BlockSpec(memory_space=pl.ANY)], out_specs=pl.BlockSpec(memory_space=pl.ANY),
        scratch_shapes=[pltpu.SemaphoreType.DMA((7,)), pltpu.SemaphoreType.DMA((7,)), pltpu.SemaphoreType.DMA(())],
    )(p)


def sum_slots(*parts):
    _, r, n = parts[0].shape
    tr = _tile(r, max(16, (256 * 1024) // n))

    def body(*refs):
        acc = None
        for g_ref in refs[:-1]:
            for s in range(g_ref.shape[0]):
                term = g_ref[s].astype(F32)
                acc = term if acc is None else acc + term
        refs[-1][...] = acc

    return pl.pallas_call(
        body, name="sum_slots", grid=(r // tr,),
        in_specs=[pl.BlockSpec((p.shape[0], tr, n), lambda i: (0, i, 0)) for p in parts],
        out_specs=pl.BlockSpec((tr, n), lambda i: (i, 0)), out_shape=jax.ShapeDtypeStruct((r, n), F32),
        compiler_params=_params(("parallel",)))(*parts)


def adamw(w, m, v, g, name):
    nl, r, n = w.shape
    partials = g.ndim == 4
    tr = _tile(r, max(8, (256 * 1024) // n))
    c1 = 1.0 / (1.0 - ADAM_B1 ** ADAM_STEP)
    c2 = 1.0 / (1.0 - ADAM_B2 ** ADAM_STEP)

    def body(w_ref, m_ref, v_ref, g_ref, go_ref, d_ref, mo_ref, vo_ref):
        if partials:
            grad = g_ref[0].astype(F32)
            for s in range(1, N_DEV):
                grad = grad + g_ref[s].astype(F32)
        else:
            grad = g_ref[...]
        m_new = ADAM_B1 * m_ref[...] + (1.0 - ADAM_B1) * grad
        v_new = ADAM_B2 * v_ref[...] + (1.0 - ADAM_B2) * grad * grad
        go_ref[...] = grad
        mo_ref[...] = m_new
        vo_ref[...] = v_new
        d_ref[...] = -ADAM_LR * ((m_new * c1) / (jnp.sqrt(v_new * c2) + ADAM_EPS) + ADAM_WD * w_ref[...])

    spec = pl.BlockSpec((None, tr, n), lambda l, i: (l, i, 0))
    g_spec = pl.BlockSpec((None, N_DEV, tr, n), lambda l, i: (l, 0, i, 0)) if partials else spec
    shape = jax.ShapeDtypeStruct(w.shape, F32)
    return pl.pallas_call(
        body, name=name, grid=(nl, r // tr), in_specs=[spec, spec, spec, g_spec], out_specs=[spec] * 4,
        out_shape=[shape] * 4, compiler_params=_params(("parallel", "parallel")))(w, m, v, g)


def pack(arrays):
    flat = jnp.concatenate([a.reshape(-1).astype(F32) for a in arrays])
    rows = -(-flat.shape[0] // LANES)
    mult = 8 if rows <= 512 else 512
    rows = -(-rows // mult) * mult
    flat = jnp.pad(flat, (0, rows * LANES - flat.shape[0]))
    return flat.reshape(rows, LANES)


def unpack(slab, shapes):
    flat = slab.reshape(slab.shape[:-2] + (-1,))
    out, off = [], 0
    for s in shapes:
        size = math.prod(s)
        out.append(flat[..., off:off + size].reshape(flat.shape[:-1] + tuple(s)))
        off += size
    return out


def _s5_discretize(lam_re, lam_im, log_dt, b_re, b_im, c_re, c_im):
    dt = jnp.exp(log_dt)[:, None]
    xr, th = lam_re * dt, lam_im * dt
    er = jnp.exp(xr)
    a_re, a_im = er * jnp.cos(th), er * jnp.sin(th)
    am1 = jnp.expm1(xr) * jnp.cos(th) - 2.0 * jnp.square(jnp.sin(0.5 * th))
    den = lam_re * lam_re + lam_im * lam_im
    fr = (am1 * lam_re + a_im * lam_im) / den
    fi = (a_im * lam_re - am1 * lam_im) / den
    bb_re = fr[..., None] * b_re - fi[..., None] * b_im
    bb_im = fr[..., None] * b_im + fi[..., None] * b_re
    eye = jnp.eye(8, dtype=F32)

    def pack_b(bb):
        return jnp.einsum('qgpc,gh->qgchp', bb.reshape(8, 8, 64, 16), eye).reshape(8, 128, 512)

    def pack_c(cc):
        return jnp.einsum('qgcp,gh->qgphc', cc.reshape(8, 8, 16, 64), eye).reshape(8, 512, 128)

    b_mat = jnp.concatenate([pack_b(bb_re), pack_b(bb_im)], axis=-1)
    c_mat = jnp.concatenate([pack_c(c_re), pack_c(-c_im)], axis=1)
    return a_re.reshape(8, HALF), a_im.reshape(8, HALF), b_mat, c_mat


WEIGHT_ORDER = (("ssm_w_glu", 0, True), ("mlp_w1", 0, True), ("mlp_w2", 0, False),
                ("ssm_w_glu", 1, True), ("mlp_w1", 1, True), ("mlp_w2", 1, False), ("w_kv", None, True),
                ("attn_w_q", 0, True), ("attn_w_o", 0, False), ("mlp_w1", 2, True), ("mlp_w2", 2, False),
                ("attn_w_q", 1, True), ("attn_w_o", 1, False), ("mlp_w1", 3, True), ("mlp_w2", 3, False))


class _WeightChain:
    def __init__(self, first, upcoming):
        self.cur, self.upcoming, self.used = first, list(upcoming), 0

    def matmul(self, x, name, relu2_input=False, bf16_output=False):
        assert WEIGHT_ORDER[self.used][0] == name
        col = WEIGHT_ORDER[self.used][2]
        assert not (col and relu2_input) and (col or not bf16_output)
        self.used += 1
        w = self.cur if col else self.cur.reshape(-1, self.cur.shape[-1])
        if col:
            plain, with_next = (linear_col_bf16, linear_col_bf16_next) if bf16_output else (linear_col, linear_col_next)
        else:
            plain, with_next = (linear_relu2, linear_relu2_next) if relu2_input else (linear, linear_next)
        if not self.upcoming:
            return plain(x, w)
        y, self.cur = with_next(x, w, self.upcoming.pop(0))
        return y


def _local_loss(diff, x, target, first_weight, shards):
    bsz, seq, d = x.shape
    t = bsz * seq
    rows3 = lambda a: a.reshape(bsz, seq, a.shape[-1])
    rows2 = lambda a: a.reshape(t, a.shape[-1])
    mods, kvmod, ln_g, ssm_d, kv_g, final_g = (diff[k] for k in ("mods", "kvmod", "ln_g", "ssm_d", "kv_g", "final_g"))
    chain = _WeightChain(first_weight, shards)

    def chunks(sub):
        mod = mods[sub]
        return mod[:, None, :d], mod[:, None, d:2 * d], mod[:, None, 2 * d:], ln_g[sub // 2, sub % 2][None]

    h = x
    kv_all = None
    shift, scale, gate, gain = chunks(0)
    (u,) = modnorm(h, scale, shift, gain)
    for layer in range(4):
        if layer == 2:
            (ukv,) = modnorm(h, kvmod[:, None, d:], kvmod[:, None, :d], kv_g[None])
            kv_all = rows3(chain.matmul(rows2(ukv), "w_kv"))
        if layer < 2:
            a_re, a_im, b_mat, c_mat = _s5_discretize(*(diff[k][layer] for k in (
                "ssm_lam_re", "ssm_lam_im", "ssm_log_dt", "ssm_b_re", "ssm_b_im", "ssm_c_re", "ssm_c_im")))
            y0 = s5_ssm(u, b_mat, c_mat, a_re, a_im)
            (z,) = gelu_skip(y0, u, ssm_d[layer][None])
            (y,) = glu(rows3(chain.matmul(rows2(z), "ssm_w_glu")))
        else:
            q_all = rows3(chain.matmul(rows2(u), "attn_w_q"))
            o1, l1, o2, l2, o3, l3 = attn_branches(q_all, kv_all)
            (o,) = combine(o1, o2, o3, l1, l2, l3)
            y = rows3(chain.matmul(rows2(o), "attn_w_o"))
        shift, scale, next_gate, gain = chunks(2 * layer + 1)
        h, u = res_modnorm(h, y, gate, scale, shift, gain)
        gate = next_gate
        pre = chain.matmul(rows2(u), "mlp_w1", bf16_output=True)
        y = rows3(chain.matmul(pre, "mlp_w2", relu2_input=True))
        if layer < 3:
            shift, scale, next_gate, gain = chunks(2 * layer + 2)
            h, u = res_modnorm(h, y, gate, scale, shift, gain)
            gate = next_gate
        else:
            (h,) = gate_res(h, y, gate)
    (row_loss,) = final_loss(h, target, final_g[None])
    return jnp.sum(row_loss)


SSM_NAMES = ("ssm_lam_re", "ssm_lam_im", "ssm_log_dt", "ssm_b_re", "ssm_b_im", "ssm_c_re", "ssm_c_im")
ARG_NAMES = ("x", "c", "ln_g", "ada_w", "ada_b") + SSM_NAMES + (
    "ssm_d", "ssm_w_glu", "kv_g", "kv_ada_w", "kv_ada_b", "w_kv", "attn_w_q", "attn_w_o", "mlp_w1", "mlp_w2", "final_g")
WEIGHT_NAMES = ARG_NAMES[2:]


def kernel(x, c, ln_g, ada_w, ada_b, ssm_lam_re, ssm_lam_im, ssm_log_dt, ssm_b_re, ssm_b_im, ssm_c_re, ssm_c_im, ssm_d, ssm_w_glu, kv_g, kv_ada_w, kv_ada_b, w_kv, attn_w_q, attn_w_o, mlp_w1, mlp_w2, final_g, loss_target, m_ln_g, m_ada_w, m_ada_b, m_ssm_lam_re, m_ssm_lam_im, m_ssm_log_dt, m_ssm_b_re, m_ssm_b_im, m_ssm_c_re, m_ssm_c_im, m_ssm_d, m_ssm_w_glu, m_kv_g, m_kv_ada_w, m_kv_ada_b, m_w_kv, m_attn_w_q, m_attn_w_o, m_mlp_w1, m_mlp_w2, m_final_g, v_ln_g, v_ada_w, v_ada_b, v_ssm_lam_re, v_ssm_lam_im, v_ssm_log_dt, v_ssm_b_re, v_ssm_b_im, v_ssm_c_re, v_ssm_c_im, v_ssm_d, v_ssm_w_glu, v_kv_g, v_kv_ada_w, v_kv_ada_b, v_w_kv, v_attn_w_q, v_attn_w_o, v_mlp_w1, v_mlp_w2, v_final_g):
    args = locals()
    w = {n: args[n] for n in WEIGHT_NAMES}
    mom = {n: args["m_" + n] for n in WEIGHT_NAMES}
    var = {n: args["v_" + n] for n in WEIGHT_NAMES}
    bsz, seq, d = x.shape
    me = 4 * lax.axis_index("x") + 2 * lax.axis_index("y") + lax.axis_index("c")

    small = all_gather(pack([c, ln_g, ssm_d]), "gather_small")
    c_parts, ln_parts, d_parts = unpack(small, [c.shape, ln_g.shape, ssm_d.shape])
    c_all = c_parts.reshape(N_DEV * bsz, d)
    ln_full = jnp.moveaxis(ln_parts, 0, 2).reshape(4, 2, d)
    d_full = jnp.moveaxis(d_parts, 0, 1).reshape(2, d)

    cs = ada_w.shape[-1]
    mod_cols = ada_fwd(c_all, ada_w.reshape(8, d, cs), ada_b.reshape(8, 1, cs))
    kcs = kv_ada_w.shape[-1]
    kv_cols = ada_fwd(c_all, kv_ada_w[None], jnp.zeros((1, 1, kcs), F32))
    mod_g = all_gather(mod_cols.reshape(8 * N_DEV * bsz, cs), "gather_mod")
    kv_g_all = all_gather(kv_cols.reshape(N_DEV * bsz, kcs), "gather_kvmod")
    mods_all = jnp.moveaxis(mod_g.reshape(N_DEV, 8, N_DEV * bsz, cs), 0, 2).reshape(8, N_DEV * bsz, N_DEV * cs)
    kvmod_all = jnp.moveaxis(kv_g_all, 0, 1).reshape(N_DEV * bsz, N_DEV * kcs) + kv_ada_b[None]
    mods = lax.dynamic_slice_in_dim(mods_all, me * bsz, bsz, axis=1)
    kvmod = lax.dynamic_slice_in_dim(kvmod_all, me * bsz, bsz, axis=0)

    shards = [w[n] if l is None else w[n][l] for n, l, _ in WEIGHT_ORDER]
    first_weight = all_gather(shards[0].astype(BF16), "gather_first_weight")

    diff = {"mods": mods, "kvmod": kvmod, "ln_g": ln_full, "ssm_d": d_full, "kv_g": kv_g, "final_g": final_g}
    diff.update({n: w[n] for n in SSM_NAMES})
    loss_local, (g_diff, grad_x, g_first, g_shards) = jax.value_and_grad(_local_loss, argnums=(0, 1, 3, 4))(
        diff, x, loss_target, first_weight, shards[1:])
    loss = lax.psum(loss_local, AXES)
    g_shards = [sum_slots(exchange_partials(g_first, "exchange_first_weight"))] + list(g_shards)

    dmod = all_gather(pack([g_diff["mods"], g_diff["kvmod"]]), "gather_dmod")
    dmods_p, dkv_p = unpack(dmod, [g_diff["mods"].shape, g_diff["kvmod"].shape])
    dmods_all = jnp.moveaxis(dmods_p, 0, 1).reshape(8, N_DEV * bsz, 3 * d)
    dkv_all = dkv_p.reshape(N_DEV * bsz, 2 * d)
    g_ada_w, g_ada_b = ada_bwd(c_all, lax.dynamic_slice_in_dim(dmods_all, me * cs, cs, axis=2))
    g_kv_ada_w, _ = ada_bwd(c_all, lax.dynamic_slice_in_dim(dkv_all, me * kcs, kcs, axis=1)[None])
    _, g_kv_ada_b = ada_bwd(c_all, dkv_all[None])

    small_names = ("ln_g", "ssm_d", "kv_g", "final_g") + SSM_NAMES
    partial = all_gather(pack([g_diff[n] for n in small_names]), "gather_small_grads")
    totals = unpack(sum_slots(partial), [g_diff[n].shape for n in small_names])
    g_small = dict(zip(small_names, totals))
    g_small["ln_g"] = lax.dynamic_slice_in_dim(g_small["ln_g"], me * ln_g.shape[-1], ln_g.shape[-1], axis=2)
    g_small["ssm_d"] = lax.dynamic_slice_in_dim(g_small["ssm_d"], me * ssm_d.shape[-1], ssm_d.shape[-1], axis=1)
    g_small["ada_b"] = g_ada_b.reshape(ada_b.shape)
    g_small["kv_ada_b"] = g_kv_ada_b.reshape(kv_ada_b.shape)

    out = {}

    def put(name, res, shape):
        for kind, a in zip(("grad_", "delta_", "new_m_", "new_v_"), res):
            out[kind + name] = a.reshape(shape)

    for name in ("ssm_w_glu", "w_kv", "attn_w_q", "attn_w_o", "mlp_w1", "mlp_w2"):
        grads = jnp.stack([g for g, (n, _, _) in zip(g_shards, WEIGHT_ORDER) if n == name])
        v3 = lambda a: a.reshape(grads.shape)
        put(name, adamw(v3(w[name]), v3(mom[name]), v3(var[name]), grads, "adamw_" + name), w[name].shape)
    v3 = lambda a: a.reshape(8, d, cs)
    put("ada_w", adamw(v3(ada_w), v3(m_ada_w), v3(v_ada_w), g_ada_w, "adamw_ada_w"), ada_w.shape)
    put("kv_ada_w", adamw(kv_ada_w[None], m_kv_ada_w[None], v_kv_ada_w[None], g_kv_ada_w, "adamw_kv_ada_w"), kv_ada_w.shape)
    names = small_names + ("ada_b", "kv_ada_b")
    res = adamw(pack([w[n] for n in names])[None], pack([mom[n] for n in names])[None],
                pack([var[n] for n in names])[None], pack([g_small[n] for n in names])[None], "adamw_small")
    for kind, slab in zip(("grad_", "delta_", "new_m_", "new_v_"), res):
        for n, a in zip(names, unpack(slab[0], [w[n].shape for n in names])):
            out[kind + n] = a

    result = [loss, grad_x]
    for kind in ("grad_", "delta_", "new_m_", "new_v_"):
        result += [out[kind + n] for n in WEIGHT_NAMES]
    return tuple(result)
```

```python
import functools
import math

import jax
import jax.numpy as jnp
from jax import lax
from jax.experimental import pallas as pl
from jax.experimental.pallas import tpu as pltpu

F32 = jnp.float32
BF16 = jnp.bfloat16
N_DEV = 8
AXES = ("x", "y", "c")
V7X_VMEM_LIMIT = 56 * 1024 * 1024
LANES = 128
EPS = 1e-6
NEG = -1e30
HEAD = 64
QBLK = 128
BRANCH_DIL = (1, 4, 16)
ADAM_LR, ADAM_B1, ADAM_B2, ADAM_EPS, ADAM_WD, ADAM_STEP = 0.001, 0.9, 0.999, 1e-08, 0.01, 10

NN = (((1,), (0,)), ((), ()))
NT = (((1,), (1,)), ((), ()))
TN = (((0,), (0,)), ((), ()))


def _params(sem):
    return pltpu.CompilerParams(dimension_semantics=sem, vmem_limit_bytes=V7X_VMEM_LIMIT)


def _peer(k):
    px, py, pc = lax.axis_index("x"), lax.axis_index("y"), lax.axis_index("c")
    qx = 1 - px if k & 4 else px
    qy = 1 - py if k & 2 else py
    qc = 1 - pc if k & 1 else pc
    return (qx, qy, qc), 4 * qx + 2 * qy + qc


ROW_TILE = 2048
NEAR, FAR = (0, 1, 2, 3, 6), (4, 5, 7)


def _exchange_copies(ks, src_ref, land_ref, send_sems, recv_sems):
    copies = []
    for i, k in enumerate(ks):
        peer, slot = _peer(k)
        if k == 0:
            copies.append(pltpu.make_async_copy(src_ref.at[slot], land_ref.at[i], send_sems.at[i]))
        else:
            copies.append(pltpu.make_async_remote_copy(
                src_ref=src_ref.at[slot], dst_ref=land_ref.at[i], send_sem=send_sems.at[i], recv_sem=recv_sems.at[i],
                device_id=peer, device_id_type=pl.DeviceIdType.MESH))
    return copies


def _gather_copies(src_ref, land_ref, send_sems, recv_sems):
    _, me = _peer(0)

    def copy(pair, slot, k, src=None):
        peer, _ = _peer(k)
        return pltpu.make_async_remote_copy(
            src_ref=land_ref.at[slot] if src is None else src, dst_ref=land_ref.at[slot],
            send_sem=send_sems.at[pair], recv_sem=recv_sems.at[pair], device_id=peer,
            device_id_type=pl.DeviceIdType.MESH)

    direct = [copy(0, me, 1, src_ref)] + [copy(1 + j, me, k, src_ref) for j, k in enumerate((2, 4, 6))]
    forwards = [copy(4 + j, _peer(k)[1], 1) for j, k in enumerate((2, 4, 6))]
    from_sibling = [copy(0, _peer(1)[1], 1)] + [copy(4 + j, _peer(k ^ 1)[1], 1) for j, k in enumerate((2, 4, 6))]
    return direct, forwards, from_sibling


def _mm(name, a, b, *, grid, a_spec, b_spec, out_spec, out_shape, dims, acc_axis=None, acc_shape=None, carry=None,
        a_pre=None, post=None, cache=None):
    n_acc = grid[acc_axis] if acc_axis is not None else 1
    n_steps = math.prod(grid)
    n_ext = 0 if post is None else 1

    def finish(r, e_ref):
        return r if post is None else post[2](r, e_ref[...])

    def as_bf16(ref, pre, cache_ref):
        def convert():
            return (ref[...] if pre is None else pre(ref[...])).astype(BF16)

        if cache_ref is None:
            return convert()
        outer, inner = pl.program_id(0), pl.program_id(1)
        if cache[1] == "inner":
            @pl.when(inner == 0)
            def _():
                cache_ref[...] = convert()

            return cache_ref[...]

        @pl.when(outer == 0)
        def _():
            cache_ref[inner] = convert()

        return cache_ref[inner]

    def body(*refs):
        a_ref, b_ref = refs[:2]
        e_ref = refs[2] if post is not None else None
        if carry is None:
            o_ref = refs[2 + n_ext]
            scratch = list(refs[3 + n_ext:])
        else:
            src_ref, o_ref, land_ref = refs[2 + n_ext:5 + n_ext]
            scratch, sems = list(refs[5 + n_ext:-3]), refs[-3:]
            step = functools.reduce(lambda s, i: s * grid[i] + pl.program_id(i), range(len(grid)), 0)
        cache_ref = scratch.pop() if cache is not None else None

        if carry is not None:
            @pl.when(step == 0)
            def _():
                if carry[0] == "gather":
                    pltpu.make_async_copy(src_ref, land_ref.at[_peer(0)[1]], sems[2]).start()
                    for cp in _gather_copies(src_ref, land_ref, sems[0], sems[1])[0]:
                        cp.start()
                else:
                    for cp in _exchange_copies(carry[2], src_ref, land_ref, sems[0], sems[1]):
                        cp.start()

            if carry[0] == "gather":
                @pl.when(step == (3 * n_steps) // 4)
                def _():
                    direct, forwards, _ = _gather_copies(src_ref, land_ref, sems[0], sems[1])
                    for cp, fw in zip(direct[1:], forwards):
                        cp.wait_recv()
                        fw.start()

        a_val = as_bf16(a_ref, a_pre, cache_ref if cache is not None and cache[0] == "a" else None)
        b_val = as_bf16(b_ref, None, cache_ref if cache is not None and cache[0] == "b" else None)
        r = lax.dot_general(a_val, b_val, dims, preferred_element_type=F32)
        if acc_axis is None:
            o_ref[...] = finish(r, e_ref).astype(o_ref.dtype)
        else:
            acc = scratch[0]
            k = pl.program_id(acc_axis)

            @pl.when(k == 0)
            def _():
                acc[...] = r

            @pl.when(k > 0)
            def _():
                acc[...] += r

            @pl.when(k == n_acc - 1)
            def _():
                o_ref[...] = finish(acc[...], e_ref).astype(o_ref.dtype)

        if carry is not None:
            @pl.when(step == n_steps - 1)
            def _():
                if carry[0] == "gather":
                    direct, forwards, from_sibling = _gather_copies(src_ref, land_ref, sems[0], sems[1])
                    for cp in from_sibling:
                        cp.wait_recv()
                    for cp in direct + forwards:
                        cp.wait_send()
                    pltpu.make_async_copy(src_ref, land_ref.at[_peer(0)[1]], sems[2]).wait()
                else:
                    for cp in _exchange_copies(carry[2], src_ref, land_ref, sems[0], sems[1]):
                        cp.wait()

    scratch = [] if acc_axis is None else [pltpu.VMEM(acc_shape, F32)]
    if cache is not None:
        block = tuple(s for s in (a_spec if cache[0] == "a" else b_spec).block_shape if s is not None)
        scratch.append(pltpu.VMEM(block if cache[1] == "inner" else (grid[1],) + block, BF16))
    in_specs, operands = [a_spec, b_spec], [a, b]
    if post is not None:
        in_specs.append(post[1])
        operands.append(post[0])
    if carry is None:
        sem = tuple("arbitrary" if i == acc_axis or cache is not None else "parallel" for i in range(len(grid)))
        return pl.pallas_call(body, name=name, grid=grid, in_specs=in_specs, out_specs=out_spec,
                              out_shape=out_shape, scratch_shapes=scratch, compiler_params=_params(sem))(*operands)
    kind, src = carry[:2]
    n_land = N_DEV if kind == "gather" else len(carry[2])
    n_sem = N_DEV - 1 if kind == "gather" else n_land
    land = jax.ShapeDtypeStruct((n_land,) + src.shape[-2:], src.dtype)
    hbm = pl.BlockSpec(memory_space=pl.ANY)
    scratch += [pltpu.SemaphoreType.DMA((n_sem,)), pltpu.SemaphoreType.DMA((n_sem,)), pltpu.SemaphoreType.DMA(())]
    return pl.pallas_call(body, name=name + "_" + kind, grid=grid, in_specs=in_specs + [hbm],
                          out_specs=[out_spec, hbm], out_shape=[out_shape, land], scratch_shapes=scratch,
                          compiler_params=_params(("arbitrary",) * len(grid)))(*operands, src)


def _tile(n, t):
    if n <= t:
        return n
    for d in range(t - t % 8, 7, -8):
        if n % d == 0:
            return d
    raise ValueError((n, t))


def _col_fwd_call(x, g, out_dtype, carry=None):
    t, k = x.shape
    _, _, ns = g.shape
    tm = _tile(t, ROW_TILE)
    return _mm("col_fwd", x, g, grid=(t // tm, N_DEV),
               a_spec=pl.BlockSpec((tm, k), lambda i, j: (i, 0)),
               b_spec=pl.BlockSpec((None, k, ns), lambda i, j: (j, 0, 0)),
               out_spec=pl.BlockSpec((tm, ns), lambda i, j: (i, j)),
               out_shape=jax.ShapeDtypeStruct((t, N_DEV * ns), out_dtype), dims=NN, carry=carry,
               cache=("a", "inner"))


def _col_dx_call(dy, g, carry=None):
    _, k, ns = g.shape
    t = dy.shape[0]
    tm = _tile(t, ROW_TILE)
    return _mm("col_dx", dy, g, grid=(t // tm, N_DEV),
               a_spec=pl.BlockSpec((tm, ns), lambda i, j: (i, j)),
               b_spec=pl.BlockSpec((None, k, ns), lambda i, j: (j, 0, 0)),
               out_spec=pl.BlockSpec((tm, k), lambda i, j: (i, 0)),
               out_shape=jax.ShapeDtypeStruct((t, k), F32), dims=NT, acc_axis=1, acc_shape=(tm, k), carry=carry)


def _col_dw_call(x, dy, ns, carry=None):
    t, k = x.shape
    tt = _tile(t, ROW_TILE)
    last = t // tt - 1
    return _mm("col_dw", x, dy, grid=(N_DEV, t // tt), cache=("a", "outer"),
               a_spec=pl.BlockSpec((tt, k), lambda j, s: (jnp.where(j == 0, s, last), 0)),
               b_spec=pl.BlockSpec((tt, ns), lambda j, s: (s, j)),
               out_spec=pl.BlockSpec((None, k, ns), lambda j, s: (j, 0, 0)),
               out_shape=jax.ShapeDtypeStruct((N_DEV, k, ns), BF16), dims=TN, acc_axis=1, acc_shape=(k, ns),
               carry=carry)


def _make_linear_col(out_dtype):
    @jax.custom_vjp
    def linear_col(x, g):
        return _col_fwd_call(x, g, out_dtype)

    def fwd(x, g):
        return _col_fwd_call(x, g, out_dtype), (x, g)

    def bwd(res, dy):
        x, g = res
        return _col_dx_call(dy, g), _col_dw_call(x, dy, g.shape[2])

    linear_col.defvjp(fwd, bwd)

    @jax.custom_vjp
    def linear_col_next(x, g, nxt):
        return next_fwd(x, g, nxt)[0]

    def next_fwd(x, g, nxt):
        y, g_next = _col_fwd_call(x, g, out_dtype, carry=("gather", nxt.astype(BF16)))
        return (y, g_next), (x, g)

    def next_bwd(res, cts):
        x, g = res
        dy, dg_next = cts
        dx, near = _col_dx_call(dy, g, carry=("exchange", dg_next, NEAR))
        dg, far = _col_dw_call(x, dy, g.shape[2], carry=("exchange", dg_next, FAR))
        return dx, dg, sum_slots(near, far)

    linear_col_next.defvjp(next_fwd, next_bwd)
    return linear_col, linear_col_next


linear_col, linear_col_next = _make_linear_col(F32)
linear_col_bf16, linear_col_bf16_next = _make_linear_col(BF16)


def _relu2(a):
    return jnp.square(jnp.maximum(a.astype(F32), 0.0))


def _relu2_grad(d_act, a):
    return d_act * (2.0 * jnp.maximum(a.astype(F32), 0.0))


def _lin_fwd_call(x, w, act, carry=None):
    t, k = x.shape
    _, n = w.shape
    tm, tk = _tile(t, ROW_TILE), _tile(k, 1024)
    return _mm("lin_fwd", x, w, grid=(t // tm, k // tk),
               a_spec=pl.BlockSpec((tm, tk), lambda i, s: (i, s)),
               b_spec=pl.BlockSpec((tk, n), lambda i, s: (s, 0)),
               out_spec=pl.BlockSpec((tm, n), lambda i, s: (i, 0)),
               out_shape=jax.ShapeDtypeStruct((t, n), F32), dims=NN, acc_axis=1, acc_shape=(tm, n), carry=carry,
               a_pre=_relu2 if act else None)


def _lin_dx_call(dy, w, x, act, carry=None):
    k, n = w.shape
    t = dy.shape[0]
    tm, tk = _tile(t, 1024), _tile(k, 1024)
    out_spec = pl.BlockSpec((tm, tk), lambda i, s: (i, s))
    return _mm("lin_dx", dy, w, grid=(t // tm, k // tk),
               a_spec=pl.BlockSpec((tm, n), lambda i, s: (i, 0)),
               b_spec=pl.BlockSpec((tk, n), lambda i, s: (s, 0)),
               out_spec=out_spec, out_shape=jax.ShapeDtypeStruct((t, k), x.dtype), dims=NT, carry=carry,
               post=(x, out_spec, _relu2_grad) if act else None, cache=("a", "inner"))


def _lin_dw_call(x, dy, act, carry=None):
    t, k = x.shape
    n = dy.shape[1]
    tk, tt = _tile(k, 1024), _tile(t, ROW_TILE)
    last = t // tt - 1
    return _mm("lin_dw", x, dy, grid=(k // tk, t // tt), cache=("b", "outer"),
               a_spec=pl.BlockSpec((tt, tk), lambda s, r: (r, s)),
               b_spec=pl.BlockSpec((tt, n), lambda s, r: (jnp.where(s == 0, r, last), 0)),
               out_spec=pl.BlockSpec((tk, n), lambda s, r: (s, 0)),
               out_shape=jax.ShapeDtypeStruct((k, n), BF16), dims=TN, acc_axis=1, acc_shape=(tk, n), carry=carry,
               a_pre=_relu2 if act else None)


def _make_linear(act):
    @jax.custom_vjp
    def linear(x, w):
        return _lin_fwd_call(x, w, act)

    def fwd(x, w):
        return _lin_fwd_call(x, w, act), (x, w)

    def bwd(res, dy):
        x, w = res
        return _lin_dx_call(dy, w, x, act), _lin_dw_call(x, dy, act)

    linear.defvjp(fwd, bwd)

    @jax.custom_vjp
    def linear_next(x, w, nxt):
        return next_fwd(x, w, nxt)[0]

    def next_fwd(x, w, nxt):
        y, g_next = _lin_fwd_call(x, w, act, carry=("gather", nxt.astype(BF16)))
        return (y, g_next), (x, w)

    def next_bwd(res, cts):
        x, w = res
        dy, dg_next = cts
        dx, near = _lin_dx_call(dy, w, x, act, carry=("exchange", dg_next, NEAR))
        dw, far = _lin_dw_call(x, dy, act, carry=("exchange", dg_next, FAR))
        return dx, dw, sum_slots(near, far)

    linear_next.defvjp(next_fwd, next_bwd)
    return linear, linear_next


linear, linear_next = _make_linear(False)
linear_relu2, linear_relu2_next = _make_linear(True)


@jax.custom_vjp
def grouped_mm(x, w):
    return _grouped_fwd(x, w)[0]


def _grouped_fwd(x, w):
    t = x.shape[0]
    p, kin, kout = w.shape
    tm = _tile(t, 1024)
    y = _mm("grp_fwd", x, w, grid=(t // tm, p),
            a_spec=pl.BlockSpec((tm, kin), lambda i, q: (i, q)),
            b_spec=pl.BlockSpec((None, kin, kout), lambda i, q: (q, 0, 0)),
            out_spec=pl.BlockSpec((tm, kout), lambda i, q: (i, q)),
            out_shape=jax.ShapeDtypeStruct((t, p * kout), F32), dims=NN)
    return y, (x, w)


def _grouped_bwd(res, dy):
    x, w = res
    t = x.shape[0]
    p, kin, kout = w.shape
    tm = _tile(t, 1024)
    dx = _mm("grp_dx", dy, w, grid=(t // tm, p),
             a_spec=pl.BlockSpec((tm, kout), lambda i, q: (i, q)),
             b_spec=pl.BlockSpec((None, kin, kout), lambda i, q: (q, 0, 0)),
             out_spec=pl.BlockSpec((tm, kin), lambda i, q: (i, q)),
             out_shape=jax.ShapeDtypeStruct((t, p * kin), F32), dims=NT)
    dw = _mm("grp_dw", x, dy, grid=(p, t // tm),
             a_spec=pl.BlockSpec((tm, kin), lambda q, s: (s, q)),
             b_spec=pl.BlockSpec((tm, kout), lambda q, s: (s, q)),
             out_spec=pl.BlockSpec((None, kin, kout), lambda q, s: (q, 0, 0)),
             out_shape=jax.ShapeDtypeStruct((p, kin, kout), F32), dims=TN, acc_axis=1, acc_shape=(kin, kout))
    return dx, dw


grouped_mm.defvjp(_grouped_fwd, _grouped_bwd)


def _row_ts(widths, seq):
    per_row = 4 * sum(widths)
    ts = 512
    while ts > 8 and ts * per_row * 2 > 24 * 1024 * 1024:
        ts //= 2
    return min(ts, seq)


def make_rowop(name, f, n_row, n_batch, n_vec, f_bwd=None):
    n_in = n_row + n_batch + n_vec

    def in_specs(args, ts):
        specs = []
        for a in args[:n_row]:
            specs.append(pl.BlockSpec((None, ts, a.shape[2]), lambda b, s: (b, s, 0)))
        for a in args[n_row:n_row + n_batch]:
            specs.append(pl.BlockSpec((None, 1, a.shape[2]), lambda b, s: (b, 0, 0)))
        for a in args[n_row + n_batch:]:
            specs.append(pl.BlockSpec((1, a.shape[1]), lambda b, s: (0, 0)))
        return specs

    def out_struct(args, ts):
        blocks = [jax.ShapeDtypeStruct((ts, a.shape[2]), F32) for a in args[:n_row]]
        blocks += [jax.ShapeDtypeStruct((1, a.shape[2]), F32) for a in args[n_row:n_row + n_batch]]
        blocks += [jax.ShapeDtypeStruct((1, a.shape[1]), F32) for a in args[n_row + n_batch:]]
        return jax.eval_shape(f, *blocks)

    def run_fwd(args):
        bsz, seq = args[0].shape[:2]
        outs0 = out_struct(args, 8)
        widths = [a.shape[2] for a in args[:n_row]] + [o.shape[1] for o in outs0]
        ts = _row_ts(widths, seq)

        def body(*refs):
            outs = f(*[r[...] for r in refs[:n_in]])
            for o_ref, o in zip(refs[n_in:], outs):
                o_ref[...] = o

        return pl.pallas_call(
            body, name=name + "_fwd", grid=(bsz, seq // ts), in_specs=in_specs(args, ts),
            out_specs=[pl.BlockSpec((None, ts, o.shape[1]), lambda b, s: (b, s, 0)) for o in outs0],
            out_shape=[jax.ShapeDtypeStruct((bsz, seq, o.shape[1]), F32) for o in outs0],
            compiler_params=_params(("parallel", "parallel")))(*args)

    def run_bwd(args, cts):
        bsz, seq = args[0].shape[:2]
        widths = [a.shape[2] for a in args[:n_row]] * 2 + [c.shape[2] for c in cts]
        ts = _row_ts(widths, seq)
        n_ct = len(cts)

        def body(*refs):
            ins = [r[...] for r in refs[:n_in]]
            ct = [r[...] for r in refs[n_in:n_in + n_ct]]
            if f_bwd is None:
                _, vjp = jax.vjp(f, *ins)
                grads = vjp(tuple(ct))
            else:
                grads = f_bwd(ins, ct)
            g_refs = refs[n_in + n_ct:]
            b, s = pl.program_id(0), pl.program_id(1)
            for i in range(n_row):
                g_refs[i][...] = grads[i]
            for i in range(n_row, n_row + n_batch):
                @pl.when(s == 0)
                def _(i=i):
                    g_refs[i][...] = grads[i]

                @pl.when(s > 0)
                def _(i=i):
                    g_refs[i][...] += grads[i]
            for i in range(n_row + n_batch, n_in):
                first = jnp.logical_and(b == 0, s == 0)

                @pl.when(first)
                def _(i=i):
                    g_refs[i][...] = grads[i]

                @pl.when(jnp.logical_not(first))
                def _(i=i):
                    g_refs[i][...] += grads[i]

        ct_specs = [pl.BlockSpec((None, ts, c.shape[2]), lambda b, s: (b, s, 0)) for c in cts]
        return pl.pallas_call(
            body, name=name + "_bwd", grid=(bsz, seq // ts), in_specs=in_specs(args, ts) + ct_specs,
            out_specs=in_specs(args, ts), out_shape=[jax.ShapeDtypeStruct(a.shape, F32) for a in args],
            compiler_params=_params(("arbitrary", "arbitrary")))(*args, *cts)

    @jax.custom_vjp
    def op(*args):
        return tuple(run_fwd(args))

    def fwd(*args):
        return tuple(run_fwd(args)), args

    def bwd(args, cts):
        return tuple(run_bwd(args, list(cts)))

    op.defvjp(fwd, bwd)
    return op


def _modnorm_f(h, scale, shift, g):
    y = h * lax.rsqrt(jnp.mean(h * h, axis=-1, keepdims=True) + EPS) * g
    return (y * (1.0 + scale) + shift,)


def _gate_res_f(h, y, gate):
    return (h + gate * y,)


def _res_modnorm_f(h, y, gate, scale, shift, g):
    h2 = h + gate * y
    return (h2,) + _modnorm_f(h2, scale, shift, g)


def _relu2_f(a):
    return (jnp.square(jnp.maximum(a, 0.0)),)


def _gelu_skip_f(y, u, d):
    return (jax.nn.gelu(y + d * u),)


def _glu_f(vg):
    n = vg.shape[1] // 2
    return (vg[:, :n] * jax.nn.sigmoid(vg[:, n:]),)


def _glu_b(ins, cts):
    (vg,), (ct,) = ins, cts
    n = vg.shape[1] // 2
    sg = jax.nn.sigmoid(vg[:, n:])
    return (jnp.concatenate([ct * sg, ct * vg[:, :n] * sg * (1.0 - sg)], axis=1),)


def _combine_f(o1, o2, o3, l1, l2, l3):
    m = jnp.maximum(jnp.maximum(l1, l2), l3)
    e1, e2, e3 = jnp.exp(l1 - m), jnp.exp(l2 - m), jnp.exp(l3 - m)
    return ((e1 * o1 + e2 * o2 + e3 * o3) / (e1 + e2 + e3),)


def _final_loss_f(h, target, g):
    y = h * lax.rsqrt(jnp.mean(h * h, axis=-1, keepdims=True) + EPS) * g
    return (0.5 * jnp.mean(jnp.square(y - target), axis=-1, keepdims=True),)


modnorm = make_rowop("modnorm", _modnorm_f, 1, 2, 1)
gate_res = make_rowop("gate_res", _gate_res_f, 2, 1, 0)
res_modnorm = make_rowop("res_modnorm", _res_modnorm_f, 2, 3, 1)
relu2 = make_rowop("relu2", _relu2_f, 1, 0, 0)
gelu_skip = make_rowop("gelu_skip", _gelu_skip_f, 2, 0, 1)
glu = make_rowop("glu", _glu_f, 1, 0, 0, _glu_b)
combine = make_rowop("combine", _combine_f, 6, 0, 0)
final_loss = make_rowop("final_loss", _final_loss_f, 2, 0, 1)


SCAN_TB = 64
HALF = 512


def _scan_fwd_call(bu, a_re, a_im):
    bsz, rows, width = bu.shape
    seq = rows // 8
    tb = _tile(seq, SCAN_TB)

    def body(bu_ref, ar_ref, ai_ref, o_ref, carry):
        @pl.when(pl.program_id(0) == 0)
        def _():
            carry[...] = jnp.zeros_like(carry)

        ar, ai = ar_ref[...], ai_ref[...]

        def step(t, st):
            r0 = pl.multiple_of(t * 8, 8)
            new = []
            for b in range(bsz):
                sre, sim = st[2 * b], st[2 * b + 1]
                nre = ar * sre - ai * sim + bu_ref[b, pl.ds(r0, 8), pl.ds(0, HALF)]
                nim = ar * sim + ai * sre + bu_ref[b, pl.ds(r0, 8), pl.ds(HALF, HALF)]
                o_ref[b, pl.ds(r0, 8), pl.ds(0, HALF)] = nre
                o_ref[b, pl.ds(r0, 8), pl.ds(HALF, HALF)] = nim
                new += [nre, nim]
            return tuple(new)

        fin = lax.fori_loop(0, tb, step, tuple(carry[k] for k in range(2 * bsz)), unroll=4)
        for k in range(2 * bsz):
            carry[k] = fin[k]

    blk = pl.BlockSpec((bsz, tb * 8, width), lambda i: (0, i, 0))
    vec = pl.BlockSpec((8, HALF), lambda i: (0, 0))
    return pl.pallas_call(body, name="s5_scan_fwd", grid=(seq // tb,), in_specs=[blk, vec, vec], out_specs=blk,
                          out_shape=jax.ShapeDtypeStruct(bu.shape, F32),
                          scratch_shapes=[pltpu.VMEM((2 * bsz, 8, HALF), F32)],
                          compiler_params=_params(("arbitrary",)))(bu, a_re, a_im)


def _scan_bwd_call(g, st, a_re, a_im):
    bsz, rows, width = g.shape
    seq = rows // 8
    tb = _tile(seq, SCAN_TB)
    nt = seq // tb

    def body(g_ref, st_ref, prev_ref, ar_ref, ai_ref, db_ref, dar_ref, dai_ref, carry):
        i = pl.program_id(0)

        @pl.when(i == 0)
        def _():
            carry[...] = jnp.zeros_like(carry)
            dar_ref[...] = jnp.zeros_like(dar_ref)
            dai_ref[...] = jnp.zeros_like(dai_ref)

        ar, ai = ar_ref[...], ai_ref[...]

        def lam_step(b, r0, lre, lim):
            nre = g_ref[b, pl.ds(r0, 8), pl.ds(0, HALF)] + ar * lre + ai * lim
            nim = g_ref[b, pl.ds(r0, 8), pl.ds(HALF, HALF)] - ai * lre + ar * lim
            db_ref[b, pl.ds(r0, 8), pl.ds(0, HALF)] = nre
            db_ref[b, pl.ds(r0, 8), pl.ds(HALF, HALF)] = nim
            return nre, nim

        def step(k, c):
            t = tb - 1 - k
            r0 = pl.multiple_of(t * 8, 8)
            p0 = pl.multiple_of(t * 8 - 8, 8)
            lam, dar, dai = list(c[:2 * bsz]), c[2 * bsz], c[2 * bsz + 1]
            for b in range(bsz):
                nre, nim = lam_step(b, r0, lam[2 * b], lam[2 * b + 1])
                pre = st_ref[b, pl.ds(p0, 8), pl.ds(0, HALF)]
                pim = st_ref[b, pl.ds(p0, 8), pl.ds(HALF, HALF)]
                dar = dar + nre * pre + nim * pim
                dai = dai + nim * pre - nre * pim
                lam[2 * b], lam[2 * b + 1] = nre, nim
            return tuple(lam) + (dar, dai)

        zero = jnp.zeros((8, HALF), F32)
        init = tuple(carry[k] for k in range(2 * bsz)) + (zero, zero)
        c = lax.fori_loop(0, tb - 1, step, init, unroll=4)
        lam, dar, dai = list(c[:2 * bsz]), c[2 * bsz], c[2 * bsz + 1]
        keep = jnp.where(i == nt - 1, 0.0, 1.0)
        for b in range(bsz):
            nre, nim = lam_step(b, 0, lam[2 * b], lam[2 * b + 1])
            pre = prev_ref[b, :, pl.ds(0, HALF)] * keep
            pim = prev_ref[b, :, pl.ds(HALF, HALF)] * keep
            dar = dar + nre * pre + nim * pim
            dai = dai + nim * pre - nre * pim
            carry[2 * b], carry[2 * b + 1] = nre, nim
        dar_ref[...] += dar
        dai_ref[...] += dai

    blk = pl.BlockSpec((bsz, tb * 8, width), lambda i: (0, nt - 1 - i, 0))
    prev = pl.BlockSpec((bsz, 8, width), lambda i: (0, jnp.maximum((nt - 1 - i) * tb - 1, 0), 0))
    vec = pl.BlockSpec((8, HALF), lambda i: (0, 0))
    return pl.pallas_call(
        body, name="s5_scan_bwd", grid=(nt,), in_specs=[blk, blk, prev, vec, vec], out_specs=[blk, vec, vec],
        out_shape=[jax.ShapeDtypeStruct(g.shape, F32), jax.ShapeDtypeStruct((8, HALF), F32),
                   jax.ShapeDtypeStruct((8, HALF), F32)],
        scratch_shapes=[pltpu.VMEM((2 * bsz, 8, HALF), F32)],
        compiler_params=_params(("arbitrary",)))(g, st, st, a_re, a_im)


@jax.custom_vjp
def s5_scan(bu, a_re, a_im):
    return _scan_fwd_call(bu, a_re, a_im)


def _s5_scan_fwd(bu, a_re, a_im):
    st = _scan_fwd_call(bu, a_re, a_im)
    return st, (st, a_re, a_im)


def _s5_scan_bwd(res, g):
    st, a_re, a_im = res
    return tuple(_scan_bwd_call(g, st, a_re, a_im))


s5_scan.defvjp(_s5_scan_fwd, _s5_scan_bwd)


SSM_TB = 64
PITCH = SSM_TB + 8
PIECES = 8
CH = 128
NCH = 2 * HALF // LANES
NRE = NCH // 2


def _slab(t):
    return pl.ds(t, PIECES, stride=PITCH)


def _put_rows(ref, b, q, val, tb):
    for j in range(NCH):
        ref[b, j, pl.ds(q * PITCH, tb), :] = val[:, j * LANES:(j + 1) * LANES]


def _get_rows(ref, q, bsz, tb):
    return jnp.concatenate(
        [jnp.concatenate([ref[b, j, pl.ds(q * PITCH, tb), :] for j in range(NCH)], axis=1) for b in range(bsz)], axis=0)


def _chunks(a):
    return [a[:, j * LANES:(j + 1) * LANES] for j in range(a.shape[1] // LANES)]


def _ssm_fwd_call(u, b_mat, c_mat, a_re, a_im):
    bsz, seq, d = u.shape
    tb = SSM_TB
    nt = seq // tb
    assert seq % tb == 0 and d == PIECES * CH

    def body(u_ref, b_ref, c_ref, ar_ref, ai_ref, y_ref, st_ref, end_ref, bu, carry):
        @pl.when(pl.program_id(0) == 0)
        def _():
            carry[...] = jnp.zeros_like(carry)

        u2 = u_ref[...].reshape(bsz * tb, d).astype(BF16)
        for q in range(PIECES):
            r = lax.dot_general(u2[:, q * CH:(q + 1) * CH], b_ref[q], NN, preferred_element_type=F32)
            for b in range(bsz):
                _put_rows(bu, b, q, r[b * tb:(b + 1) * tb], tb)
        ar, ai = _chunks(ar_ref[...]), _chunks(ai_ref[...])
        st = [[carry[b, j] for j in range(NCH)] for b in range(bsz)]
        for t in range(tb):
            for b in range(bsz):
                for j in range(NRE):
                    sre, sim = st[b][j], st[b][NRE + j]
                    nre = ar[j] * sre - ai[j] * sim + bu[b, j, _slab(t), :]
                    nim = ar[j] * sim + ai[j] * sre + bu[b, NRE + j, _slab(t), :]
                    st_ref[b, j, _slab(t), :] = nre
                    st_ref[b, NRE + j, _slab(t), :] = nim
                    st[b][j], st[b][NRE + j] = nre, nim
        for b in range(bsz):
            for j in range(NCH):
                carry[b, j] = st[b][j]
                end_ref[b, j] = st[b][j]
        for q in range(PIECES):
            s2 = _get_rows(st_ref, q, bsz, tb).astype(BF16)
            r = lax.dot_general(s2, c_ref[q], NN, preferred_element_type=F32)
            for b in range(bsz):
                y_ref[b, :, q * CH:(q + 1) * CH] = r[b * tb:(b + 1) * tb]

    rows = pl.BlockSpec((bsz, tb, d), lambda i: (0, i, 0))
    vec = pl.BlockSpec((PIECES, HALF), lambda i: (0, 0))
    return pl.pallas_call(
        body, name="s5_ssm_fwd", grid=(nt,),
        in_specs=[rows, pl.BlockSpec((PIECES, CH, 2 * HALF), lambda i: (0, 0, 0)),
                  pl.BlockSpec((PIECES, 2 * HALF, CH), lambda i: (0, 0, 0)), vec, vec],
        out_specs=[rows, pl.BlockSpec((bsz, None, NCH, PIECES * PITCH, LANES), lambda i: (0, i, 0, 0, 0)),
                   pl.BlockSpec((bsz, None, NCH, PIECES, LANES), lambda i: (0, i, 0, 0, 0))],
        out_shape=[jax.ShapeDtypeStruct((bsz, seq, d), F32),
                   jax.ShapeDtypeStruct((bsz, nt, NCH, PIECES * PITCH, LANES), F32),
                   jax.ShapeDtypeStruct((bsz, nt, NCH, PIECES, LANES), F32)],
        scratch_shapes=[pltpu.VMEM((bsz, NCH, PIECES * PITCH, LANES), F32), pltpu.VMEM((bsz, NCH, PIECES, LANES), F32)],
        compiler_params=_params(("arbitrary",)))(u, b_mat, c_mat, a_re, a_im)


def _ssm_bwd_call(dy, u, st, ends, b_mat, c_mat, a_re, a_im):
    bsz, seq, d = u.shape
    tb = SSM_TB
    nt = seq // tb

    def body(dy_ref, u_ref, st_ref, prev_ref, b_ref, c_ref, ar_ref, ai_ref, du_ref, db_hbm, dc_hbm, dar_ref, dai_ref,
             lam, carry, db_acc, dc_acc):
        i = pl.program_id(0)

        @pl.when(i == 0)
        def _():
            carry[...] = jnp.zeros_like(carry)
            db_acc[...] = jnp.zeros_like(db_acc)
            dc_acc[...] = jnp.zeros_like(dc_acc)
            dar_ref[...] = jnp.zeros_like(dar_ref)
            dai_ref[...] = jnp.zeros_like(dai_ref)

        dy2 = dy_ref[...].reshape(bsz * tb, d).astype(BF16)
        u2 = u_ref[...].reshape(bsz * tb, d).astype(BF16)
        for q in range(PIECES):
            dyq = dy2[:, q * CH:(q + 1) * CH]
            g = lax.dot_general(dyq, c_ref[q], NT, preferred_element_type=F32)
            for b in range(bsz):
                _put_rows(lam, b, q, g[b * tb:(b + 1) * tb], tb)
            s2 = _get_rows(st_ref, q, bsz, tb).astype(BF16)
            dc_acc[q] += lax.dot_general(s2, dyq, TN, preferred_element_type=F32)

        ar, ai = _chunks(ar_ref[...]), _chunks(ai_ref[...])
        keep = jnp.where(i == nt - 1, 0.0, 1.0)
        lm = [[carry[b, j] for j in range(NCH)] for b in range(bsz)]
        dar = [jnp.zeros((PIECES, LANES), F32) for _ in range(NRE)]
        dai = [jnp.zeros((PIECES, LANES), F32) for _ in range(NRE)]
        for t in range(tb - 1, -1, -1):
            for b in range(bsz):
                for j in range(NRE):
                    lre, lim = lm[b][j], lm[b][NRE + j]
                    nre = lam[b, j, _slab(t), :] + ar[j] * lre + ai[j] * lim
                    nim = lam[b, NRE + j, _slab(t), :] - ai[j] * lre + ar[j] * lim
                    lam[b, j, _slab(t), :] = nre
                    lam[b, NRE + j, _slab(t), :] = nim
                    if t > 0:
                        pre, pim = st_ref[b, j, _slab(t - 1), :], st_ref[b, NRE + j, _slab(t - 1), :]
                    else:
                        pre, pim = prev_ref[b, j] * keep, prev_ref[b, NRE + j] * keep
                    dar[j] = dar[j] + nre * pre + nim * pim
                    dai[j] = dai[j] + nim * pre - nre * pim
                    lm[b][j], lm[b][NRE + j] = nre, nim
        for b in range(bsz):
            for j in range(NCH):
                carry[b, j] = lm[b][j]
        dar_ref[...] += jnp.concatenate(dar, axis=1)
        dai_ref[...] += jnp.concatenate(dai, axis=1)

        for q in range(PIECES):
            l2 = _get_rows(lam, q, bsz, tb).astype(BF16)
            r = lax.dot_general(l2, b_ref[q], NT, preferred_element_type=F32)
            for b in range(bsz):
                du_ref[b, :, q * CH:(q + 1) * CH] = r[b * tb:(b + 1) * tb]
            db_acc[q] += lax.dot_general(u2[:, q * CH:(q + 1) * CH], l2, TN, preferred_element_type=F32)

        @pl.when(i == nt - 1)
        def _():
            pltpu.sync_copy(db_acc, db_hbm)
            pltpu.sync_copy(dc_acc, dc_hbm)

    rows = pl.BlockSpec((bsz, tb, d), lambda i: (0, nt - 1 - i, 0))
    vec = pl.BlockSpec((PIECES, HALF), lambda i: (0, 0))
    hbm = pl.BlockSpec(memory_space=pl.ANY)
    return pl.pallas_call(
        body, name="s5_ssm_bwd", grid=(nt,),
        in_specs=[rows, rows,
                  pl.BlockSpec((bsz, None, NCH, PIECES * PITCH, LANES), lambda i: (0, nt - 1 - i, 0, 0, 0)),
                  pl.BlockSpec((bsz, None, NCH, PIECES, LANES), lambda i: (0, jnp.maximum(nt - 2 - i, 0), 0, 0, 0)),
                  pl.BlockSpec((PIECES, CH, 2 * HALF), lambda i: (0, 0, 0)),
                  pl.BlockSpec((PIECES, 2 * HALF, CH), lambda i: (0, 0, 0)), vec, vec],
        out_specs=[rows, hbm, hbm, vec, vec],
        out_shape=[jax.ShapeDtypeStruct((bsz, seq, d), F32), jax.ShapeDtypeStruct((PIECES, CH, 2 * HALF), F32),
                   jax.ShapeDtypeStruct((PIECES, 2 * HALF, CH), F32), jax.ShapeDtypeStruct((PIECES, HALF), F32),
                   jax.ShapeDtypeStruct((PIECES, HALF), F32)],
        scratch_shapes=[pltpu.VMEM((bsz, NCH, PIECES * PITCH, LANES), F32), pltpu.VMEM((bsz, NCH, PIECES, LANES), F32),
                        pltpu.VMEM((PIECES, CH, 2 * HALF), F32), pltpu.VMEM((PIECES, 2 * HALF, CH), F32)],
        compiler_params=_params(("arbitrary",)))(dy, u, st, ends, b_mat, c_mat, a_re, a_im)


@jax.custom_vjp
def s5_ssm(u, b_mat, c_mat, a_re, a_im):
    return _s5_ssm_fwd(u, b_mat, c_mat, a_re, a_im)[0]


def _s5_ssm_fwd(u, b_mat, c_mat, a_re, a_im):
    b16, c16 = b_mat.astype(BF16), c_mat.astype(BF16)
    y, st, ends = _ssm_fwd_call(u, b16, c16, a_re, a_im)
    return y, (u, st, ends, b16, c16, a_re, a_im)


def _s5_ssm_bwd(res, dy):
    return tuple(_ssm_bwd_call(dy, *res))


s5_ssm.defvjp(_s5_ssm_fwd, _s5_ssm_bwd)


ATT_HW = 2 * HEAD
ATT_HB = 1024 // ATT_HW


def _branch_geometry(dil, seq):
    sub = seq // dil
    assert sub % QBLK == 0
    return sub // QBLK


def _drows(dil, r, start, size):
    if dil == 1:
        return pl.ds(start, size)
    return pl.ds(r + start * dil, size, stride=dil)


def _masks():
    qi = lax.broadcasted_iota(jnp.int32, (QBLK, 2 * QBLK), 0)
    kj = lax.broadcasted_iota(jnp.int32, (QBLK, 2 * QBLK), 1) - QBLK
    dist = qi - kj
    band = jnp.logical_and(dist >= 0, dist <= QBLK)
    ci = lax.broadcasted_iota(jnp.int32, (QBLK, QBLK), 0)
    cj = lax.broadcasted_iota(jnp.int32, (QBLK, QBLK), 1)
    return band, ci >= cj


MAX_STRIDE = 4


class _Rows:
    def __init__(self, ref, dil, stage=None):
        self.ref, self.dil, self.stage = ref, dil, stage

    def _at(self, r, start, size):
        if self.stage is None:
            return self.ref, _drows(self.dil, r, start, size)
        step = self.dil // MAX_STRIDE
        return self.stage.at[r % MAX_STRIDE], pl.ds(r // MAX_STRIDE + start * step, size, stride=step)

    def get(self, r, start, size):
        ref, rows = self._at(r, start, size)
        return ref[rows, :]

    def put(self, r, start, size, val):
        ref, rows = self._at(r, start, size)
        ref[rows, :] = val

    def _parts(self):
        n = self.ref.shape[0] // MAX_STRIDE
        return [(a, pl.ds(a, n, stride=MAX_STRIDE)) for a in range(MAX_STRIDE)]

    def fill(self):
        if self.stage is not None:
            for a, rows in self._parts():
                self.stage[a] = self.ref[rows, :]

    def flush(self):
        if self.stage is not None:
            for a, rows in self._parts():
                self.ref[rows, :] = self.stage[a]


def _stage_shapes(dil, seq, n):
    return [pltpu.VMEM((MAX_STRIDE, seq // MAX_STRIDE, ATT_HW), F32)] * n if dil > MAX_STRIDE else []


def _widen_first(k2, v2, mask):
    wide = lambda a: jnp.concatenate([a, a], axis=0)
    return wide(k2), wide(v2), jnp.concatenate([mask, jnp.zeros_like(mask)], axis=1)


def _head_lanes():
    lane = lax.broadcasted_iota(jnp.int32, (QBLK, ATT_HW), 1)
    return [jnp.logical_and(lane >= h * HEAD, lane < (h + 1) * HEAD) for h in range(ATT_HW // HEAD)]


def _attn_specs(i, seq):
    q_spec = pl.BlockSpec((None, seq, ATT_HW), lambda b, h: (b, 0, i * ATT_HB + h))
    k_spec = pl.BlockSpec((None, seq, ATT_HW), lambda b, h: (b, 0, i * ATT_HB + h))
    v_spec = pl.BlockSpec((None, seq, ATT_HW), lambda b, h: (b, 0, (3 + i) * ATT_HB + h))
    o_spec = pl.BlockSpec((None, seq, ATT_HW), lambda b, h: (b, 0, h))
    return q_spec, k_spec, v_spec, o_spec


def _attn_fwd_call(q_all, kv_all, i, dil):
    bsz, seq, _ = q_all.shape
    nb = _branch_geometry(dil, seq)
    scale = HEAD ** -0.5

    def body(*refs):
        stages = refs[5:] if dil > MAX_STRIDE else (None,) * 5
        q_rows, k_rows, v_rows, o_rows, l_rows = (_Rows(ref, dil, st) for ref, st in zip(refs[:5], stages))
        band, causal = _masks()
        head_lanes = _head_lanes()
        for rows in (q_rows, k_rows, v_rows):
            rows.fill()

        def one(r, n, mask):
            k0, kn = (0, QBLK) if n == 0 else ((n - 1) * QBLK, 2 * QBLK)
            q2 = q_rows.get(r, n * QBLK, QBLK).astype(BF16)
            k2 = k_rows.get(r, k0, kn).astype(BF16)
            v2 = v_rows.get(r, k0, kn).astype(BF16)
            if n == 0:
                k2, v2, mask = _widen_first(k2, v2, mask)
            o, lse = None, None
            for h in range(ATT_HW // HEAD):
                s = lax.dot_general(jnp.where(head_lanes[h], q2, 0), k2, NT, preferred_element_type=F32) * scale
                s = jnp.where(mask, s, NEG)
                m = jnp.max(s, axis=-1, keepdims=True)
                p = jnp.exp(s - m)
                den = jnp.sum(p, axis=-1, keepdims=True)
                o_h = lax.dot_general(p.astype(BF16), v2, NN, preferred_element_type=F32) / den
                lse_h = jnp.broadcast_to(m + jnp.log(den), (QBLK, ATT_HW))
                o = o_h if o is None else jnp.where(head_lanes[h], o_h, o)
                lse = lse_h if lse is None else jnp.where(head_lanes[h], lse_h, lse)
            o_rows.put(r, n * QBLK, QBLK, o)
            l_rows.put(r, n * QBLK, QBLK, lse)

        for r in range(dil):
            for n in range(nb):
                one(r, n, causal if n == 0 else band)
        o_rows.flush()
        l_rows.flush()

    q_spec, k_spec, v_spec, o_spec = _attn_specs(i, seq)
    shape = jax.ShapeDtypeStruct((bsz, seq, 1024), F32)
    return pl.pallas_call(
        body, name="attn_fwd_d%d" % dil, grid=(bsz, ATT_HB), in_specs=[q_spec, k_spec, v_spec],
        out_specs=[o_spec, o_spec], out_shape=[shape, shape], scratch_shapes=_stage_shapes(dil, seq, 5),
        compiler_params=_params(("parallel", "parallel")))(q_all, kv_all, kv_all)


def _attn_bwd_call(q_all, kv_all, o, l, do, dl, i, dil, grads):
    bsz, seq, _ = q_all.shape
    nb = _branch_geometry(dil, seq)
    scale = HEAD ** -0.5
    first = grads is None

    def body(*refs):
        n_pass = 0 if first else 3
        blocks = list(refs[:7]) + list(refs[7 + n_pass:10 + n_pass])
        stages = refs[10 + n_pass:] if dil > MAX_STRIDE else (None,) * 10
        q_rows, k_rows, v_rows, o_rows, l_rows, do_rows, dl_rows, dq_rows, dk_rows, dv_rows = (
            _Rows(ref, dil, st) for ref, st in zip(blocks, stages))
        band, causal = _masks()
        head_lanes = _head_lanes()
        for rows in (q_rows, k_rows, v_rows, o_rows, l_rows, do_rows, dl_rows):
            rows.fill()

        def one(r, n, mask):
            q0 = n * QBLK
            k0, kn = (0, QBLK) if n == 0 else ((n - 1) * QBLK, 2 * QBLK)
            q2 = q_rows.get(r, q0, QBLK).astype(BF16)
            k2 = k_rows.get(r, k0, kn).astype(BF16)
            v2 = v_rows.get(r, k0, kn).astype(BF16)
            l2, do2 = l_rows.get(r, q0, QBLK), do_rows.get(r, q0, QBLK)
            t2 = dl_rows.get(r, q0, QBLK) - do2 * o_rows.get(r, q0, QBLK)
            do2 = do2.astype(BF16)
            dq, dk, dv = None, None, None
            for h in range(ATT_HW // HEAD):
                q = jnp.where(head_lanes[h], q2, 0)
                d_ob = jnp.where(head_lanes[h], do2, 0)
                s = lax.dot_general(q, k2, NT, preferred_element_type=F32) * scale
                s = jnp.where(mask, s, NEG)
                lse = jnp.max(jnp.where(head_lanes[h], l2, NEG), axis=-1, keepdims=True)
                p = jnp.exp(s - lse)
                row = jnp.sum(jnp.where(head_lanes[h], t2, 0.0), axis=-1, keepdims=True)
                dp = lax.dot_general(d_ob, v2, NT, preferred_element_type=F32)
                ds = (p * (dp + row)).astype(BF16)
                dq_h = lax.dot_general(ds, k2, NN, preferred_element_type=F32) * scale
                dk_h = lax.dot_general(ds, q, TN, preferred_element_type=F32) * scale
                dv_h = lax.dot_general(p.astype(BF16), d_ob, TN, preferred_element_type=F32)
                dq = dq_h if dq is None else jnp.where(head_lanes[h], dq_h, dq)
                dk = dk_h if dk is None else dk + dk_h
                dv = dv_h if dv is None else dv + dv_h
            dq_rows.put(r, q0, QBLK, dq)
            return dk, dv

        for r in range(dil):
            dk_cur, dv_cur = one(r, 0, causal)
            for n in range(1, nb):
                dk, dv = one(r, n, band)
                dk_rows.put(r, (n - 1) * QBLK, QBLK, dk_cur + dk[:QBLK])
                dv_rows.put(r, (n - 1) * QBLK, QBLK, dv_cur + dv[:QBLK])
                dk_cur, dv_cur = dk[QBLK:], dv[QBLK:]
            dk_rows.put(r, (nb - 1) * QBLK, QBLK, dk_cur)
            dv_rows.put(r, (nb - 1) * QBLK, QBLK, dv_cur)
        for rows in (dq_rows, dk_rows, dv_rows):
            rows.flush()

    q_spec, k_spec, v_spec, o_spec = _attn_specs(i, seq)
    in_specs = [q_spec, k_spec, v_spec, o_spec, o_spec, o_spec, o_spec]
    args = [q_all, kv_all, kv_all, o, l, do, dl]
    aliases = {}
    if not first:
        in_specs += [pl.BlockSpec(memory_space=pl.ANY)] * 3
        args += list(grads)
        aliases = {7: 0, 8: 1, 9: 2}
    shape = jax.ShapeDtypeStruct((bsz, seq, 3 * 1024), F32)
    return pl.pallas_call(
        body, name="attn_bwd_d%d" % dil, grid=(bsz, ATT_HB), in_specs=in_specs,
        out_specs=[q_spec, q_spec, q_spec], out_shape=[shape, shape, shape], input_output_aliases=aliases,
        scratch_shapes=_stage_shapes(dil, seq, 10), compiler_params=_params(("parallel", "parallel")))(*args)


@jax.custom_vjp
def attn_branches(q_all, kv_all):
    return _attn_branches_fwd(q_all, kv_all)[0]


def _attn_branches_fwd(q_all, kv_all):
    outs = []
    for i, dil in enumerate(BRANCH_DIL):
        outs += list(_attn_fwd_call(q_all, kv_all, i, dil))
    return tuple(outs), (q_all, kv_all, tuple(outs))


def _attn_branches_bwd(res, cts):
    q_all, kv_all, outs = res
    grads = None
    for i, dil in enumerate(BRANCH_DIL):
        grads = _attn_bwd_call(q_all, kv_all, outs[2 * i], outs[2 * i + 1], cts[2 * i], cts[2 * i + 1], i, dil, grads)
    dq, dk, dv = grads
    return dq, jnp.concatenate([dk, dv], axis=-1)


attn_branches.defvjp(_attn_branches_fwd, _attn_branches_bwd)


def ada_fwd(c_all, w, b):
    n, d, cs = w.shape
    nb = c_all.shape[0]

    def body(c_ref, w_ref, b_ref, o_ref):
        a = jax.nn.silu(c_ref[...]).astype(BF16)
        o_ref[...] = lax.dot_general(a, w_ref[...].astype(BF16), NN, preferred_element_type=F32) + b_ref[...]

    return pl.pallas_call(
        body, name="ada_fwd", grid=(n,),
        in_specs=[pl.BlockSpec((nb, d), lambda i: (0, 0)), pl.BlockSpec((None, d, cs), lambda i: (i, 0, 0)),
                  pl.BlockSpec((None, 1, cs), lambda i: (i, 0, 0))],
        out_specs=pl.BlockSpec((None, nb, cs), lambda i: (i, 0, 0)),
        out_shape=jax.ShapeDtypeStruct((n, nb, cs), F32), compiler_params=_params(("parallel",)))(c_all, w, b)


def ada_bwd(c_all, dm):
    n, nb, cs = dm.shape
    d = c_all.shape[1]

    def body(c_ref, dm_ref, dw_ref, db_ref):
        a = jax.nn.silu(c_ref[...]).astype(BF16)
        g = dm_ref[...]
        dw_ref[...] = lax.dot_general(a, g.astype(BF16), TN, preferred_element_type=F32)
        db_ref[...] = jnp.sum(g, axis=0, keepdims=True)

    return pl.pallas_call(
        body, name="ada_bwd", grid=(n,),
        in_specs=[pl.BlockSpec((nb, d), lambda i: (0, 0)), pl.BlockSpec((None, nb, cs), lambda i: (i, 0, 0))],
        out_specs=[pl.BlockSpec((None, d, cs), lambda i: (i, 0, 0)), pl.BlockSpec((None, 1, cs), lambda i: (i, 0, 0))],
        out_shape=[jax.ShapeDtypeStruct((n, d, cs), F32), jax.ShapeDtypeStruct((n, 1, cs), F32)],
        compiler_params=_params(("parallel",)))(c_all, dm)


def all_gather(x, name):
    m, n = x.shape

    def body(x_ref, out_ref, send_sems, recv_sems, local_sem):
        px, py, pc = lax.axis_index("x"), lax.axis_index("y"), lax.axis_index("c")
        me, sibling = (px, py, pc), (px, py, 1 - pc)
        chips = [(1 - px, py), (px, 1 - py), (1 - px, 1 - py)]

        def rows(qx, qy, qc):
            return out_ref.at[pl.ds((4 * qx + 2 * qy + qc) * m, m), :]

        def copy(k, block, to, src=None):
            return pltpu.make_async_remote_copy(
                src_ref=rows(*block) if src is None else src, dst_ref=rows(*block),
                send_sem=send_sems.at[k], recv_sem=recv_sems.at[k], device_id=to,
                device_id_type=pl.DeviceIdType.MESH)

        mine = pltpu.make_async_copy(x_ref, rows(*me), local_sem)
        mine.start()
        first = [copy(0, me, sibling, src=x_ref)]
        first += [copy(1 + j, me, (*chip, pc), src=x_ref) for j, chip in enumerate(chips)]
        for cp in first:
            cp.start()
        passed = [copy(4 + j, (*chip, pc), sibling) for j, chip in enumerate(chips)]
        for j, chip in enumerate(chips):
            copy(1 + j, (*chip, pc), me).wait_recv()
            passed[j].start()
        copy(0, sibling, me).wait_recv()
        for j, chip in enumerate(chips):
            copy(4 + j, (*chip, 1 - pc), me).wait_recv()
        for cp in first + passed:
            cp.wait_send()
        mine.wait()

    out = pl.pallas_call(
        body, name=name, out_shape=jax.ShapeDtypeStruct((N_DEV * m, n), x.dtype),
        in_specs=[pl.BlockSpec(memory_space=pl.ANY)], out_specs=pl.BlockSpec(memory_space=pl.ANY),
        scratch_shapes=[pltpu.SemaphoreType.DMA((7,)), pltpu.SemaphoreType.DMA((7,)), pltpu.SemaphoreType.DMA(())],
    )(x)
    return out.reshape(N_DEV, m, n)


def exchange_partials(p, name):
    _, m, n = p.shape

    def body(p_ref, out_ref, send_sems, recv_sems, local_sem):
        px, py, pc = lax.axis_index("x"), lax.axis_index("y"), lax.axis_index("c")
        me = 4 * px + 2 * py + pc
        mine = pltpu.make_async_copy(p_ref.at[me], out_ref.at[me], local_sem)
        mine.start()
        copies = []
        for k in range(1, N_DEV):
            qx = 1 - px if k & 4 else px
            qy = 1 - py if k & 2 else py
            qc = 1 - pc if k & 1 else pc
            cp = pltpu.make_async_remote_copy(
                src_ref=p_ref.at[4 * qx + 2 * qy + qc], dst_ref=out_ref.at[me],
                send_sem=send_sems.at[k - 1], recv_sem=recv_sems.at[k - 1], device_id=(qx, qy, qc),
                device_id_type=pl.DeviceIdType.MESH)
            cp.start()
            copies.append(cp)
        for cp in copies:
            cp.wait()
        mine.wait()

    return pl.pallas_call(
        body, name=name, out_shape=jax.ShapeDtypeStruct(p.shape, p.dtype),
        in_specs=[pl.BlockSpec(memory_space=pl.ANY)], out_specs=pl.BlockSpec(memory_space=pl.ANY),
        scratch_shapes=[pltpu.SemaphoreType.DMA((7,)), pltpu.SemaphoreType.DMA((7,)), pltpu.SemaphoreType.DMA(())],
    )(p)


def sum_slots(*parts):
    _, r, n = parts[0].shape
    tr = _tile(r, max(16, (256 * 1024) // n))

    def body(*refs):
        acc = None
        for g_ref in refs[:-1]:
            for s in range(g_ref.shape[0]):
                term = g_ref[s].astype(F32)
                acc = term if acc is None else acc + term
        refs[-1][...] = acc

    return pl.pallas_call(
        body, name="sum_slots", grid=(r // tr,),
        in_specs=[pl.BlockSpec((p.shape[0], tr, n), lambda i: (0, i, 0)) for p in parts],
        out_specs=pl.BlockSpec((tr, n), lambda i: (i, 0)), out_shape=jax.ShapeDtypeStruct((r, n), F32),
        compiler_params=_params(("parallel",)))(*parts)


def adamw(w, m, v, g, name):
    nl, r, n = w.shape
    partials = g.ndim == 4
    tr = _tile(r, max(8, (256 * 1024) // n))
    c1 = 1.0 / (1.0 - ADAM_B1 ** ADAM_STEP)
    c2 = 1.0 / (1.0 - ADAM_B2 ** ADAM_STEP)

    def body(w_ref, m_ref, v_ref, g_ref, go_ref, d_ref, mo_ref, vo_ref):
        if partials:
            grad = g_ref[0].astype(F32)
            for s in range(1, N_DEV):
                grad = grad + g_ref[s].astype(F32)
        else:
            grad = g_ref[...]
        m_new = ADAM_B1 * m_ref[...] + (1.0 - ADAM_B1) * grad
        v_new = ADAM_B2 * v_ref[...] + (1.0 - ADAM_B2) * grad * grad
        go_ref[...] = grad
        mo_ref[...] = m_new
        vo_ref[...] = v_new
        d_ref[...] = -ADAM_LR * ((m_new * c1) / (jnp.sqrt(v_new * c2) + ADAM_EPS) + ADAM_WD * w_ref[...])

    spec = pl.BlockSpec((None, tr, n), lambda l, i: (l, i, 0))
    g_spec = pl.BlockSpec((None, N_DEV, tr, n), lambda l, i: (l, 0, i, 0)) if partials else spec
    shape = jax.ShapeDtypeStruct(w.shape, F32)
    return pl.pallas_call(
        body, name=name, grid=(nl, r // tr), in_specs=[spec, spec, spec, g_spec], out_specs=[spec] * 4,
        out_shape=[shape] * 4, compiler_params=_params(("parallel", "parallel")))(w, m, v, g)


def pack(arrays):
    flat = jnp.concatenate([a.reshape(-1).astype(F32) for a in arrays])
    rows = -(-flat.shape[0] // LANES)
    mult = 8 if rows <= 512 else 512
    rows = -(-rows // mult) * mult
    flat = jnp.pad(flat, (0, rows * LANES - flat.shape[0]))
    return flat.reshape(rows, LANES)


def unpack(slab, shapes):
    flat = slab.reshape(slab.shape[:-2] + (-1,))
    out, off = [], 0
    for s in shapes:
        size = math.prod(s)
        out.append(flat[..., off:off + size].reshape(flat.shape[:-1] + tuple(s)))
        off += size
    return out


def _s5_discretize(lam_re, lam_im, log_dt, b_re, b_im, c_re, c_im):
    dt = jnp.exp(log_dt)[:, None]
    xr, th = lam_re * dt, lam_im * dt
    er = jnp.exp(xr)
    a_re, a_im = er * jnp.cos(th), er * jnp.sin(th)
    am1 = jnp.expm1(xr) * jnp.cos(th) - 2.0 * jnp.square(jnp.sin(0.5 * th))
    den = lam_re * lam_re + lam_im * lam_im
    fr = (am1 * lam_re + a_im * lam_im) / den
    fi = (a_im * lam_re - am1 * lam_im) / den
    bb_re = fr[..., None] * b_re - fi[..., None] * b_im
    bb_im = fr[..., None] * b_im + fi[..., None] * b_re
    eye = jnp.eye(8, dtype=F32)

    def pack_b(bb):
        return jnp.einsum('qgpc,gh->qgchp', bb.reshape(8, 8, 64, 16), eye).reshape(8, 128, 512)

    def pack_c(cc):
        return jnp.einsum('qgcp,gh->qgphc', cc.reshape(8, 8, 16, 64), eye).reshape(8, 512, 128)

    b_mat = jnp.concatenate([pack_b(bb_re), pack_b(bb_im)], axis=-1)
    c_mat = jnp.concatenate([pack_c(c_re), pack_c(-c_im)], axis=1)
    return a_re.reshape(8, HALF), a_im.reshape(8, HALF), b_mat, c_mat


WEIGHT_ORDER = (("ssm_w_glu", 0, True), ("mlp_w1", 0, True), ("mlp_w2", 0, False),
                ("ssm_w_glu", 1, True), ("mlp_w1", 1, True), ("mlp_w2", 1, False), ("w_kv", None, True),
                ("attn_w_q", 0, True), ("attn_w_o", 0, False), ("mlp_w1", 2, True), ("mlp_w2", 2, False),
                ("attn_w_q", 1, True), ("attn_w_o", 1, False), ("mlp_w1", 3, True), ("mlp_w2", 3, False))


class _WeightChain:
    def __init__(self, first, upcoming):
        self.cur, self.upcoming, self.used = first, list(upcoming), 0

    def matmul(self, x, name, relu2_input=False, bf16_output=False):
        assert WEIGHT_ORDER[self.used][0] == name
        col = WEIGHT_ORDER[self.used][2]
        assert not (col and relu2_input) and (col or not bf16_output)
        self.used += 1
        w = self.cur if col else self.cur.reshape(-1, self.cur.shape[-1])
        if col:
            plain, with_next = (linear_col_bf16, linear_col_bf16_next) if bf16_output else (linear_col, linear_col_next)
        else:
            plain, with_next = (linear_relu2, linear_relu2_next) if relu2_input else (linear, linear_next)
        if not self.upcoming:
            return plain(x, w)
        y, self.cur = with_next(x, w, self.upcoming.pop(0))
        return y


def _local_loss(diff, x, target, first_weight, shards):
    bsz, seq, d = x.shape
    t = bsz * seq
    rows3 = lambda a: a.reshape(bsz, seq, a.shape[-1])
    rows2 = lambda a: a.reshape(t, a.shape[-1])
    mods, kvmod, ln_g, ssm_d, kv_g, final_g = (diff[k] for k in ("mods", "kvmod", "ln_g", "ssm_d", "kv_g", "final_g"))
    chain = _WeightChain(first_weight, shards)

    def chunks(sub):
        mod = mods[sub]
        return mod[:, None, :d], mod[:, None, d:2 * d], mod[:, None, 2 * d:], ln_g[sub // 2, sub % 2][None]

    h = x
    kv_all = None
    shift, scale, gate, gain = chunks(0)
    (u,) = modnorm(h, scale, shift, gain)
    for layer in range(4):
        if layer == 2:
            (ukv,) = modnorm(h, kvmod[:, None, d:], kvmod[:, None, :d], kv_g[None])
            kv_all = rows3(chain.matmul(rows2(ukv), "w_kv"))
        if layer < 2:
            a_re, a_im, b_mat, c_mat = _s5_discretize(*(diff[k][layer] for k in (
                "ssm_lam_re", "ssm_lam_im", "ssm_log_dt", "ssm_b_re", "ssm_b_im", "ssm_c_re", "ssm_c_im")))
            y0 = s5_ssm(u, b_mat, c_mat, a_re, a_im)
            (z,) = gelu_skip(y0, u, ssm_d[layer][None])
            (y,) = glu(rows3(chain.matmul(rows2(z), "ssm_w_glu")))
        else:
            q_all = rows3(chain.matmul(rows2(u), "attn_w_q"))
            o1, l1, o2, l2, o3, l3 = attn_branches(q_all, kv_all)
            (o,) = combine(o1, o2, o3, l1, l2, l3)
            y = rows3(chain.matmul(rows2(o), "attn_w_o"))
        shift, scale, next_gate, gain = chunks(2 * layer + 1)
        h, u = res_modnorm(h, y, gate, scale, shift, gain)
        gate = next_gate
        pre = chain.matmul(rows2(u), "mlp_w1", bf16_output=True)
        y = rows3(chain.matmul(pre, "mlp_w2", relu2_input=True))
        if layer < 3:
            shift, scale, next_gate, gain = chunks(2 * layer + 2)
            h, u = res_modnorm(h, y, gate, scale, shift, gain)
            gate = next_gate
        else:
            (h,) = gate_res(h, y, gate)
    (row_loss,) = final_loss(h, target, final_g[None])
    return jnp.sum(row_loss)


SSM_NAMES = ("ssm_lam_re", "ssm_lam_im", "ssm_log_dt", "ssm_b_re", "ssm_b_im", "ssm_c_re", "ssm_c_im")
ARG_NAMES = ("x", "c", "ln_g", "ada_w", "ada_b") + SSM_NAMES + (
    "ssm_d", "ssm_w_glu", "kv_g", "kv_ada_w", "kv_ada_b", "w_kv", "attn_w_q", "attn_w_o", "mlp_w1", "mlp_w2", "final_g")
WEIGHT_NAMES = ARG_NAMES[2:]


def kernel(x, c, ln_g, ada_w, ada_b, ssm_lam_re, ssm_lam_im, ssm_log_dt, ssm_b_re, ssm_b_im, ssm_c_re, ssm_c_im, ssm_d, ssm_w_glu, kv_g, kv_ada_w, kv_ada_b, w_kv, attn_w_q, attn_w_o, mlp_w1, mlp_w2, final_g, loss_target, m_ln_g, m_ada_w, m_ada_b, m_ssm_lam_re, m_ssm_lam_im, m_ssm_log_dt, m_ssm_b_re, m_ssm_b_im, m_ssm_c_re, m_ssm_c_im, m_ssm_d, m_ssm_w_glu, m_kv_g, m_kv_ada_w, m_kv_ada_b, m_w_kv, m_attn_w_q, m_attn_w_o, m_mlp_w1, m_mlp_w2, m_final_g, v_ln_g, v_ada_w, v_ada_b, v_ssm_lam_re, v_ssm_lam_im, v_ssm_log_dt, v_ssm_b_re, v_ssm_b_im, v_ssm_c_re, v_ssm_c_im, v_ssm_d, v_ssm_w_glu, v_kv_g, v_kv_ada_w, v_kv_ada_b, v_w_kv, v_attn_w_q, v_attn_w_o, v_mlp_w1, v_mlp_w2, v_final_g):
    args = locals()
    w = {n: args[n] for n in WEIGHT_NAMES}
    mom = {n: args["m_" + n] for n in WEIGHT_NAMES}
    var = {n: args["v_" + n] for n in WEIGHT_NAMES}
    bsz, seq, d = x.shape
    me = 4 * lax.axis_index("x") + 2 * lax.axis_index("y") + lax.axis_index("c")

    small = all_gather(pack([c, ln_g, ssm_d]), "gather_small")
    c_parts, ln_parts, d_parts = unpack(small, [c.shape, ln_g.shape, ssm_d.shape])
    c_all = c_parts.reshape(N_DEV * bsz, d)
    ln_full = jnp.moveaxis(ln_parts, 0, 2).reshape(4, 2, d)
    d_full = jnp.moveaxis(d_parts, 0, 1).reshape(2, d)

    cs = ada_w.shape[-1]
    mod_cols = ada_fwd(c_all, ada_w.reshape(8, d, cs), ada_b.reshape(8, 1, cs))
    kcs = kv_ada_w.shape[-1]
    kv_cols = ada_fwd(c_all, kv_ada_w[None], jnp.zeros((1, 1, kcs), F32))
    mod_g = all_gather(mod_cols.reshape(8 * N_DEV * bsz, cs), "gather_mod")
    kv_g_all = all_gather(kv_cols.reshape(N_DEV * bsz, kcs), "gather_kvmod")
    mods_all = jnp.moveaxis(mod_g.reshape(N_DEV, 8, N_DEV * bsz, cs), 0, 2).reshape(8, N_DEV * bsz, N_DEV * cs)
    kvmod_all = jnp.moveaxis(kv_g_all, 0, 1).reshape(N_DEV * bsz, N_DEV * kcs) + kv_ada_b[None]
    mods = lax.dynamic_slice_in_dim(mods_all, me * bsz, bsz, axis=1)
    kvmod = lax.dynamic_slice_in_dim(kvmod_all, me * bsz, bsz, axis=0)

    shards = [w[n] if l is None else w[n][l] for n, l, _ in WEIGHT_ORDER]
    first_weight = all_gather(shards[0].astype(BF16), "gather_first_weight")

    diff = {"mods": mods, "kvmod": kvmod, "ln_g": ln_full, "ssm_d": d_full, "kv_g": kv_g, "final_g": final_g}
    diff.update({n: w[n] for n in SSM_NAMES})
    loss_local, (g_diff, grad_x, g_first, g_shards) = jax.value_and_grad(_local_loss, argnums=(0, 1, 3, 4))(
        diff, x, loss_target, first_weight, shards[1:])
    loss = lax.psum(loss_local, AXES)
    g_shards = [sum_slots(exchange_partials(g_first, "exchange_first_weight"))] + list(g_shards)

    dmod = all_gather(pack([g_diff["mods"], g_diff["kvmod"]]), "gather_dmod")
    dmods_p, dkv_p = unpack(dmod, [g_diff["mods"].shape, g_diff["kvmod"].shape])
    dmods_all = jnp.moveaxis(dmods_p, 0, 1).reshape(8, N_DEV * bsz, 3 * d)
    dkv_all = dkv_p.reshape(N_DEV * bsz, 2 * d)
    g_ada_w, g_ada_b = ada_bwd(c_all, lax.dynamic_slice_in_dim(dmods_all, me * cs, cs, axis=2))
    g_kv_ada_w, _ = ada_bwd(c_all, lax.dynamic_slice_in_dim(dkv_all, me * kcs, kcs, axis=1)[None])
    _, g_kv_ada_b = ada_bwd(c_all, dkv_all[None])

    small_names = ("ln_g", "ssm_d", "kv_g", "final_g") + SSM_NAMES
    partial = all_gather(pack([g_diff[n] for n in small_names]), "gather_small_grads")
    totals = unpack(sum_slots(partial), [g_diff[n].shape for n in small_names])
    g_small = dict(zip(small_names, totals))
    g_small["ln_g"] = lax.dynamic_slice_in_dim(g_small["ln_g"], me * ln_g.shape[-1], ln_g.shape[-1], axis=2)
    g_small["ssm_d"] = lax.dynamic_slice_in_dim(g_small["ssm_d"], me * ssm_d.shape[-1], ssm_d.shape[-1], axis=1)
    g_small["ada_b"] = g_ada_b.reshape(ada_b.shape)
    g_small["kv_ada_b"] = g_kv_ada_b.reshape(kv_ada_b.shape)

    out = {}

    def put(name, res, shape):
        for kind, a in zip(("grad_", "delta_", "new_m_", "new_v_"), res):
            out[kind + name] = a.reshape(shape)

    for name in ("ssm_w_glu", "w_kv", "attn_w_q", "attn_w_o", "mlp_w1", "mlp_w2"):
        grads = jnp.stack([g for g, (n, _, _) in zip(g_shards, WEIGHT_ORDER) if n == name])
        v3 = lambda a: a.reshape(grads.shape)
        put(name, adamw(v3(w[name]), v3(mom[name]), v3(var[name]), grads, "adamw_" + name), w[name].shape)
    v3 = lambda a: a.reshape(8, d, cs)
    put("ada_w", adamw(v3(ada_w), v3(m_ada_w), v3(v_ada_w), g_ada_w, "adamw_ada_w"), ada_w.shape)
    put("kv_ada_w", adamw(kv_ada_w[None], m_kv_ada_w[None], v_kv_ada_w[None], g_kv_ada_w, "adamw_kv_ada_w"), kv_ada_w.shape)
    names = small_names + ("ada_b", "kv_ada_b")
    res = adamw(pack([w[n] for n in names])[None], pack([mom[n] for n in names])[None],
                pack([var[n] for n in names])[None], pack([g_small[n] for n in names])[None], "adamw_small")
    for kind, slab in zip(("grad_", "delta_", "new_m_", "new_v_"), res):
        for n, a in zip(names, unpack(slab[0], [w[n].shape for n in names])):
            out[kind + n] = a

    result = [loss, grad_x]
    for kind in ("grad_", "delta_", "new_m_", "new_v_"):
        result += [out[kind + n] for n in WEIGHT_NAMES]
    return tuple(result)
```

```python
import functools
import math

import jax
import jax.numpy as jnp
from jax import lax
from jax.experimental import pallas as pl
from jax.experimental.pallas import tpu as pltpu

F32 = jnp.float32
BF16 = jnp.bfloat16
N_DEV = 8
AXES = ("x", "y", "c")
V7X_VMEM_LIMIT = 56 * 1024 * 1024
LANES = 128
EPS = 1e-6
NEG = -1e30
HEAD = 64
QBLK = 128
BRANCH_DIL = (1, 4, 16)
ADAM_LR, ADAM_B1, ADAM_B2, ADAM_EPS, ADAM_WD, ADAM_STEP = 0.001, 0.9, 0.999, 1e-08, 0.01, 10

NN = (((1,), (0,)), ((), ()))
NT = (((1,), (1,)), ((), ()))
TN = (((0,), (0,)), ((), ()))


def _params(sem):
    return pltpu.CompilerParams(dimension_semantics=sem, vmem_limit_bytes=V7X_VMEM_LIMIT)


def _peer(k):
    px, py, pc = lax.axis_index("x"), lax.axis_index("y"), lax.axis_index("c")
    qx = 1 - px if k & 4 else px
    qy = 1 - py if k & 2 else py
    qc = 1 - pc if k & 1 else pc
    return (qx, qy, qc), 4 * qx + 2 * qy + qc


ROW_TILE = 2048
NEAR, FAR = (0, 1, 2, 3, 6), (4, 5, 7)


def _exchange_copies(ks, src_ref, land_ref, send_sems, recv_sems):
    copies = []
    for i, k in enumerate(ks):
        peer, slot = _peer(k)
        if k == 0:
            copies.append(pltpu.make_async_copy(src_ref.at[slot], land_ref.at[i], send_sems.at[i]))
        else:
            copies.append(pltpu.make_async_remote_copy(
                src_ref=src_ref.at[slot], dst_ref=land_ref.at[i], send_sem=send_sems.at[i], recv_sem=recv_sems.at[i],
                device_id=peer, device_id_type=pl.DeviceIdType.MESH))
    return copies


def _gather_copies(src_ref, land_ref, send_sems, recv_sems):
    _, me = _peer(0)

    def copy(pair, slot, k, src=None):
        peer, _ = _peer(k)
        return pltpu.make_async_remote_copy(
            src_ref=land_ref.at[slot] if src is None else src, dst_ref=land_ref.at[slot],
            send_sem=send_sems.at[pair], recv_sem=recv_sems.at[pair], device_id=peer,
            device_id_type=pl.DeviceIdType.MESH)

    direct = [copy(0, me, 1, src_ref)] + [copy(1 + j, me, k, src_ref) for j, k in enumerate((2, 4, 6))]
    forwards = [copy(4 + j, _peer(k)[1], 1) for j, k in enumerate((2, 4, 6))]
    from_sibling = [copy(0, _peer(1)[1], 1)] + [copy(4 + j, _peer(k ^ 1)[1], 1) for j, k in enumerate((2, 4, 6))]
    return direct, forwards, from_sibling


def _mm(name, a, b, *, grid, a_spec, b_spec, out_spec, out_shape, dims, acc_axis=None, acc_shape=None, carry=None,
        a_pre=None, post=None, cache=None):
    n_acc = grid[acc_axis] if acc_axis is not None else 1
    n_steps = math.prod(grid)
    n_ext = 0 if post is None else 1

    def finish(r, e_ref):
        return r if post is None else post[2](r, e_ref[...])

    def as_bf16(ref, pre, cache_ref):
        def convert():
            return (ref[...] if pre is None else pre(ref[...])).astype(BF16)

        if cache_ref is None:
            return convert()
        outer, inner = pl.program_id(0), pl.program_id(1)
        if cache[1] == "inner":
            @pl.when(inner == 0)
            def _():
                cache_ref[...] = convert()

            return cache_ref[...]

        @pl.when(outer == 0)
        def _():
            cache_ref[inner] = convert()

        return cache_ref[inner]

    def body(*refs):
        a_ref, b_ref = refs[:2]
        e_ref = refs[2] if post is not None else None
        if carry is None:
            o_ref = refs[2 + n_ext]
            scratch = list(refs[3 + n_ext:])
        else:
            src_ref, o_ref, land_ref = refs[2 + n_ext:5 + n_ext]
            scratch, sems = list(refs[5 + n_ext:-3]), refs[-3:]
            step = functools.reduce(lambda s, i: s * grid[i] + pl.program_id(i), range(len(grid)), 0)
        cache_ref = scratch.pop() if cache is not None else None

        if carry is not None:
            @pl.when(step == 0)
            def _():
                if carry[0] == "gather":
                    pltpu.make_async_copy(src_ref, land_ref.at[_peer(0)[1]], sems[2]).start()
                    for cp in _gather_copies(src_ref, land_ref, sems[0], sems[1])[0]:
                        cp.start()
                else:
                    for cp in _exchange_copies(carry[2], src_ref, land_ref, sems[0], sems[1]):
                        cp.start()

            if carry[0] == "gather":
                @pl.when(step == (3 * n_steps) // 4)
                def _():
                    direct, forwards, _ = _gather_copies(src_ref, land_ref, sems[0], sems[1])
                    for cp, fw in zip(direct[1:], forwards):
                        cp.wait_recv()
                        fw.start()

        a_val = as_bf16(a_ref, a_pre, cache_ref if cache is not None and cache[0] == "a" else None)
        b_val = as_bf16(b_ref, None, cache_ref if cache is not None and cache[0] == "b" else None)
        r = lax.dot_general(a_val, b_val, dims, preferred_element_type=F32)
        if acc_axis is None:
            o_ref[...] = finish(r, e_ref).astype(o_ref.dtype)
        else:
            acc = scratch[0]
            k = pl.program_id(acc_axis)

            @pl.when(k == 0)
            def _():
                acc[...] = r

            @pl.when(k > 0)
            def _():
                acc[...] += r

            @pl.when(k == n_acc - 1)
            def _():
                o_ref[...] = finish(acc[...], e_ref).astype(o_ref.dtype)

        if carry is not None:
            @pl.when(step == n_steps - 1)
            def _():
                if carry[0] == "gather":
                    direct, forwards, from_sibling = _gather_copies(src_ref, land_ref, sems[0], sems[1])
                    for cp in from_sibling:
                        cp.wait_recv()
                    for cp in direct + forwards:
                        cp.wait_send()
                    pltpu.make_async_copy(src_ref, land_ref.at[_peer(0)[1]], sems[2]).wait()
                else:
                    for cp in _exchange_copies(carry[2], src_ref, land_ref, sems[0], sems[1]):
                        cp.wait()

    scratch = [] if acc_axis is None else [pltpu.VMEM(acc_shape, F32)]
    if cache is not None:
        block = tuple(s for s in (a_spec if cache[0] == "a" else b_spec).block_shape if s is not None)
        scratch.append(pltpu.VMEM(block if cache[1] == "inner" else (grid[1],) + block, BF16))
    in_specs, operands = [a_spec, b_spec], [a, b]
    if post is not None:
        in_specs.append(post[1])
        operands.append(post[0])
    if carry is None:
        sem = tuple("arbitrary" if i == acc_axis or cache is not None else "parallel" for i in range(len(grid)))
        return pl.pallas_call(body, name=name, grid=grid, in_specs=in_specs, out_specs=out_spec,
                              out_shape=out_shape, scratch_shapes=scratch, compiler_params=_params(sem))(*operands)
    kind, src = carry[:2]
    n_land = N_DEV if kind == "gather" else len(carry[2])
    n_sem = N_DEV - 1 if kind == "gather" else n_land
    land = jax.ShapeDtypeStruct((n_land,) + src.shape[-2:], src.dtype)
    hbm = pl.BlockSpec(memory_space=pl.ANY)
    scratch += [pltpu.SemaphoreType.DMA((n_sem,)), pltpu.SemaphoreType.DMA((n_sem,)), pltpu.SemaphoreType.DMA(())]
    return pl.pallas_call(body, name=name + "_" + kind, grid=grid, in_specs=in_specs + [hbm],
                          out_specs=[out_spec, hbm], out_shape=[out_shape, land], scratch_shapes=scratch,
                          compiler_params=_params(("arbitrary",) * len(grid)))(*operands, src)


def _tile(n, t):
    if n <= t:
        return n
    for d in range(t - t % 8, 7, -8):
        if n % d == 0:
            return d
    raise ValueError((n, t))


def _col_fwd_call(x, g, out_dtype, carry=None):
    t, k = x.shape
    _, _, ns = g.shape
    tm = _tile(t, ROW_TILE)
    return _mm("col_fwd", x, g, grid=(t // tm, N_DEV),
               a_spec=pl.BlockSpec((tm, k), lambda i, j: (i, 0)),
               b_spec=pl.BlockSpec((None, k, ns), lambda i, j: (j, 0, 0)),
               out_spec=pl.BlockSpec((tm, ns), lambda i, j: (i, j)),
               out_shape=jax.ShapeDtypeStruct((t, N_DEV * ns), out_dtype), dims=NN, carry=carry,
               cache=("a", "inner"))


def _col_dx_call(dy, g, carry=None):
    _, k, ns = g.shape
    t = dy.shape[0]
    tm = _tile(t, ROW_TILE)
    return _mm("col_dx", dy, g, grid=(t // tm, N_DEV),
               a_spec=pl.BlockSpec((tm, ns), lambda i, j: (i, j)),
               b_spec=pl.BlockSpec((None, k, ns), lambda i, j: (j, 0, 0)),
               out_spec=pl.BlockSpec((tm, k), lambda i, j: (i, 0)),
               out_shape=jax.ShapeDtypeStruct((t, k), F32), dims=NT, acc_axis=1, acc_shape=(tm, k), carry=carry)


def _col_dw_call(x, dy, ns, carry=None):
    t, k = x.shape
    tt = _tile(t, ROW_TILE)
    last = t // tt - 1
    return _mm("col_dw", x, dy, grid=(N_DEV, t // tt), cache=("a", "outer"),
               a_spec=pl.BlockSpec((tt, k), lambda j, s: (jnp.where(j == 0, s, last), 0)),
               b_spec=pl.BlockSpec((tt, ns), lambda j, s: (s, j)),
               out_spec=pl.BlockSpec((None, k, ns), lambda j, s: (j, 0, 0)),
               out_shape=jax.ShapeDtypeStruct((N_DEV, k, ns), BF16), dims=TN, acc_axis=1, acc_shape=(k, ns),
               carry=carry)


def _make_linear_col(out_dtype):
    @jax.custom_vjp
    def linear_col(x, g):
        return _col_fwd_call(x, g, out_dtype)

    def fwd(x, g):
        return _col_fwd_call(x, g, out_dtype), (x, g)

    def bwd(res, dy):
        x, g = res
        return _col_dx_call(dy, g), _col_dw_call(x, dy, g.shape[2])

    linear_col.defvjp(fwd, bwd)

    @jax.custom_vjp
    def linear_col_next(x, g, nxt):
        return next_fwd(x, g, nxt)[0]

    def next_fwd(x, g, nxt):
        y, g_next = _col_fwd_call(x, g, out_dtype, carry=("gather", nxt.astype(BF16)))
        return (y, g_next), (x, g)

    def next_bwd(res, cts):
        x, g = res
        dy, dg_next = cts
        dx, near = _col_dx_call(dy, g, carry=("exchange", dg_next, NEAR))
        dg, far = _col_dw_call(x, dy, g.shape[2], carry=("exchange", dg_next, FAR))
        return dx, dg, sum_slots(near, far)

    linear_col_next.defvjp(next_fwd, next_bwd)
    return linear_col, linear_col_next


linear_col, linear_col_next = _make_linear_col(F32)
linear_col_bf16, linear_col_bf16_next = _make_linear_col(BF16)


def _relu2(a):
    return jnp.square(jnp.maximum(a.astype(F32), 0.0))


def _relu2_grad(d_act, a):
    return d_act * (2.0 * jnp.maximum(a.astype(F32), 0.0))


def _lin_fwd_call(x, w, act, carry=None):
    t, k = x.shape
    _, n = w.shape
    tm, tk = _tile(t, ROW_TILE), _tile(k, 1024)
    return _mm("lin_fwd", x, w, grid=(t // tm, k // tk),
               a_spec=pl.BlockSpec((tm, tk), lambda i, s: (i, s)),
               b_spec=pl.BlockSpec((tk, n), lambda i, s: (s, 0)),
               out_spec=pl.BlockSpec((tm, n), lambda i, s: (i, 0)),
               out_shape=jax.ShapeDtypeStruct((t, n), F32), dims=NN, acc_axis=1, acc_shape=(tm, n), carry=carry,
               a_pre=_relu2 if act else None)


def _lin_dx_call(dy, w, x, act, carry=None):
    k, n = w.shape
    t = dy.shape[0]
    tm, tk = _tile(t, 1024), _tile(k, 1024)
    out_spec = pl.BlockSpec((tm, tk), lambda i, s: (i, s))
    return _mm("lin_dx", dy, w, grid=(t // tm, k // tk),
               a_spec=pl.BlockSpec((tm, n), lambda i, s: (i, 0)),
               b_spec=pl.BlockSpec((tk, n), lambda i, s: (s, 0)),
               out_spec=out_spec, out_shape=jax.ShapeDtypeStruct((t, k), x.dtype), dims=NT, carry=carry,
               post=(x, out_spec, _relu2_grad) if act else None, cache=("a", "inner"))


def _lin_dw_call(x, dy, act, carry=None):
    t, k = x.shape
    n = dy.shape[1]
    tk, tt = _tile(k, 1024), _tile(t, ROW_TILE)
    last = t // tt - 1
    return _mm("lin_dw", x, dy, grid=(k // tk, t // tt), cache=("b", "outer"),
               a_spec=pl.BlockSpec((tt, tk), lambda s, r: (r, s)),
               b_spec=pl.BlockSpec((tt, n), lambda s, r: (jnp.where(s == 0, r, last), 0)),
               out_spec=pl.BlockSpec((tk, n), lambda s, r: (s, 0)),
               out_shape=jax.ShapeDtypeStruct((k, n), BF16), dims=TN, acc_axis=1, acc_shape=(tk, n), carry=carry,
               a_pre=_relu2 if act else None)


def _make_linear(act):
    @jax.custom_vjp
    def linear(x, w):
        return _lin_fwd_call(x, w, act)

    def fwd(x, w):
        return _lin_fwd_call(x, w, act), (x, w)

    def bwd(res, dy):
        x, w = res
        return _lin_dx_call(dy, w, x, act), _lin_dw_call(x, dy, act)

    linear.defvjp(fwd, bwd)

    @jax.custom_vjp
    def linear_next(x, w, nxt):
        return next_fwd(x, w, nxt)[0]

    def next_fwd(x, w, nxt):
        y, g_next = _lin_fwd_call(x, w, act, carry=("gather", nxt.astype(BF16)))
        return (y, g_next), (x, w)

    def next_bwd(res, cts):
        x, w = res
        dy, dg_next = cts
        dx, near = _lin_dx_call(dy, w, x, act, carry=("exchange", dg_next, NEAR))
        dw, far = _lin_dw_call(x, dy, act, carry=("exchange", dg_next, FAR))
        return dx, dw, sum_slots(near, far)

    linear_next.defvjp(next_fwd, next_bwd)
    return linear, linear_next


linear, linear_next = _make_linear(False)
linear_relu2, linear_relu2_next = _make_linear(True)


def _row_ts(widths, seq):
    per_row = 4 * sum(widths)
    ts = 512
    while ts > 8 and ts * per_row * 2 > 24 * 1024 * 1024:
        ts //= 2
    return min(ts, seq)


def make_rowop(name, f, n_row, n_batch, n_vec, f_bwd=None):
    n_in = n_row + n_batch + n_vec

    def in_specs(args, ts):
        specs = []
        for a in args[:n_row]:
            specs.append(pl.BlockSpec((None, ts, a.shape[2]), lambda b, s: (b, s, 0)))
        for a in args[n_row:n_row + n_batch]:
            specs.append(pl.BlockSpec((None, 1, a.shape[2]), lambda b, s: (b, 0, 0)))
        for a in args[n_row + n_batch:]:
            specs.append(pl.BlockSpec((1, a.shape[1]), lambda b, s: (0, 0)))
        return specs

    def out_struct(args, ts):
        blocks = [jax.ShapeDtypeStruct((ts, a.shape[2]), F32) for a in args[:n_row]]
        blocks += [jax.ShapeDtypeStruct((1, a.shape[2]), F32) for a in args[n_row:n_row + n_batch]]
        blocks += [jax.ShapeDtypeStruct((1, a.shape[1]), F32) for a in args[n_row + n_batch:]]
        return jax.eval_shape(f, *blocks)

    def run_fwd(args):
        bsz, seq = args[0].shape[:2]
        outs0 = out_struct(args, 8)
        widths = [a.shape[2] for a in args[:n_row]] + [o.shape[1] for o in outs0]
        ts = _row_ts(widths, seq)

        def body(*refs):
            outs = f(*[r[...] for r in refs[:n_in]])
            for o_ref, o in zip(refs[n_in:], outs):
                o_ref[...] = o

        return pl.pallas_call(
            body, name=name + "_fwd", grid=(bsz, seq // ts), in_specs=in_specs(args, ts),
            out_specs=[pl.BlockSpec((None, ts, o.shape[1]), lambda b, s: (b, s, 0)) for o in outs0],
            out_shape=[jax.ShapeDtypeStruct((bsz, seq, o.shape[1]), F32) for o in outs0],
            compiler_params=_params(("parallel", "parallel")))(*args)

    def run_bwd(args, cts):
        bsz, seq = args[0].shape[:2]
        widths = [a.shape[2] for a in args[:n_row]] * 2 + [c.shape[2] for c in cts]
        ts = _row_ts(widths, seq)
        n_ct = len(cts)

        def body(*refs):
            ins = [r[...] for r in refs[:n_in]]
            ct = [r[...] for r in refs[n_in:n_in + n_ct]]
            if f_bwd is None:
                _, vjp = jax.vjp(f, *ins)
                grads = vjp(tuple(ct))
            else:
                grads = f_bwd(ins, ct)
            g_refs = refs[n_in + n_ct:]
            b, s = pl.program_id(0), pl.program_id(1)
            for i in range(n_row):
                g_refs[i][...] = grads[i]
            for i in range(n_row, n_row + n_batch):
                @pl.when(s == 0)
                def _(i=i):
                    g_refs[i][...] = grads[i]

                @pl.when(s > 0)
                def _(i=i):
                    g_refs[i][...] += grads[i]
            for i in range(n_row + n_batch, n_in):
                first = jnp.logical_and(b == 0, s == 0)

                @pl.when(first)
                def _(i=i):
                    g_refs[i][...] = grads[i]

                @pl.when(jnp.logical_not(first))
                def _(i=i):
                    g_refs[i][...] += grads[i]

        ct_specs = [pl.BlockSpec((None, ts, c.shape[2]), lambda b, s: (b, s, 0)) for c in cts]
        return pl.pallas_call(
            body, name=name + "_bwd", grid=(bsz, seq // ts), in_specs=in_specs(args, ts) + ct_specs,
            out_specs=in_specs(args, ts), out_shape=[jax.ShapeDtypeStruct(a.shape, F32) for a in args],
            compiler_params=_params(("arbitrary", "arbitrary")))(*args, *cts)

    @jax.custom_vjp
    def op(*args):
        return tuple(run_fwd(args))

    def fwd(*args):
        return tuple(run_fwd(args)), args

    def bwd(args, cts):
        return tuple(run_bwd(args, list(cts)))

    op.defvjp(fwd, bwd)
    return op


def _modnorm_f(h, scale, shift, g):
    y = h * lax.rsqrt(jnp.mean(h * h, axis=-1, keepdims=True) + EPS) * g
    return (y * (1.0 + scale) + shift,)


def _gate_res_f(h, y, gate):
    return (h + gate * y,)


def _res_modnorm_f(h, y, gate, scale, shift, g):
    h2 = h + gate * y
    return (h2,) + _modnorm_f(h2, scale, shift, g)


def _gelu_skip_f(y, u, d):
    return (jax.nn.gelu(y + d * u),)


def _glu_f(vg):
    n = vg.shape[1] // 2
    return (vg[:, :n] * jax.nn.sigmoid(vg[:, n:]),)


def _glu_b(ins, cts):
    (vg,), (ct,) = ins, cts
    n = vg.shape[1] // 2
    sg = jax.nn.sigmoid(vg[:, n:])
    return (jnp.concatenate([ct * sg, ct * vg[:, :n] * sg * (1.0 - sg)], axis=1),)


def _combine_f(o1, o2, o3, l1, l2, l3):
    m = jnp.maximum(jnp.maximum(l1, l2), l3)
    e1, e2, e3 = jnp.exp(l1 - m), jnp.exp(l2 - m), jnp.exp(l3 - m)
    return ((e1 * o1 + e2 * o2 + e3 * o3) / (e1 + e2 + e3),)


def _final_loss_f(h, target, g):
    y = h * lax.rsqrt(jnp.mean(h * h, axis=-1, keepdims=True) + EPS) * g
    return (0.5 * jnp.mean(jnp.square(y - target), axis=-1, keepdims=True),)


modnorm = make_rowop("modnorm", _modnorm_f, 1, 2, 1)
gate_res = make_rowop("gate_res", _gate_res_f, 2, 1, 0)
res_modnorm = make_rowop("res_modnorm", _res_modnorm_f, 2, 3, 1)
gelu_skip = make_rowop("gelu_skip", _gelu_skip_f, 2, 0, 1)
glu = make_rowop("glu", _glu_f, 1, 0, 0, _glu_b)
combine = make_rowop("combine", _combine_f, 6, 0, 0)
final_loss = make_rowop("final_loss", _final_loss_f, 2, 0, 1)


HALF = 512


SSM_TB = 64
PITCH = SSM_TB + 8
PIECES = 8
CH = 128
NCH = 2 * HALF // LANES
NRE = NCH // 2


def _slab(t):
    return pl.ds(t, PIECES, stride=PITCH)


def _put_rows(ref, b, q, val, tb):
    for j in range(NCH):
        ref[b, j, pl.ds(q * PITCH, tb), :] = val[:, j * LANES:(j + 1) * LANES]


def _get_rows(ref, q, bsz, tb):
    return jnp.concatenate(
        [jnp.concatenate([ref[b, j, pl.ds(q * PITCH, tb), :] for j in range(NCH)], axis=1) for b in range(bsz)], axis=0)


def _chunks(a):
    return [a[:, j * LANES:(j + 1) * LANES] for j in range(a.shape[1] // LANES)]


def _ssm_fwd_call(u, b_mat, c_mat, a_re, a_im):
    bsz, seq, d = u.shape
    tb = SSM_TB
    nt = seq // tb
    assert seq % tb == 0 and d == PIECES * CH

    def body(u_ref, b_ref, c_ref, ar_ref, ai_ref, y_ref, st_ref, end_ref, bu, carry):
        @pl.when(pl.program_id(0) == 0)
        def _():
            carry[...] = jnp.zeros_like(carry)

        u2 = u_ref[...].reshape(bsz * tb, d).astype(BF16)
        for q in range(PIECES):
            r = lax.dot_general(u2[:, q * CH:(q + 1) * CH], b_ref[q], NN, preferred_element_type=F32)
            for b in range(bsz):
                _put_rows(bu, b, q, r[b * tb:(b + 1) * tb], tb)
        ar, ai = _chunks(ar_ref[...]), _chunks(ai_ref[...])
        st = [[carry[b, j] for j in range(NCH)] for b in range(bsz)]
        for t in range(tb):
            for b in range(bsz):
                for j in range(NRE):
                    sre, sim = st[b][j], st[b][NRE + j]
                    nre = ar[j] * sre - ai[j] * sim + bu[b, j, _slab(t), :]
                    nim = ar[j] * sim + ai[j] * sre + bu[b, NRE + j, _slab(t), :]
                    st_ref[b, j, _slab(t), :] = nre
                    st_ref[b, NRE + j, _slab(t), :] = nim
                    st[b][j], st[b][NRE + j] = nre, nim
        for b in range(bsz):
            for j in range(NCH):
                carry[b, j] = st[b][j]
                end_ref[b, j] = st[b][j]
        for q in range(PIECES):
            s2 = _get_rows(st_ref, q, bsz, tb).astype(BF16)
            r = lax.dot_general(s2, c_ref[q], NN, preferred_element_type=F32)
            for b in range(bsz):
                y_ref[b, :, q * CH:(q + 1) * CH] = r[b * tb:(b + 1) * tb]

    rows = pl.BlockSpec((bsz, tb, d), lambda i: (0, i, 0))
    vec = pl.BlockSpec((PIECES, HALF), lambda i: (0, 0))
    return pl.pallas_call(
        body, name="s5_ssm_fwd", grid=(nt,),
        in_specs=[rows, pl.BlockSpec((PIECES, CH, 2 * HALF), lambda i: (0, 0, 0)),
                  pl.BlockSpec((PIECES, 2 * HALF, CH), lambda i: (0, 0, 0)), vec, vec],
        out_specs=[rows, pl.BlockSpec((bsz, None, NCH, PIECES * PITCH, LANES), lambda i: (0, i, 0, 0, 0)),
                   pl.BlockSpec((bsz, None, NCH, PIECES, LANES), lambda i: (0, i, 0, 0, 0))],
        out_shape=[jax.ShapeDtypeStruct((bsz, seq, d), F32),
                   jax.ShapeDtypeStruct((bsz, nt, NCH, PIECES * PITCH, LANES), F32),
                   jax.ShapeDtypeStruct((bsz, nt, NCH, PIECES, LANES), F32)],
        scratch_shapes=[pltpu.VMEM((bsz, NCH, PIECES * PITCH, LANES), F32), pltpu.VMEM((bsz, NCH, PIECES, LANES), F32)],
        compiler_params=_params(("arbitrary",)))(u, b_mat, c_mat, a_re, a_im)


def _ssm_bwd_call(dy, u, st, ends, b_mat, c_mat, a_re, a_im):
    bsz, seq, d = u.shape
    tb = SSM_TB
    nt = seq // tb

    def body(dy_ref, u_ref, st_ref, prev_ref, b_ref, c_ref, ar_ref, ai_ref, du_ref, db_hbm, dc_hbm, dar_ref, dai_ref,
             lam, carry, db_acc, dc_acc):
        i = pl.program_id(0)

        @pl.when(i == 0)
        def _():
            carry[...] = jnp.zeros_like(carry)
            db_acc[...] = jnp.zeros_like(db_acc)
            dc_acc[...] = jnp.zeros_like(dc_acc)
            dar_ref[...] = jnp.zeros_like(dar_ref)
            dai_ref[...] = jnp.zeros_like(dai_ref)

        dy2 = dy_ref[...].reshape(bsz * tb, d).astype(BF16)
        u2 = u_ref[...].reshape(bsz * tb, d).astype(BF16)
        for q in range(PIECES):
            dyq = dy2[:, q * CH:(q + 1) * CH]
            g = lax.dot_general(dyq, c_ref[q], NT, preferred_element_type=F32)
            for b in range(bsz):
                _put_rows(lam, b, q, g[b * tb:(b + 1) * tb], tb)
            s2 = _get_rows(st_ref, q, bsz, tb).astype(BF16)
            dc_acc[q] += lax.dot_general(s2, dyq, TN, preferred_element_type=F32)

        ar, ai = _chunks(ar_ref[...]), _chunks(ai_ref[...])
        keep = jnp.where(i == nt - 1, 0.0, 1.0)
        lm = [[carry[b, j] for j in range(NCH)] for b in range(bsz)]
        dar = [jnp.zeros((PIECES, LANES), F32) for _ in range(NRE)]
        dai = [jnp.zeros((PIECES, LANES), F32) for _ in range(NRE)]
        for t in range(tb - 1, -1, -1):
            for b in range(bsz):
                for j in range(NRE):
                    lre, lim = lm[b][j], lm[b][NRE + j]
                    nre = lam[b, j, _slab(t), :] + ar[j] * lre + ai[j] * lim
                    nim = lam[b, NRE + j, _slab(t), :] - ai[j] * lre + ar[j] * lim
                    lam[b, j, _slab(t), :] = nre
                    lam[b, NRE + j, _slab(t), :] = nim
                    if t > 0:
                        pre, pim = st_ref[b, j, _slab(t - 1), :], st_ref[b, NRE + j, _slab(t - 1), :]
                    else:
                        pre, pim = prev_ref[b, j] * keep, prev_ref[b, NRE + j] * keep
                    dar[j] = dar[j] + nre * pre + nim * pim
                    dai[j] = dai[j] + nim * pre - nre * pim
                    lm[b][j], lm[b][NRE + j] = nre, nim
        for b in range(bsz):
            for j in range(NCH):
                carry[b, j] = lm[b][j]
        dar_ref[...] += jnp.concatenate(dar, axis=1)
        dai_ref[...] += jnp.concatenate(dai, axis=1)

        for q in range(PIECES):
            l2 = _get_rows(lam, q, bsz, tb).astype(BF16)
            r = lax.dot_general(l2, b_ref[q], NT, preferred_element_type=F32)
            for b in range(bsz):
                du_ref[b, :, q * CH:(q + 1) * CH] = r[b * tb:(b + 1) * tb]
            db_acc[q] += lax.dot_general(u2[:, q * CH:(q + 1) * CH], l2, TN, preferred_element_type=F32)

        @pl.when(i == nt - 1)
        def _():
            pltpu.sync_copy(db_acc, db_hbm)
            pltpu.sync_copy(dc_acc, dc_hbm)

    rows = pl.BlockSpec((bsz, tb, d), lambda i: (0, nt - 1 - i, 0))
    vec = pl.BlockSpec((PIECES, HALF), lambda i: (0, 0))
    hbm = pl.BlockSpec(memory_space=pl.ANY)
    return pl.pallas_call(
        body, name="s5_ssm_bwd", grid=(nt,),
        in_specs=[rows, rows,
                  pl.BlockSpec((bsz, None, NCH, PIECES * PITCH, LANES), lambda i: (0, nt - 1 - i, 0, 0, 0)),
                  pl.BlockSpec((bsz, None, NCH, PIECES, LANES), lambda i: (0, jnp.maximum(nt - 2 - i, 0), 0, 0, 0)),
                  pl.BlockSpec((PIECES, CH, 2 * HALF), lambda i: (0, 0, 0)),
                  pl.BlockSpec((PIECES, 2 * HALF, CH), lambda i: (0, 0, 0)), vec, vec],
        out_specs=[rows, hbm, hbm, vec, vec],
        out_shape=[jax.ShapeDtypeStruct((bsz, seq, d), F32), jax.ShapeDtypeStruct((PIECES, CH, 2 * HALF), F32),
                   jax.ShapeDtypeStruct((PIECES, 2 * HALF, CH), F32), jax.ShapeDtypeStruct((PIECES, HALF), F32),
                   jax.ShapeDtypeStruct((PIECES, HALF), F32)],
        scratch_shapes=[pltpu.VMEM((bsz, NCH, PIECES * PITCH, LANES), F32), pltpu.VMEM((bsz, NCH, PIECES, LANES), F32),
                        pltpu.VMEM((PIECES, CH, 2 * HALF), F32), pltpu.VMEM((PIECES, 2 * HALF, CH), F32)],
        compiler_params=_params(("arbitrary",)))(dy, u, st, ends, b_mat, c_mat, a_re, a_im)


@jax.custom_vjp
def s5_ssm(u, b_mat, c_mat, a_re, a_im):
    return _s5_ssm_fwd(u, b_mat, c_mat, a_re, a_im)[0]


def _s5_ssm_fwd(u, b_mat, c_mat, a_re, a_im):
    b16, c16 = b_mat.astype(BF16), c_mat.astype(BF16)
    y, st, ends = _ssm_fwd_call(u, b16, c16, a_re, a_im)
    return y, (u, st, ends, b16, c16, a_re, a_im)


def _s5_ssm_bwd(res, dy):
    return tuple(_ssm_bwd_call(dy, *res))


s5_ssm.defvjp(_s5_ssm_fwd, _s5_ssm_bwd)


ATT_HW = 2 * HEAD
ATT_HB = 1024 // ATT_HW


def _branch_geometry(dil, seq):
    sub = seq // dil
    assert sub % QBLK == 0
    return sub // QBLK


def _drows(dil, r, start, size):
    if dil == 1:
        return pl.ds(start, size)
    return pl.ds(r + start * dil, size, stride=dil)


def _masks():
    qi = lax.broadcasted_iota(jnp.int32, (QBLK, 2 * QBLK), 0)
    kj = lax.broadcasted_iota(jnp.int32, (QBLK, 2 * QBLK), 1) - QBLK
    dist = qi - kj
    band = jnp.logical_and(dist >= 0, dist <= QBLK)
    ci = lax.broadcasted_iota(jnp.int32, (QBLK, QBLK), 0)
    cj = lax.broadcasted_iota(jnp.int32, (QBLK, QBLK), 1)
    return band, ci >= cj


MAX_STRIDE = 4


class _Rows:
    def __init__(self, ref, dil, stage=None):
        self.ref, self.dil, self.stage = ref, dil, stage

    def _at(self, r, start, size):
        if self.stage is None:
            return self.ref, _drows(self.dil, r, start, size)
        step = self.dil // MAX_STRIDE
        return self.stage.at[r % MAX_STRIDE], pl.ds(r // MAX_STRIDE + start * step, size, stride=step)

    def get(self, r, start, size):
        ref, rows = self._at(r, start, size)
        return ref[rows, :]

    def put(self, r, start, size, val):
        ref, rows = self._at(r, start, size)
        ref[rows, :] = val

    def _parts(self):
        n = self.ref.shape[0] // MAX_STRIDE
        return [(a, pl.ds(a, n, stride=MAX_STRIDE)) for a in range(MAX_STRIDE)]

    def fill(self):
        if self.stage is not None:
            for a, rows in self._parts():
                self.stage[a] = self.ref[rows, :]

    def flush(self):
        if self.stage is not None:
            for a, rows in self._parts():
                self.ref[rows, :] = self.stage[a]


def _stage_shapes(dil, seq, n):
    return [pltpu.VMEM((MAX_STRIDE, seq // MAX_STRIDE, ATT_HW), F32)] * n if dil > MAX_STRIDE else []


def _widen_first(k2, v2, mask):
    wide = lambda a: jnp.concatenate([a, a], axis=0)
    return wide(k2), wide(v2), jnp.concatenate([mask, jnp.zeros_like(mask)], axis=1)


def _head_lanes():
    lane = lax.broadcasted_iota(jnp.int32, (QBLK, ATT_HW), 1)
    return [jnp.logical_and(lane >= h * HEAD, lane < (h + 1) * HEAD) for h in range(ATT_HW // HEAD)]


def _attn_specs(i, seq):
    q_spec = pl.BlockSpec((None, seq, ATT_HW), lambda b, h: (b, 0, i * ATT_HB + h))
    k_spec = pl.BlockSpec((None, seq, ATT_HW), lambda b, h: (b, 0, i * ATT_HB + h))
    v_spec = pl.BlockSpec((None, seq, ATT_HW), lambda b, h: (b, 0, (3 + i) * ATT_HB + h))
    o_spec = pl.BlockSpec((None, seq, ATT_HW), lambda b, h: (b, 0, h))
    return q_spec, k_spec, v_spec, o_spec


def _attn_fwd_call(q_all, kv_all, i, dil):
    bsz, seq, _ = q_all.shape
    nb = _branch_geometry(dil, seq)
    scale = HEAD ** -0.5

    def body(*refs):
        stages = refs[5:] if dil > MAX_STRIDE else (None,) * 5
        q_rows, k_rows, v_rows, o_rows, l_rows = (_Rows(ref, dil, st) for ref, st in zip(refs[:5], stages))
        band, causal = _masks()
        head_lanes = _head_lanes()
        for rows in (q_rows, k_rows, v_rows):
            rows.fill()

        def one(r, n, mask):
            k0, kn = (0, QBLK) if n == 0 else ((n - 1) * QBLK, 2 * QBLK)
            q2 = q_rows.get(r, n * QBLK, QBLK).astype(BF16)
            k2 = k_rows.get(r, k0, kn).astype(BF16)
            v2 = v_rows.get(r, k0, kn).astype(BF16)
            if n == 0:
                k2, v2, mask = _widen_first(k2, v2, mask)
            o, lse = None, None
            for h in range(ATT_HW // HEAD):
                s = lax.dot_general(jnp.where(head_lanes[h], q2, 0), k2, NT, preferred_element_type=F32) * scale
                s = jnp.where(mask, s, NEG)
                m = jnp.max(s, axis=-1, keepdims=True)
                p = jnp.exp(s - m)
                den = jnp.sum(p, axis=-1, keepdims=True)
                o_h = lax.dot_general(p.astype(BF16), v2, NN, preferred_element_type=F32) / den
                lse_h = jnp.broadcast_to(m + jnp.log(den), (QBLK, ATT_HW))
                o = o_h if o is None else jnp.where(head_lanes[h], o_h, o)
                lse = lse_h if lse is None else jnp.where(head_lanes[h], lse_h, lse)
            o_rows.put(r, n * QBLK, QBLK, o)
            l_rows.put(r, n * QBLK, QBLK, lse)

        for r in range(dil):
            for n in range(nb):
                one(r, n, causal if n == 0 else band)
        o_rows.flush()
        l_rows.flush()

    q_spec, k_spec, v_spec, o_spec = _attn_specs(i, seq)
    shape = jax.ShapeDtypeStruct((bsz, seq, 1024), F32)
    return pl.pallas_call(
        body, name="attn_fwd_d%d" % dil, grid=(bsz, ATT_HB), in_specs=[q_spec, k_spec, v_spec],
        out_specs=[o_spec, o_spec], out_shape=[shape, shape], scratch_shapes=_stage_shapes(dil, seq, 5),
        compiler_params=_params(("parallel", "parallel")))(q_all, kv_all, kv_all)


def _attn_bwd_call(q_all, kv_all, o, l, do, dl, i, dil, grads):
    bsz, seq, _ = q_all.shape
    nb = _branch_geometry(dil, seq)
    scale = HEAD ** -0.5
    first = grads is None

    def body(*refs):
        n_pass = 0 if first else 3
        blocks = list(refs[:7]) + list(refs[7 + n_pass:10 + n_pass])
        stages = refs[10 + n_pass:] if dil > MAX_STRIDE else (None,) * 10
        q_rows, k_rows, v_rows, o_rows, l_rows, do_rows, dl_rows, dq_rows, dk_rows, dv_rows = (
            _Rows(ref, dil, st) for ref, st in zip(blocks, stages))
        band, causal = _masks()
        head_lanes = _head_lanes()
        for rows in (q_rows, k_rows, v_rows, o_rows, l_rows, do_rows, dl_rows):
            rows.fill()

        def one(r, n, mask):
            q0 = n * QBLK
            k0, kn = (0, QBLK) if n == 0 else ((n - 1) * QBLK, 2 * QBLK)
            q2 = q_rows.get(r, q0, QBLK).astype(BF16)
            k2 = k_rows.get(r, k0, kn).astype(BF16)
            v2 = v_rows.get(r, k0, kn).astype(BF16)
            l2, do2 = l_rows.get(r, q0, QBLK), do_rows.get(r, q0, QBLK)
            t2 = dl_rows.get(r, q0, QBLK) - do2 * o_rows.get(r, q0, QBLK)
            do2 = do2.astype(BF16)
            dq, dk, dv = None, None, None
            for h in range(ATT_HW // HEAD):
                q = jnp.where(head_lanes[h], q2, 0)
                d_ob = jnp.where(head_lanes[h], do2, 0)
                s = lax.dot_general(q, k2, NT, preferred_element_type=F32) * scale
                s = jnp.where(mask, s, NEG)
                lse = jnp.max(jnp.where(head_lanes[h], l2, NEG), axis=-1, keepdims=True)
                p = jnp.exp(s - lse)
                row = jnp.sum(jnp.where(head_lanes[h], t2, 0.0), axis=-1, keepdims=True)
                dp = lax.dot_general(d_ob, v2, NT, preferred_element_type=F32)
                ds = (p * (dp + row)).astype(BF16)
                dq_h = lax.dot_general(ds, k2, NN, preferred_element_type=F32) * scale
                dk_h = lax.dot_general(ds, q, TN, preferred_element_type=F32) * scale
                dv_h = lax.dot_general(p.astype(BF16), d_ob, TN, preferred_element_type=F32)
                dq = dq_h if dq is None else jnp.where(head_lanes[h], dq_h, dq)
                dk = dk_h if dk is None else dk + dk_h
                dv = dv_h if dv is None else dv + dv_h
            dq_rows.put(r, q0, QBLK, dq)
            return dk, dv

        for r in range(dil):
            dk_cur, dv_cur = one(r, 0, causal)
            for n in range(1, nb):
                dk, dv = one(r, n, band)
                dk_rows.put(r, (n - 1) * QBLK, QBLK, dk_cur + dk[:QBLK])
                dv_rows.put(r, (n - 1) * QBLK, QBLK, dv_cur + dv[:QBLK])
                dk_cur, dv_cur = dk[QBLK:], dv[QBLK:]
            dk_rows.put(r, (nb - 1) * QBLK, QBLK, dk_cur)
            dv_rows.put(r, (nb - 1) * QBLK, QBLK, dv_cur)
        for rows in (dq_rows, dk_rows, dv_rows):
            rows.flush()

    q_spec, k_spec, v_spec, o_spec = _attn_specs(i, seq)
    in_specs = [q_spec, k_spec, v_spec, o_spec, o_spec, o_spec, o_spec]
    args = [q_all, kv_all, kv_all, o, l, do, dl]
    aliases = {}
    if not first:
        in_specs += [pl.BlockSpec(memory_space=pl.ANY)] * 3
        args += list(grads)
        aliases = {7: 0, 8: 1, 9: 2}
    shape = jax.ShapeDtypeStruct((bsz, seq, 3 * 1024), F32)
    return pl.pallas_call(
        body, name="attn_bwd_d%d" % dil, grid=(bsz, ATT_HB), in_specs=in_specs,
        out_specs=[q_spec, q_spec, q_spec], out_shape=[shape, shape, shape], input_output_aliases=aliases,
        scratch_shapes=_stage_shapes(dil, seq, 10), compiler_params=_params(("parallel", "parallel")))(*args)


@jax.custom_vjp
def attn_branches(q_all, kv_all):
    return _attn_branches_fwd(q_all, kv_all)[0]


def _attn_branches_fwd(q_all, kv_all):
    outs = []
    for i, dil in enumerate(BRANCH_DIL):
        outs += list(_attn_fwd_call(q_all, kv_all, i, dil))
    return tuple(outs), (q_all, kv_all, tuple(outs))


def _attn_branches_bwd(res, cts):
    q_all, kv_all, outs = res
    grads = None
    for i, dil in enumerate(BRANCH_DIL):
        grads = _attn_bwd_call(q_all, kv_all, outs[2 * i], outs[2 * i + 1], cts[2 * i], cts[2 * i + 1], i, dil, grads)
    dq, dk, dv = grads
    return dq, jnp.concatenate([dk, dv], axis=-1)


attn_branches.defvjp(_attn_branches_fwd, _attn_branches_bwd)


def ada_fwd(c_all, w, b):
    n, d, cs = w.shape
    nb = c_all.shape[0]

    def body(c_ref, w_ref, b_ref, o_ref):
        a = jax.nn.silu(c_ref[...]).astype(BF16)
        o_ref[...] = lax.dot_general(a, w_ref[...].astype(BF16), NN, preferred_element_type=F32) + b_ref[...]

    return pl.pallas_call(
        body, name="ada_fwd", grid=(n,),
        in_specs=[pl.BlockSpec((nb, d), lambda i: (0, 0)), pl.BlockSpec((None, d, cs), lambda i: (i, 0, 0)),
                  pl.BlockSpec((None, 1, cs), lambda i: (i, 0, 0))],
        out_specs=pl.BlockSpec((None, nb, cs), lambda i: (i, 0, 0)),
        out_shape=jax.ShapeDtypeStruct((n, nb, cs), F32), compiler_params=_params(("parallel",)))(c_all, w, b)


def ada_bwd(c_all, dm):
    n, nb, cs = dm.shape
    d = c_all.shape[1]

    def body(c_ref, dm_ref, dw_ref, db_ref):
        a = jax.nn.silu(c_ref[...]).astype(BF16)
        g = dm_ref[...]
        dw_ref[...] = lax.dot_general(a, g.astype(BF16), TN, preferred_element_type=F32)
        db_ref[...] = jnp.sum(g, axis=0, keepdims=True)

    return pl.pallas_call(
        body, name="ada_bwd", grid=(n,),
        in_specs=[pl.BlockSpec((nb, d), lambda i: (0, 0)), pl.BlockSpec((None, nb, cs), lambda i: (i, 0, 0))],
        out_specs=[pl.BlockSpec((None, d, cs), lambda i: (i, 0, 0)), pl.BlockSpec((None, 1, cs), lambda i: (i, 0, 0))],
        out_shape=[jax.ShapeDtypeStruct((n, d, cs), F32), jax.ShapeDtypeStruct((n, 1, cs), F32)],
        compiler_params=_params(("parallel",)))(c_all, dm)


def all_gather(x, name):
    m, n = x.shape

    def body(x_ref, out_ref, send_sems, recv_sems, local_sem):
        px, py, pc = lax.axis_index("x"), lax.axis_index("y"), lax.axis_index("c")
        me, sibling = (px, py, pc), (px, py, 1 - pc)
        chips = [(1 - px, py), (px, 1 - py), (1 - px, 1 - py)]

        def rows(qx, qy, qc):
            return out_ref.at[pl.ds((4 * qx + 2 * qy + qc) * m, m), :]

        def copy(k, block, to, src=None):
            return pltpu.make_async_remote_copy(
                src_ref=rows(*block) if src is None else src, dst_ref=rows(*block),
                send_sem=send_sems.at[k], recv_sem=recv_sems.at[k], device_id=to,
                device_id_type=pl.DeviceIdType.MESH)

        mine = pltpu.make_async_copy(x_ref, rows(*me), local_sem)
        mine.start()
        first = [copy(0, me, sibling, src=x_ref)]
        first += [copy(1 + j, me, (*chip, pc), src=x_ref) for j, chip in enumerate(chips)]
        for cp in first:
            cp.start()
        passed = [copy(4 + j, (*chip, pc), sibling) for j, chip in enumerate(chips)]
        for j, chip in enumerate(chips):
            copy(1 + j, (*chip, pc), me).wait_recv()
            passed[j].start()
        copy(0, sibling, me).wait_recv()
        for j, chip in enumerate(chips):
            copy(4 + j, (*chip, 1 - pc), me).wait_recv()
        for cp in first + passed:
            cp.wait_send()
        mine.wait()

    out = pl.pallas_call(
        body, name=name, out_shape=jax.ShapeDtypeStruct((N_DEV * m, n), x.dtype),
        in_specs=[pl.BlockSpec(memory_space=pl.ANY)], out_specs=pl.BlockSpec(memory_space=pl.ANY),
        scratch_shapes=[pltpu.SemaphoreType.DMA((7,)), pltpu.SemaphoreType.DMA((7,)), pltpu.SemaphoreType.DMA(())],
    )(x)
    return out.reshape(N_DEV, m, n)


def exchange_partials(p, name):
    _, m, n = p.shape

    def body(p_ref, out_ref, send_sems, recv_sems, local_sem):
        px, py, pc = lax.axis_index("x"), lax.axis_index("y"), lax.axis_index("c")
        me = 4 * px + 2 * py + pc
        mine = pltpu.make_async_copy(p_ref.at[me], out_ref.at[me], local_sem)
        mine.start()
        copies = []
        for k in range(1, N_DEV):
            qx = 1 - px if k & 4 else px
            qy = 1 - py if k & 2 else py
            qc = 1 - pc if k & 1 else pc
            cp = pltpu.make_async_remote_copy(
                src_ref=p_ref.at[4 * qx + 2 * qy + qc], dst_ref=out_ref.at[me],
                send_sem=send_sems.at[k - 1], recv_sem=recv_sems.at[k - 1], device_id=(qx, qy, qc),
                device_id_type=pl.DeviceIdType.MESH)
            cp.start()
            copies.append(cp)
        for cp in copies:
            cp.wait()
        mine.wait()

    return pl.pallas_call(
        body, name=name, out_shape=jax.ShapeDtypeStruct(p.shape, p.dtype),
        in_specs=[pl.BlockSpec(memory_space=pl.ANY)], out_specs=pl.BlockSpec(memory_space=pl.ANY),
        scratch_shapes=[pltpu.SemaphoreType.DMA((7,)), pltpu.SemaphoreType.DMA((7,)), pltpu.SemaphoreType.DMA(())],
    )(p)


def sum_slots(*parts):
    _, r, n = parts[0].shape
    tr = _tile(r, max(16, (256 * 1024) // n))

    def body(*refs):
        acc = None
        for g_ref in refs[:-1]:
            for s in range(g_ref.shape[0]):
                term = g_ref[s].astype(F32)
                acc = term if acc is None else acc + term
        refs[-1][...] = acc

    return pl.pallas_call(
        body, name="sum_slots", grid=(r // tr,),
        in_specs=[pl.BlockSpec((p.shape[0], tr, n), lambda i: (0, i, 0)) for p in parts],
        out_specs=pl.BlockSpec((tr, n), lambda i: (i, 0)), out_shape=jax.ShapeDtypeStruct((r, n), F32),
        compiler_params=_params(("parallel",)))(*parts)


def adamw(w, m, v, g, name):
    nl, r, n = w.shape
    partials = g.ndim == 4
    tr = _tile(r, max(8, (256 * 1024) // n))
    c1 = 1.0 / (1.0 - ADAM_B1 ** ADAM_STEP)
    c2 = 1.0 / (1.0 - ADAM_B2 ** ADAM_STEP)

    def body(w_ref, m_ref, v_ref, g_ref, go_ref, d_ref, mo_ref, vo_ref):
        if partials:
            grad = g_ref[0].astype(F32)
            for s in range(1, N_DEV):
                grad = grad + g_ref[s].astype(F32)
        else:
            grad = g_ref[...]
        m_new = ADAM_B1 * m_ref[...] + (1.0 - ADAM_B1) * grad
        v_new = ADAM_B2 * v_ref[...] + (1.0 - ADAM_B2) * grad * grad
        go_ref[...] = grad
        mo_ref[...] = m_new
        vo_ref[...] = v_new
        d_ref[...] = -ADAM_LR * ((m_new * c1) / (jnp.sqrt(v_new * c2) + ADAM_EPS) + ADAM_WD * w_ref[...])

    spec = pl.BlockSpec((None, tr, n), lambda l, i: (l, i, 0))
    g_spec = pl.BlockSpec((None, N_DEV, tr, n), lambda l, i: (l, 0, i, 0)) if partials else spec
    shape = jax.ShapeDtypeStruct(w.shape, F32)
    return pl.pallas_call(
        body, name=name, grid=(nl, r // tr), in_specs=[spec, spec, spec, g_spec], out_specs=[spec] * 4,
        out_shape=[shape] * 4, compiler_params=_params(("parallel", "parallel")))(w, m, v, g)


def pack(arrays):
    flat = jnp.concatenate([a.reshape(-1).astype(F32) for a in arrays])
    rows = -(-flat.shape[0] // LANES)
    mult = 8 if rows <= 512 else 512
    rows = -(-rows // mult) * mult
    flat = jnp.pad(flat, (0, rows * LANES - flat.shape[0]))
    return flat.reshape(rows, LANES)


def unpack(slab, shapes):
    flat = slab.reshape(slab.shape[:-2] + (-1,))
    out, off = [], 0
    for s in shapes:
        size = math.prod(s)
        out.append(flat[..., off:off + size].reshape(flat.shape[:-1] + tuple(s)))
        off += size
    return out


def _s5_discretize(lam_re, lam_im, log_dt, b_re, b_im, c_re, c_im):
    dt = jnp.exp(log_dt)[:, None]
    xr, th = lam_re * dt, lam_im * dt
    er = jnp.exp(xr)
    a_re, a_im = er * jnp.cos(th), er * jnp.sin(th)
    am1 = jnp.expm1(xr) * jnp.cos(th) - 2.0 * jnp.square(jnp.sin(0.5 * th))
    den = lam_re * lam_re + lam_im * lam_im
    fr = (am1 * lam_re + a_im * lam_im) / den
    fi = (a_im * lam_re - am1 * lam_im) / den
    bb_re = fr[..., None] * b_re - fi[..., None] * b_im
    bb_im = fr[..., None] * b_im + fi[..., None] * b_re
    eye = jnp.eye(8, dtype=F32)

    def pack_b(bb):
        return jnp.einsum('qgpc,gh->qgchp', bb.reshape(8, 8, 64, 16), eye).reshape(8, 128, 512)

    def pack_c(cc):
        return jnp.einsum('qgcp,gh->qgphc', cc.reshape(8, 8, 16, 64), eye).reshape(8, 512, 128)

    b_mat = jnp.concatenate([pack_b(bb_re), pack_b(bb_im)], axis=-1)
    c_mat = jnp.concatenate([pack_c(c_re), pack_c(-c_im)], axis=1)
    return a_re.reshape(8, HALF), a_im.reshape(8, HALF), b_mat, c_mat


WEIGHT_ORDER = (("ssm_w_glu", 0, True), ("mlp_w1", 0, True), ("mlp_w2", 0, False),
                ("ssm_w_glu", 1, True), ("mlp_w1", 1, True), ("mlp_w2", 1, False), ("w_kv", None, True),
                ("attn_w_q", 0, True), ("attn_w_o", 0, False), ("mlp_w1", 2, True), ("mlp_w2", 2, False),
                ("attn_w_q", 1, True), ("attn_w_o", 1, False), ("mlp_w1", 3, True), ("mlp_w2", 3, False))


class _WeightChain:
    def __init__(self, first, upcoming):
        self.cur, self.upcoming, self.used = first, list(upcoming), 0

    def matmul(self, x, name, relu2_input=False, bf16_output=False):
        assert WEIGHT_ORDER[self.used][0] == name
        col = WEIGHT_ORDER[self.used][2]
        assert not (col and relu2_input) and (col or not bf16_output)
        self.used += 1
        w = self.cur if col else self.cur.reshape(-1, self.cur.shape[-1])
        if col:
            plain, with_next = (linear_col_bf16, linear_col_bf16_next) if bf16_output else (linear_col, linear_col_next)
        else:
            plain, with_next = (linear_relu2, linear_relu2_next) if relu2_input else (linear, linear_next)
        if not self.upcoming:
            return plain(x, w)
        y, self.cur = with_next(x, w, self.upcoming.pop(0))
        return y


def _local_loss(diff, x, target, first_weight, shards):
    bsz, seq, d = x.shape
    t = bsz * seq
    rows3 = lambda a: a.reshape(bsz, seq, a.shape[-1])
    rows2 = lambda a: a.reshape(t, a.shape[-1])
    mods, kvmod, ln_g, ssm_d, kv_g, final_g = (diff[k] for k in ("mods", "kvmod", "ln_g", "ssm_d", "kv_g", "final_g"))
    chain = _WeightChain(first_weight, shards)

    def chunks(sub):
        mod = mods[sub]
        return mod[:, None, :d], mod[:, None, d:2 * d], mod[:, None, 2 * d:], ln_g[sub // 2, sub % 2][None]

    h = x
    kv_all = None
    shift, scale, gate, gain = chunks(0)
    (u,) = modnorm(h, scale, shift, gain)
    for layer in range(4):
        if layer == 2:
            (ukv,) = modnorm(h, kvmod[:, None, d:], kvmod[:, None, :d], kv_g[None])
            kv_all = rows3(chain.matmul(rows2(ukv), "w_kv"))
        if layer < 2:
            a_re, a_im, b_mat, c_mat = _s5_discretize(*(diff[k][layer] for k in (
                "ssm_lam_re", "ssm_lam_im", "ssm_log_dt", "ssm_b_re", "ssm_b_im", "ssm_c_re", "ssm_c_im")))
            y0 = s5_ssm(u, b_mat, c_mat, a_re, a_im)
            (z,) = gelu_skip(y0, u, ssm_d[layer][None])
            (y,) = glu(rows3(chain.matmul(rows2(z), "ssm_w_glu")))
        else:
            q_all = rows3(chain.matmul(rows2(u), "attn_w_q"))
            o1, l1, o2, l2, o3, l3 = attn_branches(q_all, kv_all)
            (o,) = combine(o1, o2, o3, l1, l2, l3)
            y = rows3(chain.matmul(rows2(o), "attn_w_o"))
        shift, scale, next_gate, gain = chunks(2 * layer + 1)
        h, u = res_modnorm(h, y, gate, scale, shift, gain)
        gate = next_gate
        pre = chain.matmul(rows2(u), "mlp_w1", bf16_output=True)
        y = rows3(chain.matmul(pre, "mlp_w2", relu2_input=True))
        if layer < 3:
            shift, scale, next_gate, gain = chunks(2 * layer + 2)
            h, u = res_modnorm(h, y, gate, scale, shift, gain)
            gate = next_gate
        else:
            (h,) = gate_res(h, y, gate)
    (row_loss,) = final_loss(h, target, final_g[None])
    return jnp.sum(row_loss)


SSM_NAMES = ("ssm_lam_re", "ssm_lam_im", "ssm_log_dt", "ssm_b_re", "ssm_b_im", "ssm_c_re", "ssm_c_im")
ARG_NAMES = ("x", "c", "ln_g", "ada_w", "ada_b") + SSM_NAMES + (
    "ssm_d", "ssm_w_glu", "kv_g", "kv_ada_w", "kv_ada_b", "w_kv", "attn_w_q", "attn_w_o", "mlp_w1", "mlp_w2", "final_g")
WEIGHT_NAMES = ARG_NAMES[2:]


def kernel(x, c, ln_g, ada_w, ada_b, ssm_lam_re, ssm_lam_im, ssm_log_dt, ssm_b_re, ssm_b_im, ssm_c_re, ssm_c_im, ssm_d, ssm_w_glu, kv_g, kv_ada_w, kv_ada_b, w_kv, attn_w_q, attn_w_o, mlp_w1, mlp_w2, final_g, loss_target, m_ln_g, m_ada_w, m_ada_b, m_ssm_lam_re, m_ssm_lam_im, m_ssm_log_dt, m_ssm_b_re, m_ssm_b_im, m_ssm_c_re, m_ssm_c_im, m_ssm_d, m_ssm_w_glu, m_kv_g, m_kv_ada_w, m_kv_ada_b, m_w_kv, m_attn_w_q, m_attn_w_o, m_mlp_w1, m_mlp_w2, m_final_g, v_ln_g, v_ada_w, v_ada_b, v_ssm_lam_re, v_ssm_lam_im, v_ssm_log_dt, v_ssm_b_re, v_ssm_b_im, v_ssm_c_re, v_ssm_c_im, v_ssm_d, v_ssm_w_glu, v_kv_g, v_kv_ada_w, v_kv_ada_b, v_w_kv, v_attn_w_q, v_attn_w_o, v_mlp_w1, v_mlp_w2, v_final_g):
    args = locals()
    w = {n: args[n] for n in WEIGHT_NAMES}
    mom = {n: args["m_" + n] for n in WEIGHT_NAMES}
    var = {n: args["v_" + n] for n in WEIGHT_NAMES}
    bsz, seq, d = x.shape
    me = 4 * lax.axis_index("x") + 2 * lax.axis_index("y") + lax.axis_index("c")

    small = all_gather(pack([c, ln_g, ssm_d]), "gather_small")
    c_parts, ln_parts, d_parts = unpack(small, [c.shape, ln_g.shape, ssm_d.shape])
    c_all = c_parts.reshape(N_DEV * bsz, d)
    ln_full = jnp.moveaxis(ln_parts, 0, 2).reshape(4, 2, d)
    d_full = jnp.moveaxis(d_parts, 0, 1).reshape(2, d)

    cs = ada_w.shape[-1]
    mod_cols = ada_fwd(c_all, ada_w.reshape(8, d, cs), ada_b.reshape(8, 1, cs))
    kcs = kv_ada_w.shape[-1]
    kv_cols = ada_fwd(c_all, kv_ada_w[None], jnp.zeros((1, 1, kcs), F32))
    mod_g, kv_g_all = unpack(all_gather(pack([mod_cols, kv_cols]), "gather_mod"),
                             [(8, N_DEV * bsz, cs), (N_DEV * bsz, kcs)])
    mods_all = jnp.moveaxis(mod_g, 0, 2).reshape(8, N_DEV * bsz, N_DEV * cs)
    kvmod_all = jnp.moveaxis(kv_g_all, 0, 1).reshape(N_DEV * bsz, N_DEV * kcs) + kv_ada_b[None]
    mods = lax.dynamic_slice_in_dim(mods_all, me * bsz, bsz, axis=1)
    kvmod = lax.dynamic_slice_in_dim(kvmod_all, me * bsz, bsz, axis=0)

    shards = [w[n] if l is None else w[n][l] for n, l, _ in WEIGHT_ORDER]
    first_weight = all_gather(shards[0].astype(BF16), "gather_first_weight")

    diff = {"mods": mods, "kvmod": kvmod, "ln_g": ln_full, "ssm_d": d_full, "kv_g": kv_g, "final_g": final_g}
    diff.update({n: w[n] for n in SSM_NAMES})
    loss_local, (g_diff, grad_x, g_first, g_shards) = jax.value_and_grad(_local_loss, argnums=(0, 1, 3, 4))(
        diff, x, loss_target, first_weight, shards[1:])
    loss = lax.psum(loss_local, AXES)
    g_shards = [sum_slots(exchange_partials(g_first, "exchange_first_weight"))] + list(g_shards)

    dmod = all_gather(pack([g_diff["mods"], g_diff["kvmod"]]), "gather_dmod")
    dmods_p, dkv_p = unpack(dmod, [g_diff["mods"].shape, g_diff["kvmod"].shape])
    dmods_all = jnp.moveaxis(dmods_p, 0, 1).reshape(8, N_DEV * bsz, 3 * d)
    dkv_all = dkv_p.reshape(N_DEV * bsz, 2 * d)
    g_ada_w, g_ada_b = ada_bwd(c_all, lax.dynamic_slice_in_dim(dmods_all, me * cs, cs, axis=2))
    g_kv_ada_w, _ = ada_bwd(c_all, lax.dynamic_slice_in_dim(dkv_all, me * kcs, kcs, axis=1)[None])
    _, g_kv_ada_b = ada_bwd(c_all, dkv_all[None])

    small_names = ("ln_g", "ssm_d", "kv_g", "final_g") + SSM_NAMES
    partial = all_gather(pack([g_diff[n] for n in small_names]), "gather_small_grads")
    totals = unpack(sum_slots(partial), [g_diff[n].shape for n in small_names])
    g_small = dict(zip(small_names, totals))
    g_small["ln_g"] = lax.dynamic_slice_in_dim(g_small["ln_g"], me * ln_g.shape[-1], ln_g.shape[-1], axis=2)
    g_small["ssm_d"] = lax.dynamic_slice_in_dim(g_small["ssm_d"], me * ssm_d.shape[-1], ssm_d.shape[-1], axis=1)
    g_small["ada_b"] = g_ada_b.reshape(ada_b.shape)
    g_small["kv_ada_b"] = g_kv_ada_b.reshape(kv_ada_b.shape)

    out = {}

    def put(name, res, shape):
        for kind, a in zip(("grad_", "delta_", "new_m_", "new_v_"), res):
            out[kind + name] = a.reshape(shape)

    for name in ("ssm_w_glu", "w_kv", "attn_w_q", "attn_w_o", "mlp_w1", "mlp_w2"):
        grads = jnp.stack([g for g, (n, _, _) in zip(g_shards, WEIGHT_ORDER) if n == name])
        v3 = lambda a: a.reshape(grads.shape)
        put(name, adamw(v3(w[name]), v3(mom[name]), v3(var[name]), grads, "adamw_" + name), w[name].shape)
    v3 = lambda a: a.reshape(8, d, cs)
    put("ada_w", adamw(v3(ada_w), v3(m_ada_w), v3(v_ada_w), g_ada_w, "adamw_ada_w"), ada_w.shape)
    put("kv_ada_w", adamw(kv_ada_w[None], m_kv_ada_w[None], v_kv_ada_w[None], g_kv_ada_w, "adamw_kv_ada_w"), kv_ada_w.shape)
    names = small_names + ("ada_b", "kv_ada_b")
    res = adamw(pack([w[n] for n in names])[None], pack([mom[n] for n in names])[None],
                pack([var[n] for n in names])[None], pack([g_small[n] for n in names])[None], "adamw_small")
    for kind, slab in zip(("grad_", "delta_", "new_m_", "new_v_"), res):
        for n, a in zip(names, unpack(slab[0], [w[n].shape for n in names])):
            out[kind + n] = a

    result = [loss, grad_x]
    for kind in ("grad_", "delta_", "new_m_", "new_v_"):
        result += [out[kind + n] for n in WEIGHT_NAMES]
    return tuple(result)
```

```python
import functools
import math

import jax
import jax.numpy as jnp
from jax import lax
from jax.experimental import pallas as pl
from jax.experimental.pallas import tpu as pltpu

F32 = jnp.float32
BF16 = jnp.bfloat16
N_DEV = 8
AXES = ("x", "y", "c")
V7X_VMEM_LIMIT = 56 * 1024 * 1024
LANES = 128
EPS = 1e-6
NEG = -1e30
HEAD = 64
QBLK = 128
BRANCH_DIL = (1, 4, 16)
ADAM_LR, ADAM_B1, ADAM_B2, ADAM_EPS, ADAM_WD, ADAM_STEP = 0.001, 0.9, 0.999, 1e-08, 0.01, 10

NN = (((1,), (0,)), ((), ()))
NT = (((1,), (1,)), ((), ()))
TN = (((0,), (0,)), ((), ()))


def _params(sem):
    return pltpu.CompilerParams(dimension_semantics=sem, vmem_limit_bytes=V7X_VMEM_LIMIT)


def _peer(k):
    px, py, pc = lax.axis_index("x"), lax.axis_index("y"), lax.axis_index("c")
    qx = 1 - px if k & 4 else px
    qy = 1 - py if k & 2 else py
    qc = 1 - pc if k & 1 else pc
    return (qx, qy, qc), 4 * qx + 2 * qy + qc


ROW_TILE = 2048
NEAR, FAR = (0, 1, 2, 3, 6), (4, 5, 7)


def _exchange_copies(ks, src_ref, land_ref, send_sems, recv_sems):
    copies = []
    for i, k in enumerate(ks):
        peer, slot = _peer(k)
        if k == 0:
            copies.append(pltpu.make_async_copy(src_ref.at[slot], land_ref.at[i], send_sems.at[i]))
        else:
            copies.append(pltpu.make_async_remote_copy(
                src_ref=src_ref.at[slot], dst_ref=land_ref.at[i], send_sem=send_sems.at[i], recv_sem=recv_sems.at[i],
                device_id=peer, device_id_type=pl.DeviceIdType.MESH))
    return copies


def _gather_copies(src_ref, land_ref, send_sems, recv_sems):
    _, me = _peer(0)

    def copy(pair, slot, k, src=None):
        peer, _ = _peer(k)
        return pltpu.make_async_remote_copy(
            src_ref=land_ref.at[slot] if src is None else src, dst_ref=land_ref.at[slot],
            send_sem=send_sems.at[pair], recv_sem=recv_sems.at[pair], device_id=peer,
            device_id_type=pl.DeviceIdType.MESH)

    direct = [copy(0, me, 1, src_ref)] + [copy(1 + j, me, k, src_ref) for j, k in enumerate((2, 4, 6))]
    forwards = [copy(4 + j, _peer(k)[1], 1) for j, k in enumerate((2, 4, 6))]
    from_sibling = [copy(0, _peer(1)[1], 1)] + [copy(4 + j, _peer(k ^ 1)[1], 1) for j, k in enumerate((2, 4, 6))]
    return direct, forwards, from_sibling


def _mm(name, a, b, *, grid, a_spec, b_spec, out_spec, out_shape, dims, acc_axis=None, acc_shape=None, carry=None,
        a_pre=None, post=None, cache=None):
    n_acc = grid[acc_axis] if acc_axis is not None else 1
    n_steps = math.prod(grid)
    n_ext = 0 if post is None else 1

    def finish(r, e_ref):
        return r if post is None else post[2](r, e_ref[...])

    def as_bf16(ref, pre, cache_ref):
        def convert():
            return (ref[...] if pre is None else pre(ref[...])).astype(BF16)

        if cache_ref is None:
            return convert()
        outer, inner = pl.program_id(0), pl.program_id(1)
        if cache[1] == "inner":
            @pl.when(inner == 0)
            def _():
                cache_ref[...] = convert()

            return cache_ref[...]

        @pl.when(outer == 0)
        def _():
            cache_ref[inner] = convert()

        return cache_ref[inner]

    def body(*refs):
        a_ref, b_ref = refs[:2]
        e_ref = refs[2] if post is not None else None
        if carry is None:
            o_ref = refs[2 + n_ext]
            scratch = list(refs[3 + n_ext:])
        else:
            src_ref, o_ref, land_ref = refs[2 + n_ext:5 + n_ext]
            scratch, sems = list(refs[5 + n_ext:-3]), refs[-3:]
            step = functools.reduce(lambda s, i: s * grid[i] + pl.program_id(i), range(len(grid)), 0)
        cache_ref = scratch.pop() if cache is not None else None

        if carry is not None:
            @pl.when(step == 0)
            def _():
                if carry[0] == "gather":
                    pltpu.make_async_copy(src_ref, land_ref.at[_peer(0)[1]], sems[2]).start()
                    for cp in _gather_copies(src_ref, land_ref, sems[0], sems[1])[0]:
                        cp.start()
                else:
                    for cp in _exchange_copies(carry[2], src_ref, land_ref, sems[0], sems[1]):
                        cp.start()

            if carry[0] == "gather":
                @pl.when(step == (3 * n_steps) // 4)
                def _():
                    direct, forwards, _ = _gather_copies(src_ref, land_ref, sems[0], sems[1])
                    for cp, fw in zip(direct[1:], forwards):
                        cp.wait_recv()
                        fw.start()

        a_val = as_bf16(a_ref, a_pre, cache_ref if cache is not None and cache[0] == "a" else None)
        b_val = as_bf16(b_ref, None, cache_ref if cache is not None and cache[0] == "b" else None)
        r = lax.dot_general(a_val, b_val, dims, preferred_element_type=F32)
        if acc_axis is None:
            o_ref[...] = finish(r, e_ref).astype(o_ref.dtype)
        else:
            acc = scratch[0]
            k = pl.program_id(acc_axis)

            @pl.when(k == 0)
            def _():
                acc[...] = r

            @pl.when(k > 0)
            def _():
                acc[...] += r

            @pl.when(k == n_acc - 1)
            def _():
                o_ref[...] = finish(acc[...], e_ref).astype(o_ref.dtype)

        if carry is not None:
            @pl.when(step == n_steps - 1)
            def _():
                if carry[0] == "gather":
                    direct, forwards, from_sibling = _gather_copies(src_ref, land_ref, sems[0], sems[1])
                    for cp in from_sibling:
                        cp.wait_recv()
                    for cp in direct + forwards:
                        cp.wait_send()
                    pltpu.make_async_copy(src_ref, land_ref.at[_peer(0)[1]], sems[2]).wait()
                else:
                    for cp in _exchange_copies(carry[2], src_ref, land_ref, sems[0], sems[1]):
                        cp.wait()

    scratch = [] if acc_axis is None else [pltpu.VMEM(acc_shape, F32)]
    if cache is not None:
        block = tuple(s for s in (a_spec if cache[0] == "a" else b_spec).block_shape if s is not None)
        scratch.append(pltpu.VMEM(block if cache[1] == "inner" else (grid[1],) + block, BF16))
    in_specs, operands = [a_spec, b_spec], [a, b]
    if post is not None:
        in_specs.append(post[1])
        operands.append(post[0])
    if carry is None:
        sem = tuple("arbitrary" if i == acc_axis or cache is not None else "parallel" for i in range(len(grid)))
        return pl.pallas_call(body, name=name, grid=grid, in_specs=in_specs, out_specs=out_spec,
                              out_shape=out_shape, scratch_shapes=scratch, compiler_params=_params(sem))(*operands)
    kind, src = carry[:2]
    n_land = N_DEV if kind == "gather" else len(carry[2])
    n_sem = N_DEV - 1 if kind == "gather" else n_land
    land = jax.ShapeDtypeStruct((n_land,) + src.shape[-2:], src.dtype)
    hbm = pl.BlockSpec(memory_space=pl.ANY)
    scratch += [pltpu.SemaphoreType.DMA((n_sem,)), pltpu.SemaphoreType.DMA((n_sem,)), pltpu.SemaphoreType.DMA(())]
    return pl.pallas_call(body, name=name + "_" + kind, grid=grid, in_specs=in_specs + [hbm],
                          out_specs=[out_spec, hbm], out_shape=[out_shape, land], scratch_shapes=scratch,
                          compiler_params=_params(("arbitrary",) * len(grid)))(*operands, src)


def _tile(n, t):
    if n <= t:
        return n
    for d in range(t - t % 8, 7, -8):
        if n % d == 0:
            return d
    raise ValueError((n, t))


def _col_fwd_call(x, g, out_dtype, carry=None):
    t, k = x.shape
    _, _, ns = g.shape
    tm = _tile(t, ROW_TILE)
    return _mm("col_fwd", x, g, grid=(t // tm, N_DEV),
               a_spec=pl.BlockSpec((tm, k), lambda i, j: (i, 0)),
               b_spec=pl.BlockSpec((None, k, ns), lambda i, j: (j, 0, 0)),
               out_spec=pl.BlockSpec((tm, ns), lambda i, j: (i, j)),
               out_shape=jax.ShapeDtypeStruct((t, N_DEV * ns), out_dtype), dims=NN, carry=carry,
               cache=("a", "inner"))


def _col_dx_call(dy, g, carry=None):
    _, k, ns = g.shape
    t = dy.shape[0]
    tm = _tile(t, ROW_TILE)
    return _mm("col_dx", dy, g, grid=(t // tm, N_DEV),
               a_spec=pl.BlockSpec((tm, ns), lambda i, j: (i, j)),
               b_spec=pl.BlockSpec((None, k, ns), lambda i, j: (j, 0, 0)),
               out_spec=pl.BlockSpec((tm, k), lambda i, j: (i, 0)),
               out_shape=jax.ShapeDtypeStruct((t, k), F32), dims=NT, acc_axis=1, acc_shape=(tm, k), carry=carry)


def _col_dw_call(x, dy, ns, carry=None):
    t, k = x.shape
    tt = _tile(t, ROW_TILE)
    last = t // tt - 1
    return _mm("col_dw", x, dy, grid=(N_DEV, t // tt), cache=("a", "outer"),
               a_spec=pl.BlockSpec((tt, k), lambda j, s: (jnp.where(j == 0, s, last), 0)),
               b_spec=pl.BlockSpec((tt, ns), lambda j, s: (s, j)),
               out_spec=pl.BlockSpec((None, k, ns), lambda j, s: (j, 0, 0)),
               out_shape=jax.ShapeDtypeStruct((N_DEV, k, ns), BF16), dims=TN, acc_axis=1, acc_shape=(k, ns),
               carry=carry)


def _make_linear_col(out_dtype):
    @jax.custom_vjp
    def linear_col(x, g):
        return _col_fwd_call(x, g, out_dtype)

    def fwd(x, g):
        return _col_fwd_call(x, g, out_dtype), (x, g)

    def bwd(res, dy):
        x, g = res
        return _col_dx_call(dy, g), _col_dw_call(x, dy, g.shape[2])

    linear_col.defvjp(fwd, bwd)

    @jax.custom_vjp
    def linear_col_next(x, g, nxt):
        return next_fwd(x, g, nxt)[0]

    def next_fwd(x, g, nxt):
        y, g_next = _col_fwd_call(x, g, out_dtype, carry=("gather", nxt.astype(BF16)))
        return (y, g_next), (x, g)

    def next_bwd(res, cts):
        x, g = res
        dy, dg_next = cts
        dx, near = _col_dx_call(dy, g, carry=("exchange", dg_next, NEAR))
        dg, far = _col_dw_call(x, dy, g.shape[2], carry=("exchange", dg_next, FAR))
        return dx, dg, sum_slots(near, far)

    linear_col_next.defvjp(next_fwd, next_bwd)
    return linear_col, linear_col_next


linear_col, linear_col_next = _make_linear_col(F32)
linear_col_bf16, linear_col_bf16_next = _make_linear_col(BF16)


def _relu2(a):
    return jnp.square(jnp.maximum(a.astype(F32), 0.0))


def _relu2_grad(d_act, a):
    return d_act * (2.0 * jnp.maximum(a.astype(F32), 0.0))


def _lin_fwd_call(x, w, act, carry=None):
    t, k = x.shape
    _, n = w.shape
    tm, tk = _tile(t, ROW_TILE), _tile(k, 1024)
    return _mm("lin_fwd", x, w, grid=(t // tm, k // tk),
               a_spec=pl.BlockSpec((tm, tk), lambda i, s: (i, s)),
               b_spec=pl.BlockSpec((tk, n), lambda i, s: (s, 0)),
               out_spec=pl.BlockSpec((tm, n), lambda i, s: (i, 0)),
               out_shape=jax.ShapeDtypeStruct((t, n), F32), dims=NN, acc_axis=1, acc_shape=(tm, n), carry=carry,
               a_pre=_relu2 if act else None)


def _lin_dx_call(dy, w, x, act, carry=None):
    k, n = w.shape
    t = dy.shape[0]
    tm, tk = _tile(t, 1024), _tile(k, 1024)
    out_spec = pl.BlockSpec((tm, tk), lambda i, s: (i, s))
    return _mm("lin_dx", dy, w, grid=(t // tm, k // tk),
               a_spec=pl.BlockSpec((tm, n), lambda i, s: (i, 0)),
               b_spec=pl.BlockSpec((tk, n), lambda i, s: (s, 0)),
               out_spec=out_spec, out_shape=jax.ShapeDtypeStruct((t, k), x.dtype), dims=NT, carry=carry,
               post=(x, out_spec, _relu2_grad) if act else None, cache=("a", "inner"))


def _lin_dw_call(x, dy, act, carry=None):
    t, k = x.shape
    n = dy.shape[1]
    tk, tt = _tile(k, 1024), _tile(t, ROW_TILE)
    return _mm("lin_dw", x, dy, grid=(k // tk, t // tt),
               a_spec=pl.BlockSpec((tt, tk), lambda s, r: (r, s)),
               b_spec=pl.BlockSpec((tt, n), lambda s, r: (r, 0)),
               out_spec=pl.BlockSpec((tk, n), lambda s, r: (s, 0)),
               out_shape=jax.ShapeDtypeStruct((k, n), BF16), dims=TN, acc_axis=1, acc_shape=(tk, n), carry=carry,
               a_pre=_relu2 if act else None)


def _make_linear(act):
    @jax.custom_vjp
    def linear(x, w):
        return _lin_fwd_call(x, w, act)

    def fwd(x, w):
        return _lin_fwd_call(x, w, act), (x, w)

    def bwd(res, dy):
        x, w = res
        return _lin_dx_call(dy, w, x, act), _lin_dw_call(x, dy, act)

    linear.defvjp(fwd, bwd)

    @jax.custom_vjp
    def linear_next(x, w, nxt):
        return next_fwd(x, w, nxt)[0]

    def next_fwd(x, w, nxt):
        y, g_next = _lin_fwd_call(x, w, act, carry=("gather", nxt.astype(BF16)))
        return (y, g_next), (x, w)

    def next_bwd(res, cts):
        x, w = res
        dy, dg_next = cts
        dx, near = _lin_dx_call(dy, w, x, act, carry=("exchange", dg_next, NEAR))
        dw, far = _lin_dw_call(x, dy, act, carry=("exchange", dg_next, FAR))
        return dx, dw, sum_slots(near, far)

    linear_next.defvjp(next_fwd, next_bwd)
    return linear, linear_next


linear, linear_next = _make_linear(False)
linear_relu2, linear_relu2_next = _make_linear(True)


def _row_ts(widths, seq):
    per_row = 4 * sum(widths)
    ts = 512
    while ts > 8 and ts * per_row * 2 > 24 * 1024 * 1024:
        ts //= 2
    return min(ts, seq)


def make_rowop(name, f, n_row, n_batch, n_vec, f_bwd=None):
    n_in = n_row + n_batch + n_vec

    def in_specs(args, ts):
        specs = []
        for a in args[:n_row]:
            specs.append(pl.BlockSpec((None, ts, a.shape[2]), lambda b, s: (b, s, 0)))
        for a in args[n_row:n_row + n_batch]:
            specs.append(pl.BlockSpec((None, 1, a.shape[2]), lambda b, s: (b, 0, 0)))
        for a in args[n_row + n_batch:]:
            specs.append(pl.BlockSpec((1, a.shape[1]), lambda b, s: (0, 0)))
        return specs

    def out_struct(args, ts):
        blocks = [jax.ShapeDtypeStruct((ts, a.shape[2]), F32) for a in args[:n_row]]
        blocks += [jax.ShapeDtypeStruct((1, a.shape[2]), F32) for a in args[n_row:n_row + n_batch]]
        blocks += [jax.ShapeDtypeStruct((1, a.shape[1]), F32) for a in args[n_row + n_batch:]]
        return jax.eval_shape(f, *blocks)

    def run_fwd(args):
        bsz, seq = args[0].shape[:2]
        outs0 = out_struct(args, 8)
        widths = [a.shape[2] for a in args[:n_row]] + [o.shape[1] for o in outs0]
        ts = _row_ts(widths, seq)

        def body(*refs):
            outs = f(*[r[...] for r in refs[:n_in]])
            for o_ref, o in zip(refs[n_in:], outs):
                o_ref[...] = o

        return pl.pallas_call(
            body, name=name + "_fwd", grid=(bsz, seq // ts), in_specs=in_specs(args, ts),
            out_specs=[pl.BlockSpec((None, ts, o.shape[1]), lambda b, s: (b, s, 0)) for o in outs0],
            out_shape=[jax.ShapeDtypeStruct((bsz, seq, o.shape[1]), F32) for o in outs0],
            compiler_params=_params(("parallel", "parallel")))(*args)

    def run_bwd(args, cts):
        bsz, seq = args[0].shape[:2]
        widths = [a.shape[2] for a in args[:n_row]] * 2 + [c.shape[2] for c in cts]
        ts = _row_ts(widths, seq)
        n_ct = len(cts)

        def body(*refs):
            ins = [r[...] for r in refs[:n_in]]
            ct = [r[...] for r in refs[n_in:n_in + n_ct]]
            if f_bwd is None:
                _, vjp = jax.vjp(f, *ins)
                grads = vjp(tuple(ct))
            else:
                grads = f_bwd(ins, ct)
            g_refs = refs[n_in + n_ct:]
            b, s = pl.program_id(0), pl.program_id(1)
            for i in range(n_row):
                g_refs[i][...] = grads[i]
            for i in range(n_row, n_row + n_batch):
                @pl.when(s == 0)
                def _(i=i):
                    g_refs[i][...] = grads[i]

                @pl.when(s > 0)
                def _(i=i):
                    g_refs[i][...] += grads[i]
            for i in range(n_row + n_batch, n_in):
                first = jnp.logical_and(b == 0, s == 0)

                @pl.when(first)
                def _(i=i):
                    g_refs[i][...] = grads[i]

                @pl.when(jnp.logical_not(first))
                def _(i=i):
                    g_refs[i][...] += grads[i]

        ct_specs = [pl.BlockSpec((None, ts, c.shape[2]), lambda b, s: (b, s, 0)) for c in cts]
        return pl.pallas_call(
            body, name=name + "_bwd", grid=(bsz, seq // ts), in_specs=in_specs(args, ts) + ct_specs,
            out_specs=in_specs(args, ts), out_shape=[jax.ShapeDtypeStruct(a.shape, F32) for a in args],
            compiler_params=_params(("arbitrary", "arbitrary")))(*args, *cts)

    @jax.custom_vjp
    def op(*args):
        return tuple(run_fwd(args))

    def fwd(*args):
        return tuple(run_fwd(args)), args

    def bwd(args, cts):
        return tuple(run_bwd(args, list(cts)))

    op.defvjp(fwd, bwd)
    return op


def _modnorm_f(h, scale, shift, g):
    y = h * lax.rsqrt(jnp.mean(h * h, axis=-1, keepdims=True) + EPS) * g
    return (y * (1.0 + scale) + shift,)


def _gate_res_f(h, y, gate):
    return (h + gate * y,)


def _res_modnorm_f(h, y, gate, scale, shift, g):
    h2 = h + gate * y
    return (h2,) + _modnorm_f(h2, scale, shift, g)


def _gelu_skip_f(y, u, d):
    return (jax.nn.gelu(y + d * u),)


def _glu_f(vg):
    n = vg.shape[1] // 2
    return (vg[:, :n] * jax.nn.sigmoid(vg[:, n:]),)


def _glu_b(ins, cts):
    (vg,), (ct,) = ins, cts
    n = vg.shape[1] // 2
    sg = jax.nn.sigmoid(vg[:, n:])
    return (jnp.concatenate([ct * sg, ct * vg[:, :n] * sg * (1.0 - sg)], axis=1),)


def _combine_f(o1, o2, o3, l1, l2, l3):
    m = jnp.maximum(jnp.maximum(l1, l2), l3)
    e1, e2, e3 = jnp.exp(l1 - m), jnp.exp(l2 - m), jnp.exp(l3 - m)
    return ((e1 * o1 + e2 * o2 + e3 * o3) / (e1 + e2 + e3),)


def _final_loss_f(h, target, g):
    y = h * lax.rsqrt(jnp.mean(h * h, axis=-1, keepdims=True) + EPS) * g
    return (0.5 * jnp.mean(jnp.square(y - target), axis=-1, keepdims=True),)


modnorm = make_rowop("modnorm", _modnorm_f, 1, 2, 1)
gate_res = make_rowop("gate_res", _gate_res_f, 2, 1, 0)
res_modnorm = make_rowop("res_modnorm", _res_modnorm_f, 2, 3, 1)
gelu_skip = make_rowop("gelu_skip", _gelu_skip_f, 2, 0, 1)
glu = make_rowop("glu", _glu_f, 1, 0, 0, _glu_b)
combine = make_rowop("combine", _combine_f, 6, 0, 0)
final_loss = make_rowop("final_loss", _final_loss_f, 2, 0, 1)


HALF = 512


SSM_TB = 64
PITCH = SSM_TB + 8
PIECES = 8
CH = 128
NCH = 2 * HALF // LANES
NRE = NCH // 2


def _slab(t):
    return pl.ds(t, PIECES, stride=PITCH)


def _put_rows(ref, b, q, val, tb):
    for j in range(NCH):
        ref[b, j, pl.ds(q * PITCH, tb), :] = val[:, j * LANES:(j + 1) * LANES]


def _get_rows(ref, q, bsz, tb):
    return jnp.concatenate(
        [jnp.concatenate([ref[b, j, pl.ds(q * PITCH, tb), :] for j in range(NCH)], axis=1) for b in range(bsz)], axis=0)


def _chunks(a):
    return [a[:, j * LANES:(j + 1) * LANES] for j in range(a.shape[1] // LANES)]


def _ssm_fwd_call(u, b_mat, c_mat, a_re, a_im):
    bsz, seq, d = u.shape
    tb = SSM_TB
    nt = seq // tb
    assert seq % tb == 0 and d == PIECES * CH

    def body(u_ref, b_ref, c_ref, ar_ref, ai_ref, y_ref, st_ref, end_ref, bu, carry):
        @pl.when(pl.program_id(0) == 0)
        def _():
            carry[...] = jnp.zeros_like(carry)

        u2 = u_ref[...].reshape(bsz * tb, d).astype(BF16)
        for q in range(PIECES):
            r = lax.dot_general(u2[:, q * CH:(q + 1) * CH], b_ref[q], NN, preferred_element_type=F32)
            for b in range(bsz):
                _put_rows(bu, b, q, r[b * tb:(b + 1) * tb], tb)
        ar, ai = _chunks(ar_ref[...]), _chunks(ai_ref[...])
        st = [[carry[b, j] for j in range(NCH)] for b in range(bsz)]
        for t in range(tb):
            for b in range(bsz):
                for j in range(NRE):
                    sre, sim = st[b][j], st[b][NRE + j]
                    nre = ar[j] * sre - ai[j] * sim + bu[b, j, _slab(t), :]
                    nim = ar[j] * sim + ai[j] * sre + bu[b, NRE + j, _slab(t), :]
                    st_ref[b, j, _slab(t), :] = nre
                    st_ref[b, NRE + j, _slab(t), :] = nim
                    st[b][j], st[b][NRE + j] = nre, nim
        for b in range(bsz):
            for j in range(NCH):
                carry[b, j] = st[b][j]
                end_ref[b, j] = st[b][j]
        for q in range(PIECES):
            s2 = _get_rows(st_ref, q, bsz, tb).astype(BF16)
            r = lax.dot_general(s2, c_ref[q], NN, preferred_element_type=F32)
            for b in range(bsz):
                y_ref[b, :, q * CH:(q + 1) * CH] = r[b * tb:(b + 1) * tb]

    rows = pl.BlockSpec((bsz, tb, d), lambda i: (0, i, 0))
    vec = pl.BlockSpec((PIECES, HALF), lambda i: (0, 0))
    return pl.pallas_call(
        body, name="s5_ssm_fwd", grid=(nt,),
        in_specs=[rows, pl.BlockSpec((PIECES, CH, 2 * HALF), lambda i: (0, 0, 0)),
                  pl.BlockSpec((PIECES, 2 * HALF, CH), lambda i: (0, 0, 0)), vec, vec],
        out_specs=[rows, pl.BlockSpec((bsz, None, NCH, PIECES * PITCH, LANES), lambda i: (0, i, 0, 0, 0)),
                   pl.BlockSpec((bsz, None, NCH, PIECES, LANES), lambda i: (0, i, 0, 0, 0))],
        out_shape=[jax.ShapeDtypeStruct((bsz, seq, d), F32),
                   jax.ShapeDtypeStruct((bsz, nt, NCH, PIECES * PITCH, LANES), F32),
                   jax.ShapeDtypeStruct((bsz, nt, NCH, PIECES, LANES), F32)],
        scratch_shapes=[pltpu.VMEM((bsz, NCH, PIECES * PITCH, LANES), F32), pltpu.VMEM((bsz, NCH, PIECES, LANES), F32)],
        compiler_params=_params(("arbitrary",)))(u, b_mat, c_mat, a_re, a_im)


def _ssm_bwd_call(dy, u, st, ends, b_mat, c_mat, a_re, a_im):
    bsz, seq, d = u.shape
    tb = SSM_TB
    nt = seq // tb

    def body(dy_ref, u_ref, st_ref, prev_ref, b_ref, c_ref, ar_ref, ai_ref, du_ref, db_hbm, dc_hbm, dar_ref, dai_ref,
             lam, carry, db_acc, dc_acc):
        i = pl.program_id(0)

        @pl.when(i == 0)
        def _():
            carry[...] = jnp.zeros_like(carry)
            db_acc[...] = jnp.zeros_like(db_acc)
            dc_acc[...] = jnp.zeros_like(dc_acc)
            dar_ref[...] = jnp.zeros_like(dar_ref)
            dai_ref[...] = jnp.zeros_like(dai_ref)

        dy2 = dy_ref[...].reshape(bsz * tb, d).astype(BF16)
        u2 = u_ref[...].reshape(bsz * tb, d).astype(BF16)
        for q in range(PIECES):
            dyq = dy2[:, q * CH:(q + 1) * CH]
            g = lax.dot_general(dyq, c_ref[q], NT, preferred_element_type=F32)
            for b in range(bsz):
                _put_rows(lam, b, q, g[b * tb:(b + 1) * tb], tb)
            s2 = _get_rows(st_ref, q, bsz, tb).astype(BF16)
            dc_acc[q] += lax.dot_general(s2, dyq, TN, preferred_element_type=F32)

        ar, ai = _chunks(ar_ref[...]), _chunks(ai_ref[...])
        keep = jnp.where(i == nt - 1, 0.0, 1.0)
        lm = [[carry[b, j] for j in range(NCH)] for b in range(bsz)]
        dar = [jnp.zeros((PIECES, LANES), F32) for _ in range(NRE)]
        dai = [jnp.zeros((PIECES, LANES), F32) for _ in range(NRE)]
        for t in range(tb - 1, -1, -1):
            for b in range(bsz):
                for j in range(NRE):
                    lre, lim = lm[b][j], lm[b][NRE + j]
                    nre = lam[b, j, _slab(t), :] + ar[j] * lre + ai[j] * lim
                    nim = lam[b, NRE + j, _slab(t), :] - ai[j] * lre + ar[j] * lim
                    lam[b, j, _slab(t), :] = nre
                    lam[b, NRE + j, _slab(t), :] = nim
                    if t > 0:
                        pre, pim = st_ref[b, j, _slab(t - 1), :], st_ref[b, NRE + j, _slab(t - 1), :]
                    else:
                        pre, pim = prev_ref[b, j] * keep, prev_ref[b, NRE + j] * keep
                    dar[j] = dar[j] + nre * pre + nim * pim
                    dai[j] = dai[j] + nim * pre - nre * pim
                    lm[b][j], lm[b][NRE + j] = nre, nim
        for b in range(bsz):
            for j in range(NCH):
                carry[b, j] = lm[b][j]
        dar_ref[...] += jnp.concatenate(dar, axis=1)
        dai_ref[...] += jnp.concatenate(dai, axis=1)

        for q in range(PIECES):
            l2 = _get_rows(lam, q, bsz, tb).astype(BF16)
            r = lax.dot_general(l2, b_ref[q], NT, preferred_element_type=F32)
            for b in range(bsz):
                du_ref[b, :, q * CH:(q + 1) * CH] = r[b * tb:(b + 1) * tb]
            db_acc[q] += lax.dot_general(u2[:, q * CH:(q + 1) * CH], l2, TN, preferred_element_type=F32)

        @pl.when(i == nt - 1)
        def _():
            pltpu.sync_copy(db_acc, db_hbm)
            pltpu.sync_copy(dc_acc, dc_hbm)

    rows = pl.BlockSpec((bsz, tb, d), lambda i: (0, nt - 1 - i, 0))
    vec = pl.BlockSpec((PIECES, HALF), lambda i: (0, 0))
    hbm = pl.BlockSpec(memory_space=pl.ANY)
    return pl.pallas_call(
        body, name="s5_ssm_bwd", grid=(nt,),
        in_specs=[rows, rows,
                  pl.BlockSpec((bsz, None, NCH, PIECES * PITCH, LANES), lambda i: (0, nt - 1 - i, 0, 0, 0)),
                  pl.BlockSpec((bsz, None, NCH, PIECES, LANES), lambda i: (0, jnp.maximum(nt - 2 - i, 0), 0, 0, 0)),
                  pl.BlockSpec((PIECES, CH, 2 * HALF), lambda i: (0, 0, 0)),
                  pl.BlockSpec((PIECES, 2 * HALF, CH), lambda i: (0, 0, 0)), vec, vec],
        out_specs=[rows, hbm, hbm, vec, vec],
        out_shape=[jax.ShapeDtypeStruct((bsz, seq, d), F32), jax.ShapeDtypeStruct((PIECES, CH, 2 * HALF), F32),
                   jax.ShapeDtypeStruct((PIECES, 2 * HALF, CH), F32), jax.ShapeDtypeStruct((PIECES, HALF), F32),
                   jax.ShapeDtypeStruct((PIECES, HALF), F32)],
        scratch_shapes=[pltpu.VMEM((bsz, NCH, PIECES * PITCH, LANES), F32), pltpu.VMEM((bsz, NCH, PIECES, LANES), F32),
                        pltpu.VMEM((PIECES, CH, 2 * HALF), F32), pltpu.VMEM((PIECES, 2 * HALF, CH), F32)],
        compiler_params=_params(("arbitrary",)))(dy, u, st, ends, b_mat, c_mat, a_re, a_im)


@jax.custom_vjp
def s5_ssm(u, b_mat, c_mat, a_re, a_im):
    return _s5_ssm_fwd(u, b_mat, c_mat, a_re, a_im)[0]


def _s5_ssm_fwd(u, b_mat, c_mat, a_re, a_im):
    b16, c16 = b_mat.astype(BF16), c_mat.astype(BF16)
    y, st, ends = _ssm_fwd_call(u, b16, c16, a_re, a_im)
    return y, (u, st, ends, b16, c16, a_re, a_im)


def _s5_ssm_bwd(res, dy):
    return tuple(_ssm_bwd_call(dy, *res))


s5_ssm.defvjp(_s5_ssm_fwd, _s5_ssm_bwd)


ATT_HW = 2 * HEAD
ATT_HB = 1024 // ATT_HW


def _branch_geometry(dil, seq):
    sub = seq // dil
    assert sub % QBLK == 0
    return sub // QBLK


def _drows(dil, r, start, size):
    if dil == 1:
        return pl.ds(start, size)
    return pl.ds(r + start * dil, size, stride=dil)


def _masks():
    qi = lax.broadcasted_iota(jnp.int32, (QBLK, 2 * QBLK), 0)
    kj = lax.broadcasted_iota(jnp.int32, (QBLK, 2 * QBLK), 1) - QBLK
    dist = qi - kj
    band = jnp.logical_and(dist >= 0, dist <= QBLK)
    ci = lax.broadcasted_iota(jnp.int32, (QBLK, QBLK), 0)
    cj = lax.broadcasted_iota(jnp.int32, (QBLK, QBLK), 1)
    return band, ci >= cj


MAX_STRIDE = 4


class _Rows:
    def __init__(self, ref, dil, stage=None):
        self.ref, self.dil, self.stage = ref, dil, stage

    def _at(self, r, start, size):
        if self.stage is None:
            return self.ref, _drows(self.dil, r, start, size)
        step = self.dil // MAX_STRIDE
        return self.stage.at[r % MAX_STRIDE], pl.ds(r // MAX_STRIDE + start * step, size, stride=step)

    def get(self, r, start, size):
        ref, rows = self._at(r, start, size)
        return ref[rows, :]

    def put(self, r, start, size, val):
        ref, rows = self._at(r, start, size)
        ref[rows, :] = val

    def _parts(self):
        n = self.ref.shape[0] // MAX_STRIDE
        return [(a, pl.ds(a, n, stride=MAX_STRIDE)) for a in range(MAX_STRIDE)]

    def fill(self):
        if self.stage is not None:
            for a, rows in self._parts():
                self.stage[a] = self.ref[rows, :]

    def flush(self):
        if self.stage is not None:
            for a, rows in self._parts():
                self.ref[rows, :] = self.stage[a]


def _stage_shapes(dil, seq, n):
    return [pltpu.VMEM((MAX_STRIDE, seq // MAX_STRIDE, ATT_HW), F32)] * n if dil > MAX_STRIDE else []


def _widen_first(k2, v2, mask):
    wide = lambda a: jnp.concatenate([a, a], axis=0)
    return wide(k2), wide(v2), jnp.concatenate([mask, jnp.zeros_like(mask)], axis=1)


def _head_lanes():
    lane = lax.broadcasted_iota(jnp.int32, (QBLK, ATT_HW), 1)
    return [jnp.logical_and(lane >= h * HEAD, lane < (h + 1) * HEAD) for h in range(ATT_HW // HEAD)]


def _attn_specs(i, seq):
    q_spec = pl.BlockSpec((None, seq, ATT_HW), lambda b, h: (b, 0, i * ATT_HB + h))
    k_spec = pl.BlockSpec((None, seq, ATT_HW), lambda b, h: (b, 0, i * ATT_HB + h))
    v_spec = pl.BlockSpec((None, seq, ATT_HW), lambda b, h: (b, 0, (3 + i) * ATT_HB + h))
    o_spec = pl.BlockSpec((None, seq, ATT_HW), lambda b, h: (b, 0, h))
    return q_spec, k_spec, v_spec, o_spec


def _attn_fwd_call(q_all, kv_all, i, dil):
    bsz, seq, _ = q_all.shape
    nb = _branch_geometry(dil, seq)
    scale = HEAD ** -0.5

    def body(*refs):
        stages = refs[5:] if dil > MAX_STRIDE else (None,) * 5
        q_rows, k_rows, v_rows, o_rows, l_rows = (_Rows(ref, dil, st) for ref, st in zip(refs[:5], stages))
        band, causal = _masks()
        head_lanes = _head_lanes()
        for rows in (q_rows, k_rows, v_rows):
            rows.fill()

        def one(r, n, mask):
            k0, kn = (0, QBLK) if n == 0 else ((n - 1) * QBLK, 2 * QBLK)
            q2 = q_rows.get(r, n * QBLK, QBLK).astype(BF16)
            k2 = k_rows.get(r, k0, kn).astype(BF16)
            v2 = v_rows.get(r, k0, kn).astype(BF16)
            if n == 0:
                k2, v2, mask = _widen_first(k2, v2, mask)
            o, lse = None, None
            for h in range(ATT_HW // HEAD):
                s = lax.dot_general(jnp.where(head_lanes[h], q2, 0), k2, NT, preferred_element_type=F32) * scale
                s = jnp.where(mask, s, NEG)
                m = jnp.max(s, axis=-1, keepdims=True)
                p = jnp.exp(s - m)
                den = jnp.sum(p, axis=-1, keepdims=True)
                o_h = lax.dot_general(p.astype(BF16), v2, NN, preferred_element_type=F32) / den
                lse_h = jnp.broadcast_to(m + jnp.log(den), (QBLK, ATT_HW))
                o = o_h if o is None else jnp.where(head_lanes[h], o_h, o)
                lse = lse_h if lse is None else jnp.where(head_lanes[h], lse_h, lse)
            o_rows.put(r, n * QBLK, QBLK, o)
            l_rows.put(r, n * QBLK, QBLK, lse)

        for r in range(dil):
            for n in range(nb):
                one(r, n, causal if n == 0 else band)
        o_rows.flush()
        l_rows.flush()

    q_spec, k_spec, v_spec, o_spec = _attn_specs(i, seq)
    shape = jax.ShapeDtypeStruct((bsz, seq, 1024), F32)
    return pl.pallas_call(
        body, name="attn_fwd_d%d" % dil, grid=(bsz, ATT_HB), in_specs=[q_spec, k_spec, v_spec],
        out_specs=[o_spec, o_spec], out_shape=[shape, shape], scratch_shapes=_stage_shapes(dil, seq, 5),
        compiler_params=_params(("parallel", "parallel")))(q_all, kv_all, kv_all)


def _attn_bwd_call(q_all, kv_all, o, l, do, dl, i, dil, grads):
    bsz, seq, _ = q_all.shape
    nb = _branch_geometry(dil, seq)
    scale = HEAD ** -0.5
    first = grads is None

    def body(*refs):
        n_pass = 0 if first else 3
        blocks = list(refs[:7]) + list(refs[7 + n_pass:10 + n_pass])
        stages = refs[10 + n_pass:] if dil > MAX_STRIDE else (None,) * 10
        q_rows, k_rows, v_rows, o_rows, l_rows, do_rows, dl_rows, dq_rows, dk_rows, dv_rows = (
            _Rows(ref, dil, st) for ref, st in zip(blocks, stages))
        band, causal = _masks()
        head_lanes = _head_lanes()
        for rows in (q_rows, k_rows, v_rows, o_rows, l_rows, do_rows, dl_rows):
            rows.fill()

        def one(r, n, mask):
            q0 = n * QBLK
            k0, kn = (0, QBLK) if n == 0 else ((n - 1) * QBLK, 2 * QBLK)
            q2 = q_rows.get(r, q0, QBLK).astype(BF16)
            k2 = k_rows.get(r, k0, kn).astype(BF16)
            v2 = v_rows.get(r, k0, kn).astype(BF16)
            l2, do2 = l_rows.get(r, q0, QBLK), do_rows.get(r, q0, QBLK)
            t2 = dl_rows.get(r, q0, QBLK) - do2 * o_rows.get(r, q0, QBLK)
            do2 = do2.astype(BF16)
            dq, dk, dv = None, None, None
            for h in range(ATT_HW // HEAD):
                q = jnp.where(head_lanes[h], q2, 0)
                d_ob = jnp.where(head_lanes[h], do2, 0)
                s = lax.dot_general(q, k2, NT, preferred_element_type=F32) * scale
                s = jnp.where(mask, s, NEG)
                lse = jnp.max(jnp.where(head_lanes[h], l2, NEG), axis=-1, keepdims=True)
                p = jnp.exp(s - lse)
                row = jnp.sum(jnp.where(head_lanes[h], t2, 0.0), axis=-1, keepdims=True)
                dp = lax.dot_general(d_ob, v2, NT, preferred_element_type=F32)
                ds = (p * (dp + row)).astype(BF16)
                dq_h = lax.dot_general(ds, k2, NN, preferred_element_type=F32) * scale
                dk_h = lax.dot_general(ds, q, TN, preferred_element_type=F32) * scale
                dv_h = lax.dot_general(p.astype(BF16), d_ob, TN, preferred_element_type=F32)
                dq = dq_h if dq is None else jnp.where(head_lanes[h], dq_h, dq)
                dk = dk_h if dk is None else dk + dk_h
                dv = dv_h if dv is None else dv + dv_h
            dq_rows.put(r, q0, QBLK, dq)
            return dk, dv

        for r in range(dil):
            dk_cur, dv_cur = one(r, 0, causal)
            for n in range(1, nb):
                dk, dv = one(r, n, band)
                dk_rows.put(r, (n - 1) * QBLK, QBLK, dk_cur + dk[:QBLK])
                dv_rows.put(r, (n - 1) * QBLK, QBLK, dv_cur + dv[:QBLK])
                dk_cur, dv_cur = dk[QBLK:], dv[QBLK:]
            dk_rows.put(r, (nb - 1) * QBLK, QBLK, dk_cur)
            dv_rows.put(r, (nb - 1) * QBLK, QBLK, dv_cur)
        for rows in (dq_rows, dk_rows, dv_rows):
            rows.flush()

    q_spec, k_spec, v_spec, o_spec = _attn_specs(i, seq)
    in_specs = [q_spec, k_spec, v_spec, o_spec, o_spec, o_spec, o_spec]
    args = [q_all, kv_all, kv_all, o, l, do, dl]
    aliases = {}
    if not first:
        in_specs += [pl.BlockSpec(memory_space=pl.ANY)] * 3
        args += list(grads)
        aliases = {7: 0, 8: 1, 9: 2}
    shape = jax.ShapeDtypeStruct((bsz, seq, 3 * 1024), F32)
    return pl.pallas_call(
        body, name="attn_bwd_d%d" % dil, grid=(bsz, ATT_HB), in_specs=in_specs,
        out_specs=[q_spec, q_spec, q_spec], out_shape=[shape, shape, shape], input_output_aliases=aliases,
        scratch_shapes=_stage_shapes(dil, seq, 10), compiler_params=_params(("parallel", "parallel")))(*args)


@jax.custom_vjp
def attn_branches(q_all, kv_all):
    return _attn_branches_fwd(q_all, kv_all)[0]


def _attn_branches_fwd(q_all, kv_all):
    outs = []
    for i, dil in enumerate(BRANCH_DIL):
        outs += list(_attn_fwd_call(q_all, kv_all, i, dil))
    return tuple(outs), (q_all, kv_all, tuple(outs))


def _attn_branches_bwd(res, cts):
    q_all, kv_all, outs = res
    grads = None
    for i, dil in enumerate(BRANCH_DIL):
        grads = _attn_bwd_call(q_all, kv_all, outs[2 * i], outs[2 * i + 1], cts[2 * i], cts[2 * i + 1], i, dil, grads)
    dq, dk, dv = grads
    return dq, jnp.concatenate([dk, dv], axis=-1)


attn_branches.defvjp(_attn_branches_fwd, _attn_branches_bwd)


def ada_fwd(c_all, w, b):
    n, d, cs = w.shape
    nb = c_all.shape[0]

    def body(c_ref, w_ref, b_ref, o_ref):
        a = jax.nn.silu(c_ref[...]).astype(BF16)
        o_ref[...] = lax.dot_general(a, w_ref[...].astype(BF16), NN, preferred_element_type=F32) + b_ref[...]

    return pl.pallas_call(
        body, name="ada_fwd", grid=(n,),
        in_specs=[pl.BlockSpec((nb, d), lambda i: (0, 0)), pl.BlockSpec((None, d, cs), lambda i: (i, 0, 0)),
                  pl.BlockSpec((None, 1, cs), lambda i: (i, 0, 0))],
        out_specs=pl.BlockSpec((None, nb, cs), lambda i: (i, 0, 0)),
        out_shape=jax.ShapeDtypeStruct((n, nb, cs), F32), compiler_params=_params(("parallel",)))(c_all, w, b)


def ada_bwd(c_all, dm):
    n, nb, cs = dm.shape
    d = c_all.shape[1]

    def body(c_ref, dm_ref, dw_ref, db_ref):
        a = jax.nn.silu(c_ref[...]).astype(BF16)
        g = dm_ref[...]
        dw_ref[...] = lax.dot_general(a, g.astype(BF16), TN, preferred_element_type=F32)
        db_ref[...] = jnp.sum(g, axis=0, keepdims=True)

    return pl.pallas_call(
        body, name="ada_bwd", grid=(n,),
        in_specs=[pl.BlockSpec((nb, d), lambda i: (0, 0)), pl.BlockSpec((None, nb, cs), lambda i: (i, 0, 0))],
        out_specs=[pl.BlockSpec((None, d, cs), lambda i: (i, 0, 0)), pl.BlockSpec((None, 1, cs), lambda i: (i, 0, 0))],
        out_shape=[jax.ShapeDtypeStruct((n, d, cs), F32), jax.ShapeDtypeStruct((n, 1, cs), F32)],
        compiler_params=_params(("parallel",)))(c_all, dm)


def all_gather(x, name):
    m, n = x.shape

    def body(x_ref, out_ref, send_sems, recv_sems, local_sem):
        px, py, pc = lax.axis_index("x"), lax.axis_index("y"), lax.axis_index("c")
        me, sibling = (px, py, pc), (px, py, 1 - pc)
        chips = [(1 - px, py), (px, 1 - py), (1 - px, 1 - py)]

        def rows(qx, qy, qc):
            return out_ref.at[pl.ds((4 * qx + 2 * qy + qc) * m, m), :]

        def copy(k, block, to, src=None):
            return pltpu.make_async_remote_copy(
                src_ref=rows(*block) if src is None else src, dst_ref=rows(*block),
                send_sem=send_sems.at[k], recv_sem=recv_sems.at[k], device_id=to,
                device_id_type=pl.DeviceIdType.MESH)

        mine = pltpu.make_async_copy(x_ref, rows(*me), local_sem)
        mine.start()
        first = [copy(0, me, sibling, src=x_ref)]
        first += [copy(1 + j, me, (*chip, pc), src=x_ref) for j, chip in enumerate(chips)]
        for cp in first:
            cp.start()
        passed = [copy(4 + j, (*chip, pc), sibling) for j, chip in enumerate(chips)]
        for j, chip in enumerate(chips):
            copy(1 + j, (*chip, pc), me).wait_recv()
            passed[j].start()
        copy(0, sibling, me).wait_recv()
        for j, chip in enumerate(chips):
            copy(4 + j, (*chip, 1 - pc), me).wait_recv()
        for cp in first + passed:
            cp.wait_send()
        mine.wait()

    out = pl.pallas_call(
        body, name=name, out_shape=jax.ShapeDtypeStruct((N_DEV * m, n), x.dtype),
        in_specs=[pl.BlockSpec(memory_space=pl.ANY)], out_specs=pl.BlockSpec(memory_space=pl.ANY),
        scratch_shapes=[pltpu.SemaphoreType.DMA((7,)), pltpu.SemaphoreType.DMA((7,)), pltpu.SemaphoreType.DMA(())],
    )(x)
    return out.reshape(N_DEV, m, n)


def exchange_partials(p, name):
    _, m, n = p.shape

    def body(p_ref, out_ref, send_sems, recv_sems, local_sem):
        px, py, pc = lax.axis_index("x"), lax.axis_index("y"), lax.axis_index("c")
        me = 4 * px + 2 * py + pc
        mine = pltpu.make_async_copy(p_ref.at[me], out_ref.at[me], local_sem)
        mine.start()
        copies = []
        for k in range(1, N_DEV):
            qx = 1 - px if k & 4 else px
            qy = 1 - py if k & 2 else py
            qc = 1 - pc if k & 1 else pc
            cp = pltpu.make_async_remote_copy(
                src_ref=p_ref.at[4 * qx + 2 * qy + qc], dst_ref=out_ref.at[me],
                send_sem=send_sems.at[k - 1], recv_sem=recv_sems.at[k - 1], device_id=(qx, qy, qc),
                device_id_type=pl.DeviceIdType.MESH)
            cp.start()
            copies.append(cp)
        for cp in copies:
            cp.wait()
        mine.wait()

    return pl.pallas_call(
        body, name=name, out_shape=jax.ShapeDtypeStruct(p.shape, p.dtype),
        in_specs=[pl.BlockSpec(memory_space=pl.ANY)], out_specs=pl.BlockSpec(memory_space=pl.ANY),
        scratch_shapes=[pltpu.SemaphoreType.DMA((7,)), pltpu.SemaphoreType.DMA((7,)), pltpu.SemaphoreType.DMA(())],
    )(p)


def sum_slots(*parts):
    _, r, n = parts[0].shape
    tr = _tile(r, max(16, (256 * 1024) // n))

    def body(*refs):
        acc = None
        for g_ref in refs[:-1]:
            for s in range(g_ref.shape[0]):
                term = g_ref[s].astype(F32)
                acc = term if acc is None else acc + term
        refs[-1][...] = acc

    return pl.pallas_call(
        body, name="sum_slots", grid=(r // tr,),
        in_specs=[pl.BlockSpec((p.shape[0], tr, n), lambda i: (0, i, 0)) for p in parts],
        out_specs=pl.BlockSpec((tr, n), lambda i: (i, 0)), out_shape=jax.ShapeDtypeStruct((r, n), F32),
        compiler_params=_params(("parallel",)))(*parts)


def adamw(w, m, v, g, name):
    nl, r, n = w.shape
    partials = g.ndim == 4
    tr = _tile(r, max(8, (256 * 1024) // n))
    c1 = 1.0 / (1.0 - ADAM_B1 ** ADAM_STEP)
    c2 = 1.0 / (1.0 - ADAM_B2 ** ADAM_STEP)

    def body(w_ref, m_ref, v_ref, g_ref, go_ref, d_ref, mo_ref, vo_ref):
        if partials:
            grad = g_ref[0].astype(F32)
            for s in range(1, N_DEV):
                grad = grad + g_ref[s].astype(F32)
        else:
            grad = g_ref[...]
        m_new = ADAM_B1 * m_ref[...] + (1.0 - ADAM_B1) * grad
        v_new = ADAM_B2 * v_ref[...] + (1.0 - ADAM_B2) * grad * grad
        go_ref[...] = grad
        mo_ref[...] = m_new
        vo_ref[...] = v_new
        d_ref[...] = -ADAM_LR * ((m_new * c1) / (jnp.sqrt(v_new * c2) + ADAM_EPS) + ADAM_WD * w_ref[...])

    spec = pl.BlockSpec((None, tr, n), lambda l, i: (l, i, 0))
    g_spec = pl.BlockSpec((None, N_DEV, tr, n), lambda l, i: (l, 0, i, 0)) if partials else spec
    shape = jax.ShapeDtypeStruct(w.shape, F32)
    return pl.pallas_call(
        body, name=name, grid=(nl, r // tr), in_specs=[spec, spec, spec, g_spec], out_specs=[spec] * 4,
        out_shape=[shape] * 4, compiler_params=_params(("parallel", "parallel")))(w, m, v, g)


def pack(arrays):
    flat = jnp.concatenate([a.reshape(-1).astype(F32) for a in arrays])
    rows = -(-flat.shape[0] // LANES)
    mult = 8 if rows <= 512 else 512
    rows = -(-rows // mult) * mult
    flat = jnp.pad(flat, (0, rows * LANES - flat.shape[0]))
    return flat.reshape(rows, LANES)


def unpack(slab, shapes):
    flat = slab.reshape(slab.shape[:-2] + (-1,))
    out, off = [], 0
    for s in shapes:
        size = math.prod(s)
        out.append(flat[..., off:off + size].reshape(flat.shape[:-1] + tuple(s)))
        off += size
    return out


def _s5_discretize(lam_re, lam_im, log_dt, b_re, b_im, c_re, c_im):
    dt = jnp.exp(log_dt)[:, None]
    xr, th = lam_re * dt, lam_im * dt
    er = jnp.exp(xr)
    a_re, a_im = er * jnp.cos(th), er * jnp.sin(th)
    am1 = jnp.expm1(xr) * jnp.cos(th) - 2.0 * jnp.square(jnp.sin(0.5 * th))
    den = lam_re * lam_re + lam_im * lam_im
    fr = (am1 * lam_re + a_im * lam_im) / den
    fi = (a_im * lam_re - am1 * lam_im) / den
    bb_re = fr[..., None] * b_re - fi[..., None] * b_im
    bb_im = fr[..., None] * b_im + fi[..., None] * b_re
    eye = jnp.eye(8, dtype=F32)

    def pack_b(bb):
        return jnp.einsum('qgpc,gh->qgchp', bb.reshape(8, 8, 64, 16), eye).reshape(8, 128, 512)

    def pack_c(cc):
        return jnp.einsum('qgcp,gh->qgphc', cc.reshape(8, 8, 16, 64), eye).reshape(8, 512, 128)

    b_mat = jnp.concatenate([pack_b(bb_re), pack_b(bb_im)], axis=-1)
    c_mat = jnp.concatenate([pack_c(c_re), pack_c(-c_im)], axis=1)
    return a_re.reshape(8, HALF), a_im.reshape(8, HALF), b_mat, c_mat


WEIGHT_ORDER = (("ssm_w_glu", 0, True), ("mlp_w1", 0, True), ("mlp_w2", 0, False),
                ("ssm_w_glu", 1, True), ("mlp_w1", 1, True), ("mlp_w2", 1, False), ("w_kv", None, True),
                ("attn_w_q", 0, True), ("attn_w_o", 0, False), ("mlp_w1", 2, True), ("mlp_w2", 2, False),
                ("attn_w_q", 1, True), ("attn_w_o", 1, False), ("mlp_w1", 3, True), ("mlp_w2", 3, False))


class _WeightChain:
    def __init__(self, first, upcoming):
        self.cur, self.upcoming, self.used = first, list(upcoming), 0

    def matmul(self, x, name, relu2_input=False, bf16_output=False):
        assert WEIGHT_ORDER[self.used][0] == name
        col = WEIGHT_ORDER[self.used][2]
        assert not (col and relu2_input) and (col or not bf16_output)
        self.used += 1
        w = self.cur if col else self.cur.reshape(-1, self.cur.shape[-1])
        if col:
            plain, with_next = (linear_col_bf16, linear_col_bf16_next) if bf16_output else (linear_col, linear_col_next)
        else:
            plain, with_next = (linear_relu2, linear_relu2_next) if relu2_input else (linear, linear_next)
        if not self.upcoming:
            return plain(x, w)
        y, self.cur = with_next(x, w, self.upcoming.pop(0))
        return y


def _local_loss(diff, x, target, first_weight, shards):
    bsz, seq, d = x.shape
    t = bsz * seq
    rows3 = lambda a: a.reshape(bsz, seq, a.shape[-1])
    rows2 = lambda a: a.reshape(t, a.shape[-1])
    mods, kvmod, ln_g, ssm_d, kv_g, final_g = (diff[k] for k in ("mods", "kvmod", "ln_g", "ssm_d", "kv_g", "final_g"))
    chain = _WeightChain(first_weight, shards)

    def chunks(sub):
        mod = mods[sub]
        return mod[:, None, :d], mod[:, None, d:2 * d], mod[:, None, 2 * d:], ln_g[sub // 2, sub % 2][None]

    h = x
    kv_all = None
    shift, scale, gate, gain = chunks(0)
    (u,) = modnorm(h, scale, shift, gain)
    for layer in range(4):
        if layer == 2:
            (ukv,) = modnorm(h, kvmod[:, None, d:], kvmod[:, None, :d], kv_g[None])
            kv_all = rows3(chain.matmul(rows2(ukv), "w_kv"))
        if layer < 2:
            a_re, a_im, b_mat, c_mat = _s5_discretize(*(diff[k][layer] for k in (
                "ssm_lam_re", "ssm_lam_im", "ssm_log_dt", "ssm_b_re", "ssm_b_im", "ssm_c_re", "ssm_c_im")))
            y0 = s5_ssm(u, b_mat, c_mat, a_re, a_im)
            (z,) = gelu_skip(y0, u, ssm_d[layer][None])
            (y,) = glu(rows3(chain.matmul(rows2(z), "ssm_w_glu")))
        else:
            q_all = rows3(chain.matmul(rows2(u), "attn_w_q"))
            o1, l1, o2, l2, o3, l3 = attn_branches(q_all, kv_all)
            (o,) = combine(o1, o2, o3, l1, l2, l3)
            y = rows3(chain.matmul(rows2(o), "attn_w_o"))
        shift, scale, next_gate, gain = chunks(2 * layer + 1)
        h, u = res_modnorm(h, y, gate, scale, shift, gain)
        gate = next_gate
        pre = chain.matmul(rows2(u), "mlp_w1", bf16_output=True)
        y = rows3(chain.matmul(pre, "mlp_w2", relu2_input=True))
        if layer < 3:
            shift, scale, next_gate, gain = chunks(2 * layer + 2)
            h, u = res_modnorm(h, y, gate, scale, shift, gain)
            gate = next_gate
        else:
            (h,) = gate_res(h, y, gate)
    (row_loss,) = final_loss(h, target, final_g[None])
    return jnp.sum(row_loss)


SSM_NAMES = ("ssm_lam_re", "ssm_lam_im", "ssm_log_dt", "ssm_b_re", "ssm_b_im", "ssm_c_re", "ssm_c_im")
ARG_NAMES = ("x", "c", "ln_g", "ada_w", "ada_b") + SSM_NAMES + (
    "ssm_d", "ssm_w_glu", "kv_g", "kv_ada_w", "kv_ada_b", "w_kv", "attn_w_q", "attn_w_o", "mlp_w1", "mlp_w2", "final_g")
WEIGHT_NAMES = ARG_NAMES[2:]


def kernel(x, c, ln_g, ada_w, ada_b, ssm_lam_re, ssm_lam_im, ssm_log_dt, ssm_b_re, ssm_b_im, ssm_c_re, ssm_c_im, ssm_d, ssm_w_glu, kv_g, kv_ada_w, kv_ada_b, w_kv, attn_w_q, attn_w_o, mlp_w1, mlp_w2, final_g, loss_target, m_ln_g, m_ada_w, m_ada_b, m_ssm_lam_re, m_ssm_lam_im, m_ssm_log_dt, m_ssm_b_re, m_ssm_b_im, m_ssm_c_re, m_ssm_c_im, m_ssm_d, m_ssm_w_glu, m_kv_g, m_kv_ada_w, m_kv_ada_b, m_w_kv, m_attn_w_q, m_attn_w_o, m_mlp_w1, m_mlp_w2, m_final_g, v_ln_g, v_ada_w, v_ada_b, v_ssm_lam_re, v_ssm_lam_im, v_ssm_log_dt, v_ssm_b_re, v_ssm_b_im, v_ssm_c_re, v_ssm_c_im, v_ssm_d, v_ssm_w_glu, v_kv_g, v_kv_ada_w, v_kv_ada_b, v_w_kv, v_attn_w_q, v_attn_w_o, v_mlp_w1, v_mlp_w2, v_final_g):
    args = locals()
    w = {n: args[n] for n in WEIGHT_NAMES}
    mom = {n: args["m_" + n] for n in WEIGHT_NAMES}
    var = {n: args["v_" + n] for n in WEIGHT_NAMES}
    bsz, seq, d = x.shape
    me = 4 * lax.axis_index("x") + 2 * lax.axis_index("y") + lax.axis_index("c")

    small = all_gather(pack([c, ln_g, ssm_d]), "gather_small")
    c_parts, ln_parts, d_parts = unpack(small, [c.shape, ln_g.shape, ssm_d.shape])
    c_all = c_parts.reshape(N_DEV * bsz, d)
    ln_full = jnp.moveaxis(ln_parts, 0, 2).reshape(4, 2, d)
    d_full = jnp.moveaxis(d_parts, 0, 1).reshape(2, d)

    cs = ada_w.shape[-1]
    mod_cols = ada_fwd(c_all, ada_w.reshape(8, d, cs), ada_b.reshape(8, 1, cs))
    kcs = kv_ada_w.shape[-1]
    kv_cols = ada_fwd(c_all, kv_ada_w[None], jnp.zeros((1, 1, kcs), F32))
    mod_g = all_gather(mod_cols.reshape(8 * N_DEV * bsz, cs), "gather_mod")
    kv_g_all = all_gather(kv_cols.reshape(N_DEV * bsz, kcs), "gather_kvmod")
    mods_all = jnp.moveaxis(mod_g.reshape(N_DEV, 8, N_DEV * bsz, cs), 0, 2).reshape(8, N_DEV * bsz, N_DEV * cs)
    kvmod_all = jnp.moveaxis(kv_g_all, 0, 1).reshape(N_DEV * bsz, N_DEV * kcs) + kv_ada_b[None]
    mods = lax.dynamic_slice_in_dim(mods_all, me * bsz, bsz, axis=1)
    kvmod = lax.dynamic_slice_in_dim(kvmod_all, me * bsz, bsz, axis=0)

    shards = [w[n] if l is None else w[n][l] for n, l, _ in WEIGHT_ORDER]
    first_weight = all_gather(shards[0].astype(BF16), "gather_first_weight")

    diff = {"mods": mods, "kvmod": kvmod, "ln_g": ln_full, "ssm_d": d_full, "kv_g": kv_g, "final_g": final_g}
    diff.update({n: w[n] for n in SSM_NAMES})
    loss_local, (g_diff, grad_x, g_first, g_shards) = jax.value_and_grad(_local_loss, argnums=(0, 1, 3, 4))(
        diff, x, loss_target, first_weight, shards[1:])
    loss = lax.psum(loss_local, AXES)
    g_shards = [sum_slots(exchange_partials(g_first, "exchange_first_weight"))] + list(g_shards)

    dmod = all_gather(pack([g_diff["mods"], g_diff["kvmod"]]), "gather_dmod")
    dmods_p, dkv_p = unpack(dmod, [g_diff["mods"].shape, g_diff["kvmod"].shape])
    dmods_all = jnp.moveaxis(dmods_p, 0, 1).reshape(8, N_DEV * bsz, 3 * d)
    dkv_all = dkv_p.reshape(N_DEV * bsz, 2 * d)
    g_ada_w, g_ada_b = ada_bwd(c_all, lax.dynamic_slice_in_dim(dmods_all, me * cs, cs, axis=2))
    g_kv_ada_w, _ = ada_bwd(c_all, lax.dynamic_slice_in_dim(dkv_all, me * kcs, kcs, axis=1)[None])
    _, g_kv_ada_b = ada_bwd(c_all, dkv_all[None])

    small_names = ("ln_g", "ssm_d", "kv_g", "final_g") + SSM_NAMES
    partial = all_gather(pack([g_diff[n] for n in small_names]), "gather_small_grads")
    totals = unpack(sum_slots(partial), [g_diff[n].shape for n in small_names])
    g_small = dict(zip(small_names, totals))
    g_small["ln_g"] = lax.dynamic_slice_in_dim(g_small["ln_g"], me * ln_g.shape[-1], ln_g.shape[-1], axis=2)
    g_small["ssm_d"] = lax.dynamic_slice_in_dim(g_small["ssm_d"], me * ssm_d.shape[-1], ssm_d.shape[-1], axis=1)
    g_small["ada_b"] = g_ada_b.reshape(ada_b.shape)
    g_small["kv_ada_b"] = g_kv_ada_b.reshape(kv_ada_b.shape)

    out = {}

    def put(name, res, shape):
        for kind, a in zip(("grad_", "delta_", "new_m_", "new_v_"), res):
            out[kind + name] = a.reshape(shape)

    for name in ("ssm_w_glu", "w_kv", "attn_w_q", "attn_w_o", "mlp_w1", "mlp_w2"):
        grads = jnp.stack([g for g, (n, _, _) in zip(g_shards, WEIGHT_ORDER) if n == name])
        v3 = lambda a: a.reshape(grads.shape)
        put(name, adamw(v3(w[name]), v3(mom[name]), v3(var[name]), grads, "adamw_" + name), w[name].shape)
    v3 = lambda a: a.reshape(8, d, cs)
    put("ada_w", adamw(v3(ada_w), v3(m_ada_w), v3(v_ada_w), g_ada_w, "adamw_ada_w"), ada_w.shape)
    put("kv_ada_w", adamw(kv_ada_w[None], m_kv_ada_w[None], v_kv_ada_w[None], g_kv_ada_w, "adamw_kv_ada_w"), kv_ada_w.shape)
    names = small_names + ("ada_b", "kv_ada_b")
    res = adamw(pack([w[n] for n in names])[None], pack([mom[n] for n in names])[None],
                pack([var[n] for n in names])[None], pack([g_small[n] for n in names])[None], "adamw_small")
    for kind, slab in zip(("grad_", "delta_", "new_m_", "new_v_"), res):
        for n, a in zip(names, unpack(slab[0], [w[n].shape for n in names])):
            out[kind + n] = a

    result = [loss, grad_x]
    for kind in ("grad_", "delta_", "new_m_", "new_v_"):
        result += [out[kind + n] for n in WEIGHT_NAMES]
    return tuple(result)
```
